```python
import math
import jax, jax.numpy as jnp
from jax import lax
import numpy as np

D_MODEL = 1024
BATCH = 8
SEQ = 2048
DEPTH = 1
DEC_BATCH = 128
DEC_SEQ = 8
PAST_LEN = 16384
PAGE_SIZE = 128

POOL_WINDOWS = (2, 4, 8, 16)
POOL_GROUPS = len(POOL_WINDOWS)
POOL_GROUP_DIM = D_MODEL // 8
POOL_WIDTH = POOL_GROUPS * POOL_GROUP_DIM
POOL_HIST = max(POOL_WINDOWS) - 1
ML_HEADS = 4
ML_WIDTH = D_MODEL
ML_HEAD_DIM = ML_WIDTH // ML_HEADS
ML_CHUNK = 64
M_INIT = -1e30
D_FF = ((8 * D_MODEL + 3 * 256 - 1) // (3 * 256)) * 256
EPS = 1e-6
IN_WIDTHS = (POOL_WIDTH, ML_WIDTH, ML_WIDTH, ML_WIDTH, ML_WIDTH, ML_HEADS, ML_HEADS, D_MODEL, D_MODEL)
N_IN = sum(IN_WIDTHS)
IN_SPLITS = tuple(sum(IN_WIDTHS[:i + 1]) for i in range(len(IN_WIDTHS) - 1))

kernel_name = "hybrid_pool_mlstm_adaln_decode_step"


def rmsnorm(x, g):
    xf = x.astype(jnp.float32)
    y = xf * lax.rsqrt(jnp.mean(xf * xf, axis=-1, keepdims=True) + EPS)
    return (y * g.astype(jnp.float32)).astype(x.dtype)


def pool_mixer(u, hist, pos0, w_pool, s_pool):
    B, T, _ = u.shape
    ext = jnp.concatenate([hist, u], axis=1)
    ef = ext.astype(jnp.float32)
    cs = jnp.concatenate([jnp.zeros_like(ef[:, :1]), jnp.cumsum(ef, axis=1)], axis=1)
    end = cs[:, POOL_HIST + 1:]
    pos = pos0 + jnp.arange(T)
    means = []
    for g, w in enumerate(POOL_WINDOWS):
        sl = slice(g * POOL_GROUP_DIM, (g + 1) * POOL_GROUP_DIM)
        start = cs[:, POOL_HIST + 1 - w: POOL_HIST + 1 - w + T, sl]
        cnt = jnp.minimum(pos + 1, w).astype(jnp.float32)[None, :, None]
        means.append((end[..., sl] - start) / cnt)
    mean = jnp.concatenate(means, axis=-1)
    d = (mean - u.astype(jnp.float32)).astype(u.dtype).reshape(B, T, POOL_GROUPS, POOL_GROUP_DIM)
    y = jnp.einsum('btgc,gcd->btgd', d, w_pool).reshape(B, T, POOL_WIDTH) * s_pool
    return y, ext[:, -POOL_HIST:]


def mlstm_chunk(carry, xs):
    C, n, m = carry
    q, k, v, ig, lf = xs
    L = q.shape[2]
    b = jnp.cumsum(lf, axis=-1)
    causal = jnp.tril(jnp.ones((L, L), dtype=bool))
    logd = jnp.where(causal, b[..., :, None] - b[..., None, :] + ig[..., None, :], -jnp.inf)
    a = b + m[..., None]
    m_t = jnp.maximum(a, jnp.max(logd, axis=-1))
    w_intra = jnp.exp(logd - m_t[..., None])
    w_inter = jnp.exp(a - m_t)
    s = jnp.einsum('bhtd,bhsd->bhts', q, k) * w_intra
    num = w_inter[..., None] * jnp.einsum('bhtd,bhde->bhte', q, C) + jnp.einsum('bhts,bhse->bhte', s, v)
    den = w_inter * jnp.einsum('bhtd,bhd->bht', q, n) + jnp.sum(s, axis=-1)
    h = num / jnp.maximum(jnp.abs(den), jnp.exp(-m_t))[..., None]
    m_new = m_t[..., -1]
    decay = jnp.exp(a[..., -1] - m_new)
    w_end = jnp.exp(b[..., -1:] - b + ig - m_new[..., None])
    C_new = decay[..., None, None] * C + jnp.einsum('bhs,bhsd,bhse->bhde', w_end, k, v)
    n_new = decay[..., None] * n + jnp.einsum('bhs,bhsd->bhd', w_end, k)
    return (C_new, n_new, m_new), h


def mlstm_mixer(q, k, v, ig, lf, C0, n0, m0):
    B, T = q.shape[:2]
    L = math.gcd(T, ML_CHUNK)
    nc = T // L

    def to_chunks(z):
        z = z.reshape((B, nc, L) + z.shape[2:])
        return jnp.transpose(z, (1, 0, 3, 2) + tuple(range(4, z.ndim)))

    xs = (to_chunks(q), to_chunks(k), to_chunks(v), to_chunks(ig), to_chunks(lf))
    (C1, n1, m1), h = lax.scan(mlstm_chunk, (C0, n0, m0), xs)
    h = jnp.transpose(h, (1, 0, 3, 2, 4)).reshape(B, T, ML_HEADS, ML_HEAD_DIM)
    return h, C1, n1, m1


def layer_forward(x, c, pool_hist, C0, n0, m0, pos0, g_mix, g_ffn, w_ada, b_ada, w_in, b_igate, b_fgate,
                  w_pool, s_pool, g_head, w_bp, w_bm, w_out, w_gu, w_down):
    B, T, _ = x.shape
    f32 = jnp.float32
    mod = (jax.nn.silu(c) @ w_ada + b_ada)[:, None, :]
    sh1, sc1, gt1, sh2, sc2, gt2 = jnp.split(mod, 6, axis=-1)
    h = rmsnorm(x, g_mix) * (1 + sc1) + sh1
    z = h @ w_in
    u, q, k, v, o, ig, fg, gp, gm = jnp.split(z, IN_SPLITS, axis=-1)
    y_pool, new_hist = pool_mixer(u, pool_hist, pos0, w_pool, s_pool)
    hd = (B, T, ML_HEADS, ML_HEAD_DIM)
    qf = q.reshape(hd).astype(f32)
    kf = k.reshape(hd).astype(f32) * (ML_HEAD_DIM ** -0.5)
    vf = v.reshape(hd).astype(f32)
    igf = ig.astype(f32) + b_igate.astype(f32)
    lff = jax.nn.log_sigmoid(fg.astype(f32) + b_fgate.astype(f32))
    hm, C1, n1, m1 = mlstm_mixer(qf, kf, vf, igf, lff, C0, n0, m0)
    hm = hm * lax.rsqrt(jnp.mean(hm * hm, axis=-1, keepdims=True) + EPS)
    y_ml = (hm.reshape(B, T, ML_WIDTH) * g_head.astype(f32)).astype(x.dtype) * jax.nn.sigmoid(o)
    merged = jax.nn.sigmoid(gp) * (y_pool @ w_bp) + jax.nn.sigmoid(gm) * (y_ml @ w_bm)
    x = x + gt1 * (merged @ w_out)
    h2 = rmsnorm(x, g_ffn) * (1 + sc2) + sh2
    gate, up = jnp.split(h2 @ w_gu, 2, axis=-1)
    x = x + gt2 * ((jax.nn.silu(gate) * up) @ w_down)
    return x, new_hist, C1, n1, m1


def setup_inputs(seed: int = 0) -> dict:
    key = jax.random.key(seed)
    ks = jax.random.split(key, 32)
    nrm = lambda i, shape, s=1.0: jax.random.normal(ks[i], shape, jnp.float32) * s
    D, H, DK = D_MODEL, ML_HEADS, ML_HEAD_DIM
    return {
        "x_prompt": nrm(0, (BATCH, SEQ, D)),
        "x_sample": nrm(1, (DEC_BATCH, DEC_SEQ, D)),
        "c_prompt": nrm(2, (BATCH, D)),
        "c_sample": nrm(3, (DEC_BATCH, D)),
        "state_pool": nrm(4, (DEPTH, DEC_BATCH, POOL_HIST, POOL_WIDTH)),
        "state_mlstm_c": nrm(5, (DEPTH, DEC_BATCH, H, DK, DK), 0.5),
        "state_mlstm_n": nrm(6, (DEPTH, DEC_BATCH, H, DK), 0.5),
        "state_mlstm_m": nrm(7, (DEPTH, DEC_BATCH, H)),
        "g_mix": 1.0 + nrm(8, (DEPTH, D), 0.02),
        "g_ffn": 1.0 + nrm(9, (DEPTH, D), 0.02),
        "g_final": 1.0 + nrm(10, (D,), 0.02),
        "w_ada": nrm(11, (DEPTH, D, 6 * D), 0.5 * D ** -0.5),
        "b_ada": nrm(12, (DEPTH, 6 * D), 0.02),
        "w_in": nrm(13, (DEPTH, D, N_IN), D ** -0.5),
        "b_igate": nrm(14, (DEPTH, H), 0.1),
        "b_fgate": jnp.linspace(3.0, 6.0, H, dtype=jnp.float32)[None, :] + nrm(15, (DEPTH, H), 0.1),
        "w_pool": nrm(16, (DEPTH, POOL_GROUPS, POOL_GROUP_DIM, POOL_GROUP_DIM), POOL_GROUP_DIM ** -0.5),
        "s_pool": 0.5 + nrm(17, (DEPTH, POOL_WIDTH), 0.1),
        "g_head": 1.0 + nrm(18, (DEPTH, ML_WIDTH), 0.02),
        "w_branch_pool": nrm(19, (DEPTH, POOL_WIDTH, D), POOL_WIDTH ** -0.5),
        "w_branch_mlstm": nrm(20, (DEPTH, ML_WIDTH, D), ML_WIDTH ** -0.5),
        "w_out": nrm(21, (DEPTH, D, D), D ** -0.5),
        "w_gate_up": nrm(22, (DEPTH, D, 2 * D_FF), D ** -0.5),
        "w_down": nrm(23, (DEPTH, D_FF, D), D_FF ** -0.5),
    }


def reference(x_prompt, x_sample, c_prompt, c_sample, state_pool, state_mlstm_c, state_mlstm_n, state_mlstm_m,
              g_mix, g_ffn, g_final, w_ada, b_ada, w_in, b_igate, b_fgate, w_pool, s_pool, g_head,
              w_branch_pool, w_branch_mlstm, w_out, w_gate_up, w_down):
    f32 = jnp.float32
    B = x_prompt.shape[0]
    xp, xs = x_prompt, x_sample
    pool_p, c_p, n_p, m_p = [], [], [], []
    pool_s, c_s, n_s, m_s = [], [], [], []
    for l in range(DEPTH):
        w = (g_mix[l], g_ffn[l], w_ada[l], b_ada[l], w_in[l], b_igate[l], b_fgate[l], w_pool[l], s_pool[l],
             g_head[l], w_branch_pool[l], w_branch_mlstm[l], w_out[l], w_gate_up[l], w_down[l])
        hist0 = jnp.zeros((B, POOL_HIST, POOL_WIDTH), xp.dtype)
        C0 = jnp.zeros((B, ML_HEADS, ML_HEAD_DIM, ML_HEAD_DIM), f32)
        n0 = jnp.zeros((B, ML_HEADS, ML_HEAD_DIM), f32)
        m0 = jnp.full((B, ML_HEADS), M_INIT, f32)
        xp, hp, cp, np_, mp = layer_forward(xp, c_prompt, hist0, C0, n0, m0, 0, *w)
        xs, hs, cs, ns, ms = layer_forward(
            xs, c_sample, state_pool[l], state_mlstm_c[l].astype(f32), state_mlstm_n[l].astype(f32),
            state_mlstm_m[l].astype(f32), PAST_LEN, *w)
        pool_p.append(hp); c_p.append(cp.astype(state_mlstm_c.dtype))
        n_p.append(np_.astype(state_mlstm_n.dtype)); m_p.append(mp.astype(state_mlstm_m.dtype))
        pool_s.append(hs.astype(state_pool.dtype)); c_s.append(cs.astype(state_mlstm_c.dtype))
        n_s.append(ns.astype(state_mlstm_n.dtype)); m_s.append(ms.astype(state_mlstm_m.dtype))
    y_prompt = rmsnorm(xp, g_final)
    y_sample = rmsnorm(xs, g_final)
    return (y_prompt, y_sample,
            jnp.stack(pool_p), jnp.stack(c_p), jnp.stack(n_p), jnp.stack(m_p),
            jnp.stack(pool_s), jnp.stack(c_s), jnp.stack(n_s), jnp.stack(m_s))
```

```python
import functools

import jax
import jax.numpy as jnp
from jax import lax
from jax.experimental import pallas as pl
from jax.experimental.pallas import tpu as pltpu

D_MODEL = 1024
POOL_WINDOWS = (2, 4, 8, 16)
POOL_GROUP_DIM = 128
POOL_WIDTH = 512
POOL_HIST = 15
HIST_ROWS = 16
ML_HEADS = 4
ML_HEAD_DIM = 256
D_FF = 2816
EPS = 1e-6
M_INIT = -1e30
PAST_LEN = 16384
LANES = 128
SUBLANES = 8
GATE_LANES = LANES
AUG = ML_HEAD_DIM + LANES
VMEM_LIMIT = 56 * 1024 * 1024

TOKEN_TILE = 512
PROMPT_CHUNK = 256
SAMPLE_SEQ_BLOCK = 16

BF16 = jnp.bfloat16
F32 = jnp.float32

_U0, _Q0, _K0, _V0, _G0, _W_IN_COLS = 0, 512, 1536, 2560, 3584, 3712


def _dot(a, b):
    return jnp.dot(a, b, preferred_element_type=F32)


def _const_spec(shape):
    zeros = (0,) * len(shape)
    return pl.BlockSpec(shape, lambda *_: zeros, pipeline_mode=pl.Buffered(1))


def _params(sem):
    return pltpu.CompilerParams(dimension_semantics=sem, vmem_limit_bytes=VMEM_LIMIT)


def _norm_mod(x_ref, sc_ref, sh_ref, g_ref):
    x = x_ref[...]
    nb, tt, d = x.shape
    ms = jnp.mean(x * x, axis=-1, keepdims=True)
    y = x * lax.rsqrt(ms + EPS) * g_ref[...]
    hmod = y * (1.0 + sc_ref[...]) + sh_ref[...]
    return x.reshape(nb * tt, d), hmod.reshape(nb * tt, d)


def _split3(x):
    hi = x.astype(BF16)
    r1 = x - hi.astype(F32)
    mid = r1.astype(BF16)
    lo = (r1 - mid.astype(F32)).astype(BF16)
    return hi, mid, lo


def _cumsum_cols(tri, x):
    hi, mid, lo = _split3(x)
    return _dot(tri, hi) + _dot(tri, mid) + _dot(tri, lo)


def _cumsum_rows(x, tri):
    hi, mid, lo = _split3(x)
    return _dot(hi, tri) + _dot(mid, tri) + _dot(lo, tri)


def _ada_kernel(c_ref, w_ref, b_ref, o_ref):
    c = c_ref[...]
    a = (c * jax.nn.sigmoid(c)).astype(BF16)
    o_ref[...] = _dot(a, w_ref[...].astype(BF16)) + b_ref[...]


def _ada(c_all, w_ada, b_ada):
    rows = c_all.shape[0]
    ncol = w_ada.shape[1]
    tn = 1024
    return pl.pallas_call(
        _ada_kernel,
        grid=(ncol // tn,),
        in_specs=[pl.BlockSpec((rows, D_MODEL), lambda j: (0, 0)),
                  pl.BlockSpec((D_MODEL, tn), lambda j: (0, j)),
                  pl.BlockSpec((1, tn), lambda j: (0, j))],
        out_specs=pl.BlockSpec((rows, tn), lambda j: (0, j)),
        out_shape=jax.ShapeDtypeStruct((rows, ncol), F32),
        compiler_params=_params(("arbitrary",)),
        name="ada",
    )(c_all, w_ada, b_ada)


def _project(hb, w_ref, wkT_ref, bg_ref, q_ref, kT_ref, v_ref, gc_ref, gr_ref):
    q_ref[...] = _dot(hb, w_ref[:, _Q0:_K0]).astype(BF16)
    kT = lax.dot_general(wkT_ref[...], hb, (((1,), (1,)), ((), ())), preferred_element_type=F32)
    kT_ref[...] = (kT * (ML_HEAD_DIM ** -0.5)).astype(BF16)
    v_ref[...] = _dot(hb, w_ref[:, _V0:_G0]).astype(BF16)
    zg = _dot(hb, w_ref[:, _G0:_W_IN_COLS]) + bg_ref[...]
    log_f = jnp.minimum(zg, 0.0) - jnp.log1p(jnp.exp(-jnp.abs(zg)))
    lane = lax.broadcasted_iota(jnp.int32, zg.shape, 1)
    gc = jnp.where(lane < ML_HEADS, zg, log_f)
    gc_ref[...] = gc
    gr_ref[...] = gc.T


def _pool_project(acc_fn, u, cnt_fn, wpool_ref, spool_ref):
    outs = []
    for g, w in enumerate(POOL_WINDOWS):
        cols = slice(g * POOL_GROUP_DIM, (g + 1) * POOL_GROUP_DIM)
        ug = u[:, cols]
        d = acc_fn(g, w, ug) / cnt_fn(w) - ug
        outs.append(_dot(d.astype(BF16), wpool_ref[g]) * spool_ref[:, cols])
    return jnp.concatenate(outs, axis=-1).astype(BF16)


def _inproj_prompt_kernel(x_ref, sc_ref, sh_ref, g_ref, w_ref, wkT_ref, bg_ref, wpool_ref, spool_ref,
                          yp_ref, q_ref, kT_ref, v_ref, gc_ref, gr_ref, hout_ref, ext_ref, *, tm):
    t = pl.program_id(1)

    @pl.when(t == 0)
    def _():
        ext_ref[0:HIST_ROWS, :] = jnp.zeros((HIST_ROWS, POOL_WIDTH), F32)

    _, hmod = _norm_mod(x_ref, sc_ref, sh_ref, g_ref)
    hb = hmod.astype(BF16)
    _project(hb, w_ref, wkT_ref, bg_ref, q_ref, kT_ref, v_ref, gc_ref, gr_ref)

    u = _dot(hb, w_ref[:, _U0:_Q0])
    ext_ref[HIST_ROWS:HIST_ROWS + tm, :] = u
    pos = t * tm + lax.broadcasted_iota(jnp.int32, (tm, 1), 0)

    def acc_fn(g, w, ug):
        acc = ug
        for j in range(1, w):
            acc = acc + ext_ref[pl.ds(HIST_ROWS - j, tm), g * POOL_GROUP_DIM:(g + 1) * POOL_GROUP_DIM]
        return acc

    def cnt_fn(w):
        return jnp.minimum(pos + 1, w).astype(F32)

    yp_ref[...] = _pool_project(acc_fn, u, cnt_fn, wpool_ref, spool_ref)
    last = ext_ref[tm:tm + HIST_ROWS, :]
    hout_ref[0] = last
    ext_ref[0:HIST_ROWS, :] = last


def _inproj_sample_kernel(x_ref, sc_ref, sh_ref, g_ref, w_ref, wkT_ref, bg_ref, wpool_ref, spool_ref, hist_ref,
                          yp_ref, q_ref, kT_ref, v_ref, gc_ref, gr_ref, k_ref, hout_ref, *, pos0):
    _, hmod = _norm_mod(x_ref, sc_ref, sh_ref, g_ref)
    hb = hmod.astype(BF16)
    _project(hb, w_ref, wkT_ref, bg_ref, q_ref, kT_ref, v_ref, gc_ref, gr_ref)
    k_ref[...] = (_dot(hb, w_ref[:, _K0:_V0]) * (ML_HEAD_DIM ** -0.5)).astype(BF16)

    u = _dot(hb, w_ref[:, _U0:_Q0])
    nb, tt = x_ref.shape[0], x_ref.shape[1]
    u3 = u.reshape(nb, tt, POOL_WIDTH)
    h0 = hist_ref[:, 0:SUBLANES, :]
    h1 = hist_ref[:, SUBLANES:HIST_ROWS, :]
    tok = lax.broadcasted_iota(jnp.int32, (nb, tt, POOL_GROUP_DIM), 1)

    def acc_fn(g, w, ug):
        cols = slice(g * POOL_GROUP_DIM, (g + 1) * POOL_GROUP_DIM)
        new, mid, old = u3[:, :, cols], h1[:, :, cols], h0[:, :, cols]
        acc = new
        for j in range(1, w):
            if j < tt:
                term = jnp.where(tok >= j, pltpu.roll(new, j, 1), pltpu.roll(mid, j, 1))
            elif j == tt:
                term = mid
            else:
                term = jnp.where(tok >= j - tt, pltpu.roll(mid, j - tt, 1), pltpu.roll(old, j - tt, 1))
            acc = acc + term
        return acc.reshape(nb * tt, POOL_GROUP_DIM)

    pos = pos0 + lax.broadcasted_iota(jnp.int32, (nb, tt, 1), 1).reshape(nb * tt, 1)

    def cnt_fn(w):
        return jnp.minimum(pos + 1, w).astype(F32)

    yp_ref[...] = _pool_project(acc_fn, u, cnt_fn, wpool_ref, spool_ref)
    hout_ref[:, 0:SUBLANES, :] = h1
    hout_ref[:, SUBLANES:HIST_ROWS, :] = u3


def _inproj_prompt(x, sc, sh, g_mix, w_in_r, w_kT, bgate, w_pool, s_pool):
    nbatch, seq, _ = x.shape
    tm = TOKEN_TILE
    nt = seq // tm
    ntok = nbatch * seq
    row = lambda b, t: (b * nt + t, 0)
    col = lambda b, t: (0, b * nt + t)
    mod = pl.BlockSpec((1, 1, D_MODEL), lambda b, t: (b, 0, 0))
    return pl.pallas_call(
        functools.partial(_inproj_prompt_kernel, tm=tm),
        grid=(nbatch, nt),
        in_specs=[pl.BlockSpec((1, tm, D_MODEL), lambda b, t: (b, t, 0)), mod, mod,
                  _const_spec((1, D_MODEL)), _const_spec((D_MODEL, _W_IN_COLS)), _const_spec((D_MODEL, D_MODEL)),
                  _const_spec((1, GATE_LANES)), _const_spec((4, POOL_GROUP_DIM, POOL_GROUP_DIM)),
                  _const_spec((1, POOL_WIDTH))],
        out_specs=[pl.BlockSpec((tm, POOL_WIDTH), row), pl.BlockSpec((tm, D_MODEL), row),
                   pl.BlockSpec((D_MODEL, tm), col), pl.BlockSpec((tm, D_MODEL), row),
                   pl.BlockSpec((tm, GATE_LANES), row), pl.BlockSpec((GATE_LANES, tm), col),
                   pl.BlockSpec((1, HIST_ROWS, POOL_WIDTH), lambda b, t: (b, 0, 0))],
        out_shape=[jax.ShapeDtypeStruct((ntok, POOL_WIDTH), BF16), jax.ShapeDtypeStruct((ntok, D_MODEL), BF16),
                   jax.ShapeDtypeStruct((D_MODEL, ntok), BF16), jax.ShapeDtypeStruct((ntok, D_MODEL), BF16),
                   jax.ShapeDtypeStruct((ntok, GATE_LANES), F32), jax.ShapeDtypeStruct((GATE_LANES, ntok), F32),
                   jax.ShapeDtypeStruct((nbatch, HIST_ROWS, POOL_WIDTH), F32)],
        scratch_shapes=[pltpu.VMEM((tm + HIST_ROWS, POOL_WIDTH), F32)],
        compiler_params=_params(("arbitrary", "arbitrary")),
        name="inproj_prompt",
    )(x, sc, sh, g_mix, w_in_r, w_kT, bgate, w_pool, s_pool)


def _inproj_sample(x, sc, sh, g_mix, w_in_r, w_kT, bgate, w_pool, s_pool, hist_pad):
    nseq, tt, _ = x.shape
    nb = TOKEN_TILE // tt
    tm = nb * tt
    ntok = nseq * tt
    row = lambda i: (i, 0)
    col = lambda i: (0, i)
    mod = pl.BlockSpec((nb, 1, D_MODEL), lambda i: (i, 0, 0))
    hist = pl.BlockSpec((nb, HIST_ROWS, POOL_WIDTH), lambda i: (i, 0, 0))
    return pl.pallas_call(
        functools.partial(_inproj_sample_kernel, pos0=PAST_LEN),
        grid=(nseq // nb,),
        in_specs=[pl.BlockSpec((nb, tt, D_MODEL), lambda i: (i, 0, 0)), mod, mod,
                  _const_spec((1, D_MODEL)), _const_spec((D_MODEL, _W_IN_COLS)), _const_spec((D_MODEL, D_MODEL)),
                  _const_spec((1, GATE_LANES)), _const_spec((4, POOL_GROUP_DIM, POOL_GROUP_DIM)),
                  _const_spec((1, POOL_WIDTH)), hist],
        out_specs=[pl.BlockSpec((tm, POOL_WIDTH), row), pl.BlockSpec((tm, D_MODEL), row),
                   pl.BlockSpec((D_MODEL, tm), col), pl.BlockSpec((tm, D_MODEL), row),
                   pl.BlockSpec((tm, GATE_LANES), row), pl.BlockSpec((GATE_LANES, tm), col),
                   pl.BlockSpec((tm, D_MODEL), row), hist],
        out_shape=[jax.ShapeDtypeStruct((ntok, POOL_WIDTH), BF16), jax.ShapeDtypeStruct((ntok, D_MODEL), BF16),
                   jax.ShapeDtypeStruct((D_MODEL, ntok), BF16), jax.ShapeDtypeStruct((ntok, D_MODEL), BF16),
                   jax.ShapeDtypeStruct((ntok, GATE_LANES), F32), jax.ShapeDtypeStruct((GATE_LANES, ntok), F32),
                   jax.ShapeDtypeStruct((ntok, D_MODEL), BF16),
                   jax.ShapeDtypeStruct((nseq, HIST_ROWS, POOL_WIDTH), F32)],
        compiler_params=_params(("arbitrary",)),
        name="inproj_sample",
    )(x, sc, sh, g_mix, w_in_r, w_kT, bgate, w_pool, s_pool, hist_pad)


def _mlstm_core(q, kT, v, b_c, b_r, ig_r, mask, m_prev, num_inter, qn):
    logd = jnp.where(mask, (b_c - b_r) + ig_r, -jnp.inf)
    a_c = b_c + m_prev
    m_t = jnp.maximum(a_c, jnp.max(logd, axis=-1, keepdims=True))
    w_intra = jnp.exp(logd - m_t)
    w_inter = jnp.exp(a_c - m_t)
    s = _dot(q, kT) * w_intra
    num = w_inter * num_inter + _dot(s.astype(BF16), v)
    den = w_inter * qn + jnp.sum(s, axis=-1, keepdims=True)
    hh = num / jnp.maximum(jnp.abs(den), jnp.exp(-m_t))
    return hh, m_t, a_c


def _mlstm_prompt_kernel(q_ref, kT_ref, v_ref, gc_ref, gr_ref, hh_ref, cout_ref, nout_ref, mout_ref,
                         cn_ref, m_ref, *, chunk):
    c = pl.program_id(1)

    @pl.when(c == 0)
    def _():
        cn_ref[...] = jnp.zeros(cn_ref.shape, F32)
        m_ref[...] = jnp.full(m_ref.shape, M_INIT, F32)

    gc = gc_ref[...]
    gr = gr_ref[...]
    ri = lax.broadcasted_iota(jnp.int32, (chunk, chunk), 0)
    ci = lax.broadcasted_iota(jnp.int32, (chunk, chunk), 1)
    causal = ri >= ci
    bcol = _cumsum_cols(causal.astype(BF16), gc)
    brow = _cumsum_rows(gr, (ri <= ci).astype(BF16))
    for h in range(ML_HEADS):
        hs = slice(h * ML_HEAD_DIM, (h + 1) * ML_HEAD_DIM)
        ig_c = gc[:, h:h + 1]
        b_c = bcol[:, ML_HEADS + h:ML_HEADS + h + 1]
        ig_r = gr[h:h + 1, :]
        b_r = brow[ML_HEADS + h:ML_HEADS + h + 1, :]
        m_prev = m_ref[h:h + 1, 0:1]
        cn = cn_ref[h]
        q = q_ref[:, hs]
        kT = kT_ref[hs, :]
        v = v_ref[:, hs]
        qc = _dot(q, cn.astype(BF16))
        hh, m_t, a_c = _mlstm_core(q, kT, v, b_c, b_r, ig_r, causal, m_prev,
                                   qc[:, :ML_HEAD_DIM], qc[:, ML_HEAD_DIM:ML_HEAD_DIM + 1])
        hh_ref[:, hs] = hh
        m_new = m_t[chunk - 1:chunk, :]
        decay = jnp.exp(a_c[chunk - 1:chunk, :] - m_new)
        w_end = jnp.exp((b_c[chunk - 1:chunk, :] - b_c) + ig_c - m_new)
        vp = jnp.concatenate([w_end * v.astype(F32), jnp.broadcast_to(w_end, (chunk, LANES))], axis=-1)
        cn_ref[h] = decay * cn + _dot(kT, vp.astype(BF16))
        m_ref[h:h + 1, :] = jnp.broadcast_to(m_new, (1, LANES))

    @pl.when(c == pl.num_programs(1) - 1)
    def _():
        for h in range(ML_HEADS):
            cout_ref[0, h] = cn_ref[h, :, 0:ML_HEAD_DIM]
            nout_ref[0, h] = cn_ref[h, :, ML_HEAD_DIM:AUG]
        mout_ref[0] = m_ref[...]


def _mlstm_prompt(q, kT, v, gc, gr, nbatch, seq):
    L = PROMPT_CHUNK
    nc = seq // L
    row = lambda b, c: (b * nc + c, 0)
    col = lambda b, c: (0, b * nc + c)
    return pl.pallas_call(
        functools.partial(_mlstm_prompt_kernel, chunk=L),
        grid=(nbatch, nc),
        in_specs=[pl.BlockSpec((L, D_MODEL), row), pl.BlockSpec((D_MODEL, L), col), pl.BlockSpec((L, D_MODEL), row),
                  pl.BlockSpec((L, GATE_LANES), row), pl.BlockSpec((GATE_LANES, L), col)],
        out_specs=[pl.BlockSpec((L, D_MODEL), row),
                   pl.BlockSpec((1, ML_HEADS, ML_HEAD_DIM, ML_HEAD_DIM), lambda b, c: (b, 0, 0, 0)),
                   pl.BlockSpec((1, ML_HEADS, ML_HEAD_DIM, LANES), lambda b, c: (b, 0, 0, 0)),
                   pl.BlockSpec((1, SUBLANES, LANES), lambda b, c: (b, 0, 0))],
        out_shape=[jax.ShapeDtypeStruct((nbatch * seq, D_MODEL), F32),
                   jax.ShapeDtypeStruct((nbatch, ML_HEADS, ML_HEAD_DIM, ML_HEAD_DIM), F32),
                   jax.ShapeDtypeStruct((nbatch, ML_HEADS, ML_HEAD_DIM, LANES), F32),
                   jax.ShapeDtypeStruct((nbatch, SUBLANES, LANES), F32)],
        scratch_shapes=[pltpu.VMEM((ML_HEADS, ML_HEAD_DIM, AUG), F32), pltpu.VMEM((SUBLANES, LANES), F32)],
        compiler_params=_params(("arbitrary", "arbitrary")),
        name="mlstm_prompt",
    )(q, kT, v, gc, gr)


def _last_in_group(x, group):
    rows = x.shape[0]
    x3 = jnp.broadcast_to(x, (rows, LANES)).reshape(rows // group, group, LANES)
    last = jnp.broadcast_to(x3[:, group - 1:group, :], x3.shape)
    return last.reshape(rows, LANES)[:, 0:1]


def _mlstm_sample_kernel(q_ref, kT_ref, k_ref, v_ref, gc_ref, gr_ref, m0_ref, c_ref, n_ref,
                         hh_ref, cout_ref, nout_ref, mt_ref, ni_ref, qn_ref, dec_ref, wk_ref, *, tt):
    h = pl.program_id(1)
    L = q_ref.shape[0]
    nseq = L // tt
    gc = gc_ref[...]
    gr = gr_ref[...]
    ri = lax.broadcasted_iota(jnp.int32, (L, L), 0)
    ci = lax.broadcasted_iota(jnp.int32, (L, L), 1)
    same = (ri // tt) == (ci // tt)
    mask = same & (ri >= ci)
    bcol = _cumsum_cols(mask.astype(BF16), gc)
    brow = _cumsum_rows(gr, (same & (ri <= ci)).astype(BF16))
    lane = lax.broadcasted_iota(jnp.int32, (L, GATE_LANES), 1)
    sub = lax.broadcasted_iota(jnp.int32, (GATE_LANES, L), 0)
    pick_col = lambda arr, idx: jnp.sum(jnp.where(lane == idx, arr, 0.0), axis=-1, keepdims=True)
    pick_row = lambda arr, idx: jnp.sum(jnp.where(sub == idx, arr, 0.0), axis=0, keepdims=True)
    ig_c = pick_col(gc, h)
    b_c = pick_col(bcol, ML_HEADS + h)
    ig_r = pick_row(gr, h)
    b_r = pick_row(brow, ML_HEADS + h)
    m_prev = m0_ref[0]

    def inter(j, carry):
        r0 = pl.multiple_of(j * tt, tt)
        qj = q_ref[pl.ds(r0, tt), :]
        ni_ref[pl.ds(r0, tt), :] = _dot(qj, c_ref[j, 0].astype(BF16))
        nj = n_ref[j, 0].astype(BF16).astype(F32)
        qn = jnp.sum(qj.astype(F32) * nj, axis=-1, keepdims=True)
        qn_ref[pl.ds(r0, tt), :] = jnp.broadcast_to(qn, (tt, LANES))
        return carry

    lax.fori_loop(0, nseq, inter, 0)

    q = q_ref[...]
    kT = kT_ref[...]
    v = v_ref[...]
    hh, m_t, a_c = _mlstm_core(q, kT, v, b_c, b_r, ig_r, mask, m_prev, ni_ref[...], qn_ref[:, 0:1])
    hh_ref[...] = hh
    m_new = _last_in_group(m_t, tt)
    decay = jnp.exp(_last_in_group(a_c, tt) - m_new)
    w_end = jnp.exp((_last_in_group(b_c, tt) - b_c) + ig_c - m_new)
    mt_ref[0] = jnp.broadcast_to(m_t, (L, LANES))
    dec_ref[...] = jnp.broadcast_to(decay, (L, LANES))
    wv = w_end * v.astype(F32)
    wk_ref[...] = w_end.astype(BF16).astype(F32) * k_ref[...].astype(F32)
    rowi = lax.broadcasted_iota(jnp.int32, (L, 1), 0)

    def update(j, carry):
        r0 = pl.multiple_of(j * tt, tt)
        upd = _dot(kT, jnp.where((rowi // tt) == j, wv, 0.0).astype(BF16))
        dj = dec_ref[pl.ds(r0, 1), :][:, 0:1]
        cout_ref[j, 0] = dj * c_ref[j, 0] + upd
        nout_ref[j, 0] = dj * n_ref[j, 0] + jnp.sum(wk_ref[pl.ds(r0, tt), :], axis=0, keepdims=True)
        return carry

    lax.fori_loop(0, nseq, update, 0)


def _mlstm_sample(q, kT, k, v, gc, gr, m0_tok, c0, n0, tt):
    ntok = q.shape[0]
    nseq = ntok // tt
    sb = SAMPLE_SEQ_BLOCK
    L = sb * tt
    qspec = pl.BlockSpec((L, ML_HEAD_DIM), lambda i, h: (i, h))
    cspec = pl.BlockSpec((sb, 1, ML_HEAD_DIM, ML_HEAD_DIM), lambda i, h: (i, h, 0, 0))
    nspec = pl.BlockSpec((sb, 1, 1, ML_HEAD_DIM), lambda i, h: (i, h, 0, 0))
    return pl.pallas_call(
        functools.partial(_mlstm_sample_kernel, tt=tt),
        grid=(nseq // sb, ML_HEADS),
        in_specs=[qspec, pl.BlockSpec((ML_HEAD_DIM, L), lambda i, h: (h, i)), qspec, qspec,
                  pl.BlockSpec((L, GATE_LANES), lambda i, h: (i, 0)),
                  pl.BlockSpec((GATE_LANES, L), lambda i, h: (0, i)),
                  pl.BlockSpec((1, L, 1), lambda i, h: (h, i, 0)), cspec, nspec],
        out_specs=[qspec, cspec, nspec, pl.BlockSpec((1, L, LANES), lambda i, h: (h, i, 0))],
        out_shape=[jax.ShapeDtypeStruct((ntok, D_MODEL), F32),
                   jax.ShapeDtypeStruct((nseq, ML_HEADS, ML_HEAD_DIM, ML_HEAD_DIM), F32),
                   jax.ShapeDtypeStruct((nseq, ML_HEADS, 1, ML_HEAD_DIM), F32),
                   jax.ShapeDtypeStruct((ML_HEADS, ntok, LANES), F32)],
        scratch_shapes=[pltpu.VMEM((L, ML_HEAD_DIM), F32), pltpu.VMEM((L, LANES), F32),
                        pltpu.VMEM((L, LANES), F32), pltpu.VMEM((L, ML_HEAD_DIM), F32)],
        compiler_params=_params(("arbitrary", "arbitrary")),
        name="mlstm_sample",
    )(q, kT, k, v, gc, gr, m0_tok, c0, n0)


def _post_kernel(x_ref, sc_ref, sh_ref, gt_ref, g_ref, hh_ref, yp_ref, w_ref, ghead_ref, wbp_ref, wbm_ref, wout_ref,
                 o_ref):
    x, hmod = _norm_mod(x_ref, sc_ref, sh_ref, g_ref)
    hb = hmod.astype(BF16)
    nb, tt, d = x_ref.shape
    o = _dot(hb, w_ref[:, 0:D_MODEL])
    parts = []
    for h in range(ML_HEADS):
        hh = hh_ref[:, h * ML_HEAD_DIM:(h + 1) * ML_HEAD_DIM]
        parts.append(hh * lax.rsqrt(jnp.mean(hh * hh, axis=-1, keepdims=True) + EPS))
    yml = (jnp.concatenate(parts, axis=-1) * ghead_ref[...]) * jax.nn.sigmoid(o)
    gp = _dot(hb, w_ref[:, D_MODEL:2 * D_MODEL])
    gm = _dot(hb, w_ref[:, 2 * D_MODEL:3 * D_MODEL])
    merged = (jax.nn.sigmoid(gp) * _dot(yp_ref[...], wbp_ref[...])
              + jax.nn.sigmoid(gm) * _dot(yml.astype(BF16), wbm_ref[...]))
    y = _dot(merged.astype(BF16), wout_ref[...]).reshape(nb, tt, d)
    o_ref[...] = x_ref[...] + gt_ref[...] * y


def _tile_blocks(x):
    g, t, _ = x.shape
    if t >= TOKEN_TILE:
        nb, tt = 1, TOKEN_TILE
    else:
        nb, tt = TOKEN_TILE // t, t
    return nb, tt, (g // nb) * (t // tt), t // tt


def _post(x, sc, sh, gt, g_mix, hh, yp, w_post, g_head, w_bp, w_bm, w_out):
    nb, tt, steps, per = _tile_blocks(x)
    tm = nb * tt
    xspec = pl.BlockSpec((nb, tt, D_MODEL), lambda i: (i // per, i % per, 0))
    mod = pl.BlockSpec((nb, 1, D_MODEL), lambda i: (i // per, 0, 0))
    row = lambda i: (i, 0)
    return pl.pallas_call(
        _post_kernel,
        grid=(steps,),
        in_specs=[xspec, mod, mod, mod, _const_spec((1, D_MODEL)),
                  pl.BlockSpec((tm, D_MODEL), row), pl.BlockSpec((tm, POOL_WIDTH), row),
                  _const_spec((D_MODEL, 3 * D_MODEL)), _const_spec((1, D_MODEL)),
                  _const_spec((POOL_WIDTH, D_MODEL)), _const_spec((D_MODEL, D_MODEL)),
                  _const_spec((D_MODEL, D_MODEL))],
        out_specs=xspec,
        out_shape=jax.ShapeDtypeStruct(x.shape, F32),
        compiler_params=_params(("arbitrary",)),
        name="post",
    )(x, sc, sh, gt, g_mix, hh, yp, w_post, g_head, w_bp, w_bm, w_out)


_FF_SPLITS = ((0, 1536), (1536, D_FF))


def _ffn_kernel(x_ref, sc_ref, sh_ref, gt_ref, g_ref, gfin_ref, wgu_ref, wdn_ref, o_ref):
    _, hmod = _norm_mod(x_ref, sc_ref, sh_ref, g_ref)
    hb = hmod.astype(BF16)
    nb, tt, d = x_ref.shape
    dn = None
    for lo, hi in _FF_SPLITS:
        gate = _dot(hb, wgu_ref[:, lo:hi])
        up = _dot(hb, wgu_ref[:, D_FF + lo:D_FF + hi])
        act = (gate * jax.nn.sigmoid(gate) * up).astype(BF16)
        part = _dot(act, wdn_ref[lo:hi, :])
        dn = part if dn is None else dn + part
    x2 = x_ref[...] + gt_ref[...] * dn.reshape(nb, tt, d)
    ms = jnp.mean(x2 * x2, axis=-1, keepdims=True)
    o_ref[...] = x2 * lax.rsqrt(ms + EPS) * gfin_ref[...]


def _ffn(x, sc, sh, gt, g_ffn, g_final, w_gu, w_down):
    nb, tt, steps, per = _tile_blocks(x)
    xspec = pl.BlockSpec((nb, tt, D_MODEL), lambda i: (i // per, i % per, 0))
    mod = pl.BlockSpec((nb, 1, D_MODEL), lambda i: (i // per, 0, 0))
    return pl.pallas_call(
        _ffn_kernel,
        grid=(steps,),
        in_specs=[xspec, mod, mod, mod, _const_spec((1, D_MODEL)), _const_spec((1, D_MODEL)),
                  _const_spec((D_MODEL, 2 * D_FF)), _const_spec((D_FF, D_MODEL))],
        out_specs=xspec,
        out_shape=jax.ShapeDtypeStruct(x.shape, F32),
        compiler_params=_params(("arbitrary",)),
        name="ffn",
    )(x, sc, sh, gt, g_ffn, g_final, w_gu, w_down)


def kernel(x_prompt, x_sample, c_prompt, c_sample, state_pool, state_mlstm_c, state_mlstm_n, state_mlstm_m, g_mix, g_ffn, g_final, w_ada, b_ada, w_in, b_igate, b_fgate, w_pool, s_pool, g_head, w_branch_pool, w_branch_mlstm, w_out, w_gate_up, w_down):
    depth = w_in.shape[0]
    assert depth == 1, "single-layer trunk"
    nbatch, seq, _ = x_prompt.shape
    nseq, tt, _ = x_sample.shape
    l = 0

    wi = w_in[l]
    o0 = _G0
    g0 = o0 + D_MODEL
    p0 = g0 + 2 * ML_HEADS
    gate_w = jnp.pad(wi[:, g0:p0], ((0, 0), (0, GATE_LANES - 2 * ML_HEADS)))
    w_in_r = jnp.concatenate([wi[:, :_G0], gate_w], axis=1).astype(BF16)
    w_kT = wi[:, _K0:_V0].T.astype(BF16)
    w_post = jnp.concatenate([wi[:, o0:g0], wi[:, p0:]], axis=1).astype(BF16)
    bgate = jnp.pad(jnp.concatenate([b_igate[l], b_fgate[l]])[None, :], ((0, 0), (0, GATE_LANES - 2 * ML_HEADS)))
    w_pool_b = w_pool[l].astype(BF16)
    w_bp = w_branch_pool[l].astype(BF16)
    w_bm = w_branch_mlstm[l].astype(BF16)
    w_o = w_out[l].astype(BF16)
    w_gu = w_gate_up[l].astype(BF16)
    w_dn = w_down[l].astype(BF16)
    g_mix_r, g_ffn_r, g_fin_r = g_mix[l][None, :], g_ffn[l][None, :], g_final[None, :]
    s_pool_r, g_head_r = s_pool[l][None, :], g_head[l][None, :]

    mod = _ada(jnp.concatenate([c_prompt, c_sample], axis=0), w_ada[l], b_ada[l][None, :])
    mod_p = [m[:, None, :] for m in jnp.split(mod[:nbatch], 6, axis=-1)]
    mod_s = [m[:, None, :] for m in jnp.split(mod[nbatch:], 6, axis=-1)]

    sh1, sc1, gt1, sh2, sc2, gt2 = mod_p
    yp, q, kT, v, gc, gr, hist_p = _inproj_prompt(x_prompt, sc1, sh1, g_mix_r, w_in_r, w_kT, bgate, w_pool_b, s_pool_r)
    hh, c_p, n_p, m_p = _mlstm_prompt(q, kT, v, gc, gr, nbatch, seq)
    x1 = _post(x_prompt, sc1, sh1, gt1, g_mix_r, hh, yp, w_post, g_head_r, w_bp, w_bm, w_o)
    y_prompt = _ffn(x1, sc2, sh2, gt2, g_ffn_r, g_fin_r, w_gu, w_dn)

    sh1, sc1, gt1, sh2, sc2, gt2 = mod_s
    hist_pad = jnp.pad(state_pool[l], ((0, 0), (HIST_ROWS - POOL_HIST, 0), (0, 0)))
    yp, q, kT, v, gc, gr, k, hist_s = _inproj_sample(x_sample, sc1, sh1, g_mix_r, w_in_r, w_kT, bgate, w_pool_b,
                                                     s_pool_r, hist_pad)
    m0_tok = jnp.repeat(state_mlstm_m[l].astype(F32).T, tt, axis=1)[:, :, None]
    hh, c_s, n_s, mt = _mlstm_sample(q, kT, k, v, gc, gr, m0_tok, state_mlstm_c[l].astype(F32),
                                     state_mlstm_n[l].astype(F32)[:, :, None, :], tt)
    x1 = _post(x_sample, sc1, sh1, gt1, g_mix_r, hh, yp, w_post, g_head_r, w_bp, w_bm, w_o)
    y_sample = _ffn(x1, sc2, sh2, gt2, g_ffn_r, g_fin_r, w_gu, w_dn)

    cd, nd, md = state_mlstm_c.dtype, state_mlstm_n.dtype, state_mlstm_m.dtype
    return (y_prompt, y_sample,
            hist_p[None, :, HIST_ROWS - POOL_HIST:, :],
            c_p.astype(cd)[None], n_p[..., 0].astype(nd)[None], m_p[:, :ML_HEADS, 0].astype(md)[None],
            hist_s[None, :, HIST_ROWS - POOL_HIST:, :].astype(state_pool.dtype),
            c_s.astype(cd)[None], n_s[:, :, 0, :].astype(nd)[None],
            mt[:, tt - 1::tt, 0].T.astype(md)[None])
```

```python
import functools

import jax
import jax.numpy as jnp
from jax import lax
from jax.experimental import pallas as pl
from jax.experimental.pallas import tpu as pltpu

D_MODEL = 1024
POOL_WINDOWS = (2, 4, 8, 16)
POOL_GROUP_DIM = 128
POOL_WIDTH = 512
POOL_HIST = 15
HIST_ROWS = 16
ML_HEADS = 4
ML_HEAD_DIM = 256
D_FF = 2816
EPS = 1e-6
M_INIT = -1e30
PAST_LEN = 16384
LANES = 128
SUBLANES = 8
GATE_LANES = LANES
AUG = ML_HEAD_DIM + LANES
VMEM_LIMIT = 56 * 1024 * 1024

TOKEN_TILE = 512
PROMPT_CHUNK = 256
SAMPLE_SEQ_BLOCK = 16

BF16 = jnp.bfloat16
F32 = jnp.float32

_U0, _Q0, _K0, _V0, _G0, _W_IN_COLS = 0, 512, 1536, 2560, 3584, 3712


def _dot(a, b):
    return jnp.dot(a, b, preferred_element_type=F32)


def _const_spec(shape):
    zeros = (0,) * len(shape)
    return pl.BlockSpec(shape, lambda *_: zeros, pipeline_mode=pl.Buffered(1))


def _params(sem):
    return pltpu.CompilerParams(dimension_semantics=sem, vmem_limit_bytes=VMEM_LIMIT)


def _norm_mod(x_ref, sc_ref, sh_ref, g_ref):
    x = x_ref[...]
    nb, tt, d = x.shape
    ms = jnp.mean(x * x, axis=-1, keepdims=True)
    y = x * lax.rsqrt(ms + EPS) * g_ref[...]
    hmod = y * (1.0 + sc_ref[...]) + sh_ref[...]
    return x.reshape(nb * tt, d), hmod.reshape(nb * tt, d)


def _split3(x):
    hi = x.astype(BF16)
    r1 = x - hi.astype(F32)
    mid = r1.astype(BF16)
    lo = (r1 - mid.astype(F32)).astype(BF16)
    return hi, mid, lo


def _cumsum_cols(tri, x):
    hi, mid, lo = _split3(x)
    return _dot(tri, hi) + _dot(tri, mid) + _dot(tri, lo)


def _cumsum_rows(x, tri):
    hi, mid, lo = _split3(x)
    return _dot(hi, tri) + _dot(mid, tri) + _dot(lo, tri)


def _ada_kernel(c_ref, w_ref, b_ref, o_ref):
    c = c_ref[...]
    a = (c * jax.nn.sigmoid(c)).astype(BF16)
    o_ref[...] = _dot(a, w_ref[...].astype(BF16)) + b_ref[...]


def _ada(c_all, w_ada, b_ada):
    rows = c_all.shape[0]
    ncol = w_ada.shape[1]
    tn = 1024
    return pl.pallas_call(
        _ada_kernel,
        grid=(ncol // tn,),
        in_specs=[pl.BlockSpec((rows, D_MODEL), lambda j: (0, 0)),
                  pl.BlockSpec((D_MODEL, tn), lambda j: (0, j)),
                  pl.BlockSpec((1, tn), lambda j: (0, j))],
        out_specs=pl.BlockSpec((rows, tn), lambda j: (0, j)),
        out_shape=jax.ShapeDtypeStruct((rows, ncol), F32),
        compiler_params=_params(("arbitrary",)),
        name="ada",
    )(c_all, w_ada, b_ada)


def _project(hb, w_ref, wkT_ref, bg_ref, q_ref, kT_ref, v_ref, gc_ref, gr_ref):
    q_ref[...] = _dot(hb, w_ref[:, _Q0:_K0]).astype(BF16)
    kT = lax.dot_general(wkT_ref[...], hb, (((1,), (1,)), ((), ())), preferred_element_type=F32)
    kT_ref[...] = (kT * (ML_HEAD_DIM ** -0.5)).astype(BF16)
    v_ref[...] = _dot(hb, w_ref[:, _V0:_G0]).astype(BF16)
    zg = _dot(hb, w_ref[:, _G0:_W_IN_COLS]) + bg_ref[...]
    log_f = jnp.minimum(zg, 0.0) - jnp.log1p(jnp.exp(-jnp.abs(zg)))
    lane = lax.broadcasted_iota(jnp.int32, zg.shape, 1)
    gc = jnp.where(lane < ML_HEADS, zg, log_f)
    gc_ref[...] = gc
    gr_ref[...] = gc.T


def _pool_project(acc_fn, u, cnt_fn, wpool_ref, spool_ref):
    outs = []
    for g, w in enumerate(POOL_WINDOWS):
        cols = slice(g * POOL_GROUP_DIM, (g + 1) * POOL_GROUP_DIM)
        ug = u[:, cols]
        d = acc_fn(g, w, ug) / cnt_fn(w) - ug
        outs.append(_dot(d.astype(BF16), wpool_ref[g]) * spool_ref[:, cols])
    return jnp.concatenate(outs, axis=-1).astype(BF16)


def _inproj_prompt_kernel(x_ref, sc_ref, sh_ref, g_ref, w_ref, wkT_ref, bg_ref, wpool_ref, spool_ref,
                          yp_ref, q_ref, kT_ref, v_ref, gc_ref, gr_ref, hout_ref, ext_ref, *, tm):
    t = pl.program_id(1)

    @pl.when(t == 0)
    def _():
        ext_ref[0:HIST_ROWS, :] = jnp.zeros((HIST_ROWS, POOL_WIDTH), F32)

    _, hmod = _norm_mod(x_ref, sc_ref, sh_ref, g_ref)
    hb = hmod.astype(BF16)
    _project(hb, w_ref, wkT_ref, bg_ref, q_ref, kT_ref, v_ref, gc_ref, gr_ref)

    u = _dot(hb, w_ref[:, _U0:_Q0])
    ext_ref[HIST_ROWS:HIST_ROWS + tm, :] = u
    pos = t * tm + lax.broadcasted_iota(jnp.int32, (tm, 1), 0)

    def acc_fn(g, w, ug):
        acc = ug
        for j in range(1, w):
            acc = acc + ext_ref[pl.ds(HIST_ROWS - j, tm), g * POOL_GROUP_DIM:(g + 1) * POOL_GROUP_DIM]
        return acc

    def cnt_fn(w):
        return jnp.minimum(pos + 1, w).astype(F32)

    yp_ref[...] = _pool_project(acc_fn, u, cnt_fn, wpool_ref, spool_ref)
    last = ext_ref[tm:tm + HIST_ROWS, :]
    hout_ref[0] = last
    ext_ref[0:HIST_ROWS, :] = last


def _inproj_sample_kernel(x_ref, sc_ref, sh_ref, g_ref, w_ref, wkT_ref, bg_ref, wpool_ref, spool_ref, hist_ref,
                          yp_ref, q_ref, kT_ref, v_ref, gc_ref, gr_ref, k_ref, hout_ref, *, pos0):
    _, hmod = _norm_mod(x_ref, sc_ref, sh_ref, g_ref)
    hb = hmod.astype(BF16)
    _project(hb, w_ref, wkT_ref, bg_ref, q_ref, kT_ref, v_ref, gc_ref, gr_ref)
    k_ref[...] = (_dot(hb, w_ref[:, _K0:_V0]) * (ML_HEAD_DIM ** -0.5)).astype(BF16)

    u = _dot(hb, w_ref[:, _U0:_Q0])
    nb, tt = x_ref.shape[0], x_ref.shape[1]
    u3 = u.reshape(nb, tt, POOL_WIDTH)
    h0 = hist_ref[:, 0:SUBLANES, :]
    h1 = hist_ref[:, SUBLANES:HIST_ROWS, :]
    tok = lax.broadcasted_iota(jnp.int32, (nb, tt, POOL_GROUP_DIM), 1)

    def acc_fn(g, w, ug):
        cols = slice(g * POOL_GROUP_DIM, (g + 1) * POOL_GROUP_DIM)
        new, mid, old = u3[:, :, cols], h1[:, :, cols], h0[:, :, cols]
        acc = new
        for j in range(1, w):
            if j < tt:
                term = jnp.where(tok >= j, pltpu.roll(new, j, 1), pltpu.roll(mid, j, 1))
            elif j == tt:
                term = mid
            else:
                term = jnp.where(tok >= j - tt, pltpu.roll(mid, j - tt, 1), pltpu.roll(old, j - tt, 1))
            acc = acc + term
        return acc.reshape(nb * tt, POOL_GROUP_DIM)

    pos = pos0 + lax.broadcasted_iota(jnp.int32, (nb, tt, 1), 1).reshape(nb * tt, 1)

    def cnt_fn(w):
        return jnp.minimum(pos + 1, w).astype(F32)

    yp_ref[...] = _pool_project(acc_fn, u, cnt_fn, wpool_ref, spool_ref)
    hout_ref[:, 0:SUBLANES, :] = h1
    hout_ref[:, SUBLANES:HIST_ROWS, :] = u3


def _inproj_prompt(x, sc, sh, g_mix, w_in_r, w_kT, bgate, w_pool, s_pool):
    nbatch, seq, _ = x.shape
    tm = TOKEN_TILE
    nt = seq // tm
    ntok = nbatch * seq
    row = lambda b, t: (b * nt + t, 0)
    col = lambda b, t: (0, b * nt + t)
    mod = pl.BlockSpec((1, 1, D_MODEL), lambda b, t: (b, 0, 0))
    return pl.pallas_call(
        functools.partial(_inproj_prompt_kernel, tm=tm),
        grid=(nbatch, nt),
        in_specs=[pl.BlockSpec((1, tm, D_MODEL), lambda b, t: (b, t, 0)), mod, mod,
                  _const_spec((1, D_MODEL)), _const_spec((D_MODEL, _W_IN_COLS)), _const_spec((D_MODEL, D_MODEL)),
                  _const_spec((1, GATE_LANES)), _const_spec((4, POOL_GROUP_DIM, POOL_GROUP_DIM)),
                  _const_spec((1, POOL_WIDTH))],
        out_specs=[pl.BlockSpec((tm, POOL_WIDTH), row), pl.BlockSpec((tm, D_MODEL), row),
                   pl.BlockSpec((D_MODEL, tm), col), pl.BlockSpec((tm, D_MODEL), row),
                   pl.BlockSpec((tm, GATE_LANES), row), pl.BlockSpec((GATE_LANES, tm), col),
                   pl.BlockSpec((1, HIST_ROWS, POOL_WIDTH), lambda b, t: (b, 0, 0))],
        out_shape=[jax.ShapeDtypeStruct((ntok, POOL_WIDTH), BF16), jax.ShapeDtypeStruct((ntok, D_MODEL), BF16),
                   jax.ShapeDtypeStruct((D_MODEL, ntok), BF16), jax.ShapeDtypeStruct((ntok, D_MODEL), BF16),
                   jax.ShapeDtypeStruct((ntok, GATE_LANES), F32), jax.ShapeDtypeStruct((GATE_LANES, ntok), F32),
                   jax.ShapeDtypeStruct((nbatch, HIST_ROWS, POOL_WIDTH), F32)],
        scratch_shapes=[pltpu.VMEM((tm + HIST_ROWS, POOL_WIDTH), F32)],
        compiler_params=_params(("arbitrary", "arbitrary")),
        name="inproj_prompt",
    )(x, sc, sh, g_mix, w_in_r, w_kT, bgate, w_pool, s_pool)


def _inproj_sample(x, sc, sh, g_mix, w_in_r, w_kT, bgate, w_pool, s_pool, hist_pad):
    nseq, tt, _ = x.shape
    nb = TOKEN_TILE // tt
    tm = nb * tt
    ntok = nseq * tt
    row = lambda i: (i, 0)
    col = lambda i: (0, i)
    mod = pl.BlockSpec((nb, 1, D_MODEL), lambda i: (i, 0, 0))
    hist = pl.BlockSpec((nb, HIST_ROWS, POOL_WIDTH), lambda i: (i, 0, 0))
    return pl.pallas_call(
        functools.partial(_inproj_sample_kernel, pos0=PAST_LEN),
        grid=(nseq // nb,),
        in_specs=[pl.BlockSpec((nb, tt, D_MODEL), lambda i: (i, 0, 0)), mod, mod,
                  _const_spec((1, D_MODEL)), _const_spec((D_MODEL, _W_IN_COLS)), _const_spec((D_MODEL, D_MODEL)),
                  _const_spec((1, GATE_LANES)), _const_spec((4, POOL_GROUP_DIM, POOL_GROUP_DIM)),
                  _const_spec((1, POOL_WIDTH)), hist],
        out_specs=[pl.BlockSpec((tm, POOL_WIDTH), row), pl.BlockSpec((tm, D_MODEL), row),
                   pl.BlockSpec((D_MODEL, tm), col), pl.BlockSpec((tm, D_MODEL), row),
                   pl.BlockSpec((tm, GATE_LANES), row), pl.BlockSpec((GATE_LANES, tm), col),
                   pl.BlockSpec((tm, D_MODEL), row), hist],
        out_shape=[jax.ShapeDtypeStruct((ntok, POOL_WIDTH), BF16), jax.ShapeDtypeStruct((ntok, D_MODEL), BF16),
                   jax.ShapeDtypeStruct((D_MODEL, ntok), BF16), jax.ShapeDtypeStruct((ntok, D_MODEL), BF16),
                   jax.ShapeDtypeStruct((ntok, GATE_LANES), F32), jax.ShapeDtypeStruct((GATE_LANES, ntok), F32),
                   jax.ShapeDtypeStruct((ntok, D_MODEL), BF16),
                   jax.ShapeDtypeStruct((nseq, HIST_ROWS, POOL_WIDTH), F32)],
        compiler_params=_params(("arbitrary",)),
        name="inproj_sample",
    )(x, sc, sh, g_mix, w_in_r, w_kT, bgate, w_pool, s_pool, hist_pad)


def _mlstm_core(q, kT, v, b_c, b_r, ig_r, mask, m_prev, num_inter, qn):
    logd = jnp.where(mask, (b_c - b_r) + ig_r, -jnp.inf)
    a_c = b_c + m_prev
    m_t = jnp.maximum(a_c, jnp.max(logd, axis=-1, keepdims=True))
    w_intra = jnp.exp(logd - m_t)
    w_inter = jnp.exp(a_c - m_t)
    s = _dot(q, kT) * w_intra
    num = w_inter * num_inter + _dot(s.astype(BF16), v)
    den = w_inter * qn + jnp.sum(s, axis=-1, keepdims=True)
    hh = num / jnp.maximum(jnp.abs(den), jnp.exp(-m_t))
    return hh, m_t, a_c


def _mlstm_prompt_kernel(q_ref, kT_ref, v_ref, gc_ref, gr_ref, hh_ref, cout_ref, nout_ref, mout_ref,
                         cn_ref, m_ref, *, chunk):
    c = pl.program_id(1)

    @pl.when(c == 0)
    def _():
        cn_ref[...] = jnp.zeros(cn_ref.shape, F32)
        m_ref[...] = jnp.full(m_ref.shape, M_INIT, F32)

    gr = gr_ref[...]
    ri = lax.broadcasted_iota(jnp.int32, (chunk, chunk), 0)
    ci = lax.broadcasted_iota(jnp.int32, (chunk, chunk), 1)
    causal = ri >= ci
    bcol = _cumsum_cols(causal.astype(BF16), gc_ref[...])
    brow = _cumsum_rows(gr, (ri <= ci).astype(BF16))
    ones = jnp.ones((chunk, LANES), BF16)
    wide = lambda x: jnp.concatenate([x, x], axis=-1)
    for h in range(ML_HEADS):
        hs = slice(h * ML_HEAD_DIM, (h + 1) * ML_HEAD_DIM)
        b_r = brow[ML_HEADS + h:ML_HEADS + h + 1, :]
        g_r = gr[h:h + 1, :] - b_r
        m_prev = m_ref[h:h + 1, 0:1]
        cn = cn_ref[h]
        q = q_ref[:, hs]
        kT = kT_ref[hs, :]
        vaug = jnp.concatenate([v_ref[:, hs], ones], axis=-1)
        gm = jnp.where(causal, g_r, -jnp.inf)
        big_m = jnp.maximum(m_prev, jnp.max(gm, axis=-1, keepdims=True))
        m_rep = jnp.broadcast_to(big_m, (chunk, LANES))
        mt_rep = jnp.broadcast_to(bcol[:, ML_HEADS + h:ML_HEADS + h + 1] + big_m, (chunk, LANES))
        s = _dot(q, kT) * jnp.exp(gm - wide(m_rep))
        sva = _dot(s.astype(BF16), vaug)
        qc = _dot(q, cn.astype(BF16))
        w_inter = jnp.exp(m_prev - m_rep)
        den = w_inter * qc[:, ML_HEAD_DIM:] + sva[:, ML_HEAD_DIM:]
        rinv = 1.0 / jnp.maximum(jnp.abs(den), jnp.exp(-mt_rep))
        hh_ref[:, hs] = (wide(w_inter) * qc[:, :ML_HEAD_DIM] + sva[:, :ML_HEAD_DIM]) * wide(rinv)

        b_last = b_r[:, chunk - 1:chunk]
        m_new = b_last + jnp.maximum(m_prev, jnp.max(g_r, axis=-1, keepdims=True))
        decay = jnp.exp((b_last + m_prev) - m_new)
        w_end = jnp.exp((g_r + b_last) - m_new)
        cn_ref[h] = decay * cn + _dot((kT.astype(F32) * w_end).astype(BF16), vaug)
        m_ref[h:h + 1, :] = jnp.broadcast_to(m_new, (1, LANES))

    @pl.when(c == pl.num_programs(1) - 1)
    def _():
        for h in range(ML_HEADS):
            cout_ref[0, h] = cn_ref[h, :, 0:ML_HEAD_DIM]
            nout_ref[0, h] = cn_ref[h, :, ML_HEAD_DIM:AUG]
        mout_ref[0] = m_ref[...]


def _mlstm_prompt(q, kT, v, gc, gr, nbatch, seq):
    L = PROMPT_CHUNK
    nc = seq // L
    row = lambda b, c: (b * nc + c, 0)
    col = lambda b, c: (0, b * nc + c)
    return pl.pallas_call(
        functools.partial(_mlstm_prompt_kernel, chunk=L),
        grid=(nbatch, nc),
        in_specs=[pl.BlockSpec((L, D_MODEL), row), pl.BlockSpec((D_MODEL, L), col), pl.BlockSpec((L, D_MODEL), row),
                  pl.BlockSpec((L, GATE_LANES), row), pl.BlockSpec((GATE_LANES, L), col)],
        out_specs=[pl.BlockSpec((L, D_MODEL), row),
                   pl.BlockSpec((1, ML_HEADS, ML_HEAD_DIM, ML_HEAD_DIM), lambda b, c: (b, 0, 0, 0)),
                   pl.BlockSpec((1, ML_HEADS, ML_HEAD_DIM, LANES), lambda b, c: (b, 0, 0, 0)),
                   pl.BlockSpec((1, SUBLANES, LANES), lambda b, c: (b, 0, 0))],
        out_shape=[jax.ShapeDtypeStruct((nbatch * seq, D_MODEL), F32),
                   jax.ShapeDtypeStruct((nbatch, ML_HEADS, ML_HEAD_DIM, ML_HEAD_DIM), F32),
                   jax.ShapeDtypeStruct((nbatch, ML_HEADS, ML_HEAD_DIM, LANES), F32),
                   jax.ShapeDtypeStruct((nbatch, SUBLANES, LANES), F32)],
        scratch_shapes=[pltpu.VMEM((ML_HEADS, ML_HEAD_DIM, AUG), F32), pltpu.VMEM((SUBLANES, LANES), F32)],
        compiler_params=_params(("arbitrary", "arbitrary")),
        name="mlstm_prompt",
    )(q, kT, v, gc, gr)


def _last_in_group(x, group):
    rows = x.shape[0]
    x3 = jnp.broadcast_to(x, (rows, LANES)).reshape(rows // group, group, LANES)
    last = jnp.broadcast_to(x3[:, group - 1:group, :], x3.shape)
    return last.reshape(rows, LANES)[:, 0:1]


def _mlstm_sample_kernel(q_ref, kT_ref, k_ref, v_ref, gc_ref, gr_ref, m0_ref, c_ref, n_ref,
                         hh_ref, cout_ref, nout_ref, mt_ref, ni_ref, qn_ref, dec_ref, wk_ref, *, tt):
    h = pl.program_id(1)
    L = q_ref.shape[0]
    nseq = L // tt
    gc = gc_ref[...]
    gr = gr_ref[...]
    ri = lax.broadcasted_iota(jnp.int32, (L, L), 0)
    ci = lax.broadcasted_iota(jnp.int32, (L, L), 1)
    same = (ri // tt) == (ci // tt)
    mask = same & (ri >= ci)
    bcol = _cumsum_cols(mask.astype(BF16), gc)
    brow = _cumsum_rows(gr, (same & (ri <= ci)).astype(BF16))
    lane = lax.broadcasted_iota(jnp.int32, (L, GATE_LANES), 1)
    sub = lax.broadcasted_iota(jnp.int32, (GATE_LANES, L), 0)
    pick_col = lambda arr, idx: jnp.sum(jnp.where(lane == idx, arr, 0.0), axis=-1, keepdims=True)
    pick_row = lambda arr, idx: jnp.sum(jnp.where(sub == idx, arr, 0.0), axis=0, keepdims=True)
    ig_c = pick_col(gc, h)
    b_c = pick_col(bcol, ML_HEADS + h)
    ig_r = pick_row(gr, h)
    b_r = pick_row(brow, ML_HEADS + h)
    m_prev = m0_ref[0]

    for j in range(nseq):
        rows = slice(j * tt, (j + 1) * tt)
        qj = q_ref[rows, :]
        ni_ref[rows, :] = _dot(qj, c_ref[j, 0].astype(BF16))
        nj = n_ref[j, 0].astype(BF16).astype(F32)
        qn = jnp.sum(qj.astype(F32) * nj, axis=-1, keepdims=True)
        qn_ref[rows, :] = jnp.broadcast_to(qn, (tt, LANES))

    q = q_ref[...]
    kT = kT_ref[...]
    v = v_ref[...]
    hh, m_t, a_c = _mlstm_core(q, kT, v, b_c, b_r, ig_r, mask, m_prev, ni_ref[...], qn_ref[:, 0:1])
    hh_ref[...] = hh
    m_new = _last_in_group(m_t, tt)
    decay = jnp.exp(_last_in_group(a_c, tt) - m_new)
    w_end = jnp.exp((_last_in_group(b_c, tt) - b_c) + ig_c - m_new)
    mt_ref[0] = jnp.broadcast_to(m_t, (L, LANES))
    dec_ref[...] = jnp.broadcast_to(decay, (L, LANES))
    wv = w_end * v.astype(F32)
    wk_ref[...] = w_end.astype(BF16).astype(F32) * k_ref[...].astype(F32)
    rowi = lax.broadcasted_iota(jnp.int32, (L, 1), 0)

    for j in range(nseq):
        rows = slice(j * tt, (j + 1) * tt)
        upd = _dot(kT, jnp.where((rowi // tt) == j, wv, 0.0).astype(BF16))
        dj = dec_ref[j * tt:j * tt + 1, 0:1]
        cout_ref[j, 0] = dj * c_ref[j, 0] + upd
        nout_ref[j, 0] = dj * n_ref[j, 0] + jnp.sum(wk_ref[rows, :], axis=0, keepdims=True)


def _mlstm_sample(q, kT, k, v, gc, gr, m0_tok, c0, n0, tt):
    ntok = q.shape[0]
    nseq = ntok // tt
    sb = SAMPLE_SEQ_BLOCK
    L = sb * tt
    qspec = pl.BlockSpec((L, ML_HEAD_DIM), lambda i, h: (i, h))
    cspec = pl.BlockSpec((sb, 1, ML_HEAD_DIM, ML_HEAD_DIM), lambda i, h: (i, h, 0, 0))
    nspec = pl.BlockSpec((sb, 1, 1, ML_HEAD_DIM), lambda i, h: (i, h, 0, 0))
    return pl.pallas_call(
        functools.partial(_mlstm_sample_kernel, tt=tt),
        grid=(nseq // sb, ML_HEADS),
        in_specs=[qspec, pl.BlockSpec((ML_HEAD_DIM, L), lambda i, h: (h, i)), qspec, qspec,
                  pl.BlockSpec((L, GATE_LANES), lambda i, h: (i, 0)),
                  pl.BlockSpec((GATE_LANES, L), lambda i, h: (0, i)),
                  pl.BlockSpec((1, L, 1), lambda i, h: (h, i, 0)), cspec, nspec],
        out_specs=[qspec, cspec, nspec, pl.BlockSpec((1, L, LANES), lambda i, h: (h, i, 0))],
        out_shape=[jax.ShapeDtypeStruct((ntok, D_MODEL), F32),
                   jax.ShapeDtypeStruct((nseq, ML_HEADS, ML_HEAD_DIM, ML_HEAD_DIM), F32),
                   jax.ShapeDtypeStruct((nseq, ML_HEADS, 1, ML_HEAD_DIM), F32),
                   jax.ShapeDtypeStruct((ML_HEADS, ntok, LANES), F32)],
        scratch_shapes=[pltpu.VMEM((L, ML_HEAD_DIM), F32), pltpu.VMEM((L, LANES), F32),
                        pltpu.VMEM((L, LANES), F32), pltpu.VMEM((L, ML_HEAD_DIM), F32)],
        compiler_params=_params(("arbitrary", "arbitrary")),
        name="mlstm_sample",
    )(q, kT, k, v, gc, gr, m0_tok, c0, n0)


def _post_kernel(x_ref, sc_ref, sh_ref, gt_ref, g_ref, hh_ref, yp_ref, w_ref, ghead_ref, wbp_ref, wbm_ref, wout_ref,
                 o_ref):
    x, hmod = _norm_mod(x_ref, sc_ref, sh_ref, g_ref)
    hb = hmod.astype(BF16)
    nb, tt, d = x_ref.shape
    o = _dot(hb, w_ref[:, 0:D_MODEL])
    parts = []
    for h in range(ML_HEADS):
        hh = hh_ref[:, h * ML_HEAD_DIM:(h + 1) * ML_HEAD_DIM]
        parts.append(hh * lax.rsqrt(jnp.mean(hh * hh, axis=-1, keepdims=True) + EPS))
    yml = (jnp.concatenate(parts, axis=-1) * ghead_ref[...]) * jax.nn.sigmoid(o)
    gp = _dot(hb, w_ref[:, D_MODEL:2 * D_MODEL])
    gm = _dot(hb, w_ref[:, 2 * D_MODEL:3 * D_MODEL])
    merged = (jax.nn.sigmoid(gp) * _dot(yp_ref[...], wbp_ref[...])
              + jax.nn.sigmoid(gm) * _dot(yml.astype(BF16), wbm_ref[...]))
    y = _dot(merged.astype(BF16), wout_ref[...]).reshape(nb, tt, d)
    o_ref[...] = x_ref[...] + gt_ref[...] * y


def _tile_blocks(x):
    g, t, _ = x.shape
    if t >= TOKEN_TILE:
        nb, tt = 1, TOKEN_TILE
    else:
        nb, tt = TOKEN_TILE // t, t
    return nb, tt, (g // nb) * (t // tt), t // tt


def _post(x, sc, sh, gt, g_mix, hh, yp, w_post, g_head, w_bp, w_bm, w_out):
    nb, tt, steps, per = _tile_blocks(x)
    tm = nb * tt
    xspec = pl.BlockSpec((nb, tt, D_MODEL), lambda i: (i // per, i % per, 0))
    mod = pl.BlockSpec((nb, 1, D_MODEL), lambda i: (i // per, 0, 0))
    row = lambda i: (i, 0)
    return pl.pallas_call(
        _post_kernel,
        grid=(steps,),
        in_specs=[xspec, mod, mod, mod, _const_spec((1, D_MODEL)),
                  pl.BlockSpec((tm, D_MODEL), row), pl.BlockSpec((tm, POOL_WIDTH), row),
                  _const_spec((D_MODEL, 3 * D_MODEL)), _const_spec((1, D_MODEL)),
                  _const_spec((POOL_WIDTH, D_MODEL)), _const_spec((D_MODEL, D_MODEL)),
                  _const_spec((D_MODEL, D_MODEL))],
        out_specs=xspec,
        out_shape=jax.ShapeDtypeStruct(x.shape, F32),
        compiler_params=_params(("arbitrary",)),
        name="post",
    )(x, sc, sh, gt, g_mix, hh, yp, w_post, g_head, w_bp, w_bm, w_out)


_FF_SPLITS = ((0, 1536), (1536, D_FF))


def _ffn_kernel(x_ref, sc_ref, sh_ref, gt_ref, g_ref, gfin_ref, wgu_ref, wdn_ref, o_ref):
    _, hmod = _norm_mod(x_ref, sc_ref, sh_ref, g_ref)
    hb = hmod.astype(BF16)
    nb, tt, d = x_ref.shape
    dn = None
    for lo, hi in _FF_SPLITS:
        gate = _dot(hb, wgu_ref[:, lo:hi])
        up = _dot(hb, wgu_ref[:, D_FF + lo:D_FF + hi])
        act = (gate * jax.nn.sigmoid(gate) * up).astype(BF16)
        part = _dot(act, wdn_ref[lo:hi, :])
        dn = part if dn is None else dn + part
    x2 = x_ref[...] + gt_ref[...] * dn.reshape(nb, tt, d)
    ms = jnp.mean(x2 * x2, axis=-1, keepdims=True)
    o_ref[...] = x2 * lax.rsqrt(ms + EPS) * gfin_ref[...]


def _ffn(x, sc, sh, gt, g_ffn, g_final, w_gu, w_down):
    nb, tt, steps, per = _tile_blocks(x)
    xspec = pl.BlockSpec((nb, tt, D_MODEL), lambda i: (i // per, i % per, 0))
    mod = pl.BlockSpec((nb, 1, D_MODEL), lambda i: (i // per, 0, 0))
    return pl.pallas_call(
        _ffn_kernel,
        grid=(steps,),
        in_specs=[xspec, mod, mod, mod, _const_spec((1, D_MODEL)), _const_spec((1, D_MODEL)),
                  _const_spec((D_MODEL, 2 * D_FF)), _const_spec((D_FF, D_MODEL))],
        out_specs=xspec,
        out_shape=jax.ShapeDtypeStruct(x.shape, F32),
        compiler_params=_params(("arbitrary",)),
        name="ffn",
    )(x, sc, sh, gt, g_ffn, g_final, w_gu, w_down)


def kernel(x_prompt, x_sample, c_prompt, c_sample, state_pool, state_mlstm_c, state_mlstm_n, state_mlstm_m, g_mix, g_ffn, g_final, w_ada, b_ada, w_in, b_igate, b_fgate, w_pool, s_pool, g_head, w_branch_pool, w_branch_mlstm, w_out, w_gate_up, w_down):
    depth = w_in.shape[0]
    assert depth == 1, "single-layer trunk"
    nbatch, seq, _ = x_prompt.shape
    nseq, tt, _ = x_sample.shape
    l = 0

    wi = w_in[l]
    o0 = _G0
    g0 = o0 + D_MODEL
    p0 = g0 + 2 * ML_HEADS
    gate_w = jnp.pad(wi[:, g0:p0], ((0, 0), (0, GATE_LANES - 2 * ML_HEADS)))
    w_in_r = jnp.concatenate([wi[:, :_G0], gate_w], axis=1).astype(BF16)
    w_kT = wi[:, _K0:_V0].T.astype(BF16)
    w_post = jnp.concatenate([wi[:, o0:g0], wi[:, p0:]], axis=1).astype(BF16)
    bgate = jnp.pad(jnp.concatenate([b_igate[l], b_fgate[l]])[None, :], ((0, 0), (0, GATE_LANES - 2 * ML_HEADS)))
    w_pool_b = w_pool[l].astype(BF16)
    w_bp = w_branch_pool[l].astype(BF16)
    w_bm = w_branch_mlstm[l].astype(BF16)
    w_o = w_out[l].astype(BF16)
    w_gu = w_gate_up[l].astype(BF16)
    w_dn = w_down[l].astype(BF16)
    g_mix_r, g_ffn_r, g_fin_r = g_mix[l][None, :], g_ffn[l][None, :], g_final[None, :]
    s_pool_r, g_head_r = s_pool[l][None, :], g_head[l][None, :]

    mod = _ada(jnp.concatenate([c_prompt, c_sample], axis=0), w_ada[l], b_ada[l][None, :])
    mod_p = [m[:, None, :] for m in jnp.split(mod[:nbatch], 6, axis=-1)]
    mod_s = [m[:, None, :] for m in jnp.split(mod[nbatch:], 6, axis=-1)]

    sh1, sc1, gt1, sh2, sc2, gt2 = mod_p
    yp, q, kT, v, gc, gr, hist_p = _inproj_prompt(x_prompt, sc1, sh1, g_mix_r, w_in_r, w_kT, bgate, w_pool_b, s_pool_r)
    hh, c_p, n_p, m_p = _mlstm_prompt(q, kT, v, gc, gr, nbatch, seq)
    x1 = _post(x_prompt, sc1, sh1, gt1, g_mix_r, hh, yp, w_post, g_head_r, w_bp, w_bm, w_o)
    y_prompt = _ffn(x1, sc2, sh2, gt2, g_ffn_r, g_fin_r, w_gu, w_dn)

    sh1, sc1, gt1, sh2, sc2, gt2 = mod_s
    hist_pad = jnp.pad(state_pool[l], ((0, 0), (HIST_ROWS - POOL_HIST, 0), (0, 0)))
    yp, q, kT, v, gc, gr, k, hist_s = _inproj_sample(x_sample, sc1, sh1, g_mix_r, w_in_r, w_kT, bgate, w_pool_b,
                                                     s_pool_r, hist_pad)
    m0_tok = jnp.repeat(state_mlstm_m[l].astype(F32).T, tt, axis=1)[:, :, None]
    hh, c_s, n_s, mt = _mlstm_sample(q, kT, k, v, gc, gr, m0_tok, state_mlstm_c[l].astype(F32),
                                     state_mlstm_n[l].astype(F32)[:, :, None, :], tt)
    x1 = _post(x_sample, sc1, sh1, gt1, g_mix_r, hh, yp, w_post, g_head_r, w_bp, w_bm, w_o)
    y_sample = _ffn(x1, sc2, sh2, gt2, g_ffn_r, g_fin_r, w_gu, w_dn)

    cd, nd, md = state_mlstm_c.dtype, state_mlstm_n.dtype, state_mlstm_m.dtype
    return (y_prompt, y_sample,
            hist_p[None, :, HIST_ROWS - POOL_HIST:, :],
            c_p.astype(cd)[None], n_p[..., 0].astype(nd)[None], m_p[:, :ML_HEADS, 0].astype(md)[None],
            hist_s[None, :, HIST_ROWS - POOL_HIST:, :].astype(state_pool.dtype),
            c_s.astype(cd)[None], n_s[:, :, 0, :].astype(nd)[None],
            mt[:, tt - 1::tt, 0].T.astype(md)[None])
```

```python
import functools

import jax
import jax.numpy as jnp
from jax import lax
from jax.experimental import pallas as pl
from jax.experimental.pallas import tpu as pltpu

D_MODEL = 1024
POOL_WINDOWS = (2, 4, 8, 16)
POOL_GROUP_DIM = 128
POOL_WIDTH = 512
POOL_HIST = 15
HIST_ROWS = 16
ML_HEADS = 4
ML_HEAD_DIM = 256
D_FF = 2816
EPS = 1e-6
M_INIT = -1e30
PAST_LEN = 16384
LANES = 128
SUBLANES = 8
GATE_LANES = LANES
AUG = ML_HEAD_DIM + LANES
VMEM_LIMIT = 56 * 1024 * 1024

TOKEN_TILE = 512
PROMPT_CHUNK = 256
MLSTM_TILE = 512
SAMPLE_SEQ_BLOCK = 16

BF16 = jnp.bfloat16
F32 = jnp.float32

_U0, _Q0, _K0, _V0, _G0 = 0, 512, 1536, 2560, 3584


def _dot(a, b):
    return jnp.dot(a, b, preferred_element_type=F32)


def _const_spec(shape):
    zeros = (0,) * len(shape)
    return pl.BlockSpec(shape, lambda *_: zeros, pipeline_mode=pl.Buffered(1))


def _params(sem):
    return pltpu.CompilerParams(dimension_semantics=sem, vmem_limit_bytes=VMEM_LIMIT)


def _norm_mod(x_ref, sc_ref, sh_ref, g_ref):
    x = x_ref[...]
    nb, tt, d = x.shape
    ms = jnp.mean(x * x, axis=-1, keepdims=True)
    y = x * lax.rsqrt(ms + EPS) * g_ref[...]
    hmod = y * (1.0 + sc_ref[...]) + sh_ref[...]
    return x.reshape(nb * tt, d), hmod.reshape(nb * tt, d)


def _split3(x):
    hi = x.astype(BF16)
    r1 = x - hi.astype(F32)
    mid = r1.astype(BF16)
    lo = (r1 - mid.astype(F32)).astype(BF16)
    return hi, mid, lo


def _cumsum_cols(tri, x):
    hi, mid, lo = _split3(x)
    return _dot(tri, hi) + _dot(tri, mid) + _dot(tri, lo)


def _cumsum_rows(x, tri):
    hi, mid, lo = _split3(x)
    return _dot(hi, tri) + _dot(mid, tri) + _dot(lo, tri)


def _ada_kernel(c_ref, w_ref, b_ref, o_ref):
    c = c_ref[...]
    a = (c * jax.nn.sigmoid(c)).astype(BF16)
    o_ref[...] = _dot(a, w_ref[...].astype(BF16)) + b_ref[...]


def _ada(c_all, w_ada, b_ada):
    rows = c_all.shape[0]
    ncol = w_ada.shape[1]
    tn = 1024
    return pl.pallas_call(
        _ada_kernel,
        grid=(ncol // tn,),
        in_specs=[pl.BlockSpec((rows, D_MODEL), lambda j: (0, 0)),
                  pl.BlockSpec((D_MODEL, tn), lambda j: (0, j)),
                  pl.BlockSpec((1, tn), lambda j: (0, j))],
        out_specs=pl.BlockSpec((rows, tn), lambda j: (0, j)),
        out_shape=jax.ShapeDtypeStruct((rows, ncol), F32),
        compiler_params=_params(("arbitrary",)),
        name="ada",
    )(c_all, w_ada, b_ada)


def _project(hb, w_ref, wg_ref, wkT_ref, bg_ref, q_ref, kT_ref, v_ref, gc_ref, gr_ref):
    q_ref[...] = _dot(hb, w_ref[:, _Q0:_K0]).astype(BF16)
    kT = lax.dot_general(wkT_ref[...], hb, (((1,), (1,)), ((), ())), preferred_element_type=F32)
    kT_ref[...] = (kT * (ML_HEAD_DIM ** -0.5)).astype(BF16)
    v_ref[...] = _dot(hb, w_ref[:, _V0:_G0]).astype(BF16)
    zg = _dot(hb, wg_ref[...]) + bg_ref[...]
    log_f = jnp.minimum(zg, 0.0) - jnp.log1p(jnp.exp(-jnp.abs(zg)))
    lane = lax.broadcasted_iota(jnp.int32, zg.shape, 1)
    gc = jnp.where(lane < ML_HEADS, zg, log_f)
    gc_ref[...] = gc
    gr_ref[...] = gc.T[0:SUBLANES, :]


def _pool_project(acc_fn, u, cnt_fn, wpool_ref, spool_ref):
    outs = []
    for g, w in enumerate(POOL_WINDOWS):
        cols = slice(g * POOL_GROUP_DIM, (g + 1) * POOL_GROUP_DIM)
        ug = u[:, cols]
        d = acc_fn(g, w, ug) / cnt_fn(w) - ug
        outs.append(_dot(d.astype(BF16), wpool_ref[g]) * spool_ref[:, cols])
    return jnp.concatenate(outs, axis=-1).astype(BF16)


def _inproj_prompt_kernel(x_ref, sc_ref, sh_ref, g_ref, w_ref, wg_ref, wkT_ref, bg_ref, wpool_ref, spool_ref,
                          yp_ref, q_ref, kT_ref, v_ref, gc_ref, gr_ref, hout_ref, ext_ref, *, tm):
    t = pl.program_id(1)

    @pl.when(t == 0)
    def _():
        ext_ref[0:HIST_ROWS, :] = jnp.zeros((HIST_ROWS, POOL_WIDTH), F32)

    _, hmod = _norm_mod(x_ref, sc_ref, sh_ref, g_ref)
    hb = hmod.astype(BF16)
    _project(hb, w_ref, wg_ref, wkT_ref, bg_ref, q_ref, kT_ref, v_ref, gc_ref, gr_ref)

    u = _dot(hb, w_ref[:, _U0:_Q0])
    ext_ref[HIST_ROWS:HIST_ROWS + tm, :] = u
    pos = t * tm + lax.broadcasted_iota(jnp.int32, (tm, 1), 0)

    def acc_fn(g, w, ug):
        acc = ug
        for j in range(1, w):
            acc = acc + ext_ref[pl.ds(HIST_ROWS - j, tm), g * POOL_GROUP_DIM:(g + 1) * POOL_GROUP_DIM]
        return acc

    def cnt_fn(w):
        return jnp.minimum(pos + 1, w).astype(F32)

    yp_ref[...] = _pool_project(acc_fn, u, cnt_fn, wpool_ref, spool_ref)
    last = ext_ref[tm:tm + HIST_ROWS, :]
    hout_ref[0] = last
    ext_ref[0:HIST_ROWS, :] = last


def _inproj_sample_kernel(x_ref, sc_ref, sh_ref, g_ref, w_ref, wg_ref, wkT_ref, bg_ref, wpool_ref, spool_ref, hist_ref,
                          yp_ref, q_ref, kT_ref, v_ref, gc_ref, gr_ref, k_ref, hout_ref, *, pos0):
    _, hmod = _norm_mod(x_ref, sc_ref, sh_ref, g_ref)
    hb = hmod.astype(BF16)
    _project(hb, w_ref, wg_ref, wkT_ref, bg_ref, q_ref, kT_ref, v_ref, gc_ref, gr_ref)
    k_ref[...] = (_dot(hb, w_ref[:, _K0:_V0]) * (ML_HEAD_DIM ** -0.5)).astype(BF16)

    u = _dot(hb, w_ref[:, _U0:_Q0])
    nb, tt = x_ref.shape[0], x_ref.shape[1]
    u3 = u.reshape(nb, tt, POOL_WIDTH)
    h0 = hist_ref[:, 0:SUBLANES, :]
    h1 = hist_ref[:, SUBLANES:HIST_ROWS, :]
    tok = lax.broadcasted_iota(jnp.int32, (nb, tt, POOL_GROUP_DIM), 1)

    def acc_fn(g, w, ug):
        cols = slice(g * POOL_GROUP_DIM, (g + 1) * POOL_GROUP_DIM)
        new, mid, old = u3[:, :, cols], h1[:, :, cols], h0[:, :, cols]
        acc = new
        for j in range(1, w):
            if j < tt:
                term = jnp.where(tok >= j, pltpu.roll(new, j, 1), pltpu.roll(mid, j, 1))
            elif j == tt:
                term = mid
            else:
                term = jnp.where(tok >= j - tt, pltpu.roll(mid, j - tt, 1), pltpu.roll(old, j - tt, 1))
            acc = acc + term
        return acc.reshape(nb * tt, POOL_GROUP_DIM)

    pos = pos0 + lax.broadcasted_iota(jnp.int32, (nb, tt, 1), 1).reshape(nb * tt, 1)

    def cnt_fn(w):
        return jnp.minimum(pos + 1, w).astype(F32)

    yp_ref[...] = _pool_project(acc_fn, u, cnt_fn, wpool_ref, spool_ref)
    hout_ref[:, 0:SUBLANES, :] = h1
    hout_ref[:, SUBLANES:HIST_ROWS, :] = u3


def _inproj_prompt(x, sc, sh, g_mix, w_in_r, w_gate, w_kT, bgate, w_pool, s_pool):
    nbatch, seq, _ = x.shape
    tm = TOKEN_TILE
    nt = seq // tm
    ntok = nbatch * seq
    row = lambda b, t: (b * nt + t, 0)
    col = lambda b, t: (0, b * nt + t)
    mod = pl.BlockSpec((1, 1, D_MODEL), lambda b, t: (b, 0, 0))
    return pl.pallas_call(
        functools.partial(_inproj_prompt_kernel, tm=tm),
        grid=(nbatch, nt),
        in_specs=[pl.BlockSpec((1, tm, D_MODEL), lambda b, t: (b, t, 0)), mod, mod,
                  _const_spec((1, D_MODEL)), _const_spec((D_MODEL, _G0)), _const_spec((D_MODEL, GATE_LANES)),
                  _const_spec((D_MODEL, D_MODEL)),
                  _const_spec((1, GATE_LANES)), _const_spec((4, POOL_GROUP_DIM, POOL_GROUP_DIM)),
                  _const_spec((1, POOL_WIDTH))],
        out_specs=[pl.BlockSpec((tm, POOL_WIDTH), row), pl.BlockSpec((tm, D_MODEL), row),
                   pl.BlockSpec((D_MODEL, tm), col), pl.BlockSpec((tm, D_MODEL), row),
                   pl.BlockSpec((tm, GATE_LANES), row), pl.BlockSpec((SUBLANES, tm), col),
                   pl.BlockSpec((1, HIST_ROWS, POOL_WIDTH), lambda b, t: (b, 0, 0))],
        out_shape=[jax.ShapeDtypeStruct((ntok, POOL_WIDTH), BF16), jax.ShapeDtypeStruct((ntok, D_MODEL), BF16),
                   jax.ShapeDtypeStruct((D_MODEL, ntok), BF16), jax.ShapeDtypeStruct((ntok, D_MODEL), BF16),
                   jax.ShapeDtypeStruct((ntok, GATE_LANES), F32), jax.ShapeDtypeStruct((SUBLANES, ntok), F32),
                   jax.ShapeDtypeStruct((nbatch, HIST_ROWS, POOL_WIDTH), F32)],
        scratch_shapes=[pltpu.VMEM((tm + HIST_ROWS, POOL_WIDTH), F32)],
        compiler_params=_params(("arbitrary", "arbitrary")),
        name="inproj_prompt",
    )(x, sc, sh, g_mix, w_in_r, w_gate, w_kT, bgate, w_pool, s_pool)


def _inproj_sample(x, sc, sh, g_mix, w_in_r, w_gate, w_kT, bgate, w_pool, s_pool, hist_pad):
    nseq, tt, _ = x.shape
    nb = TOKEN_TILE // tt
    tm = nb * tt
    ntok = nseq * tt
    row = lambda i: (i, 0)
    col = lambda i: (0, i)
    mod = pl.BlockSpec((nb, 1, D_MODEL), lambda i: (i, 0, 0))
    hist = pl.BlockSpec((nb, HIST_ROWS, POOL_WIDTH), lambda i: (i, 0, 0))
    return pl.pallas_call(
        functools.partial(_inproj_sample_kernel, pos0=PAST_LEN),
        grid=(nseq // nb,),
        in_specs=[pl.BlockSpec((nb, tt, D_MODEL), lambda i: (i, 0, 0)), mod, mod,
                  _const_spec((1, D_MODEL)), _const_spec((D_MODEL, _G0)), _const_spec((D_MODEL, GATE_LANES)),
                  _const_spec((D_MODEL, D_MODEL)),
                  _const_spec((1, GATE_LANES)), _const_spec((4, POOL_GROUP_DIM, POOL_GROUP_DIM)),
                  _const_spec((1, POOL_WIDTH)), hist],
        out_specs=[pl.BlockSpec((tm, POOL_WIDTH), row), pl.BlockSpec((tm, D_MODEL), row),
                   pl.BlockSpec((D_MODEL, tm), col), pl.BlockSpec((tm, D_MODEL), row),
                   pl.BlockSpec((tm, GATE_LANES), row), pl.BlockSpec((SUBLANES, tm), col),
                   pl.BlockSpec((tm, D_MODEL), row), hist],
        out_shape=[jax.ShapeDtypeStruct((ntok, POOL_WIDTH), BF16), jax.ShapeDtypeStruct((ntok, D_MODEL), BF16),
                   jax.ShapeDtypeStruct((D_MODEL, ntok), BF16), jax.ShapeDtypeStruct((ntok, D_MODEL), BF16),
                   jax.ShapeDtypeStruct((ntok, GATE_LANES), F32), jax.ShapeDtypeStruct((SUBLANES, ntok), F32),
                   jax.ShapeDtypeStruct((ntok, D_MODEL), BF16),
                   jax.ShapeDtypeStruct((nseq, HIST_ROWS, POOL_WIDTH), F32)],
        compiler_params=_params(("arbitrary",)),
        name="inproj_sample",
    )(x, sc, sh, g_mix, w_in_r, w_gate, w_kT, bgate, w_pool, s_pool, hist_pad)


def _mlstm_core(q, kT, v, b_c, b_r, ig_r, mask, m_prev, num_inter, qn):
    logd = jnp.where(mask, (b_c - b_r) + ig_r, -jnp.inf)
    a_c = b_c + m_prev
    m_t = jnp.maximum(a_c, jnp.max(logd, axis=-1, keepdims=True))
    w_intra = jnp.exp(logd - m_t)
    w_inter = jnp.exp(a_c - m_t)
    s = _dot(q, kT) * w_intra
    num = w_inter * num_inter + _dot(s.astype(BF16), v)
    den = w_inter * qn + jnp.sum(s, axis=-1, keepdims=True)
    hh = num / jnp.maximum(jnp.abs(den), jnp.exp(-m_t))
    return hh, m_t, a_c


def _mlstm_chunk(q_ref, kT_ref, v_ref, gc_ref, gr_ref, hh_ref, cn_ref, m_ref, r0, chunk):
    rows = slice(r0, r0 + chunk)
    gr = gr_ref[:, rows]
    ri = lax.broadcasted_iota(jnp.int32, (chunk, chunk), 0)
    ci = lax.broadcasted_iota(jnp.int32, (chunk, chunk), 1)
    causal = ri >= ci
    bcol = _cumsum_cols(causal.astype(BF16), gc_ref[rows, :])
    brow = _cumsum_rows(gr, (ri <= ci).astype(BF16))
    ones = jnp.ones((chunk, LANES), BF16)
    wide = lambda x: jnp.concatenate([x, x], axis=-1)
    for h in range(ML_HEADS):
        hs = slice(h * ML_HEAD_DIM, (h + 1) * ML_HEAD_DIM)
        b_r = brow[ML_HEADS + h:ML_HEADS + h + 1, :]
        g_r = gr[h:h + 1, :] - b_r
        m_prev = m_ref[h:h + 1, 0:1]
        cn = cn_ref[h]
        q = q_ref[rows, hs]
        kT = kT_ref[hs, rows]
        vaug = jnp.concatenate([v_ref[rows, hs], ones], axis=-1)
        gm = jnp.where(causal, g_r, -jnp.inf)
        big_m = jnp.maximum(m_prev, jnp.max(gm, axis=-1, keepdims=True))
        m_rep = jnp.broadcast_to(big_m, (chunk, LANES))
        mt_rep = jnp.broadcast_to(bcol[:, ML_HEADS + h:ML_HEADS + h + 1] + big_m, (chunk, LANES))
        s = _dot(q, kT) * jnp.exp(gm - wide(m_rep))
        sva = _dot(s.astype(BF16), vaug)
        qc = _dot(q, cn.astype(BF16))
        w_inter = jnp.exp(m_prev - m_rep)
        den = w_inter * qc[:, ML_HEAD_DIM:] + sva[:, ML_HEAD_DIM:]
        rinv = 1.0 / jnp.maximum(jnp.abs(den), jnp.exp(-mt_rep))
        hh_ref[rows, hs] = (wide(w_inter) * qc[:, :ML_HEAD_DIM] + sva[:, :ML_HEAD_DIM]) * wide(rinv)

        b_last = b_r[:, chunk - 1:chunk]
        m_new = b_last + jnp.maximum(m_prev, jnp.max(g_r, axis=-1, keepdims=True))
        decay = jnp.exp((b_last + m_prev) - m_new)
        w_end = jnp.exp((g_r + b_last) - m_new)
        cn_ref[h] = decay * cn + _dot((kT.astype(F32) * w_end).astype(BF16), vaug)
        m_ref[h:h + 1, :] = jnp.broadcast_to(m_new, (1, LANES))


def _mlstm_prompt_kernel(q_ref, kT_ref, v_ref, gc_ref, gr_ref, hh_ref, cout_ref, nout_ref, mout_ref,
                         cn_ref, m_ref, *, chunk):
    t = pl.program_id(1)

    @pl.when(t == 0)
    def _():
        cn_ref[...] = jnp.zeros(cn_ref.shape, F32)
        m_ref[...] = jnp.full(m_ref.shape, M_INIT, F32)

    for r0 in range(0, q_ref.shape[0], chunk):
        _mlstm_chunk(q_ref, kT_ref, v_ref, gc_ref, gr_ref, hh_ref, cn_ref, m_ref, r0, chunk)

    @pl.when(t == pl.num_programs(1) - 1)
    def _():
        for h in range(ML_HEADS):
            cout_ref[0, h] = cn_ref[h, :, 0:ML_HEAD_DIM]
            nout_ref[0, h] = cn_ref[h, :, ML_HEAD_DIM:AUG]
        mout_ref[0] = m_ref[...]


def _mlstm_prompt(q, kT, v, gc, gr, nbatch, seq):
    tm = MLSTM_TILE
    nt = seq // tm
    row = lambda b, t: (b * nt + t, 0)
    col = lambda b, t: (0, b * nt + t)
    return pl.pallas_call(
        functools.partial(_mlstm_prompt_kernel, chunk=PROMPT_CHUNK),
        grid=(nbatch, nt),
        in_specs=[pl.BlockSpec((tm, D_MODEL), row), pl.BlockSpec((D_MODEL, tm), col), pl.BlockSpec((tm, D_MODEL), row),
                  pl.BlockSpec((tm, GATE_LANES), row), pl.BlockSpec((SUBLANES, tm), col)],
        out_specs=[pl.BlockSpec((tm, D_MODEL), row),
                   pl.BlockSpec((1, ML_HEADS, ML_HEAD_DIM, ML_HEAD_DIM), lambda b, t: (b, 0, 0, 0)),
                   pl.BlockSpec((1, ML_HEADS, ML_HEAD_DIM, LANES), lambda b, t: (b, 0, 0, 0)),
                   pl.BlockSpec((1, SUBLANES, LANES), lambda b, t: (b, 0, 0))],
        out_shape=[jax.ShapeDtypeStruct((nbatch * seq, D_MODEL), F32),
                   jax.ShapeDtypeStruct((nbatch, ML_HEADS, ML_HEAD_DIM, ML_HEAD_DIM), F32),
                   jax.ShapeDtypeStruct((nbatch, ML_HEADS, ML_HEAD_DIM, LANES), F32),
                   jax.ShapeDtypeStruct((nbatch, SUBLANES, LANES), F32)],
        scratch_shapes=[pltpu.VMEM((ML_HEADS, ML_HEAD_DIM, AUG), F32), pltpu.VMEM((SUBLANES, LANES), F32)],
        compiler_params=_params(("arbitrary", "arbitrary")),
        name="mlstm_prompt",
    )(q, kT, v, gc, gr)


def _last_in_group(x, group):
    rows = x.shape[0]
    x3 = jnp.broadcast_to(x, (rows, LANES)).reshape(rows // group, group, LANES)
    last = jnp.broadcast_to(x3[:, group - 1:group, :], x3.shape)
    return last.reshape(rows, LANES)[:, 0:1]


def _mlstm_sample_kernel(q_ref, kT_ref, k_ref, v_ref, gc_ref, gr_ref, m0_ref, c_ref, n_ref,
                         hh_ref, cout_ref, nout_ref, mt_ref, ni_ref, qn_ref, dec_ref, wk_ref, *, tt):
    h = pl.program_id(1)
    L = q_ref.shape[0]
    nseq = L // tt
    gc = gc_ref[...]
    gr = gr_ref[...]
    ri = lax.broadcasted_iota(jnp.int32, (L, L), 0)
    ci = lax.broadcasted_iota(jnp.int32, (L, L), 1)
    same = (ri // tt) == (ci // tt)
    mask = same & (ri >= ci)
    bcol = _cumsum_cols(mask.astype(BF16), gc)
    brow = _cumsum_rows(gr, (same & (ri <= ci)).astype(BF16))
    lane = lax.broadcasted_iota(jnp.int32, (L, GATE_LANES), 1)
    sub = lax.broadcasted_iota(jnp.int32, (SUBLANES, L), 0)
    pick_col = lambda arr, idx: jnp.sum(jnp.where(lane == idx, arr, 0.0), axis=-1, keepdims=True)
    pick_row = lambda arr, idx: jnp.sum(jnp.where(sub == idx, arr, 0.0), axis=0, keepdims=True)
    ig_c = pick_col(gc, h)
    b_c = pick_col(bcol, ML_HEADS + h)
    ig_r = pick_row(gr, h)
    b_r = pick_row(brow, ML_HEADS + h)
    m_prev = m0_ref[0]

    for j in range(nseq):
        rows = slice(j * tt, (j + 1) * tt)
        qj = q_ref[rows, :]
        ni_ref[rows, :] = _dot(qj, c_ref[j, 0].astype(BF16))
        nj = n_ref[j, 0].astype(BF16).astype(F32)
        qn = jnp.sum(qj.astype(F32) * nj, axis=-1, keepdims=True)
        qn_ref[rows, :] = jnp.broadcast_to(qn, (tt, LANES))

    q = q_ref[...]
    kT = kT_ref[...]
    v = v_ref[...]
    hh, m_t, a_c = _mlstm_core(q, kT, v, b_c, b_r, ig_r, mask, m_prev, ni_ref[...], qn_ref[:, 0:1])
    hh_ref[...] = hh
    m_new = _last_in_group(m_t, tt)
    decay = jnp.exp(_last_in_group(a_c, tt) - m_new)
    w_end = jnp.exp((_last_in_group(b_c, tt) - b_c) + ig_c - m_new)
    mt_ref[0] = jnp.broadcast_to(m_t, (L, LANES))
    dec_ref[...] = jnp.broadcast_to(decay, (L, LANES))
    wv = w_end * v.astype(F32)
    wk_ref[...] = w_end.astype(BF16).astype(F32) * k_ref[...].astype(F32)
    rowi = lax.broadcasted_iota(jnp.int32, (L, 1), 0)

    for j in range(nseq):
        rows = slice(j * tt, (j + 1) * tt)
        upd = _dot(kT, jnp.where((rowi // tt) == j, wv, 0.0).astype(BF16))
        dj = dec_ref[j * tt:j * tt + 1, 0:1]
        cout_ref[j, 0] = dj * c_ref[j, 0] + upd
        nout_ref[j, 0] = dj * n_ref[j, 0] + jnp.sum(wk_ref[rows, :], axis=0, keepdims=True)


def _mlstm_sample(q, kT, k, v, gc, gr, m0_tok, c0, n0, tt):
    ntok = q.shape[0]
    nseq = ntok // tt
    sb = SAMPLE_SEQ_BLOCK
    L = sb * tt
    qspec = pl.BlockSpec((L, ML_HEAD_DIM), lambda i, h: (i, h))
    cspec = pl.BlockSpec((sb, 1, ML_HEAD_DIM, ML_HEAD_DIM), lambda i, h: (i, h, 0, 0))
    nspec = pl.BlockSpec((sb, 1, 1, ML_HEAD_DIM), lambda i, h: (i, h, 0, 0))
    return pl.pallas_call(
        functools.partial(_mlstm_sample_kernel, tt=tt),
        grid=(nseq // sb, ML_HEADS),
        in_specs=[qspec, pl.BlockSpec((ML_HEAD_DIM, L), lambda i, h: (h, i)), qspec, qspec,
                  pl.BlockSpec((L, GATE_LANES), lambda i, h: (i, 0)),
                  pl.BlockSpec((SUBLANES, L), lambda i, h: (0, i)),
                  pl.BlockSpec((1, L, 1), lambda i, h: (h, i, 0)), cspec, nspec],
        out_specs=[qspec, cspec, nspec, pl.BlockSpec((1, L, LANES), lambda i, h: (h, i, 0))],
        out_shape=[jax.ShapeDtypeStruct((ntok, D_MODEL), F32),
                   jax.ShapeDtypeStruct((nseq, ML_HEADS, ML_HEAD_DIM, ML_HEAD_DIM), F32),
                   jax.ShapeDtypeStruct((nseq, ML_HEADS, 1, ML_HEAD_DIM), F32),
                   jax.ShapeDtypeStruct((ML_HEADS, ntok, LANES), F32)],
        scratch_shapes=[pltpu.VMEM((L, ML_HEAD_DIM), F32), pltpu.VMEM((L, LANES), F32),
                        pltpu.VMEM((L, LANES), F32), pltpu.VMEM((L, ML_HEAD_DIM), F32)],
        compiler_params=_params(("arbitrary", "arbitrary")),
        name="mlstm_sample",
    )(q, kT, k, v, gc, gr, m0_tok, c0, n0)


def _post_kernel(x_ref, sc_ref, sh_ref, gt_ref, g_ref, hh_ref, yp_ref, wo_ref, wpm_ref, ghead_ref, wbp_ref, wbm_ref, wout_ref,
                 o_ref):
    x, hmod = _norm_mod(x_ref, sc_ref, sh_ref, g_ref)
    hb = hmod.astype(BF16)
    nb, tt, d = x_ref.shape
    o = _dot(hb, wo_ref[...])
    parts = []
    for h in range(ML_HEADS):
        hh = hh_ref[:, h * ML_HEAD_DIM:(h + 1) * ML_HEAD_DIM]
        parts.append(hh * lax.rsqrt(jnp.mean(hh * hh, axis=-1, keepdims=True) + EPS))
    yml = (jnp.concatenate(parts, axis=-1) * ghead_ref[...]) * jax.nn.sigmoid(o)
    gp = _dot(hb, wpm_ref[:, 0:D_MODEL])
    gm = _dot(hb, wpm_ref[:, D_MODEL:2 * D_MODEL])
    merged = (jax.nn.sigmoid(gp) * _dot(yp_ref[...], wbp_ref[...])
              + jax.nn.sigmoid(gm) * _dot(yml.astype(BF16), wbm_ref[...]))
    y = _dot(merged.astype(BF16), wout_ref[...]).reshape(nb, tt, d)
    o_ref[...] = x_ref[...] + gt_ref[...] * y


def _tile_blocks(x):
    g, t, _ = x.shape
    if t >= TOKEN_TILE:
        nb, tt = 1, TOKEN_TILE
    else:
        nb, tt = TOKEN_TILE // t, t
    return nb, tt, (g // nb) * (t // tt), t // tt


def _post(x, sc, sh, gt, g_mix, hh, yp, w_ogate, w_merge, g_head, w_bp, w_bm, w_out):
    nb, tt, steps, per = _tile_blocks(x)
    tm = nb * tt
    xspec = pl.BlockSpec((nb, tt, D_MODEL), lambda i: (i // per, i % per, 0))
    mod = pl.BlockSpec((nb, 1, D_MODEL), lambda i: (i // per, 0, 0))
    row = lambda i: (i, 0)
    return pl.pallas_call(
        _post_kernel,
        grid=(steps,),
        in_specs=[xspec, mod, mod, mod, _const_spec((1, D_MODEL)),
                  pl.BlockSpec((tm, D_MODEL), row), pl.BlockSpec((tm, POOL_WIDTH), row),
                  _const_spec((D_MODEL, D_MODEL)), _const_spec((D_MODEL, 2 * D_MODEL)), _const_spec((1, D_MODEL)),
                  _const_spec((POOL_WIDTH, D_MODEL)), _const_spec((D_MODEL, D_MODEL)),
                  _const_spec((D_MODEL, D_MODEL))],
        out_specs=xspec,
        out_shape=jax.ShapeDtypeStruct(x.shape, F32),
        compiler_params=_params(("arbitrary",)),
        name="post",
    )(x, sc, sh, gt, g_mix, hh, yp, w_ogate, w_merge, g_head, w_bp, w_bm, w_out)


_FF_SPLITS = ((0, 1536), (1536, D_FF))


def _ffn_kernel(x_ref, sc_ref, sh_ref, gt_ref, g_ref, gfin_ref, wgu_ref, wdn_ref, o_ref):
    _, hmod = _norm_mod(x_ref, sc_ref, sh_ref, g_ref)
    hb = hmod.astype(BF16)
    nb, tt, d = x_ref.shape
    dn = None
    for lo, hi in _FF_SPLITS:
        gate = _dot(hb, wgu_ref[:, lo:hi])
        up = _dot(hb, wgu_ref[:, D_FF + lo:D_FF + hi])
        act = (gate * jax.nn.sigmoid(gate) * up).astype(BF16)
        part = _dot(act, wdn_ref[lo:hi, :])
        dn = part if dn is None else dn + part
    x2 = x_ref[...] + gt_ref[...] * dn.reshape(nb, tt, d)
    ms = jnp.mean(x2 * x2, axis=-1, keepdims=True)
    o_ref[...] = x2 * lax.rsqrt(ms + EPS) * gfin_ref[...]


def _ffn(x, sc, sh, gt, g_ffn, g_final, w_gu, w_down):
    nb, tt, steps, per = _tile_blocks(x)
    xspec = pl.BlockSpec((nb, tt, D_MODEL), lambda i: (i // per, i % per, 0))
    mod = pl.BlockSpec((nb, 1, D_MODEL), lambda i: (i // per, 0, 0))
    return pl.pallas_call(
        _ffn_kernel,
        grid=(steps,),
        in_specs=[xspec, mod, mod, mod, _const_spec((1, D_MODEL)), _const_spec((1, D_MODEL)),
                  _const_spec((D_MODEL, 2 * D_FF)), _const_spec((D_FF, D_MODEL))],
        out_specs=xspec,
        out_shape=jax.ShapeDtypeStruct(x.shape, F32),
        compiler_params=_params(("arbitrary",)),
        name="ffn",
    )(x, sc, sh, gt, g_ffn, g_final, w_gu, w_down)


def kernel(x_prompt, x_sample, c_prompt, c_sample, state_pool, state_mlstm_c, state_mlstm_n, state_mlstm_m, g_mix, g_ffn, g_final, w_ada, b_ada, w_in, b_igate, b_fgate, w_pool, s_pool, g_head, w_branch_pool, w_branch_mlstm, w_out, w_gate_up, w_down):
    depth = w_in.shape[0]
    assert depth == 1, "single-layer trunk"
    nbatch, seq, _ = x_prompt.shape
    nseq, tt, _ = x_sample.shape
    l = 0

    wi = w_in[l]
    o0 = _G0
    g0 = o0 + D_MODEL
    p0 = g0 + 2 * ML_HEADS
    w_gate = jnp.pad(wi[:, g0:p0], ((0, 0), (0, GATE_LANES - 2 * ML_HEADS))).astype(BF16)
    w_in_r = wi[:, :_G0].astype(BF16)
    w_kT = wi[:, _K0:_V0].T.astype(BF16)
    w_ogate = wi[:, o0:g0].astype(BF16)
    w_merge = wi[:, p0:].astype(BF16)
    bgate = jnp.pad(jnp.concatenate([b_igate[l], b_fgate[l]])[None, :], ((0, 0), (0, GATE_LANES - 2 * ML_HEADS)))
    w_pool_b = w_pool[l].astype(BF16)
    w_bp = w_branch_pool[l].astype(BF16)
    w_bm = w_branch_mlstm[l].astype(BF16)
    w_o = w_out[l].astype(BF16)
    w_gu = w_gate_up[l].astype(BF16)
    w_dn = w_down[l].astype(BF16)
    g_mix_r, g_ffn_r, g_fin_r = g_mix[l][None, :], g_ffn[l][None, :], g_final[None, :]
    s_pool_r, g_head_r = s_pool[l][None, :], g_head[l][None, :]

    mod = _ada(jnp.concatenate([c_prompt, c_sample], axis=0), w_ada[l], b_ada[l][None, :])
    mod_p = [m[:, None, :] for m in jnp.split(mod[:nbatch], 6, axis=-1)]
    mod_s = [m[:, None, :] for m in jnp.split(mod[nbatch:], 6, axis=-1)]

    sh1, sc1, gt1, sh2, sc2, gt2 = mod_p
    yp, q, kT, v, gc, gr, hist_p = _inproj_prompt(x_prompt, sc1, sh1, g_mix_r, w_in_r, w_gate, w_kT, bgate, w_pool_b, s_pool_r)
    hh, c_p, n_p, m_p = _mlstm_prompt(q, kT, v, gc, gr, nbatch, seq)
    x1 = _post(x_prompt, sc1, sh1, gt1, g_mix_r, hh, yp, w_ogate, w_merge, g_head_r, w_bp, w_bm, w_o)
    y_prompt = _ffn(x1, sc2, sh2, gt2, g_ffn_r, g_fin_r, w_gu, w_dn)

    sh1, sc1, gt1, sh2, sc2, gt2 = mod_s
    hist_pad = jnp.pad(state_pool[l], ((0, 0), (HIST_ROWS - POOL_HIST, 0), (0, 0)))
    yp, q, kT, v, gc, gr, k, hist_s = _inproj_sample(x_sample, sc1, sh1, g_mix_r, w_in_r, w_gate, w_kT, bgate, w_pool_b,
                                                     s_pool_r, hist_pad)
    m0_tok = jnp.repeat(state_mlstm_m[l].astype(F32).T, tt, axis=1)[:, :, None]
    hh, c_s, n_s, mt = _mlstm_sample(q, kT, k, v, gc, gr, m0_tok, state_mlstm_c[l].astype(F32),
                                     state_mlstm_n[l].astype(F32)[:, :, None, :], tt)
    x1 = _post(x_sample, sc1, sh1, gt1, g_mix_r, hh, yp, w_ogate, w_merge, g_head_r, w_bp, w_bm, w_o)
    y_sample = _ffn(x1, sc2, sh2, gt2, g_ffn_r, g_fin_r, w_gu, w_dn)

    cd, nd, md = state_mlstm_c.dtype, state_mlstm_n.dtype, state_mlstm_m.dtype
    return (y_prompt, y_sample,
            hist_p[None, :, HIST_ROWS - POOL_HIST:, :],
            c_p.astype(cd)[None], n_p[..., 0].astype(nd)[None], m_p[:, :ML_HEADS, 0].astype(md)[None],
            hist_s[None, :, HIST_ROWS - POOL_HIST:, :].astype(state_pool.dtype),
            c_s.astype(cd)[None], n_s[:, :, 0, :].astype(nd)[None],
            mt[:, tt - 1::tt, 0].T.astype(md)[None])
```

```python
import functools

import jax
import jax.numpy as jnp
from jax import lax
from jax.experimental import pallas as pl
from jax.experimental.pallas import tpu as pltpu

D_MODEL = 1024
POOL_WINDOWS = (2, 4, 8, 16)
POOL_GROUP_DIM = 128
POOL_WIDTH = 512
POOL_HIST = 15
HIST_ROWS = 16
ML_HEADS = 4
ML_HEAD_DIM = 256
D_FF = 2816
EPS = 1e-6
M_INIT = -1e30
PAST_LEN = 16384
LANES = 128
SUBLANES = 8
BF16_ROWS = 16
GATE_LANES = LANES
AUG = ML_HEAD_DIM + LANES
VMEM_LIMIT = 56 * 1024 * 1024

TOKEN_TILE = 512
FFN_TILE = 1024
PROMPT_CHUNK = 256
MLSTM_TILE = 512
SAMPLE_SEQ_BLOCK = 16

BF16 = jnp.bfloat16
F32 = jnp.float32

_U0, _Q0, _K0, _V0, _G0 = 0, 512, 1536, 2560, 3584
_TAIL_GATE0 = 3 * D_MODEL


def _dot(a, b):
    return jnp.dot(a, b, preferred_element_type=F32)


def _const_spec(shape):
    zeros = (0,) * len(shape)
    return pl.BlockSpec(shape, lambda *_: zeros, pipeline_mode=pl.Buffered(1))


def _params(sem):
    return pltpu.CompilerParams(dimension_semantics=sem, vmem_limit_bytes=VMEM_LIMIT)


def _norm_mod(x_ref, sc_ref, sh_ref, g_ref):
    x = x_ref[...]
    nb, tt, d = x.shape
    ms = jnp.mean(x * x, axis=-1, keepdims=True)
    y = x * lax.rsqrt(ms + EPS) * g_ref[...]
    hmod = y * (1.0 + sc_ref[...]) + sh_ref[...]
    return x.reshape(nb * tt, d), hmod.reshape(nb * tt, d)


def _split3(x):
    hi = x.astype(BF16)
    r1 = x - hi.astype(F32)
    mid = r1.astype(BF16)
    lo = (r1 - mid.astype(F32)).astype(BF16)
    return hi, mid, lo


def _cumsum_cols(tri, x):
    hi, mid, lo = _split3(x)
    return _dot(tri, hi) + _dot(tri, mid) + _dot(tri, lo)


def _cumsum_rows(x, tri):
    hi, mid, lo = _split3(x)
    return _dot(hi, tri) + _dot(mid, tri) + _dot(lo, tri)


def _ada_kernel(c_ref, w_ref, b_ref, wih_ref, wit_ref, o_ref, head_ref, tail_ref):
    c = c_ref[...]
    a = (c * jax.nn.sigmoid(c)).astype(BF16)
    o_ref[...] = _dot(a, w_ref[...].astype(BF16)) + b_ref[...]
    head_ref[...] = wih_ref[0].astype(BF16)
    tail = wit_ref[0]
    rows = tail.shape[0]
    g0 = D_MODEL
    p0 = g0 + 2 * ML_HEADS
    lane = lax.broadcasted_iota(jnp.int32, (rows, GATE_LANES), 1)
    gates = jnp.where(lane < 2 * ML_HEADS, tail[:, g0:g0 + GATE_LANES], 0.0)
    pad = jnp.zeros((rows, _G0 - _TAIL_GATE0 - GATE_LANES), F32)
    tail_ref[...] = jnp.concatenate([tail[:, :g0], tail[:, p0:p0 + 2 * D_MODEL], gates, pad], axis=1).astype(BF16)


def _ada(c_all, w_ada, b_ada, w_in):
    rows = c_all.shape[0]
    ncol = w_ada.shape[1]
    steps = 8
    tn = ncol // steps
    tr = D_MODEL // steps
    return pl.pallas_call(
        _ada_kernel,
        grid=(steps,),
        in_specs=[pl.BlockSpec((rows, D_MODEL), lambda j: (0, 0)),
                  pl.BlockSpec((D_MODEL, tn), lambda j: (0, j)),
                  pl.BlockSpec((1, tn), lambda j: (0, j)),
                  pl.BlockSpec((1, tr, _G0), lambda j: (0, j, 0)),
                  pl.BlockSpec((1, tr, _G0), lambda j: (0, j, 1))],
        out_specs=[pl.BlockSpec((rows, tn), lambda j: (0, j)),
                   pl.BlockSpec((tr, _G0), lambda j: (j, 0)),
                   pl.BlockSpec((tr, _G0), lambda j: (j, 0))],
        out_shape=[jax.ShapeDtypeStruct((rows, ncol), F32),
                   jax.ShapeDtypeStruct((D_MODEL, _G0), BF16),
                   jax.ShapeDtypeStruct((D_MODEL, _G0), BF16)],
        compiler_params=_params(("arbitrary",)),
        name="ada",
    )(c_all, w_ada, b_ada, w_in, w_in)


def _project(hb, w_ref, wg_ref, bg_ref, q_ref, kT_ref, v_ref, gc_ref, gr_ref):
    q_ref[...] = _dot(hb, w_ref[:, _Q0:_K0]).astype(BF16)
    k = _dot(hb, w_ref[:, _K0:_V0]) * (ML_HEAD_DIM ** -0.5)
    kT_ref[...] = k.T.astype(BF16)
    v_ref[...] = _dot(hb, w_ref[:, _V0:_G0]).astype(BF16)
    zg = _dot(hb, wg_ref[...]) + bg_ref[...]
    log_f = jnp.minimum(zg, 0.0) - jnp.log1p(jnp.exp(-jnp.abs(zg)))
    lane = lax.broadcasted_iota(jnp.int32, zg.shape, 1)
    gc = jnp.where(lane < ML_HEADS, zg, log_f)
    gc_ref[...] = gc
    gr_ref[...] = gc.T[0:SUBLANES, :]
    return k


def _pool_project(acc_fn, u, cnt_fn, wpool_ref, spool_ref):
    outs = []
    for g, w in enumerate(POOL_WINDOWS):
        cols = slice(g * POOL_GROUP_DIM, (g + 1) * POOL_GROUP_DIM)
        ug = u[:, cols]
        d = acc_fn(g, w, ug) / cnt_fn(w) - ug
        outs.append(_dot(d.astype(BF16), wpool_ref[g]) * spool_ref[:, cols])
    return jnp.concatenate(outs, axis=-1).astype(BF16)


def _inproj_prompt_kernel(x_ref, sc_ref, sh_ref, g_ref, w_ref, wg_ref, bg_ref, wpool_ref, spool_ref,
                          wbp_ref, wbm_ref, wout_ref, wgu_ref, wdn_ref,
                          yp_ref, q_ref, kT_ref, v_ref, gc_ref, gr_ref, hout_ref,
                          wbp_o, wbm_o, wout_o, wgu_o, wdn_o, ext_ref, *, tm):
    t = pl.program_id(1)

    @pl.when(t == 0)
    def _():
        ext_ref[0:HIST_ROWS, :] = jnp.zeros((HIST_ROWS, POOL_WIDTH), F32)

    for src, dst in ((wbp_ref, wbp_o), (wbm_ref, wbm_o), (wout_ref, wout_o), (wgu_ref, wgu_o), (wdn_ref, wdn_o)):
        dst[...] = src[0].astype(BF16)

    _, hmod = _norm_mod(x_ref, sc_ref, sh_ref, g_ref)
    hb = hmod.astype(BF16)
    _project(hb, w_ref, wg_ref, bg_ref, q_ref, kT_ref, v_ref, gc_ref, gr_ref)

    u = _dot(hb, w_ref[:, _U0:_Q0])
    ext_ref[HIST_ROWS:HIST_ROWS + tm, :] = u
    pos = t * tm + lax.broadcasted_iota(jnp.int32, (tm, 1), 0)

    def acc_fn(g, w, ug):
        acc = ug
        for j in range(1, w):
            acc = acc + ext_ref[pl.ds(HIST_ROWS - j, tm), g * POOL_GROUP_DIM:(g + 1) * POOL_GROUP_DIM]
        return acc

    def cnt_fn(w):
        return jnp.minimum(pos + 1, w).astype(F32)

    yp_ref[...] = _pool_project(acc_fn, u, cnt_fn, wpool_ref, spool_ref)
    last = ext_ref[tm:tm + HIST_ROWS, :]
    hout_ref[0] = last
    ext_ref[0:HIST_ROWS, :] = last


def _inproj_sample_kernel(x_ref, sc_ref, sh_ref, g_ref, w_ref, wg_ref, bg_ref, wpool_ref, spool_ref, hist_ref,
                          yp_ref, q_ref, kT_ref, v_ref, gc_ref, gr_ref, k_ref, hout_ref, *, pos0):
    _, hmod = _norm_mod(x_ref, sc_ref, sh_ref, g_ref)
    hb = hmod.astype(BF16)
    k_ref[...] = _project(hb, w_ref, wg_ref, bg_ref, q_ref, kT_ref, v_ref, gc_ref, gr_ref).astype(BF16)

    u = _dot(hb, w_ref[:, _U0:_Q0])
    nb, tt = x_ref.shape[0], x_ref.shape[1]
    u3 = u.reshape(nb, tt, POOL_WIDTH)
    h0 = hist_ref[:, 0:SUBLANES, :]
    h1 = hist_ref[:, SUBLANES:HIST_ROWS, :]
    tok = lax.broadcasted_iota(jnp.int32, (nb, tt, POOL_GROUP_DIM), 1)

    def acc_fn(g, w, ug):
        cols = slice(g * POOL_GROUP_DIM, (g + 1) * POOL_GROUP_DIM)
        new, mid, old = u3[:, :, cols], h1[:, :, cols], h0[:, :, cols]
        acc = new
        for j in range(1, w):
            if j < tt:
                term = jnp.where(tok >= j, pltpu.roll(new, j, 1), pltpu.roll(mid, j, 1))
            elif j == tt:
                term = mid
            else:
                term = jnp.where(tok >= j - tt, pltpu.roll(mid, j - tt, 1), pltpu.roll(old, j - tt, 1))
            acc = acc + term
        return acc.reshape(nb * tt, POOL_GROUP_DIM)

    pos = pos0 + lax.broadcasted_iota(jnp.int32, (nb, tt, 1), 1).reshape(nb * tt, 1)

    def cnt_fn(w):
        return jnp.minimum(pos + 1, w).astype(F32)

    yp_ref[...] = _pool_project(acc_fn, u, cnt_fn, wpool_ref, spool_ref)
    hout_ref[:, 0:SUBLANES, :] = h1
    hout_ref[:, SUBLANES:HIST_ROWS, :] = u3


def _gate_spec():
    return pl.BlockSpec((D_MODEL, GATE_LANES), lambda *_: (0, _TAIL_GATE0 // GATE_LANES), pipeline_mode=pl.Buffered(1))


def _cast_specs(weights, nt, steps):
    in_specs, out_specs, out_shapes = [], [], []
    for w in weights:
        _, r, c = w.shape
        n = steps
        while r % n or (r // n) % BF16_ROWS:
            n //= 2
        rows = r // n
        idx = lambda b, t, n=n: jnp.minimum(b * nt + t, n - 1)
        in_specs.append(pl.BlockSpec((1, rows, c), lambda b, t, idx=idx: (0, idx(b, t), 0)))
        out_specs.append(pl.BlockSpec((rows, c), lambda b, t, idx=idx: (idx(b, t), 0)))
        out_shapes.append(jax.ShapeDtypeStruct((r, c), BF16))
    return in_specs, out_specs, out_shapes


def _inproj_prompt(x, sc, sh, g_mix, w_head, w_tail, bgate, w_pool, s_pool, later_weights):
    nbatch, seq, _ = x.shape
    tm = TOKEN_TILE
    nt = seq // tm
    ntok = nbatch * seq
    row = lambda b, t: (b * nt + t, 0)
    col = lambda b, t: (0, b * nt + t)
    mod = pl.BlockSpec((1, 1, D_MODEL), lambda b, t: (b, 0, 0))
    cast_in, cast_out, cast_shapes = _cast_specs(later_weights, nt, nbatch * nt)
    return pl.pallas_call(
        functools.partial(_inproj_prompt_kernel, tm=tm),
        grid=(nbatch, nt),
        in_specs=[pl.BlockSpec((1, tm, D_MODEL), lambda b, t: (b, t, 0)), mod, mod,
                  _const_spec((1, D_MODEL)), _const_spec((D_MODEL, _G0)), _gate_spec(),
                  _const_spec((1, GATE_LANES)), _const_spec((4, POOL_GROUP_DIM, POOL_GROUP_DIM)),
                  _const_spec((1, POOL_WIDTH))] + cast_in,
        out_specs=[pl.BlockSpec((tm, POOL_WIDTH), row), pl.BlockSpec((tm, D_MODEL), row),
                   pl.BlockSpec((D_MODEL, tm), col), pl.BlockSpec((tm, D_MODEL), row),
                   pl.BlockSpec((tm, GATE_LANES), row), pl.BlockSpec((SUBLANES, tm), col),
                   pl.BlockSpec((1, HIST_ROWS, POOL_WIDTH), lambda b, t: (b, 0, 0))] + cast_out,
        out_shape=[jax.ShapeDtypeStruct((ntok, POOL_WIDTH), BF16), jax.ShapeDtypeStruct((ntok, D_MODEL), BF16),
                   jax.ShapeDtypeStruct((D_MODEL, ntok), BF16), jax.ShapeDtypeStruct((ntok, D_MODEL), BF16),
                   jax.ShapeDtypeStruct((ntok, GATE_LANES), F32), jax.ShapeDtypeStruct((SUBLANES, ntok), F32),
                   jax.ShapeDtypeStruct((nbatch, HIST_ROWS, POOL_WIDTH), F32)] + cast_shapes,
        scratch_shapes=[pltpu.VMEM((tm + HIST_ROWS, POOL_WIDTH), F32)],
        compiler_params=_params(("arbitrary", "arbitrary")),
        name="inproj_prompt",
    )(x, sc, sh, g_mix, w_head, w_tail, bgate, w_pool, s_pool, *later_weights)


def _inproj_sample(x, sc, sh, g_mix, w_head, w_tail, bgate, w_pool, s_pool, hist_pad):
    nseq, tt, _ = x.shape
    nb = TOKEN_TILE // tt
    tm = nb * tt
    ntok = nseq * tt
    row = lambda i: (i, 0)
    col = lambda i: (0, i)
    mod = pl.BlockSpec((nb, 1, D_MODEL), lambda i: (i, 0, 0))
    hist = pl.BlockSpec((nb, HIST_ROWS, POOL_WIDTH), lambda i: (i, 0, 0))
    return pl.pallas_call(
        functools.partial(_inproj_sample_kernel, pos0=PAST_LEN),
        grid=(nseq // nb,),
        in_specs=[pl.BlockSpec((nb, tt, D_MODEL), lambda i: (i, 0, 0)), mod, mod,
                  _const_spec((1, D_MODEL)), _const_spec((D_MODEL, _G0)), _gate_spec(),
                  _const_spec((1, GATE_LANES)), _const_spec((4, POOL_GROUP_DIM, POOL_GROUP_DIM)),
                  _const_spec((1, POOL_WIDTH)), hist],
        out_specs=[pl.BlockSpec((tm, POOL_WIDTH), row), pl.BlockSpec((tm, D_MODEL), row),
                   pl.BlockSpec((D_MODEL, tm), col), pl.BlockSpec((tm, D_MODEL), row),
                   pl.BlockSpec((tm, GATE_LANES), row), pl.BlockSpec((SUBLANES, tm), col),
                   pl.BlockSpec((tm, D_MODEL), row), hist],
        out_shape=[jax.ShapeDtypeStruct((ntok, POOL_WIDTH), BF16), jax.ShapeDtypeStruct((ntok, D_MODEL), BF16),
                   jax.ShapeDtypeStruct((D_MODEL, ntok), BF16), jax.ShapeDtypeStruct((ntok, D_MODEL), BF16),
                   jax.ShapeDtypeStruct((ntok, GATE_LANES), F32), jax.ShapeDtypeStruct((SUBLANES, ntok), F32),
                   jax.ShapeDtypeStruct((ntok, D_MODEL), BF16),
                   jax.ShapeDtypeStruct((nseq, HIST_ROWS, POOL_WIDTH), F32)],
        compiler_params=_params(("arbitrary",)),
        name="inproj_sample",
    )(x, sc, sh, g_mix, w_head, w_tail, bgate, w_pool, s_pool, hist_pad)


def _mlstm_core(q, kT, v, b_c, b_r, ig_r, mask, m_prev, num_inter, qn):
    logd = jnp.where(mask, (b_c - b_r) + ig_r, -jnp.inf)
    a_c = b_c + m_prev
    m_t = jnp.maximum(a_c, jnp.max(logd, axis=-1, keepdims=True))
    w_intra = jnp.exp(logd - m_t)
    w_inter = jnp.exp(a_c - m_t)
    s = _dot(q, kT) * w_intra
    num = w_inter * num_inter + _dot(s.astype(BF16), v)
    den = w_inter * qn + jnp.sum(s, axis=-1, keepdims=True)
    hh = num / jnp.maximum(jnp.abs(den), jnp.exp(-m_t))
    return hh, m_t, a_c


def _mlstm_chunk(q_ref, kT_ref, v_ref, gc_ref, gr_ref, hh_ref, cn_ref, m_ref, r0, chunk):
    rows = slice(r0, r0 + chunk)
    gr = gr_ref[:, rows]
    ri = lax.broadcasted_iota(jnp.int32, (chunk, chunk), 0)
    ci = lax.broadcasted_iota(jnp.int32, (chunk, chunk), 1)
    causal = ri >= ci
    bcol = _cumsum_cols(causal.astype(BF16), gc_ref[rows, :])
    brow = _cumsum_rows(gr, (ri <= ci).astype(BF16))
    ones = jnp.ones((chunk, LANES), BF16)
    wide = lambda x: jnp.concatenate([x, x], axis=-1)
    for h in range(ML_HEADS):
        hs = slice(h * ML_HEAD_DIM, (h + 1) * ML_HEAD_DIM)
        b_r = brow[ML_HEADS + h:ML_HEADS + h + 1, :]
        g_r = gr[h:h + 1, :] - b_r
        m_prev = m_ref[h:h + 1, 0:1]
        cn = cn_ref[h]
        q = q_ref[rows, hs]
        kT = kT_ref[hs, rows]
        vaug = jnp.concatenate([v_ref[rows, hs], ones], axis=-1)
        gm = jnp.where(causal, g_r, -jnp.inf)
        big_m = jnp.maximum(m_prev, jnp.max(gm, axis=-1, keepdims=True))
        m_rep = jnp.broadcast_to(big_m, (chunk, LANES))
        mt_rep = jnp.broadcast_to(bcol[:, ML_HEADS + h:ML_HEADS + h + 1] + big_m, (chunk, LANES))
        s = _dot(q, kT) * jnp.exp(gm - wide(m_rep))
        sva = _dot(s.astype(BF16), vaug)
        qc = _dot(q, cn.astype(BF16))
        w_inter = jnp.exp(m_prev - m_rep)
        den = w_inter * qc[:, ML_HEAD_DIM:] + sva[:, ML_HEAD_DIM:]
        rinv = 1.0 / jnp.maximum(jnp.abs(den), jnp.exp(-mt_rep))
        hh_ref[rows, hs] = (wide(w_inter) * qc[:, :ML_HEAD_DIM] + sva[:, :ML_HEAD_DIM]) * wide(rinv)

        b_last = b_r[:, chunk - 1:chunk]
        m_new = b_last + jnp.maximum(m_prev, jnp.max(g_r, axis=-1, keepdims=True))
        decay = jnp.exp((b_last + m_prev) - m_new)
        w_end = jnp.exp((g_r + b_last) - m_new)
        cn_ref[h] = decay * cn + _dot((kT.astype(F32) * w_end).astype(BF16), vaug)
        m_ref[h:h + 1, :] = jnp.broadcast_to(m_new, (1, LANES))


def _mlstm_prompt_kernel(q_ref, kT_ref, v_ref, gc_ref, gr_ref, hh_ref, cout_ref, nout_ref, mout_ref,
                         cn_ref, m_ref, *, chunk):
    t = pl.program_id(1)

    @pl.when(t == 0)
    def _():
        cn_ref[...] = jnp.zeros(cn_ref.shape, F32)
        m_ref[...] = jnp.full(m_ref.shape, M_INIT, F32)

    for r0 in range(0, q_ref.shape[0], chunk):
        _mlstm_chunk(q_ref, kT_ref, v_ref, gc_ref, gr_ref, hh_ref, cn_ref, m_ref, r0, chunk)

    @pl.when(t == pl.num_programs(1) - 1)
    def _():
        for h in range(ML_HEADS):
            cout_ref[0, h] = cn_ref[h, :, 0:ML_HEAD_DIM]
            nout_ref[0, h] = cn_ref[h, :, ML_HEAD_DIM:AUG]
        mout_ref[0] = m_ref[...]


def _mlstm_prompt(q, kT, v, gc, gr, nbatch, seq):
    tm = MLSTM_TILE
    nt = seq // tm
    row = lambda b, t: (b * nt + t, 0)
    col = lambda b, t: (0, b * nt + t)
    return pl.pallas_call(
        functools.partial(_mlstm_prompt_kernel, chunk=PROMPT_CHUNK),
        grid=(nbatch, nt),
        in_specs=[pl.BlockSpec((tm, D_MODEL), row), pl.BlockSpec((D_MODEL, tm), col), pl.BlockSpec((tm, D_MODEL), row),
                  pl.BlockSpec((tm, GATE_LANES), row), pl.BlockSpec((SUBLANES, tm), col)],
        out_specs=[pl.BlockSpec((tm, D_MODEL), row),
                   pl.BlockSpec((1, ML_HEADS, ML_HEAD_DIM, ML_HEAD_DIM), lambda b, t: (b, 0, 0, 0)),
                   pl.BlockSpec((1, ML_HEADS, ML_HEAD_DIM, LANES), lambda b, t: (b, 0, 0, 0)),
                   pl.BlockSpec((1, SUBLANES, LANES), lambda b, t: (b, 0, 0))],
        out_shape=[jax.ShapeDtypeStruct((nbatch * seq, D_MODEL), F32),
                   jax.ShapeDtypeStruct((nbatch, ML_HEADS, ML_HEAD_DIM, ML_HEAD_DIM), F32),
                   jax.ShapeDtypeStruct((nbatch, ML_HEADS, ML_HEAD_DIM, LANES), F32),
                   jax.ShapeDtypeStruct((nbatch, SUBLANES, LANES), F32)],
        scratch_shapes=[pltpu.VMEM((ML_HEADS, ML_HEAD_DIM, AUG), F32), pltpu.VMEM((SUBLANES, LANES), F32)],
        compiler_params=_params(("arbitrary", "arbitrary")),
        name="mlstm_prompt",
    )(q, kT, v, gc, gr)


def _last_in_group(x, group):
    rows = x.shape[0]
    x3 = jnp.broadcast_to(x, (rows, LANES)).reshape(rows // group, group, LANES)
    last = jnp.broadcast_to(x3[:, group - 1:group, :], x3.shape)
    return last.reshape(rows, LANES)[:, 0:1]


def _mlstm_sample_kernel(q_ref, kT_ref, k_ref, v_ref, gc_ref, gr_ref, m0_ref, c_ref, n_ref,
                         hh_ref, cout_ref, nout_ref, mt_ref, ni_ref, qn_ref, dec_ref, wk_ref, *, tt):
    h = pl.program_id(1)
    L = q_ref.shape[0]
    nseq = L // tt
    gc = gc_ref[...]
    gr = gr_ref[...]
    ri = lax.broadcasted_iota(jnp.int32, (L, L), 0)
    ci = lax.broadcasted_iota(jnp.int32, (L, L), 1)
    same = (ri // tt) == (ci // tt)
    mask = same & (ri >= ci)
    bcol = _cumsum_cols(mask.astype(BF16), gc)
    brow = _cumsum_rows(gr, (same & (ri <= ci)).astype(BF16))
    lane = lax.broadcasted_iota(jnp.int32, (L, GATE_LANES), 1)
    sub = lax.broadcasted_iota(jnp.int32, (SUBLANES, L), 0)
    pick_col = lambda arr, idx: jnp.sum(jnp.where(lane == idx, arr, 0.0), axis=-1, keepdims=True)
    pick_row = lambda arr, idx: jnp.sum(jnp.where(sub == idx, arr, 0.0), axis=0, keepdims=True)
    ig_c = pick_col(gc, h)
    b_c = pick_col(bcol, ML_HEADS + h)
    ig_r = pick_row(gr, h)
    b_r = pick_row(brow, ML_HEADS + h)
    m_prev = m0_ref[0]

    for j in range(nseq):
        rows = slice(j * tt, (j + 1) * tt)
        qj = q_ref[rows, :]
        ni_ref[rows, :] = _dot(qj, c_ref[j, 0].astype(BF16))
        nj = n_ref[j, 0].astype(BF16).astype(F32)
        qn = jnp.sum(qj.astype(F32) * nj, axis=-1, keepdims=True)
        qn_ref[rows, :] = jnp.broadcast_to(qn, (tt, LANES))

    q = q_ref[...]
    kT = kT_ref[...]
    v = v_ref[...]
    hh, m_t, a_c = _mlstm_core(q, kT, v, b_c, b_r, ig_r, mask, m_prev, ni_ref[...], qn_ref[:, 0:1])
    hh_ref[...] = hh
    m_new = _last_in_group(m_t, tt)
    decay = jnp.exp(_last_in_group(a_c, tt) - m_new)
    w_end = jnp.exp((_last_in_group(b_c, tt) - b_c) + ig_c - m_new)
    mt_ref[0] = jnp.broadcast_to(m_t, (L, LANES))
    dec_ref[...] = jnp.broadcast_to(decay, (L, LANES))
    wv = w_end * v.astype(F32)
    wk_ref[...] = w_end.astype(BF16).astype(F32) * k_ref[...].astype(F32)
    rowi = lax.broadcasted_iota(jnp.int32, (L, 1), 0)

    for j in range(nseq):
        rows = slice(j * tt, (j + 1) * tt)
        upd = _dot(kT, jnp.where((rowi // tt) == j, wv, 0.0).astype(BF16))
        dj = dec_ref[j * tt:j * tt + 1, 0:1]
        cout_ref[j, 0] = dj * c_ref[j, 0] + upd
        nout_ref[j, 0] = dj * n_ref[j, 0] + jnp.sum(wk_ref[rows, :], axis=0, keepdims=True)


def _mlstm_sample(q, kT, k, v, gc, gr, m0_tok, c0, n0, tt):
    ntok = q.shape[0]
    nseq = ntok // tt
    sb = SAMPLE_SEQ_BLOCK
    L = sb * tt
    qspec = pl.BlockSpec((L, ML_HEAD_DIM), lambda i, h: (i, h))
    cspec = pl.BlockSpec((sb, 1, ML_HEAD_DIM, ML_HEAD_DIM), lambda i, h: (i, h, 0, 0))
    nspec = pl.BlockSpec((sb, 1, 1, ML_HEAD_DIM), lambda i, h: (i, h, 0, 0))
    return pl.pallas_call(
        functools.partial(_mlstm_sample_kernel, tt=tt),
        grid=(nseq // sb, ML_HEADS),
        in_specs=[qspec, pl.BlockSpec((ML_HEAD_DIM, L), lambda i, h: (h, i)), qspec, qspec,
                  pl.BlockSpec((L, GATE_LANES), lambda i, h: (i, 0)),
                  pl.BlockSpec((SUBLANES, L), lambda i, h: (0, i)),
                  pl.BlockSpec((1, L, 1), lambda i, h: (h, i, 0)), cspec, nspec],
        out_specs=[qspec, cspec, nspec, pl.BlockSpec((1, L, LANES), lambda i, h: (h, i, 0))],
        out_shape=[jax.ShapeDtypeStruct((ntok, D_MODEL), F32),
                   jax.ShapeDtypeStruct((nseq, ML_HEADS, ML_HEAD_DIM, ML_HEAD_DIM), F32),
                   jax.ShapeDtypeStruct((nseq, ML_HEADS, 1, ML_HEAD_DIM), F32),
                   jax.ShapeDtypeStruct((ML_HEADS, ntok, LANES), F32)],
        scratch_shapes=[pltpu.VMEM((L, ML_HEAD_DIM), F32), pltpu.VMEM((L, LANES), F32),
                        pltpu.VMEM((L, LANES), F32), pltpu.VMEM((L, ML_HEAD_DIM), F32)],
        compiler_params=_params(("arbitrary", "arbitrary")),
        name="mlstm_sample",
    )(q, kT, k, v, gc, gr, m0_tok, c0, n0)


def _post_kernel(x_ref, sc_ref, sh_ref, gt_ref, g_ref, hh_ref, yp_ref, wt_ref, ghead_ref, wbp_ref, wbm_ref, wout_ref,
                 o_ref):
    x, hmod = _norm_mod(x_ref, sc_ref, sh_ref, g_ref)
    hb = hmod.astype(BF16)
    nb, tt, d = x_ref.shape
    o = _dot(hb, wt_ref[:, 0:D_MODEL])
    parts = []
    for h in range(ML_HEADS):
        hh = hh_ref[:, h * ML_HEAD_DIM:(h + 1) * ML_HEAD_DIM]
        parts.append(hh * lax.rsqrt(jnp.mean(hh * hh, axis=-1, keepdims=True) + EPS))
    yml = (jnp.concatenate(parts, axis=-1) * ghead_ref[...]) * jax.nn.sigmoid(o)
    gp = _dot(hb, wt_ref[:, D_MODEL:2 * D_MODEL])
    gm = _dot(hb, wt_ref[:, 2 * D_MODEL:3 * D_MODEL])
    merged = (jax.nn.sigmoid(gp) * _dot(yp_ref[...], wbp_ref[...])
              + jax.nn.sigmoid(gm) * _dot(yml.astype(BF16), wbm_ref[...]))
    y = _dot(merged.astype(BF16), wout_ref[...]).reshape(nb, tt, d)
    o_ref[...] = x_ref[...] + gt_ref[...] * y


def _tile_blocks(x, tile):
    g, t, _ = x.shape
    if t >= tile:
        nb, tt = 1, tile
    else:
        nb, tt = tile // t, t
    return nb, tt, (g // nb) * (t // tt), t // tt


def _post(x, sc, sh, gt, g_mix, hh, yp, w_tail, g_head, w_bp, w_bm, w_out):
    nb, tt, steps, per = _tile_blocks(x, TOKEN_TILE)
    tm = nb * tt
    xspec = pl.BlockSpec((nb, tt, D_MODEL), lambda i: (i // per, i % per, 0))
    mod = pl.BlockSpec((nb, 1, D_MODEL), lambda i: (i // per, 0, 0))
    row = lambda i: (i, 0)
    return pl.pallas_call(
        _post_kernel,
        grid=(steps,),
        in_specs=[xspec, mod, mod, mod, _const_spec((1, D_MODEL)),
                  pl.BlockSpec((tm, D_MODEL), row), pl.BlockSpec((tm, POOL_WIDTH), row),
                  _const_spec((D_MODEL, _G0)), _const_spec((1, D_MODEL)),
                  _const_spec((POOL_WIDTH, D_MODEL)), _const_spec((D_MODEL, D_MODEL)),
                  _const_spec((D_MODEL, D_MODEL))],
        out_specs=xspec,
        out_shape=jax.ShapeDtypeStruct(x.shape, F32),
        compiler_params=_params(("arbitrary",)),
        name="post",
    )(x, sc, sh, gt, g_mix, hh, yp, w_tail, g_head, w_bp, w_bm, w_out)


_FF_SPLITS = ((0, 768), (768, 1536), (1536, 2304), (2304, D_FF))


def _ffn_kernel(x_ref, sc_ref, sh_ref, gt_ref, g_ref, gfin_ref, wgu_ref, wdn_ref, o_ref):
    _, hmod = _norm_mod(x_ref, sc_ref, sh_ref, g_ref)
    hb = hmod.astype(BF16)
    nb, tt, d = x_ref.shape
    dn = None
    for lo, hi in _FF_SPLITS:
        gate = _dot(hb, wgu_ref[:, lo:hi])
        up = _dot(hb, wgu_ref[:, D_FF + lo:D_FF + hi])
        act = (gate * jax.nn.sigmoid(gate) * up).astype(BF16)
        part = _dot(act, wdn_ref[lo:hi, :])
        dn = part if dn is None else dn + part
    x2 = x_ref[...] + gt_ref[...] * dn.reshape(nb, tt, d)
    ms = jnp.mean(x2 * x2, axis=-1, keepdims=True)
    o_ref[...] = x2 * lax.rsqrt(ms + EPS) * gfin_ref[...]


def _ffn(x, sc, sh, gt, g_ffn, g_final, w_gu, w_down):
    nb, tt, steps, per = _tile_blocks(x, FFN_TILE)
    xspec = pl.BlockSpec((nb, tt, D_MODEL), lambda i: (i // per, i % per, 0))
    mod = pl.BlockSpec((nb, 1, D_MODEL), lambda i: (i // per, 0, 0))
    return pl.pallas_call(
        _ffn_kernel,
        grid=(steps,),
        in_specs=[xspec, mod, mod, mod, _const_spec((1, D_MODEL)), _const_spec((1, D_MODEL)),
                  _const_spec((D_MODEL, 2 * D_FF)), _const_spec((D_FF, D_MODEL))],
        out_specs=xspec,
        out_shape=jax.ShapeDtypeStruct(x.shape, F32),
        compiler_params=_params(("arbitrary",)),
        name="ffn",
    )(x, sc, sh, gt, g_ffn, g_final, w_gu, w_down)


def kernel(x_prompt, x_sample, c_prompt, c_sample, state_pool, state_mlstm_c, state_mlstm_n, state_mlstm_m, g_mix, g_ffn, g_final, w_ada, b_ada, w_in, b_igate, b_fgate, w_pool, s_pool, g_head, w_branch_pool, w_branch_mlstm, w_out, w_gate_up, w_down):
    depth = w_in.shape[0]
    assert depth == 1, "single-layer trunk"
    nbatch, seq, _ = x_prompt.shape
    nseq, tt, _ = x_sample.shape
    l = 0

    bgate = jnp.pad(jnp.concatenate([b_igate[l], b_fgate[l]])[None, :], ((0, 0), (0, GATE_LANES - 2 * ML_HEADS)))
    w_pool_b = w_pool[l].astype(BF16)
    g_mix_r, g_ffn_r, g_fin_r = g_mix[l][None, :], g_ffn[l][None, :], g_final[None, :]
    s_pool_r, g_head_r = s_pool[l][None, :], g_head[l][None, :]

    mod, w_head, w_tail = _ada(jnp.concatenate([c_prompt, c_sample], axis=0), w_ada[l], b_ada[l][None, :], w_in)
    mod_p = [m[:, None, :] for m in jnp.split(mod[:nbatch], 6, axis=-1)]
    mod_s = [m[:, None, :] for m in jnp.split(mod[nbatch:], 6, axis=-1)]

    sh1, sc1, gt1, sh2, sc2, gt2 = mod_p
    (yp, q, kT, v, gc, gr, hist_p, w_bp, w_bm, w_o, w_gu, w_dn) = _inproj_prompt(
        x_prompt, sc1, sh1, g_mix_r, w_head, w_tail, bgate, w_pool_b, s_pool_r,
        (w_branch_pool, w_branch_mlstm, w_out, w_gate_up, w_down))
    hh, c_p, n_p, m_p = _mlstm_prompt(q, kT, v, gc, gr, nbatch, seq)
    x1 = _post(x_prompt, sc1, sh1, gt1, g_mix_r, hh, yp, w_tail, g_head_r, w_bp, w_bm, w_o)
    y_prompt = _ffn(x1, sc2, sh2, gt2, g_ffn_r, g_fin_r, w_gu, w_dn)

    sh1, sc1, gt1, sh2, sc2, gt2 = mod_s
    hist_pad = jnp.pad(state_pool[l], ((0, 0), (HIST_ROWS - POOL_HIST, 0), (0, 0)))
    yp, q, kT, v, gc, gr, k, hist_s = _inproj_sample(x_sample, sc1, sh1, g_mix_r, w_head, w_tail, bgate, w_pool_b,
                                                     s_pool_r, hist_pad)
    m0_tok = jnp.repeat(state_mlstm_m[l].astype(F32).T, tt, axis=1)[:, :, None]
    hh, c_s, n_s, mt = _mlstm_sample(q, kT, k, v, gc, gr, m0_tok, state_mlstm_c[l].astype(F32),
                                     state_mlstm_n[l].astype(F32)[:, :, None, :], tt)
    x1 = _post(x_sample, sc1, sh1, gt1, g_mix_r, hh, yp, w_tail, g_head_r, w_bp, w_bm, w_o)
    y_sample = _ffn(x1, sc2, sh2, gt2, g_ffn_r, g_fin_r, w_gu, w_dn)

    cd, nd, md = state_mlstm_c.dtype, state_mlstm_n.dtype, state_mlstm_m.dtype
    return (y_prompt, y_sample,
            hist_p[None, :, HIST_ROWS - POOL_HIST:, :],
            c_p.astype(cd)[None], n_p[..., 0].astype(nd)[None], m_p[:, :ML_HEADS, 0].astype(md)[None],
            hist_s[None, :, HIST_ROWS - POOL_HIST:, :].astype(state_pool.dtype),
            c_s.astype(cd)[None], n_s[:, :, 0, :].astype(nd)[None],
            mt[:, tt - 1::tt, 0].T.astype(md)[None])
```

```python
import functools

import jax
import jax.numpy as jnp
from jax import lax
from jax.experimental import pallas as pl
from jax.experimental.pallas import tpu as pltpu

D_MODEL = 1024
POOL_WINDOWS = (2, 4, 8, 16)
POOL_GROUP_DIM = 128
POOL_WIDTH = 512
POOL_HIST = 15
HIST_ROWS = 16
ML_HEADS = 4
ML_HEAD_DIM = 256
D_FF = 2816
EPS = 1e-6
M_INIT = -1e30
PAST_LEN = 16384
LANES = 128
SUBLANES = 8
BF16_ROWS = 16
GATE_LANES = LANES
AUG = ML_HEAD_DIM + LANES
VMEM_LIMIT = 56 * 1024 * 1024

TOKEN_TILE = 512
FFN_TILE = 1024
PROMPT_CHUNK = 256
MLSTM_TILE = 512
SAMPLE_SEQ_BLOCK = 16

BF16 = jnp.bfloat16
F32 = jnp.float32

_U0, _Q0, _K0, _V0, _G0 = 0, 512, 1536, 2560, 3584


def _dot(a, b):
    return jnp.dot(a, b, preferred_element_type=F32)


def _const_spec(shape):
    zeros = (0,) * len(shape)
    return pl.BlockSpec(shape, lambda *_: zeros, pipeline_mode=pl.Buffered(1))


def _params(sem):
    return pltpu.CompilerParams(dimension_semantics=sem, vmem_limit_bytes=VMEM_LIMIT)


def _norm_mod(x_ref, sc_ref, sh_ref, g_ref):
    x = x_ref[...]
    nb, tt, d = x.shape
    ms = jnp.mean(x * x, axis=-1, keepdims=True)
    y = x * lax.rsqrt(ms + EPS) * g_ref[...]
    hmod = y * (1.0 + sc_ref[...]) + sh_ref[...]
    return x.reshape(nb * tt, d), hmod.reshape(nb * tt, d)


def _split3(x):
    hi = x.astype(BF16)
    r1 = x - hi.astype(F32)
    mid = r1.astype(BF16)
    lo = (r1 - mid.astype(F32)).astype(BF16)
    return hi, mid, lo


def _cumsum_cols(tri, x):
    hi, mid, lo = _split3(x)
    return _dot(tri, hi) + _dot(tri, mid) + _dot(tri, lo)


def _cumsum_rows(x, tri):
    hi, mid, lo = _split3(x)
    return _dot(hi, tri) + _dot(mid, tri) + _dot(lo, tri)


def _ada_kernel(c_ref, w_ref, b_ref, o_ref):
    c = c_ref[...]
    a = (c * jax.nn.sigmoid(c)).astype(BF16)
    o_ref[...] = _dot(a, w_ref[...].astype(BF16)) + b_ref[...]


def _ada(c_all, w_ada, b_ada):
    rows = c_all.shape[0]
    ncol = w_ada.shape[1]
    tn = 1024
    return pl.pallas_call(
        _ada_kernel,
        grid=(ncol // tn,),
        in_specs=[pl.BlockSpec((rows, D_MODEL), lambda j: (0, 0)),
                  pl.BlockSpec((D_MODEL, tn), lambda j: (0, j)),
                  pl.BlockSpec((1, tn), lambda j: (0, j))],
        out_specs=pl.BlockSpec((rows, tn), lambda j: (0, j)),
        out_shape=jax.ShapeDtypeStruct((rows, ncol), F32),
        compiler_params=_params(("arbitrary",)),
        name="ada",
    )(c_all, w_ada, b_ada)


_PREP_COLS = 512


def _wprep_kernel(wt_ref, nx_ref, gt_ref, head_ref, tail_ref, gate_ref, *, head_steps, shift_step, shift):
    j = pl.program_id(0)
    rows = wt_ref[0]
    shifted = jnp.concatenate([rows[shift:, :], nx_ref[0, 0:shift, :]], axis=0)
    blk = jnp.where(j >= shift_step, shifted, rows).T.astype(BF16)

    @pl.when(j < head_steps)
    def _():
        head_ref[...] = blk

    @pl.when(j >= head_steps)
    def _():
        tail_ref[...] = blk

    gates = gt_ref[0].T
    pad = jnp.zeros((gates.shape[0], GATE_LANES - gates.shape[1]), F32)
    gate_ref[...] = jnp.concatenate([gates, pad], axis=1).astype(BF16)


def _wprep(w_in_t):
    g0 = _G0 + D_MODEL
    p0 = g0 + 2 * ML_HEADS
    head_steps = _G0 // _PREP_COLS
    tail_steps = 3 * D_MODEL // _PREP_COLS
    shift_step = head_steps + D_MODEL // _PREP_COLS
    blk = (1, _PREP_COLS, D_MODEL)
    return pl.pallas_call(
        functools.partial(_wprep_kernel, head_steps=head_steps, shift_step=shift_step, shift=p0 - g0),
        grid=(head_steps + tail_steps,),
        in_specs=[pl.BlockSpec(blk, lambda j: (0, j, 0)),
                  pl.BlockSpec(blk, lambda j: (0, jnp.where(j >= shift_step, j + 1, 0), 0)),
                  pl.BlockSpec((1, 2 * ML_HEADS, D_MODEL), lambda j: (0, g0 // (2 * ML_HEADS), 0))],
        out_specs=[pl.BlockSpec((D_MODEL, _PREP_COLS), lambda j: (0, jnp.minimum(j, head_steps - 1))),
                   pl.BlockSpec((D_MODEL, _PREP_COLS), lambda j: (0, jnp.maximum(j - head_steps, 0))),
                   pl.BlockSpec((D_MODEL, GATE_LANES), lambda j: (0, 0))],
        out_shape=[jax.ShapeDtypeStruct((D_MODEL, _G0), BF16),
                   jax.ShapeDtypeStruct((D_MODEL, 3 * D_MODEL), BF16),
                   jax.ShapeDtypeStruct((D_MODEL, GATE_LANES), BF16)],
        compiler_params=_params(("arbitrary",)),
        name="wprep",
    )(w_in_t, w_in_t, w_in_t)


def _project(hb, w_ref, wg_ref, bg_ref, q_ref, kT_ref, v_ref, gc_ref, gr_ref):
    q_ref[...] = _dot(hb, w_ref[:, _Q0:_K0]).astype(BF16)
    k = _dot(hb, w_ref[:, _K0:_V0]) * (ML_HEAD_DIM ** -0.5)
    kT_ref[...] = k.T.astype(BF16)
    v_ref[...] = _dot(hb, w_ref[:, _V0:_G0]).astype(BF16)
    zg = _dot(hb, wg_ref[...]) + bg_ref[...]
    log_f = jnp.minimum(zg, 0.0) - jnp.log1p(jnp.exp(-jnp.abs(zg)))
    lane = lax.broadcasted_iota(jnp.int32, zg.shape, 1)
    gc = jnp.where(lane < ML_HEADS, zg, log_f)
    gc_ref[...] = gc
    gr_ref[...] = gc.T[0:SUBLANES, :]
    return k


def _pool_project(acc_fn, u, cnt_fn, wpool_ref, spool_ref):
    outs = []
    for g, w in enumerate(POOL_WINDOWS):
        cols = slice(g * POOL_GROUP_DIM, (g + 1) * POOL_GROUP_DIM)
        ug = u[:, cols]
        d = acc_fn(g, w, ug) / cnt_fn(w) - ug
        outs.append(_dot(d.astype(BF16), wpool_ref[g]) * spool_ref[:, cols])
    return jnp.concatenate(outs, axis=-1).astype(BF16)


def _inproj_prompt_kernel(x_ref, sc_ref, sh_ref, g_ref, w_ref, wg_ref, bg_ref, wpool_ref, spool_ref,
                          wbp_ref, wbm_ref, wout_ref, wgu_ref, wdn_ref,
                          yp_ref, q_ref, kT_ref, v_ref, gc_ref, gr_ref, hout_ref,
                          wbp_o, wbm_o, wout_o, wgu_o, wdn_o, ext_ref, *, tm):
    t = pl.program_id(1)

    @pl.when(t == 0)
    def _():
        ext_ref[0:HIST_ROWS, :] = jnp.zeros((HIST_ROWS, POOL_WIDTH), F32)

    for src, dst in ((wbp_ref, wbp_o), (wbm_ref, wbm_o), (wout_ref, wout_o), (wgu_ref, wgu_o), (wdn_ref, wdn_o)):
        dst[...] = src[0].astype(BF16)

    _, hmod = _norm_mod(x_ref, sc_ref, sh_ref, g_ref)
    hb = hmod.astype(BF16)
    _project(hb, w_ref, wg_ref, bg_ref, q_ref, kT_ref, v_ref, gc_ref, gr_ref)

    u = _dot(hb, w_ref[:, _U0:_Q0])
    ext_ref[HIST_ROWS:HIST_ROWS + tm, :] = u
    pos = t * tm + lax.broadcasted_iota(jnp.int32, (tm, 1), 0)

    def acc_fn(g, w, ug):
        acc = ug
        for j in range(1, w):
            acc = acc + ext_ref[pl.ds(HIST_ROWS - j, tm), g * POOL_GROUP_DIM:(g + 1) * POOL_GROUP_DIM]
        return acc

    def cnt_fn(w):
        return jnp.minimum(pos + 1, w).astype(F32)

    yp_ref[...] = _pool_project(acc_fn, u, cnt_fn, wpool_ref, spool_ref)
    last = ext_ref[tm:tm + HIST_ROWS, :]
    hout_ref[0] = last
    ext_ref[0:HIST_ROWS, :] = last


def _inproj_sample_kernel(x_ref, sc_ref, sh_ref, g_ref, w_ref, wg_ref, bg_ref, wpool_ref, spool_ref, hist_ref,
                          yp_ref, q_ref, kT_ref, v_ref, gc_ref, gr_ref, k_ref, hout_ref, *, pos0):
    _, hmod = _norm_mod(x_ref, sc_ref, sh_ref, g_ref)
    hb = hmod.astype(BF16)
    k_ref[...] = _project(hb, w_ref, wg_ref, bg_ref, q_ref, kT_ref, v_ref, gc_ref, gr_ref).astype(BF16)

    u = _dot(hb, w_ref[:, _U0:_Q0])
    nb, tt = x_ref.shape[0], x_ref.shape[1]
    u3 = u.reshape(nb, tt, POOL_WIDTH)
    h0 = hist_ref[:, 0:SUBLANES, :]
    h1 = hist_ref[:, SUBLANES:HIST_ROWS, :]
    tok = lax.broadcasted_iota(jnp.int32, (nb, tt, POOL_GROUP_DIM), 1)

    def acc_fn(g, w, ug):
        cols = slice(g * POOL_GROUP_DIM, (g + 1) * POOL_GROUP_DIM)
        new, mid, old = u3[:, :, cols], h1[:, :, cols], h0[:, :, cols]
        acc = new
        for j in range(1, w):
            if j < tt:
                term = jnp.where(tok >= j, pltpu.roll(new, j, 1), pltpu.roll(mid, j, 1))
            elif j == tt:
                term = mid
            else:
                term = jnp.where(tok >= j - tt, pltpu.roll(mid, j - tt, 1), pltpu.roll(old, j - tt, 1))
            acc = acc + term
        return acc.reshape(nb * tt, POOL_GROUP_DIM)

    pos = pos0 + lax.broadcasted_iota(jnp.int32, (nb, tt, 1), 1).reshape(nb * tt, 1)

    def cnt_fn(w):
        return jnp.minimum(pos + 1, w).astype(F32)

    yp_ref[...] = _pool_project(acc_fn, u, cnt_fn, wpool_ref, spool_ref)
    hout_ref[:, 0:SUBLANES, :] = h1
    hout_ref[:, SUBLANES:HIST_ROWS, :] = u3


def _cast_specs(weights, nt, steps):
    in_specs, out_specs, out_shapes = [], [], []
    for w in weights:
        _, r, c = w.shape
        n = steps
        while r % n or (r // n) % BF16_ROWS:
            n //= 2
        rows = r // n
        idx = lambda b, t, n=n: jnp.minimum(b * nt + t, n - 1)
        in_specs.append(pl.BlockSpec((1, rows, c), lambda b, t, idx=idx: (0, idx(b, t), 0)))
        out_specs.append(pl.BlockSpec((rows, c), lambda b, t, idx=idx: (idx(b, t), 0)))
        out_shapes.append(jax.ShapeDtypeStruct((r, c), BF16))
    return in_specs, out_specs, out_shapes


def _inproj_prompt(x, sc, sh, g_mix, w_head, w_gate, bgate, w_pool, s_pool, later_weights):
    nbatch, seq, _ = x.shape
    tm = TOKEN_TILE
    nt = seq // tm
    ntok = nbatch * seq
    row = lambda b, t: (b * nt + t, 0)
    col = lambda b, t: (0, b * nt + t)
    mod = pl.BlockSpec((1, 1, D_MODEL), lambda b, t: (b, 0, 0))
    cast_in, cast_out, cast_shapes = _cast_specs(later_weights, nt, nbatch * nt)
    return pl.pallas_call(
        functools.partial(_inproj_prompt_kernel, tm=tm),
        grid=(nbatch, nt),
        in_specs=[pl.BlockSpec((1, tm, D_MODEL), lambda b, t: (b, t, 0)), mod, mod,
                  _const_spec((1, D_MODEL)), _const_spec((D_MODEL, _G0)), _const_spec((D_MODEL, GATE_LANES)),
                  _const_spec((1, GATE_LANES)), _const_spec((4, POOL_GROUP_DIM, POOL_GROUP_DIM)),
                  _const_spec((1, POOL_WIDTH))] + cast_in,
        out_specs=[pl.BlockSpec((tm, POOL_WIDTH), row), pl.BlockSpec((tm, D_MODEL), row),
                   pl.BlockSpec((D_MODEL, tm), col), pl.BlockSpec((tm, D_MODEL), row),
                   pl.BlockSpec((tm, GATE_LANES), row), pl.BlockSpec((SUBLANES, tm), col),
                   pl.BlockSpec((1, HIST_ROWS, POOL_WIDTH), lambda b, t: (b, 0, 0))] + cast_out,
        out_shape=[jax.ShapeDtypeStruct((ntok, POOL_WIDTH), BF16), jax.ShapeDtypeStruct((ntok, D_MODEL), BF16),
                   jax.ShapeDtypeStruct((D_MODEL, ntok), BF16), jax.ShapeDtypeStruct((ntok, D_MODEL), BF16),
                   jax.ShapeDtypeStruct((ntok, GATE_LANES), F32), jax.ShapeDtypeStruct((SUBLANES, ntok), F32),
                   jax.ShapeDtypeStruct((nbatch, HIST_ROWS, POOL_WIDTH), F32)] + cast_shapes,
        scratch_shapes=[pltpu.VMEM((tm + HIST_ROWS, POOL_WIDTH), F32)],
        compiler_params=_params(("arbitrary", "arbitrary")),
        name="inproj_prompt",
    )(x, sc, sh, g_mix, w_head, w_gate, bgate, w_pool, s_pool, *later_weights)


def _inproj_sample(x, sc, sh, g_mix, w_head, w_gate, bgate, w_pool, s_pool, hist_pad):
    nseq, tt, _ = x.shape
    nb = TOKEN_TILE // tt
    tm = nb * tt
    ntok = nseq * tt
    row = lambda i: (i, 0)
    col = lambda i: (0, i)
    mod = pl.BlockSpec((nb, 1, D_MODEL), lambda i: (i, 0, 0))
    hist = pl.BlockSpec((nb, HIST_ROWS, POOL_WIDTH), lambda i: (i, 0, 0))
    return pl.pallas_call(
        functools.partial(_inproj_sample_kernel, pos0=PAST_LEN),
        grid=(nseq // nb,),
        in_specs=[pl.BlockSpec((nb, tt, D_MODEL), lambda i: (i, 0, 0)), mod, mod,
                  _const_spec((1, D_MODEL)), _const_spec((D_MODEL, _G0)), _const_spec((D_MODEL, GATE_LANES)),
                  _const_spec((1, GATE_LANES)), _const_spec((4, POOL_GROUP_DIM, POOL_GROUP_DIM)),
                  _const_spec((1, POOL_WIDTH)), hist],
        out_specs=[pl.BlockSpec((tm, POOL_WIDTH), row), pl.BlockSpec((tm, D_MODEL), row),
                   pl.BlockSpec((D_MODEL, tm), col), pl.BlockSpec((tm, D_MODEL), row),
                   pl.BlockSpec((tm, GATE_LANES), row), pl.BlockSpec((SUBLANES, tm), col),
                   pl.BlockSpec((tm, D_MODEL), row), hist],
        out_shape=[jax.ShapeDtypeStruct((ntok, POOL_WIDTH), BF16), jax.ShapeDtypeStruct((ntok, D_MODEL), BF16),
                   jax.ShapeDtypeStruct((D_MODEL, ntok), BF16), jax.ShapeDtypeStruct((ntok, D_MODEL), BF16),
                   jax.ShapeDtypeStruct((ntok, GATE_LANES), F32), jax.ShapeDtypeStruct((SUBLANES, ntok), F32),
                   jax.ShapeDtypeStruct((ntok, D_MODEL), BF16),
                   jax.ShapeDtypeStruct((nseq, HIST_ROWS, POOL_WIDTH), F32)],
        compiler_params=_params(("arbitrary",)),
        name="inproj_sample",
    )(x, sc, sh, g_mix, w_head, w_gate, bgate, w_pool, s_pool, hist_pad)


def _mlstm_core(q, kT, v, b_c, b_r, ig_r, mask, m_prev, num_inter, qn):
    logd = jnp.where(mask, (b_c - b_r) + ig_r, -jnp.inf)
    a_c = b_c + m_prev
    m_t = jnp.maximum(a_c, jnp.max(logd, axis=-1, keepdims=True))
    w_intra = jnp.exp(logd - m_t)
    w_inter = jnp.exp(a_c - m_t)
    s = _dot(q, kT) * w_intra
    num = w_inter * num_inter + _dot(s.astype(BF16), v)
    den = w_inter * qn + jnp.sum(s, axis=-1, keepdims=True)
    hh = num / jnp.maximum(jnp.abs(den), jnp.exp(-m_t))
    return hh, m_t, a_c


def _mlstm_chunk(q_ref, kT_ref, v_ref, gc_ref, gr_ref, hh_ref, cn_ref, m_ref, r0, chunk):
    rows = slice(r0, r0 + chunk)
    gr = gr_ref[:, rows]
    ri = lax.broadcasted_iota(jnp.int32, (chunk, chunk), 0)
    ci = lax.broadcasted_iota(jnp.int32, (chunk, chunk), 1)
    causal = ri >= ci
    bcol = _cumsum_cols(causal.astype(BF16), gc_ref[rows, :])
    brow = _cumsum_rows(gr, (ri <= ci).astype(BF16))
    ones = jnp.ones((chunk, LANES), BF16)
    wide = lambda x: jnp.concatenate([x, x], axis=-1)
    for h in range(ML_HEADS):
        hs = slice(h * ML_HEAD_DIM, (h + 1) * ML_HEAD_DIM)
        b_r = brow[ML_HEADS + h:ML_HEADS + h + 1, :]
        g_r = gr[h:h + 1, :] - b_r
        m_prev = m_ref[h:h + 1, 0:1]
        cn = cn_ref[h]
        q = q_ref[rows, hs]
        kT = kT_ref[hs, rows]
        vaug = jnp.concatenate([v_ref[rows, hs], ones], axis=-1)
        gm = jnp.where(causal, g_r, -jnp.inf)
        big_m = jnp.maximum(m_prev, jnp.max(gm, axis=-1, keepdims=True))
        m_rep = jnp.broadcast_to(big_m, (chunk, LANES))
        mt_rep = jnp.broadcast_to(bcol[:, ML_HEADS + h:ML_HEADS + h + 1] + big_m, (chunk, LANES))
        s = _dot(q, kT) * jnp.exp(gm - wide(m_rep))
        sva = _dot(s.astype(BF16), vaug)
        qc = _dot(q, cn.astype(BF16))
        w_inter = jnp.exp(m_prev - m_rep)
        den = w_inter * qc[:, ML_HEAD_DIM:] + sva[:, ML_HEAD_DIM:]
        rinv = 1.0 / jnp.maximum(jnp.abs(den), jnp.exp(-mt_rep))
        hh_ref[rows, hs] = (wide(w_inter) * qc[:, :ML_HEAD_DIM] + sva[:, :ML_HEAD_DIM]) * wide(rinv)

        b_last = b_r[:, chunk - 1:chunk]
        m_new = b_last + jnp.maximum(m_prev, jnp.max(g_r, axis=-1, keepdims=True))
        decay = jnp.exp((b_last + m_prev) - m_new)
        w_end = jnp.exp((g_r + b_last) - m_new)
        cn_ref[h] = decay * cn + _dot((kT.astype(F32) * w_end).astype(BF16), vaug)
        m_ref[h:h + 1, :] = jnp.broadcast_to(m_new, (1, LANES))


def _mlstm_prompt_kernel(q_ref, kT_ref, v_ref, gc_ref, gr_ref, hh_ref, cout_ref, nout_ref, mout_ref,
                         cn_ref, m_ref, *, chunk):
    t = pl.program_id(1)

    @pl.when(t == 0)
    def _():
        cn_ref[...] = jnp.zeros(cn_ref.shape, F32)
        m_ref[...] = jnp.full(m_ref.shape, M_INIT, F32)

    for r0 in range(0, q_ref.shape[0], chunk):
        _mlstm_chunk(q_ref, kT_ref, v_ref, gc_ref, gr_ref, hh_ref, cn_ref, m_ref, r0, chunk)

    @pl.when(t == pl.num_programs(1) - 1)
    def _():
        for h in range(ML_HEADS):
            cout_ref[0, h] = cn_ref[h, :, 0:ML_HEAD_DIM]
            nout_ref[0, h] = cn_ref[h, :, ML_HEAD_DIM:AUG]
        mout_ref[0] = m_ref[...]


def _mlstm_prompt(q, kT, v, gc, gr, nbatch, seq):
    tm = MLSTM_TILE
    nt = seq // tm
    row = lambda b, t: (b * nt + t, 0)
    col = lambda b, t: (0, b * nt + t)
    return pl.pallas_call(
        functools.partial(_mlstm_prompt_kernel, chunk=PROMPT_CHUNK),
        grid=(nbatch, nt),
        in_specs=[pl.BlockSpec((tm, D_MODEL), row), pl.BlockSpec((D_MODEL, tm), col), pl.BlockSpec((tm, D_MODEL), row),
                  pl.BlockSpec((tm, GATE_LANES), row), pl.BlockSpec((SUBLANES, tm), col)],
        out_specs=[pl.BlockSpec((tm, D_MODEL), row),
                   pl.BlockSpec((1, ML_HEADS, ML_HEAD_DIM, ML_HEAD_DIM), lambda b, t: (b, 0, 0, 0)),
                   pl.BlockSpec((1, ML_HEADS, ML_HEAD_DIM, LANES), lambda b, t: (b, 0, 0, 0)),
                   pl.BlockSpec((1, SUBLANES, LANES), lambda b, t: (b, 0, 0))],
        out_shape=[jax.ShapeDtypeStruct((nbatch * seq, D_MODEL), F32),
                   jax.ShapeDtypeStruct((nbatch, ML_HEADS, ML_HEAD_DIM, ML_HEAD_DIM), F32),
                   jax.ShapeDtypeStruct((nbatch, ML_HEADS, ML_HEAD_DIM, LANES), F32),
                   jax.ShapeDtypeStruct((nbatch, SUBLANES, LANES), F32)],
        scratch_shapes=[pltpu.VMEM((ML_HEADS, ML_HEAD_DIM, AUG), F32), pltpu.VMEM((SUBLANES, LANES), F32)],
        compiler_params=_params(("arbitrary", "arbitrary")),
        name="mlstm_prompt",
    )(q, kT, v, gc, gr)


def _last_in_group(x, group):
    rows = x.shape[0]
    x3 = jnp.broadcast_to(x, (rows, LANES)).reshape(rows // group, group, LANES)
    last = jnp.broadcast_to(x3[:, group - 1:group, :], x3.shape)
    return last.reshape(rows, LANES)[:, 0:1]


def _mlstm_sample_kernel(q_ref, kT_ref, k_ref, v_ref, gc_ref, gr_ref, m0_ref, c_ref, n_ref,
                         hh_ref, cout_ref, nout_ref, mt_ref, ni_ref, qn_ref, dec_ref, wk_ref, *, tt):
    h = pl.program_id(1)
    L = q_ref.shape[0]
    nseq = L // tt
    gc = gc_ref[...]
    gr = gr_ref[...]
    ri = lax.broadcasted_iota(jnp.int32, (L, L), 0)
    ci = lax.broadcasted_iota(jnp.int32, (L, L), 1)
    same = (ri // tt) == (ci // tt)
    mask = same & (ri >= ci)
    bcol = _cumsum_cols(mask.astype(BF16), gc)
    brow = _cumsum_rows(gr, (same & (ri <= ci)).astype(BF16))
    lane = lax.broadcasted_iota(jnp.int32, (L, GATE_LANES), 1)
    sub = lax.broadcasted_iota(jnp.int32, (SUBLANES, L), 0)
    pick_col = lambda arr, idx: jnp.sum(jnp.where(lane == idx, arr, 0.0), axis=-1, keepdims=True)
    pick_row = lambda arr, idx: jnp.sum(jnp.where(sub == idx, arr, 0.0), axis=0, keepdims=True)
    ig_c = pick_col(gc, h)
    b_c = pick_col(bcol, ML_HEADS + h)
    ig_r = pick_row(gr, h)
    b_r = pick_row(brow, ML_HEADS + h)
    m_prev = m0_ref[0]

    for j in range(nseq):
        rows = slice(j * tt, (j + 1) * tt)
        qj = q_ref[rows, :]
        ni_ref[rows, :] = _dot(qj, c_ref[j, 0].astype(BF16))
        nj = n_ref[j, 0].astype(BF16).astype(F32)
        qn = jnp.sum(qj.astype(F32) * nj, axis=-1, keepdims=True)
        qn_ref[rows, :] = jnp.broadcast_to(qn, (tt, LANES))

    q = q_ref[...]
    kT = kT_ref[...]
    v = v_ref[...]
    hh, m_t, a_c = _mlstm_core(q, kT, v, b_c, b_r, ig_r, mask, m_prev, ni_ref[...], qn_ref[:, 0:1])
    hh_ref[...] = hh
    m_new = _last_in_group(m_t, tt)
    decay = jnp.exp(_last_in_group(a_c, tt) - m_new)
    w_end = jnp.exp((_last_in_group(b_c, tt) - b_c) + ig_c - m_new)
    mt_ref[0] = jnp.broadcast_to(m_t, (L, LANES))
    dec_ref[...] = jnp.broadcast_to(decay, (L, LANES))
    wv = w_end * v.astype(F32)
    wk_ref[...] = w_end.astype(BF16).astype(F32) * k_ref[...].astype(F32)
    rowi = lax.broadcasted_iota(jnp.int32, (L, 1), 0)

    for j in range(nseq):
        rows = slice(j * tt, (j + 1) * tt)
        upd = _dot(kT, jnp.where((rowi // tt) == j, wv, 0.0).astype(BF16))
        dj = dec_ref[j * tt:j * tt + 1, 0:1]
        cout_ref[j, 0] = dj * c_ref[j, 0] + upd
        nout_ref[j, 0] = dj * n_ref[j, 0] + jnp.sum(wk_ref[rows, :], axis=0, keepdims=True)


def _mlstm_sample(q, kT, k, v, gc, gr, m0_tok, c0, n0, tt):
    ntok = q.shape[0]
    nseq = ntok // tt
    sb = SAMPLE_SEQ_BLOCK
    L = sb * tt
    qspec = pl.BlockSpec((L, ML_HEAD_DIM), lambda i, h: (i, h))
    cspec = pl.BlockSpec((sb, 1, ML_HEAD_DIM, ML_HEAD_DIM), lambda i, h: (i, h, 0, 0))
    nspec = pl.BlockSpec((sb, 1, 1, ML_HEAD_DIM), lambda i, h: (i, h, 0, 0))
    return pl.pallas_call(
        functools.partial(_mlstm_sample_kernel, tt=tt),
        grid=(nseq // sb, ML_HEADS),
        in_specs=[qspec, pl.BlockSpec((ML_HEAD_DIM, L), lambda i, h: (h, i)), qspec, qspec,
                  pl.BlockSpec((L, GATE_LANES), lambda i, h: (i, 0)),
                  pl.BlockSpec((SUBLANES, L), lambda i, h: (0, i)),
                  pl.BlockSpec((1, L, 1), lambda i, h: (h, i, 0)), cspec, nspec],
        out_specs=[qspec, cspec, nspec, pl.BlockSpec((1, L, LANES), lambda i, h: (h, i, 0))],
        out_shape=[jax.ShapeDtypeStruct((ntok, D_MODEL), F32),
                   jax.ShapeDtypeStruct((nseq, ML_HEADS, ML_HEAD_DIM, ML_HEAD_DIM), F32),
                   jax.ShapeDtypeStruct((nseq, ML_HEADS, 1, ML_HEAD_DIM), F32),
                   jax.ShapeDtypeStruct((ML_HEADS, ntok, LANES), F32)],
        scratch_shapes=[pltpu.VMEM((L, ML_HEAD_DIM), F32), pltpu.VMEM((L, LANES), F32),
                        pltpu.VMEM((L, LANES), F32), pltpu.VMEM((L, ML_HEAD_DIM), F32)],
        compiler_params=_params(("arbitrary", "arbitrary")),
        name="mlstm_sample",
    )(q, kT, k, v, gc, gr, m0_tok, c0, n0)


def _post_kernel(x_ref, sc_ref, sh_ref, gt_ref, g_ref, hh_ref, yp_ref, wt_ref, ghead_ref, wbp_ref, wbm_ref, wout_ref,
                 o_ref):
    x, hmod = _norm_mod(x_ref, sc_ref, sh_ref, g_ref)
    hb = hmod.astype(BF16)
    nb, tt, d = x_ref.shape
    o = _dot(hb, wt_ref[:, 0:D_MODEL])
    parts = []
    for h in range(ML_HEADS):
        hh = hh_ref[:, h * ML_HEAD_DIM:(h + 1) * ML_HEAD_DIM]
        parts.append(hh * lax.rsqrt(jnp.mean(hh * hh, axis=-1, keepdims=True) + EPS))
    yml = (jnp.concatenate(parts, axis=-1) * ghead_ref[...]) * jax.nn.sigmoid(o)
    gp = _dot(hb, wt_ref[:, D_MODEL:2 * D_MODEL])
    gm = _dot(hb, wt_ref[:, 2 * D_MODEL:3 * D_MODEL])
    merged = (jax.nn.sigmoid(gp) * _dot(yp_ref[...], wbp_ref[...])
              + jax.nn.sigmoid(gm) * _dot(yml.astype(BF16), wbm_ref[...]))
    y = _dot(merged.astype(BF16), wout_ref[...]).reshape(nb, tt, d)
    o_ref[...] = x_ref[...] + gt_ref[...] * y


def _tile_blocks(x, tile):
    g, t, _ = x.shape
    if t >= tile:
        nb, tt = 1, tile
    else:
        nb, tt = tile // t, t
    return nb, tt, (g // nb) * (t // tt), t // tt


def _post(x, sc, sh, gt, g_mix, hh, yp, w_tail, g_head, w_bp, w_bm, w_out):
    nb, tt, steps, per = _tile_blocks(x, TOKEN_TILE)
    tm = nb * tt
    xspec = pl.BlockSpec((nb, tt, D_MODEL), lambda i: (i // per, i % per, 0))
    mod = pl.BlockSpec((nb, 1, D_MODEL), lambda i: (i // per, 0, 0))
    row = lambda i: (i, 0)
    return pl.pallas_call(
        _post_kernel,
        grid=(steps,),
        in_specs=[xspec, mod, mod, mod, _const_spec((1, D_MODEL)),
                  pl.BlockSpec((tm, D_MODEL), row), pl.BlockSpec((tm, POOL_WIDTH), row),
                  _const_spec((D_MODEL, 3 * D_MODEL)), _const_spec((1, D_MODEL)),
                  _const_spec((POOL_WIDTH, D_MODEL)), _const_spec((D_MODEL, D_MODEL)),
                  _const_spec((D_MODEL, D_MODEL))],
        out_specs=xspec,
        out_shape=jax.ShapeDtypeStruct(x.shape, F32),
        compiler_params=_params(("arbitrary",)),
        name="post",
    )(x, sc, sh, gt, g_mix, hh, yp, w_tail, g_head, w_bp, w_bm, w_out)


_FF_SPLITS = ((0, 768), (768, 1536), (1536, 2304), (2304, D_FF))


def _ffn_kernel(x_ref, sc_ref, sh_ref, gt_ref, g_ref, gfin_ref, wgu_ref, wdn_ref, o_ref):
    _, hmod = _norm_mod(x_ref, sc_ref, sh_ref, g_ref)
    hb = hmod.astype(BF16)
    nb, tt, d = x_ref.shape
    dn = None
    for lo, hi in _FF_SPLITS:
        gate = _dot(hb, wgu_ref[:, lo:hi])
        up = _dot(hb, wgu_ref[:, D_FF + lo:D_FF + hi])
        act = (gate * jax.nn.sigmoid(gate) * up).astype(BF16)
        part = _dot(act, wdn_ref[lo:hi, :])
        dn = part if dn is None else dn + part
    x2 = x_ref[...] + gt_ref[...] * dn.reshape(nb, tt, d)
    ms = jnp.mean(x2 * x2, axis=-1, keepdims=True)
    o_ref[...] = x2 * lax.rsqrt(ms + EPS) * gfin_ref[...]


def _ffn(x, sc, sh, gt, g_ffn, g_final, w_gu, w_down):
    nb, tt, steps, per = _tile_blocks(x, FFN_TILE)
    xspec = pl.BlockSpec((nb, tt, D_MODEL), lambda i: (i // per, i % per, 0))
    mod = pl.BlockSpec((nb, 1, D_MODEL), lambda i: (i // per, 0, 0))
    return pl.pallas_call(
        _ffn_kernel,
        grid=(steps,),
        in_specs=[xspec, mod, mod, mod, _const_spec((1, D_MODEL)), _const_spec((1, D_MODEL)),
                  _const_spec((D_MODEL, 2 * D_FF)), _const_spec((D_FF, D_MODEL))],
        out_specs=xspec,
        out_shape=jax.ShapeDtypeStruct(x.shape, F32),
        compiler_params=_params(("arbitrary",)),
        name="ffn",
    )(x, sc, sh, gt, g_ffn, g_final, w_gu, w_down)


def kernel(x_prompt, x_sample, c_prompt, c_sample, state_pool, state_mlstm_c, state_mlstm_n, state_mlstm_m, g_mix, g_ffn, g_final, w_ada, b_ada, w_in, b_igate, b_fgate, w_pool, s_pool, g_head, w_branch_pool, w_branch_mlstm, w_out, w_gate_up, w_down):
    depth = w_in.shape[0]
    assert depth == 1, "single-layer trunk"
    nbatch, seq, _ = x_prompt.shape
    nseq, tt, _ = x_sample.shape
    l = 0

    bgate = jnp.pad(jnp.concatenate([b_igate[l], b_fgate[l]])[None, :], ((0, 0), (0, GATE_LANES - 2 * ML_HEADS)))
    w_pool_b = w_pool[l].astype(BF16)
    g_mix_r, g_ffn_r, g_fin_r = g_mix[l][None, :], g_ffn[l][None, :], g_final[None, :]
    s_pool_r, g_head_r = s_pool[l][None, :], g_head[l][None, :]

    w_head, w_tail, w_gate = _wprep(jnp.swapaxes(w_in, 1, 2))
    mod = _ada(jnp.concatenate([c_prompt, c_sample], axis=0), w_ada[l], b_ada[l][None, :])
    mod_p = [m[:, None, :] for m in jnp.split(mod[:nbatch], 6, axis=-1)]
    mod_s = [m[:, None, :] for m in jnp.split(mod[nbatch:], 6, axis=-1)]

    sh1, sc1, gt1, sh2, sc2, gt2 = mod_p
    (yp, q, kT, v, gc, gr, hist_p, w_bp, w_bm, w_o, w_gu, w_dn) = _inproj_prompt(
        x_prompt, sc1, sh1, g_mix_r, w_head, w_gate, bgate, w_pool_b, s_pool_r,
        (w_branch_pool, w_branch_mlstm, w_out, w_gate_up, w_down))
    hh, c_p, n_p, m_p = _mlstm_prompt(q, kT, v, gc, gr, nbatch, seq)
    x1 = _post(x_prompt, sc1, sh1, gt1, g_mix_r, hh, yp, w_tail, g_head_r, w_bp, w_bm, w_o)
    y_prompt = _ffn(x1, sc2, sh2, gt2, g_ffn_r, g_fin_r, w_gu, w_dn)

    sh1, sc1, gt1, sh2, sc2, gt2 = mod_s
    hist_pad = jnp.pad(state_pool[l], ((0, 0), (HIST_ROWS - POOL_HIST, 0), (0, 0)))
    yp, q, kT, v, gc, gr, k, hist_s = _inproj_sample(x_sample, sc1, sh1, g_mix_r, w_head, w_gate, bgate, w_pool_b,
                                                     s_pool_r, hist_pad)
    m0_tok = jnp.repeat(state_mlstm_m[l].astype(F32).T, tt, axis=1)[:, :, None]
    hh, c_s, n_s, mt = _mlstm_sample(q, kT, k, v, gc, gr, m0_tok, state_mlstm_c[l].astype(F32),
                                     state_mlstm_n[l].astype(F32)[:, :, None, :], tt)
    x1 = _post(x_sample, sc1, sh1, gt1, g_mix_r, hh, yp, w_tail, g_head_r, w_bp, w_bm, w_o)
    y_sample = _ffn(x1, sc2, sh2, gt2, g_ffn_r, g_fin_r, w_gu, w_dn)

    cd, nd, md = state_mlstm_c.dtype, state_mlstm_n.dtype, state_mlstm_m.dtype
    return (y_prompt, y_sample,
            hist_p[None, :, HIST_ROWS - POOL_HIST:, :],
            c_p.astype(cd)[None], n_p[..., 0].astype(nd)[None], m_p[:, :ML_HEADS, 0].astype(md)[None],
            hist_s[None, :, HIST_ROWS - POOL_HIST:, :].astype(state_pool.dtype),
            c_s.astype(cd)[None], n_s[:, :, 0, :].astype(nd)[None],
            mt[:, tt - 1::tt, 0].T.astype(md)[None])
```

```python
import functools

import jax
import jax.numpy as jnp
from jax import lax
from jax.experimental import pallas as pl
from jax.experimental.pallas import tpu as pltpu

D_MODEL = 1024
POOL_WINDOWS = (2, 4, 8, 16)
POOL_GROUP_DIM = 128
POOL_WIDTH = 512
POOL_HIST = 15
HIST_ROWS = 16
ML_HEADS = 4
ML_HEAD_DIM = 256
D_FF = 2816
EPS = 1e-6
M_INIT = -1e30
PAST_LEN = 16384
LANES = 128
SUBLANES = 8
BF16_ROWS = 16
GATE_LANES = LANES
AUG = ML_HEAD_DIM + LANES
VMEM_LIMIT = 56 * 1024 * 1024

TOKEN_TILE = 512
FFN_TILE = 1024
PROMPT_CHUNK = 256
MLSTM_TILE = 1024
MLSTM_BATCHES = 1
SAMPLE_SEQ_BLOCK = 16

BF16 = jnp.bfloat16
F32 = jnp.float32

_U0, _Q0, _K0, _V0, _G0 = 0, 512, 1536, 2560, 3584


def _dot(a, b):
    return jnp.dot(a, b, preferred_element_type=F32)


def _const_spec(shape):
    zeros = (0,) * len(shape)
    return pl.BlockSpec(shape, lambda *_: zeros, pipeline_mode=pl.Buffered(1))


def _params(sem):
    return pltpu.CompilerParams(dimension_semantics=sem, vmem_limit_bytes=VMEM_LIMIT)


def _norm_mod(x_ref, sc_ref, sh_ref, g_ref):
    x = x_ref[...]
    nb, tt, d = x.shape
    ms = jnp.mean(x * x, axis=-1, keepdims=True)
    y = x * lax.rsqrt(ms + EPS) * g_ref[...]
    hmod = y * (1.0 + sc_ref[...]) + sh_ref[...]
    return x.reshape(nb * tt, d), hmod.reshape(nb * tt, d)


def _split3(x):
    hi = x.astype(BF16)
    r1 = x - hi.astype(F32)
    mid = r1.astype(BF16)
    lo = (r1 - mid.astype(F32)).astype(BF16)
    return hi, mid, lo


def _cumsum_cols(tri, x):
    hi, mid, lo = _split3(x)
    return _dot(tri, hi) + _dot(tri, mid) + _dot(tri, lo)


def _cumsum_rows(x, tri):
    hi, mid, lo = _split3(x)
    return _dot(hi, tri) + _dot(mid, tri) + _dot(lo, tri)


def _ada_kernel(c_ref, w_ref, b_ref, o_ref):
    c = c_ref[...]
    a = (c * jax.nn.sigmoid(c)).astype(BF16)
    o_ref[...] = _dot(a, w_ref[...].astype(BF16)) + b_ref[...]


def _ada(c_all, w_ada, b_ada):
    rows = c_all.shape[0]
    ncol = w_ada.shape[1]
    tn = 1024
    return pl.pallas_call(
        _ada_kernel,
        grid=(ncol // tn,),
        in_specs=[pl.BlockSpec((rows, D_MODEL), lambda j: (0, 0)),
                  pl.BlockSpec((D_MODEL, tn), lambda j: (0, j)),
                  pl.BlockSpec((1, tn), lambda j: (0, j))],
        out_specs=pl.BlockSpec((rows, tn), lambda j: (0, j)),
        out_shape=jax.ShapeDtypeStruct((rows, ncol), F32),
        compiler_params=_params(("arbitrary",)),
        name="ada",
    )(c_all, w_ada, b_ada)


_PREP_COLS = 512


def _wprep_kernel(wt_ref, nx_ref, gt_ref, head_ref, tail_ref, gate_ref, *, head_steps, shift_step, shift):
    j = pl.program_id(0)
    rows = wt_ref[0]
    shifted = jnp.concatenate([rows[shift:, :], nx_ref[0, 0:shift, :]], axis=0)
    blk = jnp.where(j >= shift_step, shifted, rows).T.astype(BF16)

    @pl.when(j < head_steps)
    def _():
        head_ref[...] = blk

    @pl.when(j >= head_steps)
    def _():
        tail_ref[...] = blk

    gates = gt_ref[0].T
    pad = jnp.zeros((gates.shape[0], GATE_LANES - gates.shape[1]), F32)
    gate_ref[...] = jnp.concatenate([gates, pad], axis=1).astype(BF16)


def _wprep(w_in_t):
    g0 = _G0 + D_MODEL
    p0 = g0 + 2 * ML_HEADS
    head_steps = _G0 // _PREP_COLS
    tail_steps = 3 * D_MODEL // _PREP_COLS
    shift_step = head_steps + D_MODEL // _PREP_COLS
    blk = (1, _PREP_COLS, D_MODEL)
    return pl.pallas_call(
        functools.partial(_wprep_kernel, head_steps=head_steps, shift_step=shift_step, shift=p0 - g0),
        grid=(head_steps + tail_steps,),
        in_specs=[pl.BlockSpec(blk, lambda j: (0, j, 0)),
                  pl.BlockSpec(blk, lambda j: (0, jnp.where(j >= shift_step, j + 1, 0), 0)),
                  pl.BlockSpec((1, 2 * ML_HEADS, D_MODEL), lambda j: (0, g0 // (2 * ML_HEADS), 0))],
        out_specs=[pl.BlockSpec((D_MODEL, _PREP_COLS), lambda j: (0, jnp.minimum(j, head_steps - 1))),
                   pl.BlockSpec((D_MODEL, _PREP_COLS), lambda j: (0, jnp.maximum(j - head_steps, 0))),
                   pl.BlockSpec((D_MODEL, GATE_LANES), lambda j: (0, 0))],
        out_shape=[jax.ShapeDtypeStruct((D_MODEL, _G0), BF16),
                   jax.ShapeDtypeStruct((D_MODEL, 3 * D_MODEL), BF16),
                   jax.ShapeDtypeStruct((D_MODEL, GATE_LANES), BF16)],
        compiler_params=_params(("arbitrary",)),
        name="wprep",
    )(w_in_t, w_in_t, w_in_t)


def _project(hb, w_ref, wg_ref, bg_ref, q_ref, kT_ref, v_ref, gc_ref, gr_ref):
    q_ref[...] = _dot(hb, w_ref[:, _Q0:_K0]).astype(BF16)
    k = _dot(hb, w_ref[:, _K0:_V0]) * (ML_HEAD_DIM ** -0.5)
    kT_ref[...] = k.T.astype(BF16)
    v_ref[...] = _dot(hb, w_ref[:, _V0:_G0]).astype(BF16)
    zg = _dot(hb, wg_ref[...]) + bg_ref[...]
    log_f = jnp.minimum(zg, 0.0) - jnp.log1p(jnp.exp(-jnp.abs(zg)))
    lane = lax.broadcasted_iota(jnp.int32, zg.shape, 1)
    gc = jnp.where(lane < ML_HEADS, zg, log_f)
    gc_ref[...] = gc
    gr_ref[...] = gc.T[0:SUBLANES, :]
    return k


def _pool_project(acc_fn, u, cnt_fn, wpool_ref, spool_ref):
    outs = []
    for g, w in enumerate(POOL_WINDOWS):
        cols = slice(g * POOL_GROUP_DIM, (g + 1) * POOL_GROUP_DIM)
        ug = u[:, cols]
        d = acc_fn(g, w, ug) / cnt_fn(w) - ug
        outs.append(_dot(d.astype(BF16), wpool_ref[g]) * spool_ref[:, cols])
    return jnp.concatenate(outs, axis=-1).astype(BF16)


def _inproj_prompt_kernel(x_ref, sc_ref, sh_ref, g_ref, w_ref, wg_ref, bg_ref, wpool_ref, spool_ref,
                          wbp_ref, wbm_ref, wout_ref, wgu_ref, wdn_ref,
                          yp_ref, q_ref, kT_ref, v_ref, gc_ref, gr_ref, hout_ref,
                          wbp_o, wbm_o, wout_o, wgu_o, wdn_o, ext_ref, *, tm):
    t = pl.program_id(1)

    @pl.when(t == 0)
    def _():
        ext_ref[0:HIST_ROWS, :] = jnp.zeros((HIST_ROWS, POOL_WIDTH), F32)

    for src, dst in ((wbp_ref, wbp_o), (wbm_ref, wbm_o), (wout_ref, wout_o), (wgu_ref, wgu_o), (wdn_ref, wdn_o)):
        dst[...] = src[0].astype(BF16)

    _, hmod = _norm_mod(x_ref, sc_ref, sh_ref, g_ref)
    hb = hmod.astype(BF16)
    _project(hb, w_ref, wg_ref, bg_ref, q_ref, kT_ref.at[0], v_ref, gc_ref, gr_ref.at[0])

    u = _dot(hb, w_ref[:, _U0:_Q0])
    ext_ref[HIST_ROWS:HIST_ROWS + tm, :] = u
    pos = t * tm + lax.broadcasted_iota(jnp.int32, (tm, 1), 0)

    def acc_fn(g, w, ug):
        acc = ug
        for j in range(1, w):
            acc = acc + ext_ref[pl.ds(HIST_ROWS - j, tm), g * POOL_GROUP_DIM:(g + 1) * POOL_GROUP_DIM]
        return acc

    def cnt_fn(w):
        return jnp.minimum(pos + 1, w).astype(F32)

    yp_ref[...] = _pool_project(acc_fn, u, cnt_fn, wpool_ref, spool_ref)
    last = ext_ref[tm:tm + HIST_ROWS, :]
    hout_ref[0] = last
    ext_ref[0:HIST_ROWS, :] = last


def _inproj_sample_kernel(x_ref, sc_ref, sh_ref, g_ref, w_ref, wg_ref, bg_ref, wpool_ref, spool_ref, hist_ref,
                          yp_ref, q_ref, kT_ref, v_ref, gc_ref, gr_ref, k_ref, hout_ref, *, pos0):
    _, hmod = _norm_mod(x_ref, sc_ref, sh_ref, g_ref)
    hb = hmod.astype(BF16)
    k_ref[...] = _project(hb, w_ref, wg_ref, bg_ref, q_ref, kT_ref, v_ref, gc_ref, gr_ref).astype(BF16)

    u = _dot(hb, w_ref[:, _U0:_Q0])
    nb, tt = x_ref.shape[0], x_ref.shape[1]
    u3 = u.reshape(nb, tt, POOL_WIDTH)
    h0 = hist_ref[:, 0:SUBLANES, :]
    h1 = hist_ref[:, SUBLANES:HIST_ROWS, :]
    tok = lax.broadcasted_iota(jnp.int32, (nb, tt, POOL_GROUP_DIM), 1)

    def acc_fn(g, w, ug):
        cols = slice(g * POOL_GROUP_DIM, (g + 1) * POOL_GROUP_DIM)
        new, mid, old = u3[:, :, cols], h1[:, :, cols], h0[:, :, cols]
        acc = new
        for j in range(1, w):
            if j < tt:
                term = jnp.where(tok >= j, pltpu.roll(new, j, 1), pltpu.roll(mid, j, 1))
            elif j == tt:
                term = mid
            else:
                term = jnp.where(tok >= j - tt, pltpu.roll(mid, j - tt, 1), pltpu.roll(old, j - tt, 1))
            acc = acc + term
        return acc.reshape(nb * tt, POOL_GROUP_DIM)

    pos = pos0 + lax.broadcasted_iota(jnp.int32, (nb, tt, 1), 1).reshape(nb * tt, 1)

    def cnt_fn(w):
        return jnp.minimum(pos + 1, w).astype(F32)

    yp_ref[...] = _pool_project(acc_fn, u, cnt_fn, wpool_ref, spool_ref)
    hout_ref[:, 0:SUBLANES, :] = h1
    hout_ref[:, SUBLANES:HIST_ROWS, :] = u3


def _cast_specs(weights, nt, steps):
    in_specs, out_specs, out_shapes = [], [], []
    for w in weights:
        _, r, c = w.shape
        n = steps
        while r % n or (r // n) % BF16_ROWS:
            n //= 2
        rows = r // n
        idx = lambda b, t, n=n: jnp.minimum(b * nt + t, n - 1)
        in_specs.append(pl.BlockSpec((1, rows, c), lambda b, t, idx=idx: (0, idx(b, t), 0)))
        out_specs.append(pl.BlockSpec((rows, c), lambda b, t, idx=idx: (idx(b, t), 0)))
        out_shapes.append(jax.ShapeDtypeStruct((r, c), BF16))
    return in_specs, out_specs, out_shapes


def _inproj_prompt(x, sc, sh, g_mix, w_head, w_gate, bgate, w_pool, s_pool, later_weights):
    nbatch, seq, _ = x.shape
    tm = TOKEN_TILE
    nt = seq // tm
    ntok = nbatch * seq
    row = lambda b, t: (b * nt + t, 0)
    mod = pl.BlockSpec((1, 1, D_MODEL), lambda b, t: (b, 0, 0))
    cast_in, cast_out, cast_shapes = _cast_specs(later_weights, nt, nbatch * nt)
    return pl.pallas_call(
        functools.partial(_inproj_prompt_kernel, tm=tm),
        grid=(nbatch, nt),
        in_specs=[pl.BlockSpec((1, tm, D_MODEL), lambda b, t: (b, t, 0)), mod, mod,
                  _const_spec((1, D_MODEL)), _const_spec((D_MODEL, _G0)), _const_spec((D_MODEL, GATE_LANES)),
                  _const_spec((1, GATE_LANES)), _const_spec((4, POOL_GROUP_DIM, POOL_GROUP_DIM)),
                  _const_spec((1, POOL_WIDTH))] + cast_in,
        out_specs=[pl.BlockSpec((tm, POOL_WIDTH), row), pl.BlockSpec((tm, D_MODEL), row),
                   pl.BlockSpec((1, D_MODEL, tm), lambda b, t: (b, 0, t)), pl.BlockSpec((tm, D_MODEL), row),
                   pl.BlockSpec((tm, GATE_LANES), row), pl.BlockSpec((1, SUBLANES, tm), lambda b, t: (b, 0, t)),
                   pl.BlockSpec((1, HIST_ROWS, POOL_WIDTH), lambda b, t: (b, 0, 0))] + cast_out,
        out_shape=[jax.ShapeDtypeStruct((ntok, POOL_WIDTH), BF16), jax.ShapeDtypeStruct((ntok, D_MODEL), BF16),
                   jax.ShapeDtypeStruct((nbatch, D_MODEL, seq), BF16), jax.ShapeDtypeStruct((ntok, D_MODEL), BF16),
                   jax.ShapeDtypeStruct((ntok, GATE_LANES), F32), jax.ShapeDtypeStruct((nbatch, SUBLANES, seq), F32),
                   jax.ShapeDtypeStruct((nbatch, HIST_ROWS, POOL_WIDTH), F32)] + cast_shapes,
        scratch_shapes=[pltpu.VMEM((tm + HIST_ROWS, POOL_WIDTH), F32)],
        compiler_params=_params(("arbitrary", "arbitrary")),
        name="inproj_prompt",
    )(x, sc, sh, g_mix, w_head, w_gate, bgate, w_pool, s_pool, *later_weights)


def _inproj_sample(x, sc, sh, g_mix, w_head, w_gate, bgate, w_pool, s_pool, hist_pad):
    nseq, tt, _ = x.shape
    nb = TOKEN_TILE // tt
    tm = nb * tt
    ntok = nseq * tt
    row = lambda i: (i, 0)
    col = lambda i: (0, i)
    mod = pl.BlockSpec((nb, 1, D_MODEL), lambda i: (i, 0, 0))
    hist = pl.BlockSpec((nb, HIST_ROWS, POOL_WIDTH), lambda i: (i, 0, 0))
    return pl.pallas_call(
        functools.partial(_inproj_sample_kernel, pos0=PAST_LEN),
        grid=(nseq // nb,),
        in_specs=[pl.BlockSpec((nb, tt, D_MODEL), lambda i: (i, 0, 0)), mod, mod,
                  _const_spec((1, D_MODEL)), _const_spec((D_MODEL, _G0)), _const_spec((D_MODEL, GATE_LANES)),
                  _const_spec((1, GATE_LANES)), _const_spec((4, POOL_GROUP_DIM, POOL_GROUP_DIM)),
                  _const_spec((1, POOL_WIDTH)), hist],
        out_specs=[pl.BlockSpec((tm, POOL_WIDTH), row), pl.BlockSpec((tm, D_MODEL), row),
                   pl.BlockSpec((D_MODEL, tm), col), pl.BlockSpec((tm, D_MODEL), row),
                   pl.BlockSpec((tm, GATE_LANES), row), pl.BlockSpec((SUBLANES, tm), col),
                   pl.BlockSpec((tm, D_MODEL), row), hist],
        out_shape=[jax.ShapeDtypeStruct((ntok, POOL_WIDTH), BF16), jax.ShapeDtypeStruct((ntok, D_MODEL), BF16),
                   jax.ShapeDtypeStruct((D_MODEL, ntok), BF16), jax.ShapeDtypeStruct((ntok, D_MODEL), BF16),
                   jax.ShapeDtypeStruct((ntok, GATE_LANES), F32), jax.ShapeDtypeStruct((SUBLANES, ntok), F32),
                   jax.ShapeDtypeStruct((ntok, D_MODEL), BF16),
                   jax.ShapeDtypeStruct((nseq, HIST_ROWS, POOL_WIDTH), F32)],
        compiler_params=_params(("arbitrary",)),
        name="inproj_sample",
    )(x, sc, sh, g_mix, w_head, w_gate, bgate, w_pool, s_pool, hist_pad)


def _mlstm_core(q, kT, v, b_c, b_r, ig_r, mask, m_prev, num_inter, qn):
    logd = jnp.where(mask, (b_c - b_r) + ig_r, -jnp.inf)
    a_c = b_c + m_prev
    m_t = jnp.maximum(a_c, jnp.max(logd, axis=-1, keepdims=True))
    w_intra = jnp.exp(logd - m_t)
    w_inter = jnp.exp(a_c - m_t)
    s = _dot(q, kT) * w_intra
    num = w_inter * num_inter + _dot(s.astype(BF16), v)
    den = w_inter * qn + jnp.sum(s, axis=-1, keepdims=True)
    hh = num / jnp.maximum(jnp.abs(den), jnp.exp(-m_t))
    return hh, m_t, a_c


def _mlstm_chunk(q_ref, kT_ref, v_ref, gr_ref, hh_ref, cn_ref, m_ref, r0, chunk):
    rows = slice(r0, r0 + chunk)
    gr = gr_ref[:, rows]
    ri = lax.broadcasted_iota(jnp.int32, (chunk, chunk), 0)
    ci = lax.broadcasted_iota(jnp.int32, (chunk, chunk), 1)
    causal = ri >= ci
    brow = _cumsum_rows(gr, (ri <= ci).astype(BF16))
    bcol = brow.T
    ones = jnp.ones((chunk, LANES), BF16)
    wide = lambda x: jnp.concatenate([x, x], axis=-1)
    for h in range(ML_HEADS):
        hs = slice(h * ML_HEAD_DIM, (h + 1) * ML_HEAD_DIM)
        b_r = brow[ML_HEADS + h:ML_HEADS + h + 1, :]
        g_r = gr[h:h + 1, :] - b_r
        m_prev = m_ref[h:h + 1, 0:1]
        cn = cn_ref[h]
        q = q_ref[rows, hs]
        kT = kT_ref[hs, rows]
        vaug = jnp.concatenate([v_ref[rows, hs], ones], axis=-1)
        gm = jnp.where(causal, g_r, -jnp.inf)
        big_m = jnp.maximum(m_prev, jnp.max(gm, axis=-1, keepdims=True))
        m_rep = jnp.broadcast_to(big_m, (chunk, LANES))
        mt_rep = jnp.broadcast_to(bcol[:, ML_HEADS + h:ML_HEADS + h + 1] + big_m, (chunk, LANES))
        s = _dot(q, kT) * jnp.exp(gm - wide(m_rep))
        b_last = b_r[:, chunk - 1:chunk]
        m_new = b_last + jnp.maximum(m_prev, jnp.max(g_r, axis=-1, keepdims=True))
        decay = jnp.exp((b_last + m_prev) - m_new)
        w_end = jnp.exp((g_r + b_last) - m_new)
        both = _dot(jnp.concatenate([s.astype(BF16), (kT.astype(F32) * w_end).astype(BF16)], axis=0), vaug)
        sva = both[0:chunk]
        qc = _dot(q, cn.astype(BF16))
        w_inter = jnp.exp(m_prev - m_rep)
        den = w_inter * qc[:, ML_HEAD_DIM:] + sva[:, ML_HEAD_DIM:]
        rinv = 1.0 / jnp.maximum(jnp.abs(den), jnp.exp(-mt_rep))
        hh_ref[rows, hs] = (wide(w_inter) * qc[:, :ML_HEAD_DIM] + sva[:, :ML_HEAD_DIM]) * wide(rinv)
        cn_ref[h] = decay * cn + both[chunk:]
        m_ref[h:h + 1, :] = jnp.broadcast_to(m_new, (1, LANES))


def _mlstm_prompt_kernel(q_ref, kT_ref, v_ref, gr_ref, hh_ref, cout_ref, nout_ref, mout_ref, cn_ref, m_ref, *, chunk):
    t = pl.program_id(1)
    nb = q_ref.shape[0]

    @pl.when(t == 0)
    def _():
        cn_ref[...] = jnp.zeros(cn_ref.shape, F32)
        m_ref[...] = jnp.full(m_ref.shape, M_INIT, F32)

    for r0 in range(0, q_ref.shape[1], chunk):
        for s in range(nb):
            _mlstm_chunk(q_ref.at[s], kT_ref.at[s], v_ref.at[s], gr_ref.at[s], hh_ref.at[s], cn_ref.at[s],
                         m_ref.at[s], r0, chunk)

    @pl.when(t == pl.num_programs(1) - 1)
    def _():
        cout_ref[...] = cn_ref[:, :, :, 0:ML_HEAD_DIM]
        nout_ref[...] = cn_ref[:, :, :, ML_HEAD_DIM:AUG]
        mout_ref[...] = m_ref[...]


def _mlstm_prompt(q, kT, v, gr):
    nbatch, seq, _ = q.shape
    tm, nb = MLSTM_TILE, MLSTM_BATCHES
    row = pl.BlockSpec((nb, tm, D_MODEL), lambda b, t: (b, t, 0))
    state = lambda last: pl.BlockSpec((nb, ML_HEADS, ML_HEAD_DIM, last), lambda b, t: (b, 0, 0, 0))
    return pl.pallas_call(
        functools.partial(_mlstm_prompt_kernel, chunk=PROMPT_CHUNK),
        grid=(nbatch // nb, seq // tm),
        in_specs=[row, pl.BlockSpec((nb, D_MODEL, tm), lambda b, t: (b, 0, t)), row,
                  pl.BlockSpec((nb, SUBLANES, tm), lambda b, t: (b, 0, t))],
        out_specs=[row, state(ML_HEAD_DIM), state(LANES), pl.BlockSpec((nb, SUBLANES, LANES), lambda b, t: (b, 0, 0))],
        out_shape=[jax.ShapeDtypeStruct((nbatch, seq, D_MODEL), F32),
                   jax.ShapeDtypeStruct((nbatch, ML_HEADS, ML_HEAD_DIM, ML_HEAD_DIM), F32),
                   jax.ShapeDtypeStruct((nbatch, ML_HEADS, ML_HEAD_DIM, LANES), F32),
                   jax.ShapeDtypeStruct((nbatch, SUBLANES, LANES), F32)],
        scratch_shapes=[pltpu.VMEM((nb, ML_HEADS, ML_HEAD_DIM, AUG), F32), pltpu.VMEM((nb, SUBLANES, LANES), F32)],
        compiler_params=_params(("arbitrary", "arbitrary")),
        name="mlstm_prompt",
    )(q, kT, v, gr)


def _last_in_group(x, group):
    rows = x.shape[0]
    x3 = jnp.broadcast_to(x, (rows, LANES)).reshape(rows // group, group, LANES)
    last = jnp.broadcast_to(x3[:, group - 1:group, :], x3.shape)
    return last.reshape(rows, LANES)[:, 0:1]


def _mlstm_sample_kernel(q_ref, kT_ref, k_ref, v_ref, gc_ref, gr_ref, m0_ref, c_ref, n_ref,
                         hh_ref, cout_ref, nout_ref, mt_ref, ni_ref, qn_ref, dec_ref, wk_ref, *, tt):
    h = pl.program_id(1)
    L = q_ref.shape[0]
    nseq = L // tt
    gc = gc_ref[...]
    gr = gr_ref[...]
    ri = lax.broadcasted_iota(jnp.int32, (L, L), 0)
    ci = lax.broadcasted_iota(jnp.int32, (L, L), 1)
    same = (ri // tt) == (ci // tt)
    mask = same & (ri >= ci)
    bcol = _cumsum_cols(mask.astype(BF16), gc)
    brow = _cumsum_rows(gr, (same & (ri <= ci)).astype(BF16))
    lane = lax.broadcasted_iota(jnp.int32, (L, GATE_LANES), 1)
    sub = lax.broadcasted_iota(jnp.int32, (SUBLANES, L), 0)
    pick_col = lambda arr, idx: jnp.sum(jnp.where(lane == idx, arr, 0.0), axis=-1, keepdims=True)
    pick_row = lambda arr, idx: jnp.sum(jnp.where(sub == idx, arr, 0.0), axis=0, keepdims=True)
    ig_c = pick_col(gc, h)
    b_c = pick_col(bcol, ML_HEADS + h)
    ig_r = pick_row(gr, h)
    b_r = pick_row(brow, ML_HEADS + h)
    m_prev = m0_ref[0]

    for j in range(nseq):
        rows = slice(j * tt, (j + 1) * tt)
        qj = q_ref[rows, :]
        ni_ref[rows, :] = _dot(qj, c_ref[j, 0].astype(BF16))
        nj = n_ref[j, 0].astype(BF16).astype(F32)
        qn = jnp.sum(qj.astype(F32) * nj, axis=-1, keepdims=True)
        qn_ref[rows, :] = jnp.broadcast_to(qn, (tt, LANES))

    q = q_ref[...]
    kT = kT_ref[...]
    v = v_ref[...]
    hh, m_t, a_c = _mlstm_core(q, kT, v, b_c, b_r, ig_r, mask, m_prev, ni_ref[...], qn_ref[:, 0:1])
    hh_ref[...] = hh
    m_new = _last_in_group(m_t, tt)
    decay = jnp.exp(_last_in_group(a_c, tt) - m_new)
    w_end = jnp.exp((_last_in_group(b_c, tt) - b_c) + ig_c - m_new)
    mt_ref[0] = jnp.broadcast_to(m_t, (L, LANES))
    dec_ref[...] = jnp.broadcast_to(decay, (L, LANES))
    wv = w_end * v.astype(F32)
    wk_ref[...] = w_end.astype(BF16).astype(F32) * k_ref[...].astype(F32)
    rowi = lax.broadcasted_iota(jnp.int32, (L, 1), 0)

    for j in range(nseq):
        rows = slice(j * tt, (j + 1) * tt)
        upd = _dot(kT, jnp.where((rowi // tt) == j, wv, 0.0).astype(BF16))
        dj = dec_ref[j * tt:j * tt + 1, 0:1]
        cout_ref[j, 0] = dj * c_ref[j, 0] + upd
        nout_ref[j, 0] = dj * n_ref[j, 0] + jnp.sum(wk_ref[rows, :], axis=0, keepdims=True)


def _mlstm_sample(q, kT, k, v, gc, gr, m0_tok, c0, n0, tt):
    ntok = q.shape[0]
    nseq = ntok // tt
    sb = SAMPLE_SEQ_BLOCK
    L = sb * tt
    qspec = pl.BlockSpec((L, ML_HEAD_DIM), lambda i, h: (i, h))
    cspec = pl.BlockSpec((sb, 1, ML_HEAD_DIM, ML_HEAD_DIM), lambda i, h: (i, h, 0, 0))
    nspec = pl.BlockSpec((sb, 1, 1, ML_HEAD_DIM), lambda i, h: (i, h, 0, 0))
    return pl.pallas_call(
        functools.partial(_mlstm_sample_kernel, tt=tt),
        grid=(nseq // sb, ML_HEADS),
        in_specs=[qspec, pl.BlockSpec((ML_HEAD_DIM, L), lambda i, h: (h, i)), qspec, qspec,
                  pl.BlockSpec((L, GATE_LANES), lambda i, h: (i, 0)),
                  pl.BlockSpec((SUBLANES, L), lambda i, h: (0, i)),
                  pl.BlockSpec((1, L, 1), lambda i, h: (h, i, 0)), cspec, nspec],
        out_specs=[qspec, cspec, nspec, pl.BlockSpec((1, L, LANES), lambda i, h: (h, i, 0))],
        out_shape=[jax.ShapeDtypeStruct((ntok, D_MODEL), F32),
                   jax.ShapeDtypeStruct((nseq, ML_HEADS, ML_HEAD_DIM, ML_HEAD_DIM), F32),
                   jax.ShapeDtypeStruct((nseq, ML_HEADS, 1, ML_HEAD_DIM), F32),
                   jax.ShapeDtypeStruct((ML_HEADS, ntok, LANES), F32)],
        scratch_shapes=[pltpu.VMEM((L, ML_HEAD_DIM), F32), pltpu.VMEM((L, LANES), F32),
                        pltpu.VMEM((L, LANES), F32), pltpu.VMEM((L, ML_HEAD_DIM), F32)],
        compiler_params=_params(("arbitrary", "arbitrary")),
        name="mlstm_sample",
    )(q, kT, k, v, gc, gr, m0_tok, c0, n0)


def _post_kernel(x_ref, sc_ref, sh_ref, gt_ref, g_ref, hh_ref, yp_ref, wt_ref, ghead_ref, wbp_ref, wbm_ref, wout_ref,
                 o_ref):
    x, hmod = _norm_mod(x_ref, sc_ref, sh_ref, g_ref)
    hb = hmod.astype(BF16)
    nb, tt, d = x_ref.shape
    o = _dot(hb, wt_ref[:, 0:D_MODEL])
    parts = []
    for h in range(ML_HEADS):
        hh = hh_ref[:, h * ML_HEAD_DIM:(h + 1) * ML_HEAD_DIM]
        parts.append(hh * lax.rsqrt(jnp.mean(hh * hh, axis=-1, keepdims=True) + EPS))
    yml = (jnp.concatenate(parts, axis=-1) * ghead_ref[...]) * jax.nn.sigmoid(o)
    gp = _dot(hb, wt_ref[:, D_MODEL:2 * D_MODEL])
    gm = _dot(hb, wt_ref[:, 2 * D_MODEL:3 * D_MODEL])
    merged = (jax.nn.sigmoid(gp) * _dot(yp_ref[...], wbp_ref[...])
              + jax.nn.sigmoid(gm) * _dot(yml.astype(BF16), wbm_ref[...]))
    y = _dot(merged.astype(BF16), wout_ref[...]).reshape(nb, tt, d)
    o_ref[...] = x_ref[...] + gt_ref[...] * y


def _tile_blocks(x, tile):
    g, t, _ = x.shape
    if t >= tile:
        nb, tt = 1, tile
    else:
        nb, tt = tile // t, t
    return nb, tt, (g // nb) * (t // tt), t // tt


def _post(x, sc, sh, gt, g_mix, hh, yp, w_tail, g_head, w_bp, w_bm, w_out):
    nb, tt, steps, per = _tile_blocks(x, TOKEN_TILE)
    tm = nb * tt
    xspec = pl.BlockSpec((nb, tt, D_MODEL), lambda i: (i // per, i % per, 0))
    mod = pl.BlockSpec((nb, 1, D_MODEL), lambda i: (i // per, 0, 0))
    row = lambda i: (i, 0)
    return pl.pallas_call(
        _post_kernel,
        grid=(steps,),
        in_specs=[xspec, mod, mod, mod, _const_spec((1, D_MODEL)),
                  pl.BlockSpec((tm, D_MODEL), row), pl.BlockSpec((tm, POOL_WIDTH), row),
                  _const_spec((D_MODEL, 3 * D_MODEL)), _const_spec((1, D_MODEL)),
                  _const_spec((POOL_WIDTH, D_MODEL)), _const_spec((D_MODEL, D_MODEL)),
                  _const_spec((D_MODEL, D_MODEL))],
        out_specs=xspec,
        out_shape=jax.ShapeDtypeStruct(x.shape, F32),
        compiler_params=_params(("arbitrary",)),
        name="post",
    )(x, sc, sh, gt, g_mix, hh, yp, w_tail, g_head, w_bp, w_bm, w_out)


_FF_SPLITS = ((0, 768), (768, 1536), (1536, 2304), (2304, D_FF))


def _ffn_kernel(x_ref, sc_ref, sh_ref, gt_ref, g_ref, gfin_ref, wgu_ref, wdn_ref, o_ref):
    _, hmod = _norm_mod(x_ref, sc_ref, sh_ref, g_ref)
    hb = hmod.astype(BF16)
    nb, tt, d = x_ref.shape
    dn = None
    for lo, hi in _FF_SPLITS:
        gate = _dot(hb, wgu_ref[:, lo:hi])
        up = _dot(hb, wgu_ref[:, D_FF + lo:D_FF + hi])
        act = (gate * jax.nn.sigmoid(gate) * up).astype(BF16)
        part = _dot(act, wdn_ref[lo:hi, :])
        dn = part if dn is None else dn + part
    x2 = x_ref[...] + gt_ref[...] * dn.reshape(nb, tt, d)
    ms = jnp.mean(x2 * x2, axis=-1, keepdims=True)
    o_ref[...] = x2 * lax.rsqrt(ms + EPS) * gfin_ref[...]


def _ffn(x, sc, sh, gt, g_ffn, g_final, w_gu, w_down):
    nb, tt, steps, per = _tile_blocks(x, FFN_TILE)
    xspec = pl.BlockSpec((nb, tt, D_MODEL), lambda i: (i // per, i % per, 0))
    mod = pl.BlockSpec((nb, 1, D_MODEL), lambda i: (i // per, 0, 0))
    return pl.pallas_call(
        _ffn_kernel,
        grid=(steps,),
        in_specs=[xspec, mod, mod, mod, _const_spec((1, D_MODEL)), _const_spec((1, D_MODEL)),
                  _const_spec((D_MODEL, 2 * D_FF)), _const_spec((D_FF, D_MODEL))],
        out_specs=xspec,
        out_shape=jax.ShapeDtypeStruct(x.shape, F32),
        compiler_params=_params(("arbitrary",)),
        name="ffn",
    )(x, sc, sh, gt, g_ffn, g_final, w_gu, w_down)


def kernel(x_prompt, x_sample, c_prompt, c_sample, state_pool, state_mlstm_c, state_mlstm_n, state_mlstm_m, g_mix, g_ffn, g_final, w_ada, b_ada, w_in, b_igate, b_fgate, w_pool, s_pool, g_head, w_branch_pool, w_branch_mlstm, w_out, w_gate_up, w_down):
    depth = w_in.shape[0]
    assert depth == 1, "single-layer trunk"
    nbatch, seq, _ = x_prompt.shape
    nseq, tt, _ = x_sample.shape
    l = 0

    bgate = jnp.pad(jnp.concatenate([b_igate[l], b_fgate[l]])[None, :], ((0, 0), (0, GATE_LANES - 2 * ML_HEADS)))
    w_pool_b = w_pool[l].astype(BF16)
    g_mix_r, g_ffn_r, g_fin_r = g_mix[l][None, :], g_ffn[l][None, :], g_final[None, :]
    s_pool_r, g_head_r = s_pool[l][None, :], g_head[l][None, :]

    w_head, w_tail, w_gate = _wprep(jnp.swapaxes(w_in, 1, 2))
    mod = _ada(jnp.concatenate([c_prompt, c_sample], axis=0), w_ada[l], b_ada[l][None, :])
    mod_p = [m[:, None, :] for m in jnp.split(mod[:nbatch], 6, axis=-1)]
    mod_s = [m[:, None, :] for m in jnp.split(mod[nbatch:], 6, axis=-1)]

    sh1, sc1, gt1, sh2, sc2, gt2 = mod_p
    (yp, q, kT, v, gc, gr, hist_p, w_bp, w_bm, w_o, w_gu, w_dn) = _inproj_prompt(
        x_prompt, sc1, sh1, g_mix_r, w_head, w_gate, bgate, w_pool_b, s_pool_r,
        (w_branch_pool, w_branch_mlstm, w_out, w_gate_up, w_down))
    hh, c_p, n_p, m_p = _mlstm_prompt(q.reshape(nbatch, seq, D_MODEL), kT, v.reshape(nbatch, seq, D_MODEL), gr)
    hh = hh.reshape(nbatch * seq, D_MODEL)
    x1 = _post(x_prompt, sc1, sh1, gt1, g_mix_r, hh, yp, w_tail, g_head_r, w_bp, w_bm, w_o)
    y_prompt = _ffn(x1, sc2, sh2, gt2, g_ffn_r, g_fin_r, w_gu, w_dn)

    sh1, sc1, gt1, sh2, sc2, gt2 = mod_s
    hist_pad = jnp.pad(state_pool[l], ((0, 0), (HIST_ROWS - POOL_HIST, 0), (0, 0)))
    yp, q, kT, v, gc, gr, k, hist_s = _inproj_sample(x_sample, sc1, sh1, g_mix_r, w_head, w_gate, bgate, w_pool_b,
                                                     s_pool_r, hist_pad)
    m0_tok = jnp.repeat(state_mlstm_m[l].astype(F32).T, tt, axis=1)[:, :, None]
    hh, c_s, n_s, mt = _mlstm_sample(q, kT, k, v, gc, gr, m0_tok, state_mlstm_c[l].astype(F32),
                                     state_mlstm_n[l].astype(F32)[:, :, None, :], tt)
    x1 = _post(x_sample, sc1, sh1, gt1, g_mix_r, hh, yp, w_tail, g_head_r, w_bp, w_bm, w_o)
    y_sample = _ffn(x1, sc2, sh2, gt2, g_ffn_r, g_fin_r, w_gu, w_dn)

    cd, nd, md = state_mlstm_c.dtype, state_mlstm_n.dtype, state_mlstm_m.dtype
    return (y_prompt, y_sample,
            hist_p[None, :, HIST_ROWS - POOL_HIST:, :],
            c_p.astype(cd)[None], n_p[..., 0].astype(nd)[None], m_p[:, :ML_HEADS, 0].astype(md)[None],
            hist_s[None, :, HIST_ROWS - POOL_HIST:, :].astype(state_pool.dtype),
            c_s.astype(cd)[None], n_s[:, :, 0, :].astype(nd)[None],
            mt[:, tt - 1::tt, 0].T.astype(md)[None])
```

```python
import functools

import jax
import jax.numpy as jnp
from jax import lax
from jax.experimental import pallas as pl
from jax.experimental.pallas import tpu as pltpu

D_MODEL = 1024
POOL_WINDOWS = (2, 4, 8, 16)
POOL_GROUP_DIM = 128
POOL_WIDTH = 512
POOL_HIST = 15
HIST_ROWS = 16
ML_HEADS = 4
ML_HEAD_DIM = 256
D_FF = 2816
EPS = 1e-6
M_INIT = -1e30
PAST_LEN = 16384
LANES = 128
SUBLANES = 8
BF16_ROWS = 16
GATE_LANES = LANES
AUG = ML_HEAD_DIM + LANES
VMEM_LIMIT = 56 * 1024 * 1024

TOKEN_TILE = 512
FFN_TILE = 1024
PROMPT_FFN_TILE = 512
PROMPT_CHUNK = 256
MLSTM_TILE = 1024
MLSTM_BATCHES = 1
SAMPLE_SEQ_BLOCK = 16

BF16 = jnp.bfloat16
F32 = jnp.float32

_U0, _Q0, _K0, _V0, _G0 = 0, 512, 1536, 2560, 3584


def _dot(a, b):
    return jnp.dot(a, b, preferred_element_type=F32)


def _const_spec(shape):
    zeros = (0,) * len(shape)
    return pl.BlockSpec(shape, lambda *_: zeros, pipeline_mode=pl.Buffered(1))


def _params(sem):
    return pltpu.CompilerParams(dimension_semantics=sem, vmem_limit_bytes=VMEM_LIMIT)


def _norm_mod(x_ref, sc_ref, sh_ref, g_ref):
    x = x_ref[...]
    nb, tt, d = x.shape
    ms = jnp.mean(x * x, axis=-1, keepdims=True)
    y = x * lax.rsqrt(ms + EPS) * g_ref[...]
    hmod = y * (1.0 + sc_ref[...]) + sh_ref[...]
    return x.reshape(nb * tt, d), hmod.reshape(nb * tt, d)


def _split3(x):
    hi = x.astype(BF16)
    r1 = x - hi.astype(F32)
    mid = r1.astype(BF16)
    lo = (r1 - mid.astype(F32)).astype(BF16)
    return hi, mid, lo


def _cumsum_cols(tri, x):
    hi, mid, lo = _split3(x)
    return _dot(tri, hi) + _dot(tri, mid) + _dot(tri, lo)


def _cumsum_rows(x, tri):
    hi, mid, lo = _split3(x)
    return _dot(hi, tri) + _dot(mid, tri) + _dot(lo, tri)


def _ada_kernel(c_ref, w_ref, b_ref, o_ref):
    c = c_ref[...]
    a = (c * jax.nn.sigmoid(c)).astype(BF16)
    o_ref[...] = _dot(a, w_ref[...].astype(BF16)) + b_ref[...]


def _ada(c_all, w_ada, b_ada):
    rows = c_all.shape[0]
    ncol = w_ada.shape[1]
    tn = 1024
    return pl.pallas_call(
        _ada_kernel,
        grid=(ncol // tn,),
        in_specs=[pl.BlockSpec((rows, D_MODEL), lambda j: (0, 0)),
                  pl.BlockSpec((D_MODEL, tn), lambda j: (0, j)),
                  pl.BlockSpec((1, tn), lambda j: (0, j))],
        out_specs=pl.BlockSpec((rows, tn), lambda j: (0, j)),
        out_shape=jax.ShapeDtypeStruct((rows, ncol), F32),
        compiler_params=_params(("arbitrary",)),
        name="ada",
    )(c_all, w_ada, b_ada)


_PREP_COLS = 512


def _wprep_kernel(wt_ref, nx_ref, gt_ref, head_ref, tail_ref, gate_ref, *, head_steps, shift_step, shift):
    j = pl.program_id(0)
    rows = wt_ref[0]
    shifted = jnp.concatenate([rows[shift:, :], nx_ref[0, 0:shift, :]], axis=0)
    blk = jnp.where(j >= shift_step, shifted, rows).T.astype(BF16)

    @pl.when(j < head_steps)
    def _():
        head_ref[...] = blk

    @pl.when(j >= head_steps)
    def _():
        tail_ref[...] = blk

    gates = gt_ref[0].T
    pad = jnp.zeros((gates.shape[0], GATE_LANES - gates.shape[1]), F32)
    gate_ref[...] = jnp.concatenate([gates, pad], axis=1).astype(BF16)


def _wprep(w_in_t):
    g0 = _G0 + D_MODEL
    p0 = g0 + 2 * ML_HEADS
    head_steps = _G0 // _PREP_COLS
    tail_steps = 3 * D_MODEL // _PREP_COLS
    shift_step = head_steps + D_MODEL // _PREP_COLS
    blk = (1, _PREP_COLS, D_MODEL)
    return pl.pallas_call(
        functools.partial(_wprep_kernel, head_steps=head_steps, shift_step=shift_step, shift=p0 - g0),
        grid=(head_steps + tail_steps,),
        in_specs=[pl.BlockSpec(blk, lambda j: (0, j, 0)),
                  pl.BlockSpec(blk, lambda j: (0, jnp.where(j >= shift_step, j + 1, 0), 0)),
                  pl.BlockSpec((1, 2 * ML_HEADS, D_MODEL), lambda j: (0, g0 // (2 * ML_HEADS), 0))],
        out_specs=[pl.BlockSpec((D_MODEL, _PREP_COLS), lambda j: (0, jnp.minimum(j, head_steps - 1))),
                   pl.BlockSpec((D_MODEL, _PREP_COLS), lambda j: (0, jnp.maximum(j - head_steps, 0))),
                   pl.BlockSpec((D_MODEL, GATE_LANES), lambda j: (0, 0))],
        out_shape=[jax.ShapeDtypeStruct((D_MODEL, _G0), BF16),
                   jax.ShapeDtypeStruct((D_MODEL, 3 * D_MODEL), BF16),
                   jax.ShapeDtypeStruct((D_MODEL, GATE_LANES), BF16)],
        compiler_params=_params(("arbitrary",)),
        name="wprep",
    )(w_in_t, w_in_t, w_in_t)


def _project(hb, w_ref, wg_ref, bg_ref, q_ref, kT_ref, v_ref, gc_ref, gr_ref):
    q_ref[...] = _dot(hb, w_ref[:, _Q0:_K0]).astype(BF16)
    k = _dot(hb, w_ref[:, _K0:_V0]) * (ML_HEAD_DIM ** -0.5)
    kT_ref[...] = k.T.astype(BF16)
    v_ref[...] = _dot(hb, w_ref[:, _V0:_G0]).astype(BF16)
    zg = _dot(hb, wg_ref[...]) + bg_ref[...]
    log_f = jnp.minimum(zg, 0.0) - jnp.log1p(jnp.exp(-jnp.abs(zg)))
    lane = lax.broadcasted_iota(jnp.int32, zg.shape, 1)
    gc = jnp.where(lane < ML_HEADS, zg, log_f)
    gc_ref[...] = gc
    gr_ref[...] = gc.T[0:SUBLANES, :]
    return k


def _pool_project(acc_fn, u, cnt_fn, wpool_ref, spool_ref):
    outs = []
    for g, w in enumerate(POOL_WINDOWS):
        cols = slice(g * POOL_GROUP_DIM, (g + 1) * POOL_GROUP_DIM)
        ug = u[:, cols]
        d = acc_fn(g, w, ug) / cnt_fn(w) - ug
        outs.append(_dot(d.astype(BF16), wpool_ref[g]) * spool_ref[:, cols])
    return jnp.concatenate(outs, axis=-1).astype(BF16)


def _inproj_prompt_kernel(x_ref, sc_ref, sh_ref, g_ref, w_ref, wg_ref, bg_ref, wpool_ref, spool_ref,
                          wbp_ref, wbm_ref, wout_ref, wgu_ref, wdn_ref,
                          yp_ref, q_ref, kT_ref, v_ref, gc_ref, gr_ref, hout_ref,
                          wbp_o, wbm_o, wout_o, wgu_o, wdn_o, ext_ref, *, tm):
    t = pl.program_id(1)

    @pl.when(t == 0)
    def _():
        ext_ref[0:HIST_ROWS, :] = jnp.zeros((HIST_ROWS, POOL_WIDTH), F32)

    for src, dst in ((wbp_ref, wbp_o), (wbm_ref, wbm_o), (wout_ref, wout_o), (wgu_ref, wgu_o), (wdn_ref, wdn_o)):
        dst[...] = src[0].astype(BF16)

    _, hmod = _norm_mod(x_ref, sc_ref, sh_ref, g_ref)
    hb = hmod.astype(BF16)
    _project(hb, w_ref, wg_ref, bg_ref, q_ref, kT_ref.at[0], v_ref, gc_ref, gr_ref.at[0])

    u = _dot(hb, w_ref[:, _U0:_Q0])
    ext_ref[HIST_ROWS:HIST_ROWS + tm, :] = u
    pos = t * tm + lax.broadcasted_iota(jnp.int32, (tm, 1), 0)

    def acc_fn(g, w, ug):
        acc = ug
        for j in range(1, w):
            acc = acc + ext_ref[pl.ds(HIST_ROWS - j, tm), g * POOL_GROUP_DIM:(g + 1) * POOL_GROUP_DIM]
        return acc

    def cnt_fn(w):
        return jnp.minimum(pos + 1, w).astype(F32)

    yp_ref[...] = _pool_project(acc_fn, u, cnt_fn, wpool_ref, spool_ref)
    last = ext_ref[tm:tm + HIST_ROWS, :]
    hout_ref[0] = last
    ext_ref[0:HIST_ROWS, :] = last


def _inproj_sample_kernel(x_ref, sc_ref, sh_ref, g_ref, w_ref, wg_ref, bg_ref, wpool_ref, spool_ref, hist_ref,
                          yp_ref, q_ref, kT_ref, v_ref, gc_ref, gr_ref, k_ref, hout_ref, *, pos0):
    _, hmod = _norm_mod(x_ref, sc_ref, sh_ref, g_ref)
    hb = hmod.astype(BF16)
    k_ref[...] = _project(hb, w_ref, wg_ref, bg_ref, q_ref, kT_ref, v_ref, gc_ref, gr_ref).astype(BF16)

    u = _dot(hb, w_ref[:, _U0:_Q0])
    nb, tt = x_ref.shape[0], x_ref.shape[1]
    u3 = u.reshape(nb, tt, POOL_WIDTH)
    h0 = hist_ref[:, 0:SUBLANES, :]
    h1 = hist_ref[:, SUBLANES:HIST_ROWS, :]
    tok = lax.broadcasted_iota(jnp.int32, (nb, tt, POOL_GROUP_DIM), 1)

    def acc_fn(g, w, ug):
        cols = slice(g * POOL_GROUP_DIM, (g + 1) * POOL_GROUP_DIM)
        new, mid, old = u3[:, :, cols], h1[:, :, cols], h0[:, :, cols]
        acc = new
        for j in range(1, w):
            if j < tt:
                term = jnp.where(tok >= j, pltpu.roll(new, j, 1), pltpu.roll(mid, j, 1))
            elif j == tt:
                term = mid
            else:
                term = jnp.where(tok >= j - tt, pltpu.roll(mid, j - tt, 1), pltpu.roll(old, j - tt, 1))
            acc = acc + term
        return acc.reshape(nb * tt, POOL_GROUP_DIM)

    pos = pos0 + lax.broadcasted_iota(jnp.int32, (nb, tt, 1), 1).reshape(nb * tt, 1)

    def cnt_fn(w):
        return jnp.minimum(pos + 1, w).astype(F32)

    yp_ref[...] = _pool_project(acc_fn, u, cnt_fn, wpool_ref, spool_ref)
    hout_ref[:, 0:SUBLANES, :] = h1
    hout_ref[:, SUBLANES:HIST_ROWS, :] = u3


def _cast_specs(weights, nt, steps):
    in_specs, out_specs, out_shapes = [], [], []
    for w in weights:
        _, r, c = w.shape
        n = steps
        while r % n or (r // n) % BF16_ROWS:
            n //= 2
        rows = r // n
        idx = lambda b, t, n=n: jnp.minimum(b * nt + t, n - 1)
        in_specs.append(pl.BlockSpec((1, rows, c), lambda b, t, idx=idx: (0, idx(b, t), 0)))
        out_specs.append(pl.BlockSpec((rows, c), lambda b, t, idx=idx: (idx(b, t), 0)))
        out_shapes.append(jax.ShapeDtypeStruct((r, c), BF16))
    return in_specs, out_specs, out_shapes


def _inproj_prompt(x, sc, sh, g_mix, w_head, w_gate, bgate, w_pool, s_pool, later_weights):
    nbatch, seq, _ = x.shape
    tm = TOKEN_TILE
    nt = seq // tm
    ntok = nbatch * seq
    row = lambda b, t: (b * nt + t, 0)
    mod = pl.BlockSpec((1, 1, D_MODEL), lambda b, t: (b, 0, 0))
    cast_in, cast_out, cast_shapes = _cast_specs(later_weights, nt, nbatch * nt)
    return pl.pallas_call(
        functools.partial(_inproj_prompt_kernel, tm=tm),
        grid=(nbatch, nt),
        in_specs=[pl.BlockSpec((1, tm, D_MODEL), lambda b, t: (b, t, 0)), mod, mod,
                  _const_spec((1, D_MODEL)), _const_spec((D_MODEL, _G0)), _const_spec((D_MODEL, GATE_LANES)),
                  _const_spec((1, GATE_LANES)), _const_spec((4, POOL_GROUP_DIM, POOL_GROUP_DIM)),
                  _const_spec((1, POOL_WIDTH))] + cast_in,
        out_specs=[pl.BlockSpec((tm, POOL_WIDTH), row), pl.BlockSpec((tm, D_MODEL), row),
                   pl.BlockSpec((1, D_MODEL, tm), lambda b, t: (b, 0, t)), pl.BlockSpec((tm, D_MODEL), row),
                   pl.BlockSpec((tm, GATE_LANES), row), pl.BlockSpec((1, SUBLANES, tm), lambda b, t: (b, 0, t)),
                   pl.BlockSpec((1, HIST_ROWS, POOL_WIDTH), lambda b, t: (b, 0, 0))] + cast_out,
        out_shape=[jax.ShapeDtypeStruct((ntok, POOL_WIDTH), BF16), jax.ShapeDtypeStruct((ntok, D_MODEL), BF16),
                   jax.ShapeDtypeStruct((nbatch, D_MODEL, seq), BF16), jax.ShapeDtypeStruct((ntok, D_MODEL), BF16),
                   jax.ShapeDtypeStruct((ntok, GATE_LANES), F32), jax.ShapeDtypeStruct((nbatch, SUBLANES, seq), F32),
                   jax.ShapeDtypeStruct((nbatch, HIST_ROWS, POOL_WIDTH), F32)] + cast_shapes,
        scratch_shapes=[pltpu.VMEM((tm + HIST_ROWS, POOL_WIDTH), F32)],
        compiler_params=_params(("arbitrary", "arbitrary")),
        name="inproj_prompt",
    )(x, sc, sh, g_mix, w_head, w_gate, bgate, w_pool, s_pool, *later_weights)


def _inproj_sample(x, sc, sh, g_mix, w_head, w_gate, bgate, w_pool, s_pool, hist_pad):
    nseq, tt, _ = x.shape
    nb = TOKEN_TILE // tt
    tm = nb * tt
    ntok = nseq * tt
    row = lambda i: (i, 0)
    col = lambda i: (0, i)
    mod = pl.BlockSpec((nb, 1, D_MODEL), lambda i: (i, 0, 0))
    hist = pl.BlockSpec((nb, HIST_ROWS, POOL_WIDTH), lambda i: (i, 0, 0))
    return pl.pallas_call(
        functools.partial(_inproj_sample_kernel, pos0=PAST_LEN),
        grid=(nseq // nb,),
        in_specs=[pl.BlockSpec((nb, tt, D_MODEL), lambda i: (i, 0, 0)), mod, mod,
                  _const_spec((1, D_MODEL)), _const_spec((D_MODEL, _G0)), _const_spec((D_MODEL, GATE_LANES)),
                  _const_spec((1, GATE_LANES)), _const_spec((4, POOL_GROUP_DIM, POOL_GROUP_DIM)),
                  _const_spec((1, POOL_WIDTH)), hist],
        out_specs=[pl.BlockSpec((tm, POOL_WIDTH), row), pl.BlockSpec((tm, D_MODEL), row),
                   pl.BlockSpec((D_MODEL, tm), col), pl.BlockSpec((tm, D_MODEL), row),
                   pl.BlockSpec((tm, GATE_LANES), row), pl.BlockSpec((SUBLANES, tm), col),
                   pl.BlockSpec((tm, D_MODEL), row), hist],
        out_shape=[jax.ShapeDtypeStruct((ntok, POOL_WIDTH), BF16), jax.ShapeDtypeStruct((ntok, D_MODEL), BF16),
                   jax.ShapeDtypeStruct((D_MODEL, ntok), BF16), jax.ShapeDtypeStruct((ntok, D_MODEL), BF16),
                   jax.ShapeDtypeStruct((ntok, GATE_LANES), F32), jax.ShapeDtypeStruct((SUBLANES, ntok), F32),
                   jax.ShapeDtypeStruct((ntok, D_MODEL), BF16),
                   jax.ShapeDtypeStruct((nseq, HIST_ROWS, POOL_WIDTH), F32)],
        compiler_params=_params(("arbitrary",)),
        name="inproj_sample",
    )(x, sc, sh, g_mix, w_head, w_gate, bgate, w_pool, s_pool, hist_pad)


def _mlstm_core(q, kT, v, b_c, b_r, ig_r, mask, m_prev, num_inter, qn):
    logd = jnp.where(mask, (b_c - b_r) + ig_r, -jnp.inf)
    a_c = b_c + m_prev
    m_t = jnp.maximum(a_c, jnp.max(logd, axis=-1, keepdims=True))
    w_intra = jnp.exp(logd - m_t)
    w_inter = jnp.exp(a_c - m_t)
    s = _dot(q, kT) * w_intra
    num = w_inter * num_inter + _dot(s.astype(BF16), v)
    den = w_inter * qn + jnp.sum(s, axis=-1, keepdims=True)
    hh = num / jnp.maximum(jnp.abs(den), jnp.exp(-m_t))
    return hh, m_t, a_c


def _mlstm_chunk(q_ref, kT_ref, v_ref, gr_ref, hh_ref, cn_ref, m_ref, r0, chunk):
    rows = slice(r0, r0 + chunk)
    gr = gr_ref[:, rows]
    ri = lax.broadcasted_iota(jnp.int32, (chunk, chunk), 0)
    ci = lax.broadcasted_iota(jnp.int32, (chunk, chunk), 1)
    causal = ri >= ci
    brow = _cumsum_rows(gr, (ri <= ci).astype(BF16))
    bcol = brow.T
    ones = jnp.ones((chunk, LANES), BF16)
    wide = lambda x: jnp.concatenate([x, x], axis=-1)
    for h in range(ML_HEADS):
        hs = slice(h * ML_HEAD_DIM, (h + 1) * ML_HEAD_DIM)
        b_r = brow[ML_HEADS + h:ML_HEADS + h + 1, :]
        g_r = gr[h:h + 1, :] - b_r
        m_prev = m_ref[h:h + 1, 0:1]
        cn = cn_ref[h]
        q = q_ref[rows, hs]
        kT = kT_ref[hs, rows]
        vaug = jnp.concatenate([v_ref[rows, hs], ones], axis=-1)
        gm = jnp.where(causal, g_r, -jnp.inf)
        big_m = jnp.maximum(m_prev, jnp.max(gm, axis=-1, keepdims=True))
        m_rep = jnp.broadcast_to(big_m, (chunk, LANES))
        mt_rep = jnp.broadcast_to(bcol[:, ML_HEADS + h:ML_HEADS + h + 1] + big_m, (chunk, LANES))
        s = _dot(q, kT) * jnp.exp(gm - wide(m_rep))
        b_last = b_r[:, chunk - 1:chunk]
        m_new = b_last + jnp.maximum(m_prev, jnp.max(g_r, axis=-1, keepdims=True))
        decay = jnp.exp((b_last + m_prev) - m_new)
        w_end = jnp.exp((g_r + b_last) - m_new)
        both = _dot(jnp.concatenate([s.astype(BF16), (kT.astype(F32) * w_end).astype(BF16)], axis=0), vaug)
        sva = both[0:chunk]
        qc = _dot(q, cn.astype(BF16))
        w_inter = jnp.exp(m_prev - m_rep)
        den = w_inter * qc[:, ML_HEAD_DIM:] + sva[:, ML_HEAD_DIM:]
        rinv = 1.0 / jnp.maximum(jnp.abs(den), jnp.exp(-mt_rep))
        hh_ref[rows, hs] = (wide(w_inter) * qc[:, :ML_HEAD_DIM] + sva[:, :ML_HEAD_DIM]) * wide(rinv)
        cn_ref[h] = decay * cn + both[chunk:]
        m_ref[h:h + 1, :] = jnp.broadcast_to(m_new, (1, LANES))


def _mlstm_prompt_kernel(q_ref, kT_ref, v_ref, gr_ref, hh_ref, cout_ref, nout_ref, mout_ref, cn_ref, m_ref, *, chunk):
    t = pl.program_id(1)
    nb = q_ref.shape[0]

    @pl.when(t == 0)
    def _():
        cn_ref[...] = jnp.zeros(cn_ref.shape, F32)
        m_ref[...] = jnp.full(m_ref.shape, M_INIT, F32)

    for r0 in range(0, q_ref.shape[1], chunk):
        for s in range(nb):
            _mlstm_chunk(q_ref.at[s], kT_ref.at[s], v_ref.at[s], gr_ref.at[s], hh_ref.at[s], cn_ref.at[s],
                         m_ref.at[s], r0, chunk)

    @pl.when(t == pl.num_programs(1) - 1)
    def _():
        cout_ref[...] = cn_ref[:, :, :, 0:ML_HEAD_DIM]
        nout_ref[...] = cn_ref[:, :, :, ML_HEAD_DIM:AUG]
        mout_ref[...] = m_ref[...]


def _mlstm_prompt(q, kT, v, gr):
    nbatch, seq, _ = q.shape
    tm, nb = MLSTM_TILE, MLSTM_BATCHES
    row = pl.BlockSpec((nb, tm, D_MODEL), lambda b, t: (b, t, 0))
    state = lambda last: pl.BlockSpec((nb, ML_HEADS, ML_HEAD_DIM, last), lambda b, t: (b, 0, 0, 0))
    return pl.pallas_call(
        functools.partial(_mlstm_prompt_kernel, chunk=PROMPT_CHUNK),
        grid=(nbatch // nb, seq // tm),
        in_specs=[row, pl.BlockSpec((nb, D_MODEL, tm), lambda b, t: (b, 0, t)), row,
                  pl.BlockSpec((nb, SUBLANES, tm), lambda b, t: (b, 0, t))],
        out_specs=[row, state(ML_HEAD_DIM), state(LANES), pl.BlockSpec((nb, SUBLANES, LANES), lambda b, t: (b, 0, 0))],
        out_shape=[jax.ShapeDtypeStruct((nbatch, seq, D_MODEL), F32),
                   jax.ShapeDtypeStruct((nbatch, ML_HEADS, ML_HEAD_DIM, ML_HEAD_DIM), F32),
                   jax.ShapeDtypeStruct((nbatch, ML_HEADS, ML_HEAD_DIM, LANES), F32),
                   jax.ShapeDtypeStruct((nbatch, SUBLANES, LANES), F32)],
        scratch_shapes=[pltpu.VMEM((nb, ML_HEADS, ML_HEAD_DIM, AUG), F32), pltpu.VMEM((nb, SUBLANES, LANES), F32)],
        compiler_params=_params(("arbitrary", "arbitrary")),
        name="mlstm_prompt",
    )(q, kT, v, gr)


def _last_in_group(x, group):
    rows = x.shape[0]
    x3 = jnp.broadcast_to(x, (rows, LANES)).reshape(rows // group, group, LANES)
    last = jnp.broadcast_to(x3[:, group - 1:group, :], x3.shape)
    return last.reshape(rows, LANES)[:, 0:1]


def _mlstm_sample_body(h, q_ref, kT_ref, k_ref, v_ref, gc_ref, gr_ref, m0_ref, c_ref, n_ref,
                       hh_ref, cout_ref, nout_ref, mt_ref, ni_ref, qn_ref, dec_ref, wk_ref, *, tt):
    L = q_ref.shape[0]
    nseq = L // tt
    gc = gc_ref[...]
    gr = gr_ref[...]
    ri = lax.broadcasted_iota(jnp.int32, (L, L), 0)
    ci = lax.broadcasted_iota(jnp.int32, (L, L), 1)
    same = (ri // tt) == (ci // tt)
    mask = same & (ri >= ci)
    bcol = _cumsum_cols(mask.astype(BF16), gc)
    brow = _cumsum_rows(gr, (same & (ri <= ci)).astype(BF16))
    lane = lax.broadcasted_iota(jnp.int32, (L, GATE_LANES), 1)
    sub = lax.broadcasted_iota(jnp.int32, (SUBLANES, L), 0)
    pick_col = lambda arr, idx: jnp.sum(jnp.where(lane == idx, arr, 0.0), axis=-1, keepdims=True)
    pick_row = lambda arr, idx: jnp.sum(jnp.where(sub == idx, arr, 0.0), axis=0, keepdims=True)
    ig_c = pick_col(gc, h)
    b_c = pick_col(bcol, ML_HEADS + h)
    ig_r = pick_row(gr, h)
    b_r = pick_row(brow, ML_HEADS + h)
    m_prev = m0_ref[0]

    for j in range(nseq):
        rows = slice(j * tt, (j + 1) * tt)
        qj = q_ref[rows, :]
        ni_ref[rows, :] = _dot(qj, c_ref[j, 0].astype(BF16))
        nj = n_ref[j, 0].astype(BF16).astype(F32)
        qn = jnp.sum(qj.astype(F32) * nj, axis=-1, keepdims=True)
        qn_ref[rows, :] = jnp.broadcast_to(qn, (tt, LANES))

    q = q_ref[...]
    kT = kT_ref[...]
    v = v_ref[...]
    hh, m_t, a_c = _mlstm_core(q, kT, v, b_c, b_r, ig_r, mask, m_prev, ni_ref[...], qn_ref[:, 0:1])
    hh_ref[...] = hh
    m_new = _last_in_group(m_t, tt)
    decay = jnp.exp(_last_in_group(a_c, tt) - m_new)
    w_end = jnp.exp((_last_in_group(b_c, tt) - b_c) + ig_c - m_new)
    mt_ref[0] = jnp.broadcast_to(m_t, (L, LANES))
    dec_ref[...] = jnp.broadcast_to(decay, (L, LANES))
    wv = w_end * v.astype(F32)
    wk_ref[...] = w_end.astype(BF16).astype(F32) * k_ref[...].astype(F32)
    rowi = lax.broadcasted_iota(jnp.int32, (L, 1), 0)

    for j in range(nseq):
        rows = slice(j * tt, (j + 1) * tt)
        upd = _dot(kT, jnp.where((rowi // tt) == j, wv, 0.0).astype(BF16))
        dj = dec_ref[j * tt:j * tt + 1, 0:1]
        cout_ref[j, 0] = dj * c_ref[j, 0] + upd
        nout_ref[j, 0] = dj * n_ref[j, 0] + jnp.sum(wk_ref[rows, :], axis=0, keepdims=True)


def _mlstm_sample_specs(ntok, tt):
    nseq = ntok // tt
    sb = SAMPLE_SEQ_BLOCK
    L = sb * tt
    nh = ML_HEADS
    qspec = pl.BlockSpec((L, ML_HEAD_DIM), lambda i: (i // nh, i % nh))
    cspec = pl.BlockSpec((sb, 1, ML_HEAD_DIM, ML_HEAD_DIM), lambda i: (i // nh, i % nh, 0, 0))
    nspec = pl.BlockSpec((sb, 1, 1, ML_HEAD_DIM), lambda i: (i // nh, i % nh, 0, 0))
    in_specs = [qspec, pl.BlockSpec((ML_HEAD_DIM, L), lambda i: (i % nh, i // nh)), qspec, qspec,
                pl.BlockSpec((L, GATE_LANES), lambda i: (i // nh, 0)),
                pl.BlockSpec((SUBLANES, L), lambda i: (0, i // nh)),
                pl.BlockSpec((1, L, 1), lambda i: (i % nh, i // nh, 0)), cspec, nspec]
    out_specs = [qspec, cspec, nspec, pl.BlockSpec((1, L, LANES), lambda i: (i % nh, i // nh, 0))]
    out_shapes = [jax.ShapeDtypeStruct((ntok, D_MODEL), F32),
                  jax.ShapeDtypeStruct((nseq, ML_HEADS, ML_HEAD_DIM, ML_HEAD_DIM), F32),
                  jax.ShapeDtypeStruct((nseq, ML_HEADS, 1, ML_HEAD_DIM), F32),
                  jax.ShapeDtypeStruct((ML_HEADS, ntok, LANES), F32)]
    scratch = [pltpu.VMEM((L, ML_HEAD_DIM), F32), pltpu.VMEM((L, LANES), F32),
               pltpu.VMEM((L, LANES), F32), pltpu.VMEM((L, ML_HEAD_DIM), F32)]
    return (nseq // sb) * nh, in_specs, out_specs, out_shapes, scratch


def _post_kernel(x_ref, sc_ref, sh_ref, gt_ref, g_ref, hh_ref, yp_ref, wt_ref, ghead_ref, wbp_ref, wbm_ref, wout_ref,
                 o_ref):
    x, hmod = _norm_mod(x_ref, sc_ref, sh_ref, g_ref)
    hb = hmod.astype(BF16)
    nb, tt, d = x_ref.shape
    o = _dot(hb, wt_ref[:, 0:D_MODEL])
    parts = []
    for h in range(ML_HEADS):
        hh = hh_ref[:, h * ML_HEAD_DIM:(h + 1) * ML_HEAD_DIM]
        parts.append(hh * lax.rsqrt(jnp.mean(hh * hh, axis=-1, keepdims=True) + EPS))
    yml = (jnp.concatenate(parts, axis=-1) * ghead_ref[...]) * jax.nn.sigmoid(o)
    gp = _dot(hb, wt_ref[:, D_MODEL:2 * D_MODEL])
    gm = _dot(hb, wt_ref[:, 2 * D_MODEL:3 * D_MODEL])
    merged = (jax.nn.sigmoid(gp) * _dot(yp_ref[...], wbp_ref[...])
              + jax.nn.sigmoid(gm) * _dot(yml.astype(BF16), wbm_ref[...]))
    y = _dot(merged.astype(BF16), wout_ref[...]).reshape(nb, tt, d)
    o_ref[...] = x_ref[...] + gt_ref[...] * y


def _tile_blocks(x, tile):
    g, t, _ = x.shape
    if t >= tile:
        nb, tt = 1, tile
    else:
        nb, tt = tile // t, t
    return nb, tt, (g // nb) * (t // tt), t // tt


def _post(x, sc, sh, gt, g_mix, hh, yp, w_tail, g_head, w_bp, w_bm, w_out):
    nb, tt, steps, per = _tile_blocks(x, TOKEN_TILE)
    tm = nb * tt
    xspec = pl.BlockSpec((nb, tt, D_MODEL), lambda i: (i // per, i % per, 0))
    mod = pl.BlockSpec((nb, 1, D_MODEL), lambda i: (i // per, 0, 0))
    row = lambda i: (i, 0)
    return pl.pallas_call(
        _post_kernel,
        grid=(steps,),
        in_specs=[xspec, mod, mod, mod, _const_spec((1, D_MODEL)),
                  pl.BlockSpec((tm, D_MODEL), row), pl.BlockSpec((tm, POOL_WIDTH), row),
                  _const_spec((D_MODEL, 3 * D_MODEL)), _const_spec((1, D_MODEL)),
                  _const_spec((POOL_WIDTH, D_MODEL)), _const_spec((D_MODEL, D_MODEL)),
                  _const_spec((D_MODEL, D_MODEL))],
        out_specs=xspec,
        out_shape=jax.ShapeDtypeStruct(x.shape, F32),
        compiler_params=_params(("arbitrary",)),
        name="post",
    )(x, sc, sh, gt, g_mix, hh, yp, w_tail, g_head, w_bp, w_bm, w_out)


_FF_SPLITS = ((0, 768), (768, 1536), (1536, 2304), (2304, D_FF))


def _ffn_kernel(x_ref, sc_ref, sh_ref, gt_ref, g_ref, gfin_ref, wgu_ref, wdn_ref, o_ref):
    _, hmod = _norm_mod(x_ref, sc_ref, sh_ref, g_ref)
    hb = hmod.astype(BF16)
    nb, tt, d = x_ref.shape
    dn = None
    for lo, hi in _FF_SPLITS:
        gate = _dot(hb, wgu_ref[:, lo:hi])
        up = _dot(hb, wgu_ref[:, D_FF + lo:D_FF + hi])
        act = (gate * jax.nn.sigmoid(gate) * up).astype(BF16)
        part = _dot(act, wdn_ref[lo:hi, :])
        dn = part if dn is None else dn + part
    x2 = x_ref[...] + gt_ref[...] * dn.reshape(nb, tt, d)
    ms = jnp.mean(x2 * x2, axis=-1, keepdims=True)
    o_ref[...] = x2 * lax.rsqrt(ms + EPS) * gfin_ref[...]


_N_FFN_IN = 8


def _ffn_mlstm_kernel(*refs, n_ml_in, tt):
    ffn_in = refs[:_N_FFN_IN]
    ml_in = refs[_N_FFN_IN:_N_FFN_IN + n_ml_in]
    o_ref = refs[_N_FFN_IN + n_ml_in]
    ml_rest = refs[_N_FFN_IN + n_ml_in + 1:]
    _mlstm_sample_body(pl.program_id(0) % ML_HEADS, *ml_in, *ml_rest, tt=tt)
    _ffn_kernel(*ffn_in, o_ref)


def _ffn(x, sc, sh, gt, g_ffn, g_final, w_gu, w_down, tile, mlstm_sample=None):
    nb, tt, steps, per = _tile_blocks(x, tile)
    xspec = pl.BlockSpec((nb, tt, D_MODEL), lambda i: (i // per, i % per, 0))
    mod = pl.BlockSpec((nb, 1, D_MODEL), lambda i: (i // per, 0, 0))
    in_specs = [xspec, mod, mod, mod, _const_spec((1, D_MODEL)), _const_spec((1, D_MODEL)),
                _const_spec((D_MODEL, 2 * D_FF)), _const_spec((D_FF, D_MODEL))]
    operands = (x, sc, sh, gt, g_ffn, g_final, w_gu, w_down)
    if mlstm_sample is None:
        return pl.pallas_call(
            _ffn_kernel,
            grid=(steps,),
            in_specs=in_specs,
            out_specs=xspec,
            out_shape=jax.ShapeDtypeStruct(x.shape, F32),
            compiler_params=_params(("arbitrary",)),
            name="ffn",
        )(*operands)
    ml_operands, ml_tt = mlstm_sample
    ml_steps, ml_in, ml_out, ml_shapes, ml_scratch = _mlstm_sample_specs(ml_operands[0].shape[0], ml_tt)
    assert ml_steps == steps, "one sample mLSTM step per FFN tile"
    return pl.pallas_call(
        functools.partial(_ffn_mlstm_kernel, n_ml_in=len(ml_in), tt=ml_tt),
        grid=(steps,),
        in_specs=in_specs + ml_in,
        out_specs=[xspec] + ml_out,
        out_shape=[jax.ShapeDtypeStruct(x.shape, F32)] + ml_shapes,
        scratch_shapes=ml_scratch,
        compiler_params=_params(("arbitrary",)),
        name="ffn_mlstm",
    )(*operands, *ml_operands)


def kernel(x_prompt, x_sample, c_prompt, c_sample, state_pool, state_mlstm_c, state_mlstm_n, state_mlstm_m, g_mix, g_ffn, g_final, w_ada, b_ada, w_in, b_igate, b_fgate, w_pool, s_pool, g_head, w_branch_pool, w_branch_mlstm, w_out, w_gate_up, w_down):
    depth = w_in.shape[0]
    assert depth == 1, "single-layer trunk"
    nbatch, seq, _ = x_prompt.shape
    nseq, tt, _ = x_sample.shape
    l = 0

    bgate = jnp.pad(jnp.concatenate([b_igate[l], b_fgate[l]])[None, :], ((0, 0), (0, GATE_LANES - 2 * ML_HEADS)))
    w_pool_b = w_pool[l].astype(BF16)
    g_mix_r, g_ffn_r, g_fin_r = g_mix[l][None, :], g_ffn[l][None, :], g_final[None, :]
    s_pool_r, g_head_r = s_pool[l][None, :], g_head[l][None, :]

    w_head, w_tail, w_gate = _wprep(jnp.swapaxes(w_in, 1, 2))
    mod = _ada(jnp.concatenate([c_prompt, c_sample], axis=0), w_ada[l], b_ada[l][None, :])
    mod_p = [m[:, None, :] for m in jnp.split(mod[:nbatch], 6, axis=-1)]
    mod_s = [m[:, None, :] for m in jnp.split(mod[nbatch:], 6, axis=-1)]

    sh1, sc1, gt1, sh2, sc2, gt2 = mod_p
    (yp, q, kT, v, gc, gr, hist_p, w_bp, w_bm, w_o, w_gu, w_dn) = _inproj_prompt(
        x_prompt, sc1, sh1, g_mix_r, w_head, w_gate, bgate, w_pool_b, s_pool_r,
        (w_branch_pool, w_branch_mlstm, w_out, w_gate_up, w_down))
    hh, c_p, n_p, m_p = _mlstm_prompt(q.reshape(nbatch, seq, D_MODEL), kT, v.reshape(nbatch, seq, D_MODEL), gr)
    hh = hh.reshape(nbatch * seq, D_MODEL)
    x1 = _post(x_prompt, sc1, sh1, gt1, g_mix_r, hh, yp, w_tail, g_head_r, w_bp, w_bm, w_o)

    s_sh1, s_sc1, s_gt1, s_sh2, s_sc2, s_gt2 = mod_s
    hist_pad = jnp.pad(state_pool[l], ((0, 0), (HIST_ROWS - POOL_HIST, 0), (0, 0)))
    yp, q, kT, v, gc, gr, k, hist_s = _inproj_sample(x_sample, s_sc1, s_sh1, g_mix_r, w_head, w_gate, bgate, w_pool_b,
                                                     s_pool_r, hist_pad)
    m0_tok = jnp.repeat(state_mlstm_m[l].astype(F32).T, tt, axis=1)[:, :, None]
    ml_operands = (q, kT, k, v, gc, gr, m0_tok, state_mlstm_c[l].astype(F32),
                   state_mlstm_n[l].astype(F32)[:, :, None, :])
    y_prompt, hh, c_s, n_s, mt = _ffn(x1, sc2, sh2, gt2, g_ffn_r, g_fin_r, w_gu, w_dn, PROMPT_FFN_TILE,
                                      (ml_operands, tt))

    x1 = _post(x_sample, s_sc1, s_sh1, s_gt1, g_mix_r, hh, yp, w_tail, g_head_r, w_bp, w_bm, w_o)
    y_sample = _ffn(x1, s_sc2, s_sh2, s_gt2, g_ffn_r, g_fin_r, w_gu, w_dn, FFN_TILE)

    cd, nd, md = state_mlstm_c.dtype, state_mlstm_n.dtype, state_mlstm_m.dtype
    return (y_prompt, y_sample,
            hist_p[None, :, HIST_ROWS - POOL_HIST:, :],
            c_p.astype(cd)[None], n_p[..., 0].astype(nd)[None], m_p[:, :ML_HEADS, 0].astype(md)[None],
            hist_s[None, :, HIST_ROWS - POOL_HIST:, :].astype(state_pool.dtype),
            c_s.astype(cd)[None], n_s[:, :, 0, :].astype(nd)[None],
            mt[:, tt - 1::tt, 0].T.astype(md)[None])
```

```python
import functools

import jax
import jax.numpy as jnp
from jax import lax
from jax.experimental import pallas as pl
from jax.experimental.pallas import tpu as pltpu

D_MODEL = 1024
POOL_WINDOWS = (2, 4, 8, 16)
POOL_GROUP_DIM = 128
POOL_WIDTH = 512
POOL_HIST = 15
HIST_ROWS = 16
ML_HEADS = 4
ML_HEAD_DIM = 256
D_FF = 2816
EPS = 1e-6
M_INIT = -1e30
PAST_LEN = 16384
LANES = 128
SUBLANES = 8
BF16_ROWS = 16
GATE_LANES = LANES
AUG = ML_HEAD_DIM + LANES
VMEM_LIMIT = 56 * 1024 * 1024

TOKEN_TILE = 512
FFN_TILE = 1024
PROMPT_FFN_TILE = 512
PROMPT_CHUNK = 256
MLSTM_TILE = 1024
MLSTM_BATCHES = 1
SAMPLE_SEQ_BLOCK = 16

BF16 = jnp.bfloat16
F32 = jnp.float32

_U0, _Q0, _K0, _V0, _G0 = 0, 512, 1536, 2560, 3584


def _dot(a, b):
    return jnp.dot(a, b, preferred_element_type=F32)


def _const_spec(shape):
    zeros = (0,) * len(shape)
    return pl.BlockSpec(shape, lambda *_: zeros, pipeline_mode=pl.Buffered(1))


def _params(sem):
    return pltpu.CompilerParams(dimension_semantics=sem, vmem_limit_bytes=VMEM_LIMIT)


def _norm_mod(x_ref, sc_ref, sh_ref, g_ref):
    x = x_ref[...]
    nb, tt, d = x.shape
    ms = jnp.mean(x * x, axis=-1, keepdims=True)
    y = x * lax.rsqrt(ms + EPS) * g_ref[...]
    hmod = y * (1.0 + sc_ref[0]) + sh_ref[0]
    return x.reshape(nb * tt, d), hmod.reshape(nb * tt, d)


def _split3(x):
    hi = x.astype(BF16)
    r1 = x - hi.astype(F32)
    mid = r1.astype(BF16)
    lo = (r1 - mid.astype(F32)).astype(BF16)
    return hi, mid, lo


def _cumsum_cols(tri, x):
    hi, mid, lo = _split3(x)
    return _dot(tri, hi) + _dot(tri, mid) + _dot(tri, lo)


def _cumsum_rows(x, tri):
    hi, mid, lo = _split3(x)
    return _dot(hi, tri) + _dot(mid, tri) + _dot(lo, tri)


def _ada_kernel(cs_ref, cp_ref, w_ref, b_ref, o_ref):
    c = jnp.concatenate([cs_ref[...], cp_ref[...]], axis=0)
    a = (c * jax.nn.sigmoid(c)).astype(BF16)
    res = _dot(a, w_ref[...].astype(BF16)) + b_ref[...]
    o_ref[0] = res.reshape(res.shape[0], 1, res.shape[1])


def _ada(c_sample, c_prompt, w_ada, b_ada):
    rows = c_sample.shape[0] + c_prompt.shape[0]
    nchunk = w_ada.shape[1] // D_MODEL
    return pl.pallas_call(
        _ada_kernel,
        grid=(nchunk,),
        in_specs=[pl.BlockSpec(c_sample.shape, lambda j: (0, 0)),
                  pl.BlockSpec(c_prompt.shape, lambda j: (0, 0)),
                  pl.BlockSpec((D_MODEL, D_MODEL), lambda j: (0, j)),
                  pl.BlockSpec((1, D_MODEL), lambda j: (0, j))],
        out_specs=pl.BlockSpec((1, rows, 1, D_MODEL), lambda j: (j, 0, 0, 0)),
        out_shape=jax.ShapeDtypeStruct((nchunk, rows, 1, D_MODEL), F32),
        compiler_params=_params(("arbitrary",)),
        name="ada",
    )(c_sample, c_prompt, w_ada, b_ada)


_SH1, _SC1, _GT1, _SH2, _SC2, _GT2 = range(6)


def _mod_spec(k, nb, row0, block_index):
    return pl.BlockSpec((1, nb, 1, D_MODEL), lambda *g: (k, row0 // nb + block_index(*g), 0, 0))


_PREP_COLS = 512


def _wprep_kernel(wt_ref, nx_ref, gt_ref, wp_ref, head_ref, tail_ref, gate_ref, pool_ref, *, head_steps, shift_step,
                  shift):
    j = pl.program_id(0)
    rows = wt_ref[0]
    shifted = jnp.concatenate([rows[shift:, :], nx_ref[0, 0:shift, :]], axis=0)
    blk = jnp.where(j >= shift_step, shifted, rows).T.astype(BF16)

    @pl.when(j < head_steps)
    def _():
        head_ref[...] = blk

    @pl.when(j >= head_steps)
    def _():
        tail_ref[...] = blk

    gates = gt_ref[0].T
    pad = jnp.zeros((gates.shape[0], GATE_LANES - gates.shape[1]), F32)
    gate_ref[...] = jnp.concatenate([gates, pad], axis=1).astype(BF16)
    pool_ref[...] = wp_ref[0].astype(BF16)


def _wprep(w_in_t, w_pool):
    g0 = _G0 + D_MODEL
    p0 = g0 + 2 * ML_HEADS
    head_steps = _G0 // _PREP_COLS
    tail_steps = 3 * D_MODEL // _PREP_COLS
    shift_step = head_steps + D_MODEL // _PREP_COLS
    blk = (1, _PREP_COLS, D_MODEL)
    return pl.pallas_call(
        functools.partial(_wprep_kernel, head_steps=head_steps, shift_step=shift_step, shift=p0 - g0),
        grid=(head_steps + tail_steps,),
        in_specs=[pl.BlockSpec(blk, lambda j: (0, j, 0)),
                  pl.BlockSpec(blk, lambda j: (0, jnp.where(j >= shift_step, j + 1, 0), 0)),
                  pl.BlockSpec((1, 2 * ML_HEADS, D_MODEL), lambda j: (0, g0 // (2 * ML_HEADS), 0)),
                  pl.BlockSpec(w_pool.shape, lambda j: (0, 0, 0, 0))],
        out_specs=[pl.BlockSpec((D_MODEL, _PREP_COLS), lambda j: (0, jnp.minimum(j, head_steps - 1))),
                   pl.BlockSpec((D_MODEL, _PREP_COLS), lambda j: (0, jnp.maximum(j - head_steps, 0))),
                   pl.BlockSpec((D_MODEL, GATE_LANES), lambda j: (0, 0)),
                   pl.BlockSpec(w_pool.shape[1:], lambda j: (0, 0, 0))],
        out_shape=[jax.ShapeDtypeStruct((D_MODEL, _G0), BF16),
                   jax.ShapeDtypeStruct((D_MODEL, 3 * D_MODEL), BF16),
                   jax.ShapeDtypeStruct((D_MODEL, GATE_LANES), BF16),
                   jax.ShapeDtypeStruct(w_pool.shape[1:], BF16)],
        compiler_params=_params(("arbitrary",)),
        name="wprep",
    )(w_in_t, w_in_t, w_in_t, w_pool)


def _project(hb, w_ref, wg_ref, bg_ref, q_ref, kT_ref, v_ref, gc_ref, gr_ref):
    q_ref[...] = _dot(hb, w_ref[:, _Q0:_K0]).astype(BF16)
    k = _dot(hb, w_ref[:, _K0:_V0]) * (ML_HEAD_DIM ** -0.5)
    kT_ref[...] = k.T.astype(BF16)
    v_ref[...] = _dot(hb, w_ref[:, _V0:_G0]).astype(BF16)
    zg = _dot(hb, wg_ref[...]) + bg_ref[...]
    log_f = jnp.minimum(zg, 0.0) - jnp.log1p(jnp.exp(-jnp.abs(zg)))
    lane = lax.broadcasted_iota(jnp.int32, zg.shape, 1)
    gc = jnp.where(lane < ML_HEADS, zg, log_f)
    gc_ref[...] = gc
    gr_ref[...] = gc.T[0:SUBLANES, :]
    return k


def _pool_project(acc_fn, u, cnt_fn, wpool_ref, spool_ref):
    outs = []
    for g, w in enumerate(POOL_WINDOWS):
        cols = slice(g * POOL_GROUP_DIM, (g + 1) * POOL_GROUP_DIM)
        ug = u[:, cols]
        d = acc_fn(g, w, ug) / cnt_fn(w) - ug
        outs.append(_dot(d.astype(BF16), wpool_ref[g]) * spool_ref[:, cols])
    return jnp.concatenate(outs, axis=-1).astype(BF16)


def _inproj_prompt_kernel(x_ref, sc_ref, sh_ref, g_ref, w_ref, wg_ref, bg_ref, wpool_ref, spool_ref,
                          wbp_ref, wbm_ref, wout_ref, wgu_ref, wdn_ref,
                          yp_ref, q_ref, kT_ref, v_ref, gc_ref, gr_ref, hout_ref,
                          wbp_o, wbm_o, wout_o, wgu_o, wdn_o, ext_ref, *, tm):
    t = pl.program_id(1)

    @pl.when(t == 0)
    def _():
        ext_ref[0:HIST_ROWS, :] = jnp.zeros((HIST_ROWS, POOL_WIDTH), F32)

    for src, dst in ((wbp_ref, wbp_o), (wbm_ref, wbm_o), (wout_ref, wout_o), (wgu_ref, wgu_o), (wdn_ref, wdn_o)):
        dst[...] = src[0].astype(BF16)

    _, hmod = _norm_mod(x_ref, sc_ref, sh_ref, g_ref)
    hb = hmod.astype(BF16)
    _project(hb, w_ref, wg_ref, bg_ref, q_ref, kT_ref.at[0], v_ref, gc_ref, gr_ref.at[0])

    u = _dot(hb, w_ref[:, _U0:_Q0])
    ext_ref[HIST_ROWS:HIST_ROWS + tm, :] = u
    pos = t * tm + lax.broadcasted_iota(jnp.int32, (tm, 1), 0)

    def acc_fn(g, w, ug):
        acc = ug
        for j in range(1, w):
            acc = acc + ext_ref[pl.ds(HIST_ROWS - j, tm), g * POOL_GROUP_DIM:(g + 1) * POOL_GROUP_DIM]
        return acc

    def cnt_fn(w):
        return jnp.minimum(pos + 1, w).astype(F32)

    yp_ref[...] = _pool_project(acc_fn, u, cnt_fn, wpool_ref, spool_ref)
    last = ext_ref[tm:tm + HIST_ROWS, :]
    hout_ref[0] = last
    ext_ref[0:HIST_ROWS, :] = last


def _inproj_sample_kernel(x_ref, sc_ref, sh_ref, g_ref, w_ref, wg_ref, bg_ref, wpool_ref, spool_ref, hist_ref,
                          yp_ref, q_ref, kT_ref, v_ref, gc_ref, gr_ref, k_ref, hout_ref, *, pos0):
    _, hmod = _norm_mod(x_ref, sc_ref, sh_ref, g_ref)
    hb = hmod.astype(BF16)
    k_ref[...] = _project(hb, w_ref, wg_ref, bg_ref, q_ref, kT_ref, v_ref, gc_ref, gr_ref).astype(BF16)

    u = _dot(hb, w_ref[:, _U0:_Q0])
    nb, tt = x_ref.shape[0], x_ref.shape[1]
    u3 = u.reshape(nb, tt, POOL_WIDTH)
    h0 = hist_ref[:, 0:SUBLANES, :]
    h1 = hist_ref[:, SUBLANES:HIST_ROWS, :]
    tok = lax.broadcasted_iota(jnp.int32, (nb, tt, POOL_GROUP_DIM), 1)

    def acc_fn(g, w, ug):
        cols = slice(g * POOL_GROUP_DIM, (g + 1) * POOL_GROUP_DIM)
        new, mid, old = u3[:, :, cols], h1[:, :, cols], h0[:, :, cols]
        acc = new
        for j in range(1, w):
            if j < tt:
                term = jnp.where(tok >= j, pltpu.roll(new, j, 1), pltpu.roll(mid, j, 1))
            elif j == tt:
                term = mid
            else:
                term = jnp.where(tok >= j - tt, pltpu.roll(mid, j - tt, 1), pltpu.roll(old, j - tt, 1))
            acc = acc + term
        return acc.reshape(nb * tt, POOL_GROUP_DIM)

    pos = pos0 + lax.broadcasted_iota(jnp.int32, (nb, tt, 1), 1).reshape(nb * tt, 1)

    def cnt_fn(w):
        return jnp.minimum(pos + 1, w).astype(F32)

    yp_ref[...] = _pool_project(acc_fn, u, cnt_fn, wpool_ref, spool_ref)
    hout_ref[:, 0:SUBLANES, :] = h1
    hout_ref[:, SUBLANES:HIST_ROWS, :] = u3


def _cast_specs(weights, nt, steps):
    in_specs, out_specs, out_shapes = [], [], []
    for w in weights:
        _, r, c = w.shape
        n = steps
        while r % n or (r // n) % BF16_ROWS:
            n //= 2
        rows = r // n
        idx = lambda b, t, n=n: jnp.minimum(b * nt + t, n - 1)
        in_specs.append(pl.BlockSpec((1, rows, c), lambda b, t, idx=idx: (0, idx(b, t), 0)))
        out_specs.append(pl.BlockSpec((rows, c), lambda b, t, idx=idx: (idx(b, t), 0)))
        out_shapes.append(jax.ShapeDtypeStruct((r, c), BF16))
    return in_specs, out_specs, out_shapes


def _inproj_prompt(x, mod, row0, g_mix, w_head, w_gate, bgate, w_pool, s_pool, later_weights):
    nbatch, seq, _ = x.shape
    tm = TOKEN_TILE
    nt = seq // tm
    ntok = nbatch * seq
    row = lambda b, t: (b * nt + t, 0)
    sc, sh = (_mod_spec(k, 1, row0, lambda b, t: b) for k in (_SC1, _SH1))
    cast_in, cast_out, cast_shapes = _cast_specs(later_weights, nt, nbatch * nt)
    return pl.pallas_call(
        functools.partial(_inproj_prompt_kernel, tm=tm),
        grid=(nbatch, nt),
        in_specs=[pl.BlockSpec((1, tm, D_MODEL), lambda b, t: (b, t, 0)), sc, sh,
                  _const_spec((1, D_MODEL)), _const_spec((D_MODEL, _G0)), _const_spec((D_MODEL, GATE_LANES)),
                  _const_spec((1, GATE_LANES)), _const_spec((4, POOL_GROUP_DIM, POOL_GROUP_DIM)),
                  _const_spec((1, POOL_WIDTH))] + cast_in,
        out_specs=[pl.BlockSpec((tm, POOL_WIDTH), row), pl.BlockSpec((tm, D_MODEL), row),
                   pl.BlockSpec((1, D_MODEL, tm), lambda b, t: (b, 0, t)), pl.BlockSpec((tm, D_MODEL), row),
                   pl.BlockSpec((tm, GATE_LANES), row), pl.BlockSpec((1, SUBLANES, tm), lambda b, t: (b, 0, t)),
                   pl.BlockSpec((1, HIST_ROWS, POOL_WIDTH), lambda b, t: (b, 0, 0))] + cast_out,
        out_shape=[jax.ShapeDtypeStruct((ntok, POOL_WIDTH), BF16), jax.ShapeDtypeStruct((ntok, D_MODEL), BF16),
                   jax.ShapeDtypeStruct((nbatch, D_MODEL, seq), BF16), jax.ShapeDtypeStruct((ntok, D_MODEL), BF16),
                   jax.ShapeDtypeStruct((ntok, GATE_LANES), F32), jax.ShapeDtypeStruct((nbatch, SUBLANES, seq), F32),
                   jax.ShapeDtypeStruct((nbatch, HIST_ROWS, POOL_WIDTH), F32)] + cast_shapes,
        scratch_shapes=[pltpu.VMEM((tm + HIST_ROWS, POOL_WIDTH), F32)],
        compiler_params=_params(("arbitrary", "arbitrary")),
        name="inproj_prompt",
    )(x, mod, mod, g_mix, w_head, w_gate, bgate, w_pool, s_pool, *later_weights)


def _inproj_sample(x, mod, row0, g_mix, w_head, w_gate, bgate, w_pool, s_pool, hist_pad):
    nseq, tt, _ = x.shape
    nb = TOKEN_TILE // tt
    tm = nb * tt
    ntok = nseq * tt
    row = lambda i: (i, 0)
    col = lambda i: (0, i)
    sc, sh = (_mod_spec(k, nb, row0, lambda i: i) for k in (_SC1, _SH1))
    hist = pl.BlockSpec((nb, HIST_ROWS, POOL_WIDTH), lambda i: (i, 0, 0))
    return pl.pallas_call(
        functools.partial(_inproj_sample_kernel, pos0=PAST_LEN),
        grid=(nseq // nb,),
        in_specs=[pl.BlockSpec((nb, tt, D_MODEL), lambda i: (i, 0, 0)), sc, sh,
                  _const_spec((1, D_MODEL)), _const_spec((D_MODEL, _G0)), _const_spec((D_MODEL, GATE_LANES)),
                  _const_spec((1, GATE_LANES)), _const_spec((4, POOL_GROUP_DIM, POOL_GROUP_DIM)),
                  _const_spec((1, POOL_WIDTH)), hist],
        out_specs=[pl.BlockSpec((tm, POOL_WIDTH), row), pl.BlockSpec((tm, D_MODEL), row),
                   pl.BlockSpec((D_MODEL, tm), col), pl.BlockSpec((tm, D_MODEL), row),
                   pl.BlockSpec((tm, GATE_LANES), row), pl.BlockSpec((SUBLANES, tm), col),
                   pl.BlockSpec((tm, D_MODEL), row), hist],
        out_shape=[jax.ShapeDtypeStruct((ntok, POOL_WIDTH), BF16), jax.ShapeDtypeStruct((ntok, D_MODEL), BF16),
                   jax.ShapeDtypeStruct((D_MODEL, ntok), BF16), jax.ShapeDtypeStruct((ntok, D_MODEL), BF16),
                   jax.ShapeDtypeStruct((ntok, GATE_LANES), F32), jax.ShapeDtypeStruct((SUBLANES, ntok), F32),
                   jax.ShapeDtypeStruct((ntok, D_MODEL), BF16),
                   jax.ShapeDtypeStruct((nseq, HIST_ROWS, POOL_WIDTH), F32)],
        compiler_params=_params(("arbitrary",)),
        name="inproj_sample",
    )(x, mod, mod, g_mix, w_head, w_gate, bgate, w_pool, s_pool, hist_pad)


def _mlstm_core(q, kT, v, b_c, b_r, ig_r, mask, m_prev, num_inter, qn):
    logd = jnp.where(mask, (b_c - b_r) + ig_r, -jnp.inf)
    a_c = b_c + m_prev
    m_t = jnp.maximum(a_c, jnp.max(logd, axis=-1, keepdims=True))
    w_intra = jnp.exp(logd - m_t)
    w_inter = jnp.exp(a_c - m_t)
    s = _dot(q, kT) * w_intra
    num = w_inter * num_inter + _dot(s.astype(BF16), v)
    den = w_inter * qn + jnp.sum(s, axis=-1, keepdims=True)
    hh = num / jnp.maximum(jnp.abs(den), jnp.exp(-m_t))
    return hh, m_t, a_c


def _mlstm_chunk(q_ref, kT_ref, v_ref, gr_ref, hh_ref, cn_ref, m_ref, r0, chunk):
    rows = slice(r0, r0 + chunk)
    gr = gr_ref[:, rows]
    ri = lax.broadcasted_iota(jnp.int32, (chunk, chunk), 0)
    ci = lax.broadcasted_iota(jnp.int32, (chunk, chunk), 1)
    causal = ri >= ci
    brow = _cumsum_rows(gr, (ri <= ci).astype(BF16))
    bcol = brow.T
    ones = jnp.ones((chunk, LANES), BF16)
    wide = lambda x: jnp.concatenate([x, x], axis=-1)
    for h in range(ML_HEADS):
        hs = slice(h * ML_HEAD_DIM, (h + 1) * ML_HEAD_DIM)
        b_r = brow[ML_HEADS + h:ML_HEADS + h + 1, :]
        g_r = gr[h:h + 1, :] - b_r
        m_prev = m_ref[h:h + 1, 0:1]
        cn = cn_ref[h]
        q = q_ref[rows, hs]
        kT = kT_ref[hs, rows]
        vaug = jnp.concatenate([v_ref[rows, hs], ones], axis=-1)
        gm = jnp.where(causal, g_r, -jnp.inf)
        big_m = jnp.maximum(m_prev, jnp.max(gm, axis=-1, keepdims=True))
        m_rep = jnp.broadcast_to(big_m, (chunk, LANES))
        mt_rep = jnp.broadcast_to(bcol[:, ML_HEADS + h:ML_HEADS + h + 1] + big_m, (chunk, LANES))
        s = _dot(q, kT) * jnp.exp(gm - wide(m_rep))
        b_last = b_r[:, chunk - 1:chunk]
        m_new = b_last + jnp.maximum(m_prev, jnp.max(g_r, axis=-1, keepdims=True))
        decay = jnp.exp((b_last + m_prev) - m_new)
        w_end = jnp.exp((g_r + b_last) - m_new)
        both = _dot(jnp.concatenate([s.astype(BF16), (kT.astype(F32) * w_end).astype(BF16)], axis=0), vaug)
        sva = both[0:chunk]
        qc = _dot(q, cn.astype(BF16))
        w_inter = jnp.exp(m_prev - m_rep)
        den = w_inter * qc[:, ML_HEAD_DIM:] + sva[:, ML_HEAD_DIM:]
        rinv = 1.0 / jnp.maximum(jnp.abs(den), jnp.exp(-mt_rep))
        hh_ref[rows, hs] = (wide(w_inter) * qc[:, :ML_HEAD_DIM] + sva[:, :ML_HEAD_DIM]) * wide(rinv)
        cn_ref[h] = decay * cn + both[chunk:]
        m_ref[h:h + 1, :] = jnp.broadcast_to(m_new, (1, LANES))


def _mlstm_prompt_kernel(q_ref, kT_ref, v_ref, gr_ref, hh_ref, cout_ref, nout_ref, mout_ref, cn_ref, m_ref, *, chunk):
    t = pl.program_id(1)
    nb = q_ref.shape[0]

    @pl.when(t == 0)
    def _():
        cn_ref[...] = jnp.zeros(cn_ref.shape, F32)
        m_ref[...] = jnp.full(m_ref.shape, M_INIT, F32)

    for r0 in range(0, q_ref.shape[1], chunk):
        for s in range(nb):
            _mlstm_chunk(q_ref.at[s], kT_ref.at[s], v_ref.at[s], gr_ref.at[s], hh_ref.at[s], cn_ref.at[s],
                         m_ref.at[s], r0, chunk)

    @pl.when(t == pl.num_programs(1) - 1)
    def _():
        cout_ref[...] = cn_ref[:, :, :, 0:ML_HEAD_DIM]
        nout_ref[...] = cn_ref[:, :, :, ML_HEAD_DIM:AUG]
        mout_ref[...] = m_ref[...]


def _mlstm_prompt(q, kT, v, gr):
    nbatch, seq, _ = q.shape
    tm, nb = MLSTM_TILE, MLSTM_BATCHES
    row = pl.BlockSpec((nb, tm, D_MODEL), lambda b, t: (b, t, 0))
    state = lambda last: pl.BlockSpec((nb, ML_HEADS, ML_HEAD_DIM, last), lambda b, t: (b, 0, 0, 0))
    return pl.pallas_call(
        functools.partial(_mlstm_prompt_kernel, chunk=PROMPT_CHUNK),
        grid=(nbatch // nb, seq // tm),
        in_specs=[row, pl.BlockSpec((nb, D_MODEL, tm), lambda b, t: (b, 0, t)), row,
                  pl.BlockSpec((nb, SUBLANES, tm), lambda b, t: (b, 0, t))],
        out_specs=[row, state(ML_HEAD_DIM), state(LANES), pl.BlockSpec((nb, SUBLANES, LANES), lambda b, t: (b, 0, 0))],
        out_shape=[jax.ShapeDtypeStruct((nbatch, seq, D_MODEL), F32),
                   jax.ShapeDtypeStruct((nbatch, ML_HEADS, ML_HEAD_DIM, ML_HEAD_DIM), F32),
                   jax.ShapeDtypeStruct((nbatch, ML_HEADS, ML_HEAD_DIM, LANES), F32),
                   jax.ShapeDtypeStruct((nbatch, SUBLANES, LANES), F32)],
        scratch_shapes=[pltpu.VMEM((nb, ML_HEADS, ML_HEAD_DIM, AUG), F32), pltpu.VMEM((nb, SUBLANES, LANES), F32)],
        compiler_params=_params(("arbitrary", "arbitrary")),
        name="mlstm_prompt",
    )(q, kT, v, gr)


def _last_in_group(x, group):
    rows = x.shape[0]
    x3 = jnp.broadcast_to(x, (rows, LANES)).reshape(rows // group, group, LANES)
    last = jnp.broadcast_to(x3[:, group - 1:group, :], x3.shape)
    return last.reshape(rows, LANES)[:, 0:1]


def _mlstm_sample_body(h, q_ref, kT_ref, k_ref, v_ref, gc_ref, gr_ref, m0_ref, c_ref, n_ref,
                       hh_ref, cout_ref, nout_ref, mt_ref, ni_ref, qn_ref, dec_ref, wk_ref, *, tt):
    L = q_ref.shape[0]
    nseq = L // tt
    gc = gc_ref[...]
    gr = gr_ref[...]
    ri = lax.broadcasted_iota(jnp.int32, (L, L), 0)
    ci = lax.broadcasted_iota(jnp.int32, (L, L), 1)
    same = (ri // tt) == (ci // tt)
    mask = same & (ri >= ci)
    bcol = _cumsum_cols(mask.astype(BF16), gc)
    brow = _cumsum_rows(gr, (same & (ri <= ci)).astype(BF16))
    lane = lax.broadcasted_iota(jnp.int32, (L, GATE_LANES), 1)
    sub = lax.broadcasted_iota(jnp.int32, (SUBLANES, L), 0)
    pick_col = lambda arr, idx: jnp.sum(jnp.where(lane == idx, arr, 0.0), axis=-1, keepdims=True)
    pick_row = lambda arr, idx: jnp.sum(jnp.where(sub == idx, arr, 0.0), axis=0, keepdims=True)
    ig_c = pick_col(gc, h)
    b_c = pick_col(bcol, ML_HEADS + h)
    ig_r = pick_row(gr, h)
    b_r = pick_row(brow, ML_HEADS + h)
    m_prev = m0_ref[0]

    for j in range(nseq):
        rows = slice(j * tt, (j + 1) * tt)
        qj = q_ref[rows, :]
        ni_ref[rows, :] = _dot(qj, c_ref[j, 0].astype(BF16))
        nj = n_ref[j, pl.ds(h, 1), :].astype(BF16).astype(F32)
        qn = jnp.sum(qj.astype(F32) * nj, axis=-1, keepdims=True)
        qn_ref[rows, :] = jnp.broadcast_to(qn, (tt, LANES))

    q = q_ref[...]
    kT = kT_ref[...]
    v = v_ref[...]
    hh, m_t, a_c = _mlstm_core(q, kT, v, b_c, b_r, ig_r, mask, m_prev, ni_ref[...], qn_ref[:, 0:1])
    hh_ref[...] = hh
    m_new = _last_in_group(m_t, tt)
    decay = jnp.exp(_last_in_group(a_c, tt) - m_new)
    w_end = jnp.exp((_last_in_group(b_c, tt) - b_c) + ig_c - m_new)
    mt_ref[0] = jnp.broadcast_to(m_t, (L, LANES))
    dec_ref[...] = jnp.broadcast_to(decay, (L, LANES))
    wv = w_end * v.astype(F32)
    wk_ref[...] = w_end.astype(BF16).astype(F32) * k_ref[...].astype(F32)
    rowi = lax.broadcasted_iota(jnp.int32, (L, 1), 0)

    for j in range(nseq):
        rows = slice(j * tt, (j + 1) * tt)
        upd = _dot(kT, jnp.where((rowi // tt) == j, wv, 0.0).astype(BF16))
        dj = dec_ref[j * tt:j * tt + 1, 0:1]
        cout_ref[j, 0] = dj * c_ref[j, 0] + upd
        nout_ref[j, pl.ds(h, 1), :] = (dj * n_ref[j, pl.ds(h, 1), :]
                                       + jnp.sum(wk_ref[rows, :], axis=0, keepdims=True))


def _mlstm_sample_specs(ntok, tt):
    nseq = ntok // tt
    sb = SAMPLE_SEQ_BLOCK
    L = sb * tt
    nh = ML_HEADS
    qspec = pl.BlockSpec((L, ML_HEAD_DIM), lambda i: (i // nh, i % nh))
    cspec = pl.BlockSpec((sb, 1, ML_HEAD_DIM, ML_HEAD_DIM), lambda i: (i // nh, i % nh, 0, 0))
    nspec = pl.BlockSpec((sb, nh, ML_HEAD_DIM), lambda i: (i // nh, 0, 0))
    in_specs = [qspec, pl.BlockSpec((ML_HEAD_DIM, L), lambda i: (i % nh, i // nh)), qspec, qspec,
                pl.BlockSpec((L, GATE_LANES), lambda i: (i // nh, 0)),
                pl.BlockSpec((SUBLANES, L), lambda i: (0, i // nh)),
                pl.BlockSpec((1, L, 1), lambda i: (i % nh, i // nh, 0)), cspec, nspec]
    out_specs = [qspec, cspec, nspec, pl.BlockSpec((1, L, LANES), lambda i: (i % nh, i // nh, 0))]
    out_shapes = [jax.ShapeDtypeStruct((ntok, D_MODEL), F32),
                  jax.ShapeDtypeStruct((nseq, ML_HEADS, ML_HEAD_DIM, ML_HEAD_DIM), F32),
                  jax.ShapeDtypeStruct((nseq, ML_HEADS, ML_HEAD_DIM), F32),
                  jax.ShapeDtypeStruct((ML_HEADS, ntok, LANES), F32)]
    scratch = [pltpu.VMEM((L, ML_HEAD_DIM), F32), pltpu.VMEM((L, LANES), F32),
               pltpu.VMEM((L, LANES), F32), pltpu.VMEM((L, ML_HEAD_DIM), F32)]
    return (nseq // sb) * nh, in_specs, out_specs, out_shapes, scratch


def _post_kernel(x_ref, sc_ref, sh_ref, gt_ref, g_ref, hh_ref, yp_ref, wt_ref, ghead_ref, wbp_ref, wbm_ref, wout_ref,
                 o_ref):
    x, hmod = _norm_mod(x_ref, sc_ref, sh_ref, g_ref)
    hb = hmod.astype(BF16)
    nb, tt, d = x_ref.shape
    o = _dot(hb, wt_ref[:, 0:D_MODEL])
    parts = []
    for h in range(ML_HEADS):
        hh = hh_ref[:, h * ML_HEAD_DIM:(h + 1) * ML_HEAD_DIM]
        parts.append(hh * lax.rsqrt(jnp.mean(hh * hh, axis=-1, keepdims=True) + EPS))
    yml = (jnp.concatenate(parts, axis=-1) * ghead_ref[...]) * jax.nn.sigmoid(o)
    gp = _dot(hb, wt_ref[:, D_MODEL:2 * D_MODEL])
    gm = _dot(hb, wt_ref[:, 2 * D_MODEL:3 * D_MODEL])
    merged = (jax.nn.sigmoid(gp) * _dot(yp_ref[...], wbp_ref[...])
              + jax.nn.sigmoid(gm) * _dot(yml.astype(BF16), wbm_ref[...]))
    y = _dot(merged.astype(BF16), wout_ref[...]).reshape(nb, tt, d)
    o_ref[...] = x_ref[...] + gt_ref[0] * y


def _tile_blocks(x, tile):
    g, t, _ = x.shape
    if t >= tile:
        nb, tt = 1, tile
    else:
        nb, tt = tile // t, t
    return nb, tt, (g // nb) * (t // tt), t // tt


def _post(x, mod, row0, g_mix, hh, yp, w_tail, g_head, w_bp, w_bm, w_out):
    nb, tt, steps, per = _tile_blocks(x, TOKEN_TILE)
    tm = nb * tt
    xspec = pl.BlockSpec((nb, tt, D_MODEL), lambda i: (i // per, i % per, 0))
    sc, sh, gt = (_mod_spec(k, nb, row0, lambda i: i // per) for k in (_SC1, _SH1, _GT1))
    row = lambda i: (i, 0)
    return pl.pallas_call(
        _post_kernel,
        grid=(steps,),
        in_specs=[xspec, sc, sh, gt, _const_spec((1, D_MODEL)),
                  pl.BlockSpec((tm, D_MODEL), row), pl.BlockSpec((tm, POOL_WIDTH), row),
                  _const_spec((D_MODEL, 3 * D_MODEL)), _const_spec((1, D_MODEL)),
                  _const_spec((POOL_WIDTH, D_MODEL)), _const_spec((D_MODEL, D_MODEL)),
                  _const_spec((D_MODEL, D_MODEL))],
        out_specs=xspec,
        out_shape=jax.ShapeDtypeStruct(x.shape, F32),
        compiler_params=_params(("arbitrary",)),
        name="post",
    )(x, mod, mod, mod, g_mix, hh, yp, w_tail, g_head, w_bp, w_bm, w_out)


_FF_SPLITS = ((0, 768), (768, 1536), (1536, 2304), (2304, D_FF))


def _ffn_kernel(x_ref, sc_ref, sh_ref, gt_ref, g_ref, gfin_ref, wgu_ref, wdn_ref, o_ref):
    _, hmod = _norm_mod(x_ref, sc_ref, sh_ref, g_ref)
    hb = hmod.astype(BF16)
    nb, tt, d = x_ref.shape
    dn = None
    for lo, hi in _FF_SPLITS:
        gate = _dot(hb, wgu_ref[:, lo:hi])
        up = _dot(hb, wgu_ref[:, D_FF + lo:D_FF + hi])
        act = (gate * jax.nn.sigmoid(gate) * up).astype(BF16)
        part = _dot(act, wdn_ref[lo:hi, :])
        dn = part if dn is None else dn + part
    x2 = x_ref[...] + gt_ref[0] * dn.reshape(nb, tt, d)
    ms = jnp.mean(x2 * x2, axis=-1, keepdims=True)
    o_ref[...] = x2 * lax.rsqrt(ms + EPS) * gfin_ref[...]


_N_FFN_IN = 8


def _ffn_mlstm_kernel(*refs, n_ml_in, tt):
    ffn_in = refs[:_N_FFN_IN]
    ml_in = refs[_N_FFN_IN:_N_FFN_IN + n_ml_in]
    o_ref = refs[_N_FFN_IN + n_ml_in]
    ml_rest = refs[_N_FFN_IN + n_ml_in + 1:]
    _mlstm_sample_body(pl.program_id(0) % ML_HEADS, *ml_in, *ml_rest, tt=tt)
    _ffn_kernel(*ffn_in, o_ref)


def _ffn(x, mod, row0, g_ffn, g_final, w_gu, w_down, tile, mlstm_sample=None):
    nb, tt, steps, per = _tile_blocks(x, tile)
    xspec = pl.BlockSpec((nb, tt, D_MODEL), lambda i: (i // per, i % per, 0))
    sc, sh, gt = (_mod_spec(k, nb, row0, lambda i: i // per) for k in (_SC2, _SH2, _GT2))
    in_specs = [xspec, sc, sh, gt, _const_spec((1, D_MODEL)), _const_spec((1, D_MODEL)),
                _const_spec((D_MODEL, 2 * D_FF)), _const_spec((D_FF, D_MODEL))]
    operands = (x, mod, mod, mod, g_ffn, g_final, w_gu, w_down)
    if mlstm_sample is None:
        return pl.pallas_call(
            _ffn_kernel,
            grid=(steps,),
            in_specs=in_specs,
            out_specs=xspec,
            out_shape=jax.ShapeDtypeStruct(x.shape, F32),
            compiler_params=_params(("arbitrary",)),
            name="ffn",
        )(*operands)
    ml_operands, ml_tt = mlstm_sample
    ml_steps, ml_in, ml_out, ml_shapes, ml_scratch = _mlstm_sample_specs(ml_operands[0].shape[0], ml_tt)
    assert ml_steps == steps, "one sample mLSTM step per FFN tile"
    return pl.pallas_call(
        functools.partial(_ffn_mlstm_kernel, n_ml_in=len(ml_in), tt=ml_tt),
        grid=(steps,),
        in_specs=in_specs + ml_in,
        out_specs=[xspec] + ml_out,
        out_shape=[jax.ShapeDtypeStruct(x.shape, F32)] + ml_shapes,
        scratch_shapes=ml_scratch,
        compiler_params=_params(("arbitrary",)),
        name="ffn_mlstm",
    )(*operands, *ml_operands)


def kernel(x_prompt, x_sample, c_prompt, c_sample, state_pool, state_mlstm_c, state_mlstm_n, state_mlstm_m, g_mix, g_ffn, g_final, w_ada, b_ada, w_in, b_igate, b_fgate, w_pool, s_pool, g_head, w_branch_pool, w_branch_mlstm, w_out, w_gate_up, w_down):
    depth = w_in.shape[0]
    assert depth == 1, "single-layer trunk"
    nbatch, seq, _ = x_prompt.shape
    nseq, tt, _ = x_sample.shape
    l = 0

    bgate = jnp.pad(jnp.concatenate([b_igate[l], b_fgate[l]])[None, :], ((0, 0), (0, GATE_LANES - 2 * ML_HEADS)))
    g_mix_r, g_ffn_r, g_fin_r = g_mix[l][None, :], g_ffn[l][None, :], g_final[None, :]
    s_pool_r, g_head_r = s_pool[l][None, :], g_head[l][None, :]

    w_head, w_tail, w_gate, w_pool_b = _wprep(jnp.swapaxes(w_in, 1, 2), w_pool)
    mod = _ada(c_sample, c_prompt, w_ada[l], b_ada[l][None, :])
    row_s, row_p = 0, nseq

    (yp, q, kT, v, gc, gr, hist_p, w_bp, w_bm, w_o, w_gu, w_dn) = _inproj_prompt(
        x_prompt, mod, row_p, g_mix_r, w_head, w_gate, bgate, w_pool_b, s_pool_r,
        (w_branch_pool, w_branch_mlstm, w_out, w_gate_up, w_down))
    hh, c_p, n_p, m_p = _mlstm_prompt(q.reshape(nbatch, seq, D_MODEL), kT, v.reshape(nbatch, seq, D_MODEL), gr)
    hh = hh.reshape(nbatch * seq, D_MODEL)
    x1 = _post(x_prompt, mod, row_p, g_mix_r, hh, yp, w_tail, g_head_r, w_bp, w_bm, w_o)

    hist_pad = jnp.pad(state_pool[l], ((0, 0), (HIST_ROWS - POOL_HIST, 0), (0, 0)))
    yp, q, kT, v, gc, gr, k, hist_s = _inproj_sample(x_sample, mod, row_s, g_mix_r, w_head, w_gate, bgate, w_pool_b,
                                                     s_pool_r, hist_pad)
    m0_tok = jnp.repeat(state_mlstm_m[l].astype(F32).T, tt, axis=1)[:, :, None]
    ml_operands = (q, kT, k, v, gc, gr, m0_tok, state_mlstm_c[l].astype(F32), state_mlstm_n[l].astype(F32))
    y_prompt, hh, c_s, n_s, mt = _ffn(x1, mod, row_p, g_ffn_r, g_fin_r, w_gu, w_dn, PROMPT_FFN_TILE,
                                      (ml_operands, tt))

    x1 = _post(x_sample, mod, row_s, g_mix_r, hh, yp, w_tail, g_head_r, w_bp, w_bm, w_o)
    y_sample = _ffn(x1, mod, row_s, g_ffn_r, g_fin_r, w_gu, w_dn, FFN_TILE)

    cd, nd, md = state_mlstm_c.dtype, state_mlstm_n.dtype, state_mlstm_m.dtype
    return (y_prompt, y_sample,
            hist_p[None, :, HIST_ROWS - POOL_HIST:, :],
            c_p.astype(cd)[None], n_p[..., 0].astype(nd)[None], m_p[:, :ML_HEADS, 0].astype(md)[None],
            hist_s[None, :, HIST_ROWS - POOL_HIST:, :].astype(state_pool.dtype),
            c_s.astype(cd)[None], n_s.astype(nd)[None],
            mt[:, tt - 1::tt, 0].T.astype(md)[None])
```

```python
import functools

import jax
import jax.numpy as jnp
from jax import lax
from jax.experimental import pallas as pl
from jax.experimental.pallas import tpu as pltpu

D_MODEL = 1024
POOL_WINDOWS = (2, 4, 8, 16)
POOL_GROUP_DIM = 128
POOL_WIDTH = 512
POOL_HIST = 15
HIST_ROWS = 16
ML_HEADS = 4
ML_HEAD_DIM = 256
D_FF = 2816
EPS = 1e-6
M_INIT = -1e30
PAST_LEN = 16384
LANES = 128
SUBLANES = 8
BF16_ROWS = 16
GATE_LANES = LANES
AUG = ML_HEAD_DIM + LANES
VMEM_LIMIT = 56 * 1024 * 1024

TOKEN_TILE = 512
FFN_TILE = 1024
PROMPT_FFN_TILE = 512
PROMPT_CHUNK = 256
MLSTM_TILE = 1024
MLSTM_BATCHES = 1
SAMPLE_SEQ_BLOCK = 16

BF16 = jnp.bfloat16
F32 = jnp.float32

_U0, _Q0, _K0, _V0, _G0 = 0, 512, 1536, 2560, 3584


def _dot(a, b):
    return jnp.dot(a, b, preferred_element_type=F32)


def _const_spec(shape):
    zeros = (0,) * len(shape)
    return pl.BlockSpec(shape, lambda *_: zeros, pipeline_mode=pl.Buffered(1))


def _params(sem):
    return pltpu.CompilerParams(dimension_semantics=sem, vmem_limit_bytes=VMEM_LIMIT)


def _norm_mod(x_ref, sc_ref, sh_ref, g_ref):
    x = x_ref[...]
    nb, tt, d = x.shape
    ms = jnp.mean(x * x, axis=-1, keepdims=True)
    y = x * lax.rsqrt(ms + EPS) * g_ref[...]
    hmod = y * (1.0 + sc_ref[0]) + sh_ref[0]
    return x.reshape(nb * tt, d), hmod.reshape(nb * tt, d)


def _split3(x):
    hi = x.astype(BF16)
    r1 = x - hi.astype(F32)
    mid = r1.astype(BF16)
    lo = (r1 - mid.astype(F32)).astype(BF16)
    return hi, mid, lo


def _cumsum_cols(tri, x):
    hi, mid, lo = _split3(x)
    return _dot(tri, hi) + _dot(tri, mid) + _dot(tri, lo)


def _cumsum_rows(x, tri):
    hi, mid, lo = _split3(x)
    return _dot(hi, tri) + _dot(mid, tri) + _dot(lo, tri)


def _ada_kernel(cs_ref, cp_ref, w_ref, b_ref, o_ref):
    c = jnp.concatenate([cs_ref[...], cp_ref[...]], axis=0)
    a = (c * jax.nn.sigmoid(c)).astype(BF16)
    res = _dot(a, w_ref[...].astype(BF16)) + b_ref[...]
    o_ref[0] = res.reshape(res.shape[0], 1, res.shape[1])


def _ada(c_sample, c_prompt, w_ada, b_ada):
    rows = c_sample.shape[0] + c_prompt.shape[0]
    nchunk = w_ada.shape[1] // D_MODEL
    return pl.pallas_call(
        _ada_kernel,
        grid=(nchunk,),
        in_specs=[pl.BlockSpec(c_sample.shape, lambda j: (0, 0)),
                  pl.BlockSpec(c_prompt.shape, lambda j: (0, 0)),
                  pl.BlockSpec((D_MODEL, D_MODEL), lambda j: (0, j)),
                  pl.BlockSpec((1, D_MODEL), lambda j: (0, j))],
        out_specs=pl.BlockSpec((1, rows, 1, D_MODEL), lambda j: (j, 0, 0, 0)),
        out_shape=jax.ShapeDtypeStruct((nchunk, rows, 1, D_MODEL), F32),
        compiler_params=_params(("arbitrary",)),
        name="ada",
    )(c_sample, c_prompt, w_ada, b_ada)


_SH1, _SC1, _GT1, _SH2, _SC2, _GT2 = range(6)


def _mod_spec(k, nb, row0, block_index):
    return pl.BlockSpec((1, nb, 1, D_MODEL), lambda *g: (k, row0 // nb + block_index(*g), 0, 0))


_PREP_COLS = 512


def _wprep_kernel(wt_ref, nx_ref, gt_ref, wp_ref, head_ref, tail_ref, gate_ref, pool_ref, *, head_steps, shift_step,
                  shift):
    j = pl.program_id(0)
    rows = wt_ref[0]
    shifted = jnp.concatenate([rows[shift:, :], nx_ref[0, 0:shift, :]], axis=0)
    blk = jnp.where(j >= shift_step, shifted, rows).T.astype(BF16)

    @pl.when(j < head_steps)
    def _():
        head_ref[...] = blk

    @pl.when(j >= head_steps)
    def _():
        tail_ref[...] = blk

    gates = gt_ref[0].T
    pad = jnp.zeros((gates.shape[0], GATE_LANES - gates.shape[1]), F32)
    gate_ref[...] = jnp.concatenate([gates, pad], axis=1).astype(BF16)
    pool_ref[...] = wp_ref[0].astype(BF16)


def _wprep(w_in_t, w_pool):
    g0 = _G0 + D_MODEL
    p0 = g0 + 2 * ML_HEADS
    head_steps = _G0 // _PREP_COLS
    tail_steps = 3 * D_MODEL // _PREP_COLS
    shift_step = head_steps + D_MODEL // _PREP_COLS
    blk = (1, _PREP_COLS, D_MODEL)
    return pl.pallas_call(
        functools.partial(_wprep_kernel, head_steps=head_steps, shift_step=shift_step, shift=p0 - g0),
        grid=(head_steps + tail_steps,),
        in_specs=[pl.BlockSpec(blk, lambda j: (0, j, 0)),
                  pl.BlockSpec(blk, lambda j: (0, jnp.where(j >= shift_step, j + 1, 0), 0)),
                  pl.BlockSpec((1, 2 * ML_HEADS, D_MODEL), lambda j: (0, g0 // (2 * ML_HEADS), 0)),
                  pl.BlockSpec(w_pool.shape, lambda j: (0, 0, 0, 0))],
        out_specs=[pl.BlockSpec((D_MODEL, _PREP_COLS), lambda j: (0, jnp.minimum(j, head_steps - 1))),
                   pl.BlockSpec((D_MODEL, _PREP_COLS), lambda j: (0, jnp.maximum(j - head_steps, 0))),
                   pl.BlockSpec((D_MODEL, GATE_LANES), lambda j: (0, 0)),
                   pl.BlockSpec(w_pool.shape[1:], lambda j: (0, 0, 0))],
        out_shape=[jax.ShapeDtypeStruct((D_MODEL, _G0), BF16),
                   jax.ShapeDtypeStruct((D_MODEL, 3 * D_MODEL), BF16),
                   jax.ShapeDtypeStruct((D_MODEL, GATE_LANES), BF16),
                   jax.ShapeDtypeStruct(w_pool.shape[1:], BF16)],
        compiler_params=_params(("arbitrary",)),
        name="wprep",
    )(w_in_t, w_in_t, w_in_t, w_pool)


def _project(hb, w_ref, wg_ref, bg_ref, q_ref, kT_ref, v_ref, gc_ref, gr_ref):
    q_ref[...] = _dot(hb, w_ref[:, _Q0:_K0]).astype(BF16)
    k = _dot(hb, w_ref[:, _K0:_V0]) * (ML_HEAD_DIM ** -0.5)
    kT_ref[...] = k.T.astype(BF16)
    v_ref[...] = _dot(hb, w_ref[:, _V0:_G0]).astype(BF16)
    zg = _dot(hb, wg_ref[...]) + bg_ref[...]
    log_f = jnp.minimum(zg, 0.0) - jnp.log1p(jnp.exp(-jnp.abs(zg)))
    lane = lax.broadcasted_iota(jnp.int32, zg.shape, 1)
    gc = jnp.where(lane < ML_HEADS, zg, log_f)
    if gc_ref is not None:
        gc_ref[...] = gc
    gr_ref[...] = gc.T[0:SUBLANES, :]
    return k


def _pool_project(acc_fn, u, cnt_fn, wpool_ref, spool_ref):
    outs = []
    for g, w in enumerate(POOL_WINDOWS):
        cols = slice(g * POOL_GROUP_DIM, (g + 1) * POOL_GROUP_DIM)
        ug = u[:, cols]
        d = acc_fn(g, w, ug) / cnt_fn(w) - ug
        outs.append(_dot(d.astype(BF16), wpool_ref[g]) * spool_ref[:, cols])
    return jnp.concatenate(outs, axis=-1).astype(BF16)


def _inproj_prompt_kernel(x_ref, sc_ref, sh_ref, g_ref, w_ref, wg_ref, bg_ref, wpool_ref, spool_ref,
                          wbp_ref, wbm_ref, wout_ref, wgu_ref, wdn_ref,
                          yp_ref, hout_ref, wbp_o, wbm_o, wout_o, wgu_o, wdn_o, hh_ref, cout_ref, nout_ref, mout_ref,
                          ext_ref, q_ref, kT_ref, v_ref, gr_ref, cn_ref, m_ref, *, tm):
    t = pl.program_id(1)

    @pl.when(t == 0)
    def _():
        ext_ref[0:HIST_ROWS, :] = jnp.zeros((HIST_ROWS, POOL_WIDTH), F32)
        cn_ref[...] = jnp.zeros(cn_ref.shape, F32)
        m_ref[...] = jnp.full(m_ref.shape, M_INIT, F32)

    for src, dst in ((wbp_ref, wbp_o), (wbm_ref, wbm_o), (wout_ref, wout_o), (wgu_ref, wgu_o), (wdn_ref, wdn_o)):
        dst[...] = src[0].astype(BF16)

    _, hmod = _norm_mod(x_ref, sc_ref, sh_ref, g_ref)
    hb = hmod.astype(BF16)
    _project(hb, w_ref, wg_ref, bg_ref, q_ref, kT_ref, v_ref, None, gr_ref)

    u = _dot(hb, w_ref[:, _U0:_Q0])
    ext_ref[HIST_ROWS:HIST_ROWS + tm, :] = u
    pos = t * tm + lax.broadcasted_iota(jnp.int32, (tm, 1), 0)

    def acc_fn(g, w, ug):
        acc = ug
        for j in range(1, w):
            acc = acc + ext_ref[pl.ds(HIST_ROWS - j, tm), g * POOL_GROUP_DIM:(g + 1) * POOL_GROUP_DIM]
        return acc

    def cnt_fn(w):
        return jnp.minimum(pos + 1, w).astype(F32)

    yp_ref[...] = _pool_project(acc_fn, u, cnt_fn, wpool_ref, spool_ref)
    last = ext_ref[tm:tm + HIST_ROWS, :]
    hout_ref[0] = last
    ext_ref[0:HIST_ROWS, :] = last

    for r0 in range(0, tm, PROMPT_CHUNK):
        _mlstm_chunk(q_ref, kT_ref, v_ref, gr_ref, hh_ref, cn_ref, m_ref, r0, PROMPT_CHUNK)

    @pl.when(t == pl.num_programs(1) - 1)
    def _():
        cout_ref[0] = cn_ref[:, :, 0:ML_HEAD_DIM]
        nout_ref[0] = cn_ref[:, :, ML_HEAD_DIM:AUG]
        mout_ref[0] = m_ref[...]


def _inproj_sample_kernel(x_ref, sc_ref, sh_ref, g_ref, w_ref, wg_ref, bg_ref, wpool_ref, spool_ref, hist_ref,
                          yp_ref, q_ref, kT_ref, v_ref, gc_ref, gr_ref, k_ref, hout_ref, *, pos0):
    _, hmod = _norm_mod(x_ref, sc_ref, sh_ref, g_ref)
    hb = hmod.astype(BF16)
    k_ref[...] = _project(hb, w_ref, wg_ref, bg_ref, q_ref, kT_ref, v_ref, gc_ref, gr_ref).astype(BF16)

    u = _dot(hb, w_ref[:, _U0:_Q0])
    nb, tt = x_ref.shape[0], x_ref.shape[1]
    u3 = u.reshape(nb, tt, POOL_WIDTH)
    h0 = hist_ref[:, 0:SUBLANES, :]
    h1 = hist_ref[:, SUBLANES:HIST_ROWS, :]
    tok = lax.broadcasted_iota(jnp.int32, (nb, tt, POOL_GROUP_DIM), 1)

    def acc_fn(g, w, ug):
        cols = slice(g * POOL_GROUP_DIM, (g + 1) * POOL_GROUP_DIM)
        new, mid, old = u3[:, :, cols], h1[:, :, cols], h0[:, :, cols]
        acc = new
        for j in range(1, w):
            if j < tt:
                term = jnp.where(tok >= j, pltpu.roll(new, j, 1), pltpu.roll(mid, j, 1))
            elif j == tt:
                term = mid
            else:
                term = jnp.where(tok >= j - tt, pltpu.roll(mid, j - tt, 1), pltpu.roll(old, j - tt, 1))
            acc = acc + term
        return acc.reshape(nb * tt, POOL_GROUP_DIM)

    pos = pos0 + lax.broadcasted_iota(jnp.int32, (nb, tt, 1), 1).reshape(nb * tt, 1)

    def cnt_fn(w):
        return jnp.minimum(pos + 1, w).astype(F32)

    yp_ref[...] = _pool_project(acc_fn, u, cnt_fn, wpool_ref, spool_ref)
    hout_ref[:, 0:SUBLANES, :] = h1
    hout_ref[:, SUBLANES:HIST_ROWS, :] = u3


def _cast_specs(weights, nt, steps):
    in_specs, out_specs, out_shapes = [], [], []
    for w in weights:
        _, r, c = w.shape
        n = steps
        while r % n or (r // n) % BF16_ROWS:
            n //= 2
        rows = r // n
        idx = lambda b, t, n=n: jnp.minimum(b * nt + t, n - 1)
        in_specs.append(pl.BlockSpec((1, rows, c), lambda b, t, idx=idx: (0, idx(b, t), 0)))
        out_specs.append(pl.BlockSpec((rows, c), lambda b, t, idx=idx: (idx(b, t), 0)))
        out_shapes.append(jax.ShapeDtypeStruct((r, c), BF16))
    return in_specs, out_specs, out_shapes


def _inproj_prompt(x, mod, row0, g_mix, w_head, w_gate, bgate, w_pool, s_pool, later_weights):
    nbatch, seq, _ = x.shape
    tm = TOKEN_TILE
    nt = seq // tm
    ntok = nbatch * seq
    row = lambda b, t: (b * nt + t, 0)
    sc, sh = (_mod_spec(k, 1, row0, lambda b, t: b) for k in (_SC1, _SH1))
    cast_in, cast_out, cast_shapes = _cast_specs(later_weights, nt, nbatch * nt)
    return pl.pallas_call(
        functools.partial(_inproj_prompt_kernel, tm=tm),
        grid=(nbatch, nt),
        in_specs=[pl.BlockSpec((1, tm, D_MODEL), lambda b, t: (b, t, 0)), sc, sh,
                  _const_spec((1, D_MODEL)), _const_spec((D_MODEL, _G0)), _const_spec((D_MODEL, GATE_LANES)),
                  _const_spec((1, GATE_LANES)), _const_spec((4, POOL_GROUP_DIM, POOL_GROUP_DIM)),
                  _const_spec((1, POOL_WIDTH))] + cast_in,
        out_specs=[pl.BlockSpec((tm, POOL_WIDTH), row),
                   pl.BlockSpec((1, HIST_ROWS, POOL_WIDTH), lambda b, t: (b, 0, 0))] + cast_out + [
                       pl.BlockSpec((tm, D_MODEL), row),
                       pl.BlockSpec((1, ML_HEADS, ML_HEAD_DIM, ML_HEAD_DIM), lambda b, t: (b, 0, 0, 0)),
                       pl.BlockSpec((1, ML_HEADS, ML_HEAD_DIM, LANES), lambda b, t: (b, 0, 0, 0)),
                       pl.BlockSpec((1, SUBLANES, LANES), lambda b, t: (b, 0, 0))],
        out_shape=[jax.ShapeDtypeStruct((ntok, POOL_WIDTH), BF16),
                   jax.ShapeDtypeStruct((nbatch, HIST_ROWS, POOL_WIDTH), F32)] + cast_shapes + [
                       jax.ShapeDtypeStruct((ntok, D_MODEL), F32),
                       jax.ShapeDtypeStruct((nbatch, ML_HEADS, ML_HEAD_DIM, ML_HEAD_DIM), F32),
                       jax.ShapeDtypeStruct((nbatch, ML_HEADS, ML_HEAD_DIM, LANES), F32),
                       jax.ShapeDtypeStruct((nbatch, SUBLANES, LANES), F32)],
        scratch_shapes=[pltpu.VMEM((tm + HIST_ROWS, POOL_WIDTH), F32),
                        pltpu.VMEM((tm, D_MODEL), BF16), pltpu.VMEM((D_MODEL, tm), BF16), pltpu.VMEM((tm, D_MODEL), BF16),
                        pltpu.VMEM((SUBLANES, tm), F32),
                        pltpu.VMEM((ML_HEADS, ML_HEAD_DIM, AUG), F32), pltpu.VMEM((SUBLANES, LANES), F32)],
        compiler_params=_params(("arbitrary", "arbitrary")),
        name="inproj_prompt",
    )(x, mod, mod, g_mix, w_head, w_gate, bgate, w_pool, s_pool, *later_weights)


def _inproj_sample(x, mod, row0, g_mix, w_head, w_gate, bgate, w_pool, s_pool, hist_pad):
    nseq, tt, _ = x.shape
    nb = TOKEN_TILE // tt
    tm = nb * tt
    ntok = nseq * tt
    row = lambda i: (i, 0)
    col = lambda i: (0, i)
    sc, sh = (_mod_spec(k, nb, row0, lambda i: i) for k in (_SC1, _SH1))
    hist = pl.BlockSpec((nb, HIST_ROWS, POOL_WIDTH), lambda i: (i, 0, 0))
    return pl.pallas_call(
        functools.partial(_inproj_sample_kernel, pos0=PAST_LEN),
        grid=(nseq // nb,),
        in_specs=[pl.BlockSpec((nb, tt, D_MODEL), lambda i: (i, 0, 0)), sc, sh,
                  _const_spec((1, D_MODEL)), _const_spec((D_MODEL, _G0)), _const_spec((D_MODEL, GATE_LANES)),
                  _const_spec((1, GATE_LANES)), _const_spec((4, POOL_GROUP_DIM, POOL_GROUP_DIM)),
                  _const_spec((1, POOL_WIDTH)), hist],
        out_specs=[pl.BlockSpec((tm, POOL_WIDTH), row), pl.BlockSpec((tm, D_MODEL), row),
                   pl.BlockSpec((D_MODEL, tm), col), pl.BlockSpec((tm, D_MODEL), row),
                   pl.BlockSpec((tm, GATE_LANES), row), pl.BlockSpec((SUBLANES, tm), col),
                   pl.BlockSpec((tm, D_MODEL), row), hist],
        out_shape=[jax.ShapeDtypeStruct((ntok, POOL_WIDTH), BF16), jax.ShapeDtypeStruct((ntok, D_MODEL), BF16),
                   jax.ShapeDtypeStruct((D_MODEL, ntok), BF16), jax.ShapeDtypeStruct((ntok, D_MODEL), BF16),
                   jax.ShapeDtypeStruct((ntok, GATE_LANES), F32), jax.ShapeDtypeStruct((SUBLANES, ntok), F32),
                   jax.ShapeDtypeStruct((ntok, D_MODEL), BF16),
                   jax.ShapeDtypeStruct((nseq, HIST_ROWS, POOL_WIDTH), F32)],
        compiler_params=_params(("arbitrary",)),
        name="inproj_sample",
    )(x, mod, mod, g_mix, w_head, w_gate, bgate, w_pool, s_pool, hist_pad)


def _mlstm_core(q, kT, v, b_c, b_r, ig_r, mask, m_prev, num_inter, qn):
    logd = jnp.where(mask, (b_c - b_r) + ig_r, -jnp.inf)
    a_c = b_c + m_prev
    m_t = jnp.maximum(a_c, jnp.max(logd, axis=-1, keepdims=True))
    w_intra = jnp.exp(logd - m_t)
    w_inter = jnp.exp(a_c - m_t)
    s = _dot(q, kT) * w_intra
    num = w_inter * num_inter + _dot(s.astype(BF16), v)
    den = w_inter * qn + jnp.sum(s, axis=-1, keepdims=True)
    hh = num / jnp.maximum(jnp.abs(den), jnp.exp(-m_t))
    return hh, m_t, a_c


def _mlstm_chunk(q_ref, kT_ref, v_ref, gr_ref, hh_ref, cn_ref, m_ref, r0, chunk):
    rows = slice(r0, r0 + chunk)
    gr = gr_ref[:, rows]
    ri = lax.broadcasted_iota(jnp.int32, (chunk, chunk), 0)
    ci = lax.broadcasted_iota(jnp.int32, (chunk, chunk), 1)
    causal = ri >= ci
    brow = _cumsum_rows(gr, (ri <= ci).astype(BF16))
    bcol = brow.T
    ones = jnp.ones((chunk, LANES), BF16)
    wide = lambda x: jnp.concatenate([x, x], axis=-1)
    for h in range(ML_HEADS):
        hs = slice(h * ML_HEAD_DIM, (h + 1) * ML_HEAD_DIM)
        b_r = brow[ML_HEADS + h:ML_HEADS + h + 1, :]
        g_r = gr[h:h + 1, :] - b_r
        m_prev = m_ref[h:h + 1, 0:1]
        cn = cn_ref[h]
        q = q_ref[rows, hs]
        kT = kT_ref[hs, rows]
        vaug = jnp.concatenate([v_ref[rows, hs], ones], axis=-1)
        gm = jnp.where(causal, g_r, -jnp.inf)
        big_m = jnp.maximum(m_prev, jnp.max(gm, axis=-1, keepdims=True))
        m_rep = jnp.broadcast_to(big_m, (chunk, LANES))
        mt_rep = jnp.broadcast_to(bcol[:, ML_HEADS + h:ML_HEADS + h + 1] + big_m, (chunk, LANES))
        s = _dot(q, kT) * jnp.exp(gm - wide(m_rep))
        b_last = b_r[:, chunk - 1:chunk]
        m_new = b_last + jnp.maximum(m_prev, jnp.max(g_r, axis=-1, keepdims=True))
        decay = jnp.exp((b_last + m_prev) - m_new)
        w_end = jnp.exp((g_r + b_last) - m_new)
        both = _dot(jnp.concatenate([s.astype(BF16), (kT.astype(F32) * w_end).astype(BF16)], axis=0), vaug)
        sva = both[0:chunk]
        qc = _dot(q, cn.astype(BF16))
        w_inter = jnp.exp(m_prev - m_rep)
        den = w_inter * qc[:, ML_HEAD_DIM:] + sva[:, ML_HEAD_DIM:]
        rinv = 1.0 / jnp.maximum(jnp.abs(den), jnp.exp(-mt_rep))
        hh_ref[rows, hs] = (wide(w_inter) * qc[:, :ML_HEAD_DIM] + sva[:, :ML_HEAD_DIM]) * wide(rinv)
        cn_ref[h] = decay * cn + both[chunk:]
        m_ref[h:h + 1, :] = jnp.broadcast_to(m_new, (1, LANES))


def _mlstm_prompt_kernel(q_ref, kT_ref, v_ref, gr_ref, hh_ref, cout_ref, nout_ref, mout_ref, cn_ref, m_ref, *, chunk):
    t = pl.program_id(1)
    nb = q_ref.shape[0]

    @pl.when(t == 0)
    def _():
        cn_ref[...] = jnp.zeros(cn_ref.shape, F32)
        m_ref[...] = jnp.full(m_ref.shape, M_INIT, F32)

    for r0 in range(0, q_ref.shape[1], chunk):
        for s in range(nb):
            _mlstm_chunk(q_ref.at[s], kT_ref.at[s], v_ref.at[s], gr_ref.at[s], hh_ref.at[s], cn_ref.at[s],
                         m_ref.at[s], r0, chunk)

    @pl.when(t == pl.num_programs(1) - 1)
    def _():
        cout_ref[...] = cn_ref[:, :, :, 0:ML_HEAD_DIM]
        nout_ref[...] = cn_ref[:, :, :, ML_HEAD_DIM:AUG]
        mout_ref[...] = m_ref[...]


def _mlstm_prompt(q, kT, v, gr):
    nbatch, seq, _ = q.shape
    tm, nb = MLSTM_TILE, MLSTM_BATCHES
    row = pl.BlockSpec((nb, tm, D_MODEL), lambda b, t: (b, t, 0))
    state = lambda last: pl.BlockSpec((nb, ML_HEADS, ML_HEAD_DIM, last), lambda b, t: (b, 0, 0, 0))
    return pl.pallas_call(
        functools.partial(_mlstm_prompt_kernel, chunk=PROMPT_CHUNK),
        grid=(nbatch // nb, seq // tm),
        in_specs=[row, pl.BlockSpec((nb, D_MODEL, tm), lambda b, t: (b, 0, t)), row,
                  pl.BlockSpec((nb, SUBLANES, tm), lambda b, t: (b, 0, t))],
        out_specs=[row, state(ML_HEAD_DIM), state(LANES), pl.BlockSpec((nb, SUBLANES, LANES), lambda b, t: (b, 0, 0))],
        out_shape=[jax.ShapeDtypeStruct((nbatch, seq, D_MODEL), F32),
                   jax.ShapeDtypeStruct((nbatch, ML_HEADS, ML_HEAD_DIM, ML_HEAD_DIM), F32),
                   jax.ShapeDtypeStruct((nbatch, ML_HEADS, ML_HEAD_DIM, LANES), F32),
                   jax.ShapeDtypeStruct((nbatch, SUBLANES, LANES), F32)],
        scratch_shapes=[pltpu.VMEM((nb, ML_HEADS, ML_HEAD_DIM, AUG), F32), pltpu.VMEM((nb, SUBLANES, LANES), F32)],
        compiler_params=_params(("arbitrary", "arbitrary")),
        name="mlstm_prompt",
    )(q, kT, v, gr)


def _last_in_group(x, group):
    rows = x.shape[0]
    x3 = jnp.broadcast_to(x, (rows, LANES)).reshape(rows // group, group, LANES)
    last = jnp.broadcast_to(x3[:, group - 1:group, :], x3.shape)
    return last.reshape(rows, LANES)[:, 0:1]


def _mlstm_sample_body(h, q_ref, kT_ref, k_ref, v_ref, gc_ref, gr_ref, m0_ref, c_ref, n_ref,
                       hh_ref, cout_ref, nout_ref, mt_ref, ni_ref, qn_ref, dec_ref, wk_ref, *, tt):
    L = q_ref.shape[0]
    nseq = L // tt
    gc = gc_ref[...]
    gr = gr_ref[...]
    ri = lax.broadcasted_iota(jnp.int32, (L, L), 0)
    ci = lax.broadcasted_iota(jnp.int32, (L, L), 1)
    same = (ri // tt) == (ci // tt)
    mask = same & (ri >= ci)
    bcol = _cumsum_cols(mask.astype(BF16), gc)
    brow = _cumsum_rows(gr, (same & (ri <= ci)).astype(BF16))
    lane = lax.broadcasted_iota(jnp.int32, (L, GATE_LANES), 1)
    sub = lax.broadcasted_iota(jnp.int32, (SUBLANES, L), 0)
    pick_col = lambda arr, idx: jnp.sum(jnp.where(lane == idx, arr, 0.0), axis=-1, keepdims=True)
    pick_row = lambda arr, idx: jnp.sum(jnp.where(sub == idx, arr, 0.0), axis=0, keepdims=True)
    ig_c = pick_col(gc, h)
    b_c = pick_col(bcol, ML_HEADS + h)
    ig_r = pick_row(gr, h)
    b_r = pick_row(brow, ML_HEADS + h)
    m_prev = m0_ref[0]

    for j in range(nseq):
        rows = slice(j * tt, (j + 1) * tt)
        qj = q_ref[rows, :]
        ni_ref[rows, :] = _dot(qj, c_ref[j, 0].astype(BF16))
        nj = n_ref[j, pl.ds(h, 1), :].astype(BF16).astype(F32)
        qn = jnp.sum(qj.astype(F32) * nj, axis=-1, keepdims=True)
        qn_ref[rows, :] = jnp.broadcast_to(qn, (tt, LANES))

    q = q_ref[...]
    kT = kT_ref[...]
    v = v_ref[...]
    hh, m_t, a_c = _mlstm_core(q, kT, v, b_c, b_r, ig_r, mask, m_prev, ni_ref[...], qn_ref[:, 0:1])
    hh_ref[...] = hh
    m_new = _last_in_group(m_t, tt)
    decay = jnp.exp(_last_in_group(a_c, tt) - m_new)
    w_end = jnp.exp((_last_in_group(b_c, tt) - b_c) + ig_c - m_new)
    mt_ref[0] = jnp.broadcast_to(m_t, (L, LANES))
    dec_ref[...] = jnp.broadcast_to(decay, (L, LANES))
    wv = w_end * v.astype(F32)
    wk_ref[...] = w_end.astype(BF16).astype(F32) * k_ref[...].astype(F32)
    rowi = lax.broadcasted_iota(jnp.int32, (L, 1), 0)

    for j in range(nseq):
        rows = slice(j * tt, (j + 1) * tt)
        upd = _dot(kT, jnp.where((rowi // tt) == j, wv, 0.0).astype(BF16))
        dj = dec_ref[j * tt:j * tt + 1, 0:1]
        cout_ref[j, 0] = dj * c_ref[j, 0] + upd
        nout_ref[j, pl.ds(h, 1), :] = (dj * n_ref[j, pl.ds(h, 1), :]
                                       + jnp.sum(wk_ref[rows, :], axis=0, keepdims=True))


def _mlstm_sample_specs(ntok, tt):
    nseq = ntok // tt
    sb = SAMPLE_SEQ_BLOCK
    L = sb * tt
    nh = ML_HEADS
    qspec = pl.BlockSpec((L, ML_HEAD_DIM), lambda i: (i // nh, i % nh))
    cspec = pl.BlockSpec((sb, 1, ML_HEAD_DIM, ML_HEAD_DIM), lambda i: (i // nh, i % nh, 0, 0))
    nspec = pl.BlockSpec((sb, nh, ML_HEAD_DIM), lambda i: (i // nh, 0, 0))
    in_specs = [qspec, pl.BlockSpec((ML_HEAD_DIM, L), lambda i: (i % nh, i // nh)), qspec, qspec,
                pl.BlockSpec((L, GATE_LANES), lambda i: (i // nh, 0)),
                pl.BlockSpec((SUBLANES, L), lambda i: (0, i // nh)),
                pl.BlockSpec((1, L, 1), lambda i: (i % nh, i // nh, 0)), cspec, nspec]
    out_specs = [qspec, cspec, nspec, pl.BlockSpec((1, L, LANES), lambda i: (i % nh, i // nh, 0))]
    out_shapes = [jax.ShapeDtypeStruct((ntok, D_MODEL), F32),
                  jax.ShapeDtypeStruct((nseq, ML_HEADS, ML_HEAD_DIM, ML_HEAD_DIM), F32),
                  jax.ShapeDtypeStruct((nseq, ML_HEADS, ML_HEAD_DIM), F32),
                  jax.ShapeDtypeStruct((ML_HEADS, ntok, LANES), F32)]
    scratch = [pltpu.VMEM((L, ML_HEAD_DIM), F32), pltpu.VMEM((L, LANES), F32),
               pltpu.VMEM((L, LANES), F32), pltpu.VMEM((L, ML_HEAD_DIM), F32)]
    return (nseq // sb) * nh, in_specs, out_specs, out_shapes, scratch


def _post_kernel(x_ref, sc_ref, sh_ref, gt_ref, g_ref, hh_ref, yp_ref, wt_ref, ghead_ref, wbp_ref, wbm_ref, wout_ref,
                 o_ref):
    pool = _dot(yp_ref[...], wbp_ref[...])
    x, hmod = _norm_mod(x_ref, sc_ref, sh_ref, g_ref)
    hb = hmod.astype(BF16)
    nb, tt, d = x_ref.shape
    o = _dot(hb, wt_ref[:, 0:D_MODEL])
    parts = []
    for h in range(ML_HEADS):
        hh = hh_ref[:, h * ML_HEAD_DIM:(h + 1) * ML_HEAD_DIM]
        parts.append(hh * lax.rsqrt(jnp.mean(hh * hh, axis=-1, keepdims=True) + EPS))
    yml = (jnp.concatenate(parts, axis=-1) * ghead_ref[...]) * jax.nn.sigmoid(o)
    gp = _dot(hb, wt_ref[:, D_MODEL:2 * D_MODEL])
    gm = _dot(hb, wt_ref[:, 2 * D_MODEL:3 * D_MODEL])
    merged = jax.nn.sigmoid(gp) * pool + jax.nn.sigmoid(gm) * _dot(yml.astype(BF16), wbm_ref[...])
    y = _dot(merged.astype(BF16), wout_ref[...]).reshape(nb, tt, d)
    o_ref[...] = x_ref[...] + gt_ref[0] * y


def _tile_blocks(x, tile):
    g, t, _ = x.shape
    if t >= tile:
        nb, tt = 1, tile
    else:
        nb, tt = tile // t, t
    return nb, tt, (g // nb) * (t // tt), t // tt


def _post(x, mod, row0, g_mix, hh, yp, w_tail, g_head, w_bp, w_bm, w_out):
    nb, tt, steps, per = _tile_blocks(x, TOKEN_TILE)
    tm = nb * tt
    xspec = pl.BlockSpec((nb, tt, D_MODEL), lambda i: (i // per, i % per, 0))
    sc, sh, gt = (_mod_spec(k, nb, row0, lambda i: i // per) for k in (_SC1, _SH1, _GT1))
    row = lambda i: (i, 0)
    return pl.pallas_call(
        _post_kernel,
        grid=(steps,),
        in_specs=[xspec, sc, sh, gt, _const_spec((1, D_MODEL)),
                  pl.BlockSpec((tm, D_MODEL), row), pl.BlockSpec((tm, POOL_WIDTH), row),
                  _const_spec((D_MODEL, 3 * D_MODEL)), _const_spec((1, D_MODEL)),
                  _const_spec((POOL_WIDTH, D_MODEL)), _const_spec((D_MODEL, D_MODEL)),
                  _const_spec((D_MODEL, D_MODEL))],
        out_specs=xspec,
        out_shape=jax.ShapeDtypeStruct(x.shape, F32),
        compiler_params=_params(("arbitrary",)),
        name="post",
    )(x, mod, mod, mod, g_mix, hh, yp, w_tail, g_head, w_bp, w_bm, w_out)


_FF_SPLITS = ((0, 768), (768, 1536), (1536, 2304), (2304, D_FF))


def _ffn_kernel(x_ref, sc_ref, sh_ref, gt_ref, g_ref, gfin_ref, wgu_ref, wdn_ref, o_ref):
    _, hmod = _norm_mod(x_ref, sc_ref, sh_ref, g_ref)
    hb = hmod.astype(BF16)
    nb, tt, d = x_ref.shape
    dn = None
    for lo, hi in _FF_SPLITS:
        gate = _dot(hb, wgu_ref[:, lo:hi])
        up = _dot(hb, wgu_ref[:, D_FF + lo:D_FF + hi])
        act = (gate * jax.nn.sigmoid(gate) * up).astype(BF16)
        part = _dot(act, wdn_ref[lo:hi, :])
        dn = part if dn is None else dn + part
    x2 = x_ref[...] + gt_ref[0] * dn.reshape(nb, tt, d)
    ms = jnp.mean(x2 * x2, axis=-1, keepdims=True)
    o_ref[...] = x2 * lax.rsqrt(ms + EPS) * gfin_ref[...]


_N_FFN_IN = 8


def _ffn_mlstm_kernel(*refs, n_ml_in, tt):
    ffn_in = refs[:_N_FFN_IN]
    ml_in = refs[_N_FFN_IN:_N_FFN_IN + n_ml_in]
    o_ref = refs[_N_FFN_IN + n_ml_in]
    ml_rest = refs[_N_FFN_IN + n_ml_in + 1:]
    _mlstm_sample_body(pl.program_id(0) % ML_HEADS, *ml_in, *ml_rest, tt=tt)
    _ffn_kernel(*ffn_in, o_ref)


def _ffn(x, mod, row0, g_ffn, g_final, w_gu, w_down, tile, mlstm_sample=None):
    nb, tt, steps, per = _tile_blocks(x, tile)
    xspec = pl.BlockSpec((nb, tt, D_MODEL), lambda i: (i // per, i % per, 0))
    sc, sh, gt = (_mod_spec(k, nb, row0, lambda i: i // per) for k in (_SC2, _SH2, _GT2))
    in_specs = [xspec, sc, sh, gt, _const_spec((1, D_MODEL)), _const_spec((1, D_MODEL)),
                _const_spec((D_MODEL, 2 * D_FF)), _const_spec((D_FF, D_MODEL))]
    operands = (x, mod, mod, mod, g_ffn, g_final, w_gu, w_down)
    if mlstm_sample is None:
        return pl.pallas_call(
            _ffn_kernel,
            grid=(steps,),
            in_specs=in_specs,
            out_specs=xspec,
            out_shape=jax.ShapeDtypeStruct(x.shape, F32),
            compiler_params=_params(("arbitrary",)),
            name="ffn",
        )(*operands)
    ml_operands, ml_tt = mlstm_sample
    ml_steps, ml_in, ml_out, ml_shapes, ml_scratch = _mlstm_sample_specs(ml_operands[0].shape[0], ml_tt)
    assert ml_steps == steps, "one sample mLSTM step per FFN tile"
    return pl.pallas_call(
        functools.partial(_ffn_mlstm_kernel, n_ml_in=len(ml_in), tt=ml_tt),
        grid=(steps,),
        in_specs=in_specs + ml_in,
        out_specs=[xspec] + ml_out,
        out_shape=[jax.ShapeDtypeStruct(x.shape, F32)] + ml_shapes,
        scratch_shapes=ml_scratch,
        compiler_params=_params(("arbitrary",)),
        name="ffn_mlstm",
    )(*operands, *ml_operands)


def kernel(x_prompt, x_sample, c_prompt, c_sample, state_pool, state_mlstm_c, state_mlstm_n, state_mlstm_m, g_mix, g_ffn, g_final, w_ada, b_ada, w_in, b_igate, b_fgate, w_pool, s_pool, g_head, w_branch_pool, w_branch_mlstm, w_out, w_gate_up, w_down):
    depth = w_in.shape[0]
    assert depth == 1, "single-layer trunk"
    nbatch, seq, _ = x_prompt.shape
    nseq, tt, _ = x_sample.shape
    l = 0

    bgate = jnp.pad(jnp.concatenate([b_igate[l], b_fgate[l]])[None, :], ((0, 0), (0, GATE_LANES - 2 * ML_HEADS)))
    g_mix_r, g_ffn_r, g_fin_r = g_mix[l][None, :], g_ffn[l][None, :], g_final[None, :]
    s_pool_r, g_head_r = s_pool[l][None, :], g_head[l][None, :]

    w_head, w_tail, w_gate, w_pool_b = _wprep(jnp.swapaxes(w_in, 1, 2), w_pool)
    mod = _ada(c_sample, c_prompt, w_ada[l], b_ada[l][None, :])
    row_s, row_p = 0, nseq

    (yp, hist_p, w_bp, w_bm, w_o, w_gu, w_dn, hh, c_p, n_p, m_p) = _inproj_prompt(
        x_prompt, mod, row_p, g_mix_r, w_head, w_gate, bgate, w_pool_b, s_pool_r,
        (w_branch_pool, w_branch_mlstm, w_out, w_gate_up, w_down))
    x1 = _post(x_prompt, mod, row_p, g_mix_r, hh, yp, w_tail, g_head_r, w_bp, w_bm, w_o)

    hist_pad = jnp.pad(state_pool[l], ((0, 0), (HIST_ROWS - POOL_HIST, 0), (0, 0)))
    yp, q, kT, v, gc, gr, k, hist_s = _inproj_sample(x_sample, mod, row_s, g_mix_r, w_head, w_gate, bgate, w_pool_b,
                                                     s_pool_r, hist_pad)
    m0_tok = jnp.repeat(state_mlstm_m[l].astype(F32).T, tt, axis=1)[:, :, None]
    ml_operands = (q, kT, k, v, gc, gr, m0_tok, state_mlstm_c[l].astype(F32), state_mlstm_n[l].astype(F32))
    y_prompt, hh, c_s, n_s, mt = _ffn(x1, mod, row_p, g_ffn_r, g_fin_r, w_gu, w_dn, PROMPT_FFN_TILE,
                                      (ml_operands, tt))

    x1 = _post(x_sample, mod, row_s, g_mix_r, hh, yp, w_tail, g_head_r, w_bp, w_bm, w_o)
    y_sample = _ffn(x1, mod, row_s, g_ffn_r, g_fin_r, w_gu, w_dn, FFN_TILE)

    cd, nd, md = state_mlstm_c.dtype, state_mlstm_n.dtype, state_mlstm_m.dtype
    return (y_prompt, y_sample,
            hist_p[None, :, HIST_ROWS - POOL_HIST:, :],
            c_p.astype(cd)[None], n_p[..., 0].astype(nd)[None], m_p[:, :ML_HEADS, 0].astype(md)[None],
            hist_s[None, :, HIST_ROWS - POOL_HIST:, :].astype(state_pool.dtype),
            c_s.astype(cd)[None], n_s.astype(nd)[None],
            mt[:, tt - 1::tt, 0].T.astype(md)[None])
```

```python
import functools

import jax
import jax.numpy as jnp
from jax import lax
from jax.experimental import pallas as pl
from jax.experimental.pallas import tpu as pltpu

D_MODEL = 1024
POOL_WINDOWS = (2, 4, 8, 16)
POOL_GROUP_DIM = 128
POOL_WIDTH = 512
POOL_HIST = 15
HIST_ROWS = 16
ML_HEADS = 4
ML_HEAD_DIM = 256
D_FF = 2816
EPS = 1e-6
M_INIT = -1e30
PAST_LEN = 16384
LANES = 128
SUBLANES = 8
BF16_ROWS = 16
GATE_LANES = LANES
AUG = ML_HEAD_DIM + LANES
VMEM_LIMIT = 56 * 1024 * 1024

TOKEN_TILE = 512
FFN_TILE = 1024
PROMPT_FFN_TILE = 512
PROMPT_CHUNK = 256
MLSTM_TILE = 1024
MLSTM_BATCHES = 1
SAMPLE_SEQ_BLOCK = 16

BF16 = jnp.bfloat16
F32 = jnp.float32

_U0, _Q0, _K0, _V0, _G0 = 0, 512, 1536, 2560, 3584


def _dot(a, b):
    return jnp.dot(a, b, preferred_element_type=F32)


def _const_spec(shape):
    zeros = (0,) * len(shape)
    return pl.BlockSpec(shape, lambda *_: zeros, pipeline_mode=pl.Buffered(1))


def _params(sem):
    return pltpu.CompilerParams(dimension_semantics=sem, vmem_limit_bytes=VMEM_LIMIT)


def _norm_mod(x_ref, sc_ref, sh_ref, g_ref):
    x = x_ref[...]
    nb, tt, d = x.shape
    ms = jnp.mean(x * x, axis=-1, keepdims=True)
    y = x * lax.rsqrt(ms + EPS) * g_ref[...]
    hmod = y * (1.0 + sc_ref[0]) + sh_ref[0]
    return x.reshape(nb * tt, d), hmod.reshape(nb * tt, d)


def _split3(x):
    hi = x.astype(BF16)
    r1 = x - hi.astype(F32)
    mid = r1.astype(BF16)
    lo = (r1 - mid.astype(F32)).astype(BF16)
    return hi, mid, lo


def _cumsum_cols(tri, x):
    hi, mid, lo = _split3(x)
    return _dot(tri, hi) + _dot(tri, mid) + _dot(tri, lo)


def _cumsum_rows(x, tri):
    hi, mid, lo = _split3(x)
    return _dot(hi, tri) + _dot(mid, tri) + _dot(lo, tri)


def _ada_kernel(cs_ref, cp_ref, w_ref, b_ref, o_ref):
    c = jnp.concatenate([cs_ref[...], cp_ref[...]], axis=0)
    a = (c * jax.nn.sigmoid(c)).astype(BF16)
    res = _dot(a, w_ref[...].astype(BF16)) + b_ref[...]
    o_ref[0] = res.reshape(res.shape[0], 1, res.shape[1])


def _ada(c_sample, c_prompt, w_ada, b_ada):
    rows = c_sample.shape[0] + c_prompt.shape[0]
    nchunk = w_ada.shape[1] // D_MODEL
    return pl.pallas_call(
        _ada_kernel,
        grid=(nchunk,),
        in_specs=[pl.BlockSpec(c_sample.shape, lambda j: (0, 0)),
                  pl.BlockSpec(c_prompt.shape, lambda j: (0, 0)),
                  pl.BlockSpec((D_MODEL, D_MODEL), lambda j: (0, j)),
                  pl.BlockSpec((1, D_MODEL), lambda j: (0, j))],
        out_specs=pl.BlockSpec((1, rows, 1, D_MODEL), lambda j: (j, 0, 0, 0)),
        out_shape=jax.ShapeDtypeStruct((nchunk, rows, 1, D_MODEL), F32),
        compiler_params=_params(("arbitrary",)),
        name="ada",
    )(c_sample, c_prompt, w_ada, b_ada)


_SH1, _SC1, _GT1, _SH2, _SC2, _GT2 = range(6)


def _mod_spec(k, nb, row0, block_index):
    return pl.BlockSpec((1, nb, 1, D_MODEL), lambda *g: (k, row0 // nb + block_index(*g), 0, 0))


_PREP_COLS = 512


def _wprep_kernel(wt_ref, gt_ref, wp_ref, head_ref, gate_ref, pool_ref):
    head_ref[...] = wt_ref[0].T.astype(BF16)
    gates = gt_ref[0].T
    pad = jnp.zeros((gates.shape[0], GATE_LANES - gates.shape[1]), F32)
    gate_ref[...] = jnp.concatenate([gates, pad], axis=1).astype(BF16)
    pool_ref[...] = wp_ref[0].astype(BF16)


def _wprep(w_in_t, w_pool):
    g0 = _G0 + D_MODEL
    return pl.pallas_call(
        _wprep_kernel,
        grid=(_G0 // _PREP_COLS,),
        in_specs=[pl.BlockSpec((1, _PREP_COLS, D_MODEL), lambda j: (0, j, 0)),
                  pl.BlockSpec((1, 2 * ML_HEADS, D_MODEL), lambda j: (0, g0 // (2 * ML_HEADS), 0)),
                  pl.BlockSpec(w_pool.shape, lambda j: (0, 0, 0, 0))],
        out_specs=[pl.BlockSpec((D_MODEL, _PREP_COLS), lambda j: (0, j)),
                   pl.BlockSpec((D_MODEL, GATE_LANES), lambda j: (0, 0)),
                   pl.BlockSpec(w_pool.shape[1:], lambda j: (0, 0, 0))],
        out_shape=[jax.ShapeDtypeStruct((D_MODEL, _G0), BF16),
                   jax.ShapeDtypeStruct((D_MODEL, GATE_LANES), BF16),
                   jax.ShapeDtypeStruct(w_pool.shape[1:], BF16)],
        compiler_params=_params(("arbitrary",)),
        name="wprep",
    )(w_in_t, w_in_t, w_pool)


_TAIL_BLOCKS = 3 * D_MODEL // LANES
_TAIL_SHIFT_BLOCK = D_MODEL // LANES
_TAIL_SHIFT = 2 * ML_HEADS


def _tail_block(step):
    return jnp.minimum(step, _TAIL_BLOCKS - 1)


def _tail_prep(step, wa_ref, wb_ref, tail_ref):
    c = _tail_block(step)
    a = wa_ref[0]
    shifted = jnp.concatenate([a[_TAIL_SHIFT:, :], wb_ref[0]], axis=0)
    tail_ref[...] = jnp.where(c >= _TAIL_SHIFT_BLOCK, shifted, a).T.astype(BF16)


def _project(hb, w_ref, wg_ref, bg_ref, q_ref, kT_ref, v_ref, gc_ref, gr_ref):
    q_ref[...] = _dot(hb, w_ref[:, _Q0:_K0]).astype(BF16)
    k = _dot(hb, w_ref[:, _K0:_V0]) * (ML_HEAD_DIM ** -0.5)
    kT_ref[...] = k.T.astype(BF16)
    v_ref[...] = _dot(hb, w_ref[:, _V0:_G0]).astype(BF16)
    zg = _dot(hb, wg_ref[...]) + bg_ref[...]
    log_f = jnp.minimum(zg, 0.0) - jnp.log1p(jnp.exp(-jnp.abs(zg)))
    lane = lax.broadcasted_iota(jnp.int32, zg.shape, 1)
    gc = jnp.where(lane < ML_HEADS, zg, log_f)
    gc_ref[...] = gc
    gr_ref[...] = gc.T[0:SUBLANES, :]
    return k


def _pool_project(acc_fn, u, cnt_fn, wpool_ref, spool_ref):
    outs = []
    for g, w in enumerate(POOL_WINDOWS):
        cols = slice(g * POOL_GROUP_DIM, (g + 1) * POOL_GROUP_DIM)
        ug = u[:, cols]
        d = acc_fn(g, w, ug) / cnt_fn(w) - ug
        outs.append(_dot(d.astype(BF16), wpool_ref[g]) * spool_ref[:, cols])
    return jnp.concatenate(outs, axis=-1).astype(BF16)


def _inproj_prompt_kernel(x_ref, sc_ref, sh_ref, g_ref, w_ref, wg_ref, bg_ref, wpool_ref, spool_ref,
                          wbp_ref, wbm_ref, wout_ref, wgu_ref, wdn_ref, wa_ref, wb_ref,
                          yp_ref, q_ref, kT_ref, v_ref, gc_ref, gr_ref, hout_ref,
                          wbp_o, wbm_o, wout_o, wgu_o, wdn_o, tail_o, ext_ref, *, tm):
    t = pl.program_id(1)
    _tail_prep(pl.program_id(0) * pl.num_programs(1) + t, wa_ref, wb_ref, tail_o)

    @pl.when(t == 0)
    def _():
        ext_ref[0:HIST_ROWS, :] = jnp.zeros((HIST_ROWS, POOL_WIDTH), F32)

    for src, dst in ((wbp_ref, wbp_o), (wbm_ref, wbm_o), (wout_ref, wout_o), (wgu_ref, wgu_o), (wdn_ref, wdn_o)):
        dst[...] = src[0].astype(BF16)

    _, hmod = _norm_mod(x_ref, sc_ref, sh_ref, g_ref)
    hb = hmod.astype(BF16)
    _project(hb, w_ref, wg_ref, bg_ref, q_ref, kT_ref.at[0], v_ref, gc_ref, gr_ref.at[0])

    u = _dot(hb, w_ref[:, _U0:_Q0])
    ext_ref[HIST_ROWS:HIST_ROWS + tm, :] = u
    pos = t * tm + lax.broadcasted_iota(jnp.int32, (tm, 1), 0)

    def acc_fn(g, w, ug):
        acc = ug
        for j in range(1, w):
            acc = acc + ext_ref[pl.ds(HIST_ROWS - j, tm), g * POOL_GROUP_DIM:(g + 1) * POOL_GROUP_DIM]
        return acc

    def cnt_fn(w):
        return jnp.minimum(pos + 1, w).astype(F32)

    yp_ref[...] = _pool_project(acc_fn, u, cnt_fn, wpool_ref, spool_ref)
    last = ext_ref[tm:tm + HIST_ROWS, :]
    hout_ref[0] = last
    ext_ref[0:HIST_ROWS, :] = last


def _inproj_sample_kernel(x_ref, sc_ref, sh_ref, g_ref, w_ref, wg_ref, bg_ref, wpool_ref, spool_ref, hist_ref,
                          yp_ref, q_ref, kT_ref, v_ref, gc_ref, gr_ref, k_ref, hout_ref, u_ref, d_ref, *, pos0):
    _, hmod = _norm_mod(x_ref, sc_ref, sh_ref, g_ref)
    hb = hmod.astype(BF16)
    k_ref[...] = _project(hb, w_ref, wg_ref, bg_ref, q_ref, kT_ref, v_ref, gc_ref, gr_ref).astype(BF16)

    nb, tt = x_ref.shape[0], x_ref.shape[1]
    u = _dot(hb, w_ref[:, _U0:_Q0])
    outs = []
    for g, w in enumerate(POOL_WINDOWS):
        cols = slice(g * POOL_GROUP_DIM, (g + 1) * POOL_GROUP_DIM)
        u_ref[g] = u[:, cols]
        new = [u_ref[g, pl.ds(t, nb, stride=tt), :] for t in range(tt)]

        def ext(r):
            return hist_ref[r, :, cols] if r < POOL_HIST else new[r - POOL_HIST]

        for t in range(tt):
            acc = new[t]
            for j in range(1, w):
                acc = acc + ext(POOL_HIST + t - j)
            cnt = float(min(pos0 + t + 1, w))
            d_ref[g, pl.ds(t, nb, stride=tt), :] = acc / cnt - new[t]
        outs.append(_dot(d_ref[g].astype(BF16), wpool_ref[g]))
        for r in range(POOL_HIST):
            hout_ref[r, :, cols] = ext(r + tt)
    yp_ref[...] = (jnp.concatenate(outs, axis=-1) * spool_ref[...]).astype(BF16)


def _cast_specs(weights, nt, steps):
    in_specs, out_specs, out_shapes = [], [], []
    for w in weights:
        _, r, c = w.shape
        n = steps
        while r % n or (r // n) % BF16_ROWS:
            n //= 2
        rows = r // n
        idx = lambda b, t, n=n: jnp.minimum(b * nt + t, n - 1)
        in_specs.append(pl.BlockSpec((1, rows, c), lambda b, t, idx=idx: (0, idx(b, t), 0)))
        out_specs.append(pl.BlockSpec((rows, c), lambda b, t, idx=idx: (idx(b, t), 0)))
        out_shapes.append(jax.ShapeDtypeStruct((r, c), BF16))
    return in_specs, out_specs, out_shapes


def _inproj_prompt(x, mod, row0, g_mix, w_head, w_gate, bgate, w_pool, s_pool, later_weights, w_in_t):
    nbatch, seq, _ = x.shape
    tm = TOKEN_TILE
    nt = seq // tm
    ntok = nbatch * seq
    row = lambda b, t: (b * nt + t, 0)
    sc, sh = (_mod_spec(k, 1, row0, lambda b, t: b) for k in (_SC1, _SH1))
    cast_in, cast_out, cast_shapes = _cast_specs(later_weights, nt, nbatch * nt)
    tail_row = lambda b, t: _G0 // LANES + _tail_block(b * nt + t)
    cast_in = cast_in + [
        pl.BlockSpec((1, LANES, D_MODEL), lambda b, t: (0, tail_row(b, t), 0)),
        pl.BlockSpec((1, _TAIL_SHIFT, D_MODEL), lambda b, t: (0, (tail_row(b, t) + 1) * (LANES // _TAIL_SHIFT), 0))]
    cast_out = cast_out + [pl.BlockSpec((D_MODEL, LANES), lambda b, t: (0, _tail_block(b * nt + t)))]
    cast_shapes = cast_shapes + [jax.ShapeDtypeStruct((D_MODEL, _TAIL_BLOCKS * LANES), BF16)]
    return pl.pallas_call(
        functools.partial(_inproj_prompt_kernel, tm=tm),
        grid=(nbatch, nt),
        in_specs=[pl.BlockSpec((1, tm, D_MODEL), lambda b, t: (b, t, 0)), sc, sh,
                  _const_spec((1, D_MODEL)), _const_spec((D_MODEL, _G0)), _const_spec((D_MODEL, GATE_LANES)),
                  _const_spec((1, GATE_LANES)), _const_spec((4, POOL_GROUP_DIM, POOL_GROUP_DIM)),
                  _const_spec((1, POOL_WIDTH))] + cast_in,
        out_specs=[pl.BlockSpec((tm, POOL_WIDTH), row), pl.BlockSpec((tm, D_MODEL), row),
                   pl.BlockSpec((1, D_MODEL, tm), lambda b, t: (b, 0, t)), pl.BlockSpec((tm, D_MODEL), row),
                   pl.BlockSpec((tm, GATE_LANES), row), pl.BlockSpec((1, SUBLANES, tm), lambda b, t: (b, 0, t)),
                   pl.BlockSpec((1, HIST_ROWS, POOL_WIDTH), lambda b, t: (b, 0, 0))] + cast_out,
        out_shape=[jax.ShapeDtypeStruct((ntok, POOL_WIDTH), BF16), jax.ShapeDtypeStruct((ntok, D_MODEL), BF16),
                   jax.ShapeDtypeStruct((nbatch, D_MODEL, seq), BF16), jax.ShapeDtypeStruct((ntok, D_MODEL), BF16),
                   jax.ShapeDtypeStruct((ntok, GATE_LANES), F32), jax.ShapeDtypeStruct((nbatch, SUBLANES, seq), F32),
                   jax.ShapeDtypeStruct((nbatch, HIST_ROWS, POOL_WIDTH), F32)] + cast_shapes,
        scratch_shapes=[pltpu.VMEM((tm + HIST_ROWS, POOL_WIDTH), F32)],
        compiler_params=_params(("arbitrary", "arbitrary")),
        name="inproj_prompt",
    )(x, mod, mod, g_mix, w_head, w_gate, bgate, w_pool, s_pool, *later_weights, w_in_t, w_in_t)


def _inproj_sample(x, mod, row0, g_mix, w_head, w_gate, bgate, w_pool, s_pool, hist_tm):
    nseq, tt, _ = x.shape
    nb = TOKEN_TILE // tt
    tm = nb * tt
    ntok = nseq * tt
    row = lambda i: (i, 0)
    col = lambda i: (0, i)
    sc, sh = (_mod_spec(k, nb, row0, lambda i: i) for k in (_SC1, _SH1))
    hist = pl.BlockSpec((POOL_HIST, nb, POOL_WIDTH), lambda i: (0, i, 0))
    return pl.pallas_call(
        functools.partial(_inproj_sample_kernel, pos0=PAST_LEN),
        grid=(nseq // nb,),
        in_specs=[pl.BlockSpec((nb, tt, D_MODEL), lambda i: (i, 0, 0)), sc, sh,
                  _const_spec((1, D_MODEL)), _const_spec((D_MODEL, _G0)), _const_spec((D_MODEL, GATE_LANES)),
                  _const_spec((1, GATE_LANES)), _const_spec((4, POOL_GROUP_DIM, POOL_GROUP_DIM)),
                  _const_spec((1, POOL_WIDTH)), hist],
        out_specs=[pl.BlockSpec((tm, POOL_WIDTH), row), pl.BlockSpec((tm, D_MODEL), row),
                   pl.BlockSpec((D_MODEL, tm), col), pl.BlockSpec((tm, D_MODEL), row),
                   pl.BlockSpec((tm, GATE_LANES), row), pl.BlockSpec((SUBLANES, tm), col),
                   pl.BlockSpec((tm, D_MODEL), row), hist],
        out_shape=[jax.ShapeDtypeStruct((ntok, POOL_WIDTH), BF16), jax.ShapeDtypeStruct((ntok, D_MODEL), BF16),
                   jax.ShapeDtypeStruct((D_MODEL, ntok), BF16), jax.ShapeDtypeStruct((ntok, D_MODEL), BF16),
                   jax.ShapeDtypeStruct((ntok, GATE_LANES), F32), jax.ShapeDtypeStruct((SUBLANES, ntok), F32),
                   jax.ShapeDtypeStruct((ntok, D_MODEL), BF16),
                   jax.ShapeDtypeStruct((POOL_HIST, nseq, POOL_WIDTH), F32)],
        scratch_shapes=[pltpu.VMEM((len(POOL_WINDOWS), tm, POOL_GROUP_DIM), F32)] * 2,
        compiler_params=_params(("arbitrary",)),
        name="inproj_sample",
    )(x, mod, mod, g_mix, w_head, w_gate, bgate, w_pool, s_pool, hist_tm)


def _mlstm_core(q, kT, v, b_c, b_r, ig_r, mask, m_prev, num_inter, qn):
    logd = jnp.where(mask, (b_c - b_r) + ig_r, -jnp.inf)
    a_c = b_c + m_prev
    m_t = jnp.maximum(a_c, jnp.max(logd, axis=-1, keepdims=True))
    w_intra = jnp.exp(logd - m_t)
    w_inter = jnp.exp(a_c - m_t)
    s = _dot(q, kT) * w_intra
    num = w_inter * num_inter + _dot(s.astype(BF16), v)
    den = w_inter * qn + jnp.sum(s, axis=-1, keepdims=True)
    hh = num / jnp.maximum(jnp.abs(den), jnp.exp(-m_t))
    return hh, m_t, a_c


def _mlstm_chunk(q_ref, kT_ref, v_ref, gr_ref, hh_ref, cn_ref, m_ref, r0, chunk):
    rows = slice(r0, r0 + chunk)
    gr = gr_ref[:, rows]
    ri = lax.broadcasted_iota(jnp.int32, (chunk, chunk), 0)
    ci = lax.broadcasted_iota(jnp.int32, (chunk, chunk), 1)
    causal = ri >= ci
    brow = _cumsum_rows(gr, (ri <= ci).astype(BF16))
    bcol = brow.T
    ones = jnp.ones((chunk, LANES), BF16)
    wide = lambda x: jnp.concatenate([x, x], axis=-1)
    for h in range(ML_HEADS):
        hs = slice(h * ML_HEAD_DIM, (h + 1) * ML_HEAD_DIM)
        b_r = brow[ML_HEADS + h:ML_HEADS + h + 1, :]
        g_r = gr[h:h + 1, :] - b_r
        m_prev = m_ref[h:h + 1, 0:1]
        cn = cn_ref[h]
        q = q_ref[rows, hs]
        kT = kT_ref[hs, rows]
        vaug = jnp.concatenate([v_ref[rows, hs], ones], axis=-1)
        gm = jnp.where(causal, g_r, -jnp.inf)
        big_m = jnp.maximum(m_prev, jnp.max(gm, axis=-1, keepdims=True))
        m_rep = jnp.broadcast_to(big_m, (chunk, LANES))
        mt_rep = jnp.broadcast_to(bcol[:, ML_HEADS + h:ML_HEADS + h + 1] + big_m, (chunk, LANES))
        s = _dot(q, kT) * jnp.exp(gm - wide(m_rep))
        b_last = b_r[:, chunk - 1:chunk]
        m_new = b_last + jnp.maximum(m_prev, jnp.max(g_r, axis=-1, keepdims=True))
        decay = jnp.exp((b_last + m_prev) - m_new)
        w_end = jnp.exp((g_r + b_last) - m_new)
        both = _dot(jnp.concatenate([s.astype(BF16), (kT.astype(F32) * w_end).astype(BF16)], axis=0), vaug)
        sva = both[0:chunk]
        qc = _dot(q, cn.astype(BF16))
        w_inter = jnp.exp(m_prev - m_rep)
        den = w_inter * qc[:, ML_HEAD_DIM:] + sva[:, ML_HEAD_DIM:]
        rinv = 1.0 / jnp.maximum(jnp.abs(den), jnp.exp(-mt_rep))
        hh_ref[rows, hs] = (wide(w_inter) * qc[:, :ML_HEAD_DIM] + sva[:, :ML_HEAD_DIM]) * wide(rinv)
        cn_ref[h] = decay * cn + both[chunk:]
        m_ref[h:h + 1, :] = jnp.broadcast_to(m_new, (1, LANES))


def _mlstm_prompt_kernel(q_ref, kT_ref, v_ref, gr_ref, hh_ref, cout_ref, nout_ref, mout_ref, cn_ref, m_ref, *, chunk):
    t = pl.program_id(1)
    nb = q_ref.shape[0]

    @pl.when(t == 0)
    def _():
        cn_ref[...] = jnp.zeros(cn_ref.shape, F32)
        m_ref[...] = jnp.full(m_ref.shape, M_INIT, F32)

    for r0 in range(0, q_ref.shape[1], chunk):
        for s in range(nb):
            _mlstm_chunk(q_ref.at[s], kT_ref.at[s], v_ref.at[s], gr_ref.at[s], hh_ref.at[s], cn_ref.at[s],
                         m_ref.at[s], r0, chunk)

    @pl.when(t == pl.num_programs(1) - 1)
    def _():
        cout_ref[...] = cn_ref[:, :, :, 0:ML_HEAD_DIM]
        nout_ref[...] = cn_ref[:, :, :, ML_HEAD_DIM:AUG]
        mout_ref[...] = m_ref[...]


def _mlstm_prompt(q, kT, v, gr):
    nbatch, seq, _ = q.shape
    tm, nb = MLSTM_TILE, MLSTM_BATCHES
    row = pl.BlockSpec((nb, tm, D_MODEL), lambda b, t: (b, t, 0))
    state = lambda last: pl.BlockSpec((nb, ML_HEADS, ML_HEAD_DIM, last), lambda b, t: (b, 0, 0, 0))
    return pl.pallas_call(
        functools.partial(_mlstm_prompt_kernel, chunk=PROMPT_CHUNK),
        grid=(nbatch // nb, seq // tm),
        in_specs=[row, pl.BlockSpec((nb, D_MODEL, tm), lambda b, t: (b, 0, t)), row,
                  pl.BlockSpec((nb, SUBLANES, tm), lambda b, t: (b, 0, t))],
        out_specs=[row, state(ML_HEAD_DIM), state(LANES), pl.BlockSpec((nb, SUBLANES, LANES), lambda b, t: (b, 0, 0))],
        out_shape=[jax.ShapeDtypeStruct((nbatch, seq, D_MODEL), F32),
                   jax.ShapeDtypeStruct((nbatch, ML_HEADS, ML_HEAD_DIM, ML_HEAD_DIM), F32),
                   jax.ShapeDtypeStruct((nbatch, ML_HEADS, ML_HEAD_DIM, LANES), F32),
                   jax.ShapeDtypeStruct((nbatch, SUBLANES, LANES), F32)],
        scratch_shapes=[pltpu.VMEM((nb, ML_HEADS, ML_HEAD_DIM, AUG), F32), pltpu.VMEM((nb, SUBLANES, LANES), F32)],
        compiler_params=_params(("arbitrary", "arbitrary")),
        name="mlstm_prompt",
    )(q, kT, v, gr)


def _last_in_group(x, group):
    rows = x.shape[0]
    x3 = jnp.broadcast_to(x, (rows, LANES)).reshape(rows // group, group, LANES)
    last = jnp.broadcast_to(x3[:, group - 1:group, :], x3.shape)
    return last.reshape(rows, LANES)[:, 0:1]


def _mlstm_sample_body(h, q_ref, kT_ref, k_ref, v_ref, gc_ref, gr_ref, m0_ref, c_ref, n_ref,
                       hh_ref, cout_ref, nout_ref, mt_ref, ni_ref, qn_ref, dec_ref, wk_ref, *, tt):
    L = q_ref.shape[0]
    nseq = L // tt
    gc = gc_ref[...]
    gr = gr_ref[...]
    ri = lax.broadcasted_iota(jnp.int32, (L, L), 0)
    ci = lax.broadcasted_iota(jnp.int32, (L, L), 1)
    same = (ri // tt) == (ci // tt)
    mask = same & (ri >= ci)
    bcol = _cumsum_cols(mask.astype(BF16), gc)
    brow = _cumsum_rows(gr, (same & (ri <= ci)).astype(BF16))
    lane = lax.broadcasted_iota(jnp.int32, (L, GATE_LANES), 1)
    sub = lax.broadcasted_iota(jnp.int32, (SUBLANES, L), 0)
    pick_col = lambda arr, idx: jnp.sum(jnp.where(lane == idx, arr, 0.0), axis=-1, keepdims=True)
    pick_row = lambda arr, idx: jnp.sum(jnp.where(sub == idx, arr, 0.0), axis=0, keepdims=True)
    ig_c = pick_col(gc, h)
    b_c = pick_col(bcol, ML_HEADS + h)
    ig_r = pick_row(gr, h)
    b_r = pick_row(brow, ML_HEADS + h)
    m_prev = m0_ref[0]

    for j in range(nseq):
        rows = slice(j * tt, (j + 1) * tt)
        qj = q_ref[rows, :]
        ni_ref[rows, :] = _dot(qj, c_ref[j, 0].astype(BF16))
        nj = n_ref[j, pl.ds(h, 1), :].astype(BF16).astype(F32)
        qn = jnp.sum(qj.astype(F32) * nj, axis=-1, keepdims=True)
        qn_ref[rows, :] = jnp.broadcast_to(qn, (tt, LANES))

    q = q_ref[...]
    kT = kT_ref[...]
    v = v_ref[...]
    hh, m_t, a_c = _mlstm_core(q, kT, v, b_c, b_r, ig_r, mask, m_prev, ni_ref[...], qn_ref[:, 0:1])
    hh_ref[...] = hh
    m_new = _last_in_group(m_t, tt)
    decay = jnp.exp(_last_in_group(a_c, tt) - m_new)
    w_end = jnp.exp((_last_in_group(b_c, tt) - b_c) + ig_c - m_new)
    mt_ref[0] = jnp.broadcast_to(m_t, (L, LANES))
    dec_ref[...] = jnp.broadcast_to(decay, (L, LANES))
    wv = w_end * v.astype(F32)
    wk_ref[...] = w_end.astype(BF16).astype(F32) * k_ref[...].astype(F32)
    rowi = lax.broadcasted_iota(jnp.int32, (L, 1), 0)

    for j in range(nseq):
        rows = slice(j * tt, (j + 1) * tt)
        upd = _dot(kT, jnp.where((rowi // tt) == j, wv, 0.0).astype(BF16))
        dj = dec_ref[j * tt:j * tt + 1, 0:1]
        cout_ref[j, 0] = dj * c_ref[j, 0] + upd
        nout_ref[j, pl.ds(h, 1), :] = (dj * n_ref[j, pl.ds(h, 1), :]
                                       + jnp.sum(wk_ref[rows, :], axis=0, keepdims=True))


def _mlstm_sample_specs(ntok, tt):
    nseq = ntok // tt
    sb = SAMPLE_SEQ_BLOCK
    L = sb * tt
    nh = ML_HEADS
    qspec = pl.BlockSpec((L, ML_HEAD_DIM), lambda i: (i // nh, i % nh))
    cspec = pl.BlockSpec((sb, 1, ML_HEAD_DIM, ML_HEAD_DIM), lambda i: (i // nh, i % nh, 0, 0))
    nspec = pl.BlockSpec((sb, nh, ML_HEAD_DIM), lambda i: (i // nh, 0, 0))
    in_specs = [qspec, pl.BlockSpec((ML_HEAD_DIM, L), lambda i: (i % nh, i // nh)), qspec, qspec,
                pl.BlockSpec((L, GATE_LANES), lambda i: (i // nh, 0)),
                pl.BlockSpec((SUBLANES, L), lambda i: (0, i // nh)),
                pl.BlockSpec((1, L, 1), lambda i: (i % nh, i // nh, 0)), cspec, nspec]
    out_specs = [qspec, cspec, nspec, pl.BlockSpec((1, L, LANES), lambda i: (i % nh, i // nh, 0))]
    out_shapes = [jax.ShapeDtypeStruct((ntok, D_MODEL), F32),
                  jax.ShapeDtypeStruct((nseq, ML_HEADS, ML_HEAD_DIM, ML_HEAD_DIM), F32),
                  jax.ShapeDtypeStruct((nseq, ML_HEADS, ML_HEAD_DIM), F32),
                  jax.ShapeDtypeStruct((ML_HEADS, ntok, LANES), F32)]
    scratch = [pltpu.VMEM((L, ML_HEAD_DIM), F32), pltpu.VMEM((L, LANES), F32),
               pltpu.VMEM((L, LANES), F32), pltpu.VMEM((L, ML_HEAD_DIM), F32)]
    return (nseq // sb) * nh, in_specs, out_specs, out_shapes, scratch


def _post_kernel(x_ref, sc_ref, sh_ref, gt_ref, g_ref, hh_ref, yp_ref, wt_ref, ghead_ref, wbp_ref, wbm_ref, wout_ref,
                 o_ref):
    pool = _dot(yp_ref[...], wbp_ref[...])
    x, hmod = _norm_mod(x_ref, sc_ref, sh_ref, g_ref)
    hb = hmod.astype(BF16)
    nb, tt, d = x_ref.shape
    o = _dot(hb, wt_ref[:, 0:D_MODEL])
    parts = []
    for h in range(ML_HEADS):
        hh = hh_ref[:, h * ML_HEAD_DIM:(h + 1) * ML_HEAD_DIM]
        parts.append(hh * lax.rsqrt(jnp.mean(hh * hh, axis=-1, keepdims=True) + EPS))
    yml = (jnp.concatenate(parts, axis=-1) * ghead_ref[...]) * jax.nn.sigmoid(o)
    gp = _dot(hb, wt_ref[:, D_MODEL:2 * D_MODEL])
    gm = _dot(hb, wt_ref[:, 2 * D_MODEL:3 * D_MODEL])
    merged = jax.nn.sigmoid(gp) * pool + jax.nn.sigmoid(gm) * _dot(yml.astype(BF16), wbm_ref[...])
    y = _dot(merged.astype(BF16), wout_ref[...]).reshape(nb, tt, d)
    o_ref[...] = x_ref[...] + gt_ref[0] * y


def _tile_blocks(x, tile):
    g, t, _ = x.shape
    if t >= tile:
        nb, tt = 1, tile
    else:
        nb, tt = tile // t, t
    return nb, tt, (g // nb) * (t // tt), t // tt


def _post(x, mod, row0, g_mix, hh, yp, w_tail, g_head, w_bp, w_bm, w_out):
    nb, tt, steps, per = _tile_blocks(x, TOKEN_TILE)
    tm = nb * tt
    xspec = pl.BlockSpec((nb, tt, D_MODEL), lambda i: (i // per, i % per, 0))
    sc, sh, gt = (_mod_spec(k, nb, row0, lambda i: i // per) for k in (_SC1, _SH1, _GT1))
    row = lambda i: (i, 0)
    return pl.pallas_call(
        _post_kernel,
        grid=(steps,),
        in_specs=[xspec, sc, sh, gt, _const_spec((1, D_MODEL)),
                  pl.BlockSpec((tm, D_MODEL), row), pl.BlockSpec((tm, POOL_WIDTH), row),
                  _const_spec((D_MODEL, 3 * D_MODEL)), _const_spec((1, D_MODEL)),
                  _const_spec((POOL_WIDTH, D_MODEL)), _const_spec((D_MODEL, D_MODEL)),
                  _const_spec((D_MODEL, D_MODEL))],
        out_specs=xspec,
        out_shape=jax.ShapeDtypeStruct(x.shape, F32),
        compiler_params=_params(("arbitrary",)),
        name="post",
    )(x, mod, mod, mod, g_mix, hh, yp, w_tail, g_head, w_bp, w_bm, w_out)


_FF_SPLITS = ((0, 768), (768, 1536), (1536, 2304), (2304, D_FF))


def _ffn_kernel(x_ref, sc_ref, sh_ref, gt_ref, g_ref, gfin_ref, wgu_ref, wdn_ref, o_ref):
    _, hmod = _norm_mod(x_ref, sc_ref, sh_ref, g_ref)
    hb = hmod.astype(BF16)
    nb, tt, d = x_ref.shape
    dn = None
    for lo, hi in _FF_SPLITS:
        gate = _dot(hb, wgu_ref[:, lo:hi])
        up = _dot(hb, wgu_ref[:, D_FF + lo:D_FF + hi])
        act = (gate * jax.nn.sigmoid(gate) * up).astype(BF16)
        part = _dot(act, wdn_ref[lo:hi, :])
        dn = part if dn is None else dn + part
    x2 = x_ref[...] + gt_ref[0] * dn.reshape(nb, tt, d)
    ms = jnp.mean(x2 * x2, axis=-1, keepdims=True)
    o_ref[...] = x2 * lax.rsqrt(ms + EPS) * gfin_ref[...]


_N_FFN_IN = 8


def _ffn_mlstm_kernel(*refs, n_ml_in, tt):
    ffn_in = refs[:_N_FFN_IN]
    ml_in = refs[_N_FFN_IN:_N_FFN_IN + n_ml_in]
    o_ref = refs[_N_FFN_IN + n_ml_in]
    ml_rest = refs[_N_FFN_IN + n_ml_in + 1:]
    _mlstm_sample_body(pl.program_id(0) % ML_HEADS, *ml_in, *ml_rest, tt=tt)
    _ffn_kernel(*ffn_in, o_ref)


def _ffn(x, mod, row0, g_ffn, g_final, w_gu, w_down, tile, mlstm_sample=None):
    nb, tt, steps, per = _tile_blocks(x, tile)
    xspec = pl.BlockSpec((nb, tt, D_MODEL), lambda i: (i // per, i % per, 0))
    sc, sh, gt = (_mod_spec(k, nb, row0, lambda i: i // per) for k in (_SC2, _SH2, _GT2))
    in_specs = [xspec, sc, sh, gt, _const_spec((1, D_MODEL)), _const_spec((1, D_MODEL)),
                _const_spec((D_MODEL, 2 * D_FF)), _const_spec((D_FF, D_MODEL))]
    operands = (x, mod, mod, mod, g_ffn, g_final, w_gu, w_down)
    if mlstm_sample is None:
        return pl.pallas_call(
            _ffn_kernel,
            grid=(steps,),
            in_specs=in_specs,
            out_specs=xspec,
            out_shape=jax.ShapeDtypeStruct(x.shape, F32),
            compiler_params=_params(("arbitrary",)),
            name="ffn",
        )(*operands)
    ml_operands, ml_tt = mlstm_sample
    ml_steps, ml_in, ml_out, ml_shapes, ml_scratch = _mlstm_sample_specs(ml_operands[0].shape[0], ml_tt)
    assert ml_steps == steps, "one sample mLSTM step per FFN tile"
    return pl.pallas_call(
        functools.partial(_ffn_mlstm_kernel, n_ml_in=len(ml_in), tt=ml_tt),
        grid=(steps,),
        in_specs=in_specs + ml_in,
        out_specs=[xspec] + ml_out,
        out_shape=[jax.ShapeDtypeStruct(x.shape, F32)] + ml_shapes,
        scratch_shapes=ml_scratch,
        compiler_params=_params(("arbitrary",)),
        name="ffn_mlstm",
    )(*operands, *ml_operands)


def kernel(x_prompt, x_sample, c_prompt, c_sample, state_pool, state_mlstm_c, state_mlstm_n, state_mlstm_m, g_mix, g_ffn, g_final, w_ada, b_ada, w_in, b_igate, b_fgate, w_pool, s_pool, g_head, w_branch_pool, w_branch_mlstm, w_out, w_gate_up, w_down):
    depth = w_in.shape[0]
    assert depth == 1, "single-layer trunk"
    nbatch, seq, _ = x_prompt.shape
    nseq, tt, _ = x_sample.shape
    l = 0

    bgate = jnp.pad(jnp.concatenate([b_igate[l], b_fgate[l]])[None, :], ((0, 0), (0, GATE_LANES - 2 * ML_HEADS)))
    g_mix_r, g_ffn_r, g_fin_r = g_mix[l][None, :], g_ffn[l][None, :], g_final[None, :]
    s_pool_r, g_head_r = s_pool[l][None, :], g_head[l][None, :]

    w_in_t = jnp.swapaxes(w_in, 1, 2)
    w_head, w_gate, w_pool_b = _wprep(w_in_t, w_pool)
    mod = _ada(c_sample, c_prompt, w_ada[l], b_ada[l][None, :])
    row_s, row_p = 0, nseq

    (yp, q, kT, v, gc, gr, hist_p, w_bp, w_bm, w_o, w_gu, w_dn, w_tail) = _inproj_prompt(
        x_prompt, mod, row_p, g_mix_r, w_head, w_gate, bgate, w_pool_b, s_pool_r,
        (w_branch_pool, w_branch_mlstm, w_out, w_gate_up, w_down), w_in_t)
    hh, c_p, n_p, m_p = _mlstm_prompt(q.reshape(nbatch, seq, D_MODEL), kT, v.reshape(nbatch, seq, D_MODEL), gr)
    hh = hh.reshape(nbatch * seq, D_MODEL)
    x1 = _post(x_prompt, mod, row_p, g_mix_r, hh, yp, w_tail, g_head_r, w_bp, w_bm, w_o)

    hist_tm = jnp.swapaxes(state_pool[l], 0, 1)
    yp, q, kT, v, gc, gr, k, hist_s = _inproj_sample(x_sample, mod, row_s, g_mix_r, w_head, w_gate, bgate, w_pool_b,
                                                     s_pool_r, hist_tm)
    m0_tok = jnp.repeat(state_mlstm_m[l].astype(F32).T, tt, axis=1)[:, :, None]
    ml_operands = (q, kT, k, v, gc, gr, m0_tok, state_mlstm_c[l].astype(F32), state_mlstm_n[l].astype(F32))
    y_prompt, hh, c_s, n_s, mt = _ffn(x1, mod, row_p, g_ffn_r, g_fin_r, w_gu, w_dn, PROMPT_FFN_TILE,
                                      (ml_operands, tt))

    x1 = _post(x_sample, mod, row_s, g_mix_r, hh, yp, w_tail, g_head_r, w_bp, w_bm, w_o)
    y_sample = _ffn(x1, mod, row_s, g_ffn_r, g_fin_r, w_gu, w_dn, FFN_TILE)

    cd, nd, md = state_mlstm_c.dtype, state_mlstm_n.dtype, state_mlstm_m.dtype
    return (y_prompt, y_sample,
            hist_p[None, :, HIST_ROWS - POOL_HIST:, :],
            c_p.astype(cd)[None], n_p[..., 0].astype(nd)[None], m_p[:, :ML_HEADS, 0].astype(md)[None],
            jnp.swapaxes(hist_s, 0, 1)[None].astype(state_pool.dtype),
            c_s.astype(cd)[None], n_s.astype(nd)[None],
            mt[:, tt - 1::tt, 0].T.astype(md)[None])
```

```python
import functools

import jax
import jax.numpy as jnp
from jax import lax
from jax.experimental import pallas as pl
from jax.experimental.pallas import tpu as pltpu

D_MODEL = 1024
POOL_WINDOWS = (2, 4, 8, 16)
POOL_GROUP_DIM = 128
POOL_WIDTH = 512
POOL_HIST = 15
HIST_ROWS = 16
ML_HEADS = 4
ML_HEAD_DIM = 256
D_FF = 2816
EPS = 1e-6
M_INIT = -1e30
PAST_LEN = 16384
LANES = 128
SUBLANES = 8
BF16_ROWS = 16
GATE_LANES = LANES
AUG = ML_HEAD_DIM + LANES
VMEM_LIMIT = 56 * 1024 * 1024

TOKEN_TILE = 512
FFN_TILE = 1024
PROMPT_FFN_TILE = 512
PROMPT_CHUNK = 256
MLSTM_TILE = 1024
MLSTM_BATCHES = 1
SAMPLE_SEQ_BLOCK = 16

BF16 = jnp.bfloat16
F32 = jnp.float32

_U0, _Q0, _K0, _V0, _G0 = 0, 512, 1536, 2560, 3584


def _dot(a, b):
    return jnp.dot(a, b, preferred_element_type=F32)


def _const_spec(shape):
    zeros = (0,) * len(shape)
    return pl.BlockSpec(shape, lambda *_: zeros, pipeline_mode=pl.Buffered(1))


def _params(sem):
    return pltpu.CompilerParams(dimension_semantics=sem, vmem_limit_bytes=VMEM_LIMIT)


def _norm_mod(x_ref, sc_ref, sh_ref, g_ref):
    x = x_ref[...]
    nb, tt, d = x.shape
    ms = jnp.mean(x * x, axis=-1, keepdims=True)
    y = x * lax.rsqrt(ms + EPS) * g_ref[...]
    hmod = y * (1.0 + sc_ref[0]) + sh_ref[0]
    return x.reshape(nb * tt, d), hmod.reshape(nb * tt, d)


def _split3(x):
    hi = x.astype(BF16)
    r1 = x - hi.astype(F32)
    mid = r1.astype(BF16)
    lo = (r1 - mid.astype(F32)).astype(BF16)
    return hi, mid, lo


def _cumsum_cols(tri, x):
    n = x.shape[1]
    y = _dot(tri, jnp.concatenate(_split3(x), axis=1))
    return (y[:, 0:n] + y[:, n:2 * n]) + y[:, 2 * n:3 * n]


def _cumsum_rows(x, tri):
    n = x.shape[0]
    y = _dot(jnp.concatenate(_split3(x), axis=0), tri)
    return (y[0:n] + y[n:2 * n]) + y[2 * n:3 * n]


def _ada_kernel(cs_ref, cp_ref, w_ref, b_ref, o_ref):
    c = jnp.concatenate([cs_ref[...], cp_ref[...]], axis=0)
    a = (c * jax.nn.sigmoid(c)).astype(BF16)
    res = _dot(a, w_ref[...].astype(BF16)) + b_ref[...]
    o_ref[0] = res.reshape(res.shape[0], 1, res.shape[1])


def _ada(c_sample, c_prompt, w_ada, b_ada):
    rows = c_sample.shape[0] + c_prompt.shape[0]
    nchunk = w_ada.shape[1] // D_MODEL
    return pl.pallas_call(
        _ada_kernel,
        grid=(nchunk,),
        in_specs=[pl.BlockSpec(c_sample.shape, lambda j: (0, 0)),
                  pl.BlockSpec(c_prompt.shape, lambda j: (0, 0)),
                  pl.BlockSpec((D_MODEL, D_MODEL), lambda j: (0, j)),
                  pl.BlockSpec((1, D_MODEL), lambda j: (0, j))],
        out_specs=pl.BlockSpec((1, rows, 1, D_MODEL), lambda j: (j, 0, 0, 0)),
        out_shape=jax.ShapeDtypeStruct((nchunk, rows, 1, D_MODEL), F32),
        compiler_params=_params(("arbitrary",)),
        name="ada",
    )(c_sample, c_prompt, w_ada, b_ada)


_SH1, _SC1, _GT1, _SH2, _SC2, _GT2 = range(6)


def _mod_spec(k, nb, row0, block_index):
    return pl.BlockSpec((1, nb, 1, D_MODEL), lambda *g: (k, row0 // nb + block_index(*g), 0, 0))


_PREP_COLS = 512


def _wprep_kernel(wt_ref, gt_ref, wp_ref, head_ref, gate_ref, pool_ref):
    head_ref[...] = wt_ref[0].T.astype(BF16)
    gates = gt_ref[0].T
    pad = jnp.zeros((gates.shape[0], GATE_LANES - gates.shape[1]), F32)
    gate_ref[...] = jnp.concatenate([gates, pad], axis=1).astype(BF16)
    pool_ref[...] = wp_ref[0].astype(BF16)


def _wprep(w_in_t, w_pool):
    g0 = _G0 + D_MODEL
    return pl.pallas_call(
        _wprep_kernel,
        grid=(_G0 // _PREP_COLS,),
        in_specs=[pl.BlockSpec((1, _PREP_COLS, D_MODEL), lambda j: (0, j, 0)),
                  pl.BlockSpec((1, 2 * ML_HEADS, D_MODEL), lambda j: (0, g0 // (2 * ML_HEADS), 0)),
                  pl.BlockSpec(w_pool.shape, lambda j: (0, 0, 0, 0))],
        out_specs=[pl.BlockSpec((D_MODEL, _PREP_COLS), lambda j: (0, j)),
                   pl.BlockSpec((D_MODEL, GATE_LANES), lambda j: (0, 0)),
                   pl.BlockSpec(w_pool.shape[1:], lambda j: (0, 0, 0))],
        out_shape=[jax.ShapeDtypeStruct((D_MODEL, _G0), BF16),
                   jax.ShapeDtypeStruct((D_MODEL, GATE_LANES), BF16),
                   jax.ShapeDtypeStruct(w_pool.shape[1:], BF16)],
        compiler_params=_params(("arbitrary",)),
        name="wprep",
    )(w_in_t, w_in_t, w_pool)


_TAIL_BLOCKS = 3 * D_MODEL // LANES
_TAIL_SHIFT_BLOCK = D_MODEL // LANES
_TAIL_SHIFT = 2 * ML_HEADS


def _tail_block(step):
    return jnp.minimum(step, _TAIL_BLOCKS - 1)


def _tail_prep(step, wa_ref, wb_ref, tail_ref):
    c = _tail_block(step)
    a = wa_ref[0]
    shifted = jnp.concatenate([a[_TAIL_SHIFT:, :], wb_ref[0]], axis=0)
    tail_ref[...] = jnp.where(c >= _TAIL_SHIFT_BLOCK, shifted, a).T.astype(BF16)


def _project(hb, w_ref, wg_ref, bg_ref, q_ref, kT_ref, v_ref, gc_ref, gr_ref):
    q_ref[...] = _dot(hb, w_ref[:, _Q0:_K0]).astype(BF16)
    k = _dot(hb, w_ref[:, _K0:_V0]) * (ML_HEAD_DIM ** -0.5)
    kT_ref[...] = k.T.astype(BF16)
    v_ref[...] = _dot(hb, w_ref[:, _V0:_G0]).astype(BF16)
    zg = _dot(hb, wg_ref[...]) + bg_ref[...]
    log_f = jnp.minimum(zg, 0.0) - jnp.log1p(jnp.exp(-jnp.abs(zg)))
    lane = lax.broadcasted_iota(jnp.int32, zg.shape, 1)
    gc = jnp.where(lane < ML_HEADS, zg, log_f)
    gc_ref[...] = gc
    gr_ref[...] = gc.T[0:SUBLANES, :]
    return k


def _pool_project(acc_fn, u, cnt_fn, wpool_ref, spool_ref):
    outs = []
    for g, w in enumerate(POOL_WINDOWS):
        cols = slice(g * POOL_GROUP_DIM, (g + 1) * POOL_GROUP_DIM)
        ug = u[:, cols]
        d = acc_fn(g, w, ug) / cnt_fn(w) - ug
        outs.append(_dot(d.astype(BF16), wpool_ref[g]) * spool_ref[:, cols])
    return jnp.concatenate(outs, axis=-1).astype(BF16)


def _inproj_prompt_kernel(x_ref, sc_ref, sh_ref, g_ref, w_ref, wg_ref, bg_ref, wpool_ref, spool_ref,
                          wbp_ref, wbm_ref, wout_ref, wgu_ref, wdn_ref, wa_ref, wb_ref,
                          yp_ref, q_ref, kT_ref, v_ref, gc_ref, gr_ref, hout_ref,
                          wbp_o, wbm_o, wout_o, wgu_o, wdn_o, tail_o, ext_ref, *, tm):
    t = pl.program_id(1)
    _tail_prep(pl.program_id(0) * pl.num_programs(1) + t, wa_ref, wb_ref, tail_o)

    @pl.when(t == 0)
    def _():
        ext_ref[0:HIST_ROWS, :] = jnp.zeros((HIST_ROWS, POOL_WIDTH), F32)

    for src, dst in ((wbp_ref, wbp_o), (wbm_ref, wbm_o), (wout_ref, wout_o), (wgu_ref, wgu_o), (wdn_ref, wdn_o)):
        dst[...] = src[0].astype(BF16)

    _, hmod = _norm_mod(x_ref, sc_ref, sh_ref, g_ref)
    hb = hmod.astype(BF16)
    _project(hb, w_ref, wg_ref, bg_ref, q_ref, kT_ref.at[0], v_ref, gc_ref, gr_ref.at[0])

    u = _dot(hb, w_ref[:, _U0:_Q0])
    ext_ref[HIST_ROWS:HIST_ROWS + tm, :] = u
    pos = t * tm + lax.broadcasted_iota(jnp.int32, (tm, 1), 0)

    def acc_fn(g, w, ug):
        acc = ug
        for j in range(1, w):
            acc = acc + ext_ref[pl.ds(HIST_ROWS - j, tm), g * POOL_GROUP_DIM:(g + 1) * POOL_GROUP_DIM]
        return acc

    def cnt_fn(w):
        return jnp.minimum(pos + 1, w).astype(F32)

    yp_ref[...] = _pool_project(acc_fn, u, cnt_fn, wpool_ref, spool_ref)
    last = ext_ref[tm:tm + HIST_ROWS, :]
    hout_ref[0] = last
    ext_ref[0:HIST_ROWS, :] = last


def _inproj_sample_kernel(x_ref, sc_ref, sh_ref, g_ref, w_ref, wg_ref, bg_ref, wpool_ref, spool_ref, hist_ref,
                          yp_ref, q_ref, kT_ref, v_ref, gc_ref, gr_ref, k_ref, hout_ref, u_ref, d_ref, *, pos0):
    _, hmod = _norm_mod(x_ref, sc_ref, sh_ref, g_ref)
    hb = hmod.astype(BF16)
    k_ref[...] = _project(hb, w_ref, wg_ref, bg_ref, q_ref, kT_ref, v_ref, gc_ref, gr_ref).astype(BF16)

    nb, tt = x_ref.shape[0], x_ref.shape[1]
    u = _dot(hb, w_ref[:, _U0:_Q0])
    outs = []
    for g, w in enumerate(POOL_WINDOWS):
        cols = slice(g * POOL_GROUP_DIM, (g + 1) * POOL_GROUP_DIM)
        u_ref[g] = u[:, cols]
        new = [u_ref[g, pl.ds(t, nb, stride=tt), :] for t in range(tt)]

        def ext(r):
            return hist_ref[r, :, cols] if r < POOL_HIST else new[r - POOL_HIST]

        for t in range(tt):
            acc = new[t]
            for j in range(1, w):
                acc = acc + ext(POOL_HIST + t - j)
            cnt = float(min(pos0 + t + 1, w))
            d_ref[g, pl.ds(t, nb, stride=tt), :] = acc / cnt - new[t]
        outs.append(_dot(d_ref[g].astype(BF16), wpool_ref[g]))
        for r in range(POOL_HIST):
            hout_ref[r, :, cols] = ext(r + tt)
    yp_ref[...] = (jnp.concatenate(outs, axis=-1) * spool_ref[...]).astype(BF16)


def _cast_specs(weights, nt, steps):
    in_specs, out_specs, out_shapes = [], [], []
    for w in weights:
        _, r, c = w.shape
        n = steps
        while r % n or (r // n) % BF16_ROWS:
            n //= 2
        rows = r // n
        idx = lambda b, t, n=n: jnp.minimum(b * nt + t, n - 1)
        in_specs.append(pl.BlockSpec((1, rows, c), lambda b, t, idx=idx: (0, idx(b, t), 0)))
        out_specs.append(pl.BlockSpec((rows, c), lambda b, t, idx=idx: (idx(b, t), 0)))
        out_shapes.append(jax.ShapeDtypeStruct((r, c), BF16))
    return in_specs, out_specs, out_shapes


def _inproj_prompt(x, mod, row0, g_mix, w_head, w_gate, bgate, w_pool, s_pool, later_weights, w_in_t):
    nbatch, seq, _ = x.shape
    tm = TOKEN_TILE
    nt = seq // tm
    ntok = nbatch * seq
    row = lambda b, t: (b * nt + t, 0)
    sc, sh = (_mod_spec(k, 1, row0, lambda b, t: b) for k in (_SC1, _SH1))
    cast_in, cast_out, cast_shapes = _cast_specs(later_weights, nt, nbatch * nt)
    tail_row = lambda b, t: _G0 // LANES + _tail_block(b * nt + t)
    cast_in = cast_in + [
        pl.BlockSpec((1, LANES, D_MODEL), lambda b, t: (0, tail_row(b, t), 0)),
        pl.BlockSpec((1, _TAIL_SHIFT, D_MODEL), lambda b, t: (0, (tail_row(b, t) + 1) * (LANES // _TAIL_SHIFT), 0))]
    cast_out = cast_out + [pl.BlockSpec((D_MODEL, LANES), lambda b, t: (0, _tail_block(b * nt + t)))]
    cast_shapes = cast_shapes + [jax.ShapeDtypeStruct((D_MODEL, _TAIL_BLOCKS * LANES), BF16)]
    return pl.pallas_call(
        functools.partial(_inproj_prompt_kernel, tm=tm),
        grid=(nbatch, nt),
        in_specs=[pl.BlockSpec((1, tm, D_MODEL), lambda b, t: (b, t, 0)), sc, sh,
                  _const_spec((1, D_MODEL)), _const_spec((D_MODEL, _G0)), _const_spec((D_MODEL, GATE_LANES)),
                  _const_spec((1, GATE_LANES)), _const_spec((4, POOL_GROUP_DIM, POOL_GROUP_DIM)),
                  _const_spec((1, POOL_WIDTH))] + cast_in,
        out_specs=[pl.BlockSpec((tm, POOL_WIDTH), row), pl.BlockSpec((tm, D_MODEL), row),
                   pl.BlockSpec((1, D_MODEL, tm), lambda b, t: (b, 0, t)), pl.BlockSpec((tm, D_MODEL), row),
                   pl.BlockSpec((tm, GATE_LANES), row), pl.BlockSpec((1, SUBLANES, tm), lambda b, t: (b, 0, t)),
                   pl.BlockSpec((1, HIST_ROWS, POOL_WIDTH), lambda b, t: (b, 0, 0))] + cast_out,
        out_shape=[jax.ShapeDtypeStruct((ntok, POOL_WIDTH), BF16), jax.ShapeDtypeStruct((ntok, D_MODEL), BF16),
                   jax.ShapeDtypeStruct((nbatch, D_MODEL, seq), BF16), jax.ShapeDtypeStruct((ntok, D_MODEL), BF16),
                   jax.ShapeDtypeStruct((ntok, GATE_LANES), F32), jax.ShapeDtypeStruct((nbatch, SUBLANES, seq), F32),
                   jax.ShapeDtypeStruct((nbatch, HIST_ROWS, POOL_WIDTH), F32)] + cast_shapes,
        scratch_shapes=[pltpu.VMEM((tm + HIST_ROWS, POOL_WIDTH), F32)],
        compiler_params=_params(("arbitrary", "arbitrary")),
        name="inproj_prompt",
    )(x, mod, mod, g_mix, w_head, w_gate, bgate, w_pool, s_pool, *later_weights, w_in_t, w_in_t)


def _inproj_sample(x, mod, row0, g_mix, w_head, w_gate, bgate, w_pool, s_pool, hist_tm):
    nseq, tt, _ = x.shape
    nb = TOKEN_TILE // tt
    tm = nb * tt
    ntok = nseq * tt
    row = lambda i: (i, 0)
    col = lambda i: (0, i)
    sc, sh = (_mod_spec(k, nb, row0, lambda i: i) for k in (_SC1, _SH1))
    hist = pl.BlockSpec((POOL_HIST, nb, POOL_WIDTH), lambda i: (0, i, 0))
    return pl.pallas_call(
        functools.partial(_inproj_sample_kernel, pos0=PAST_LEN),
        grid=(nseq // nb,),
        in_specs=[pl.BlockSpec((nb, tt, D_MODEL), lambda i: (i, 0, 0)), sc, sh,
                  _const_spec((1, D_MODEL)), _const_spec((D_MODEL, _G0)), _const_spec((D_MODEL, GATE_LANES)),
                  _const_spec((1, GATE_LANES)), _const_spec((4, POOL_GROUP_DIM, POOL_GROUP_DIM)),
                  _const_spec((1, POOL_WIDTH)), hist],
        out_specs=[pl.BlockSpec((tm, POOL_WIDTH), row), pl.BlockSpec((tm, D_MODEL), row),
                   pl.BlockSpec((D_MODEL, tm), col), pl.BlockSpec((tm, D_MODEL), row),
                   pl.BlockSpec((tm, GATE_LANES), row), pl.BlockSpec((SUBLANES, tm), col),
                   pl.BlockSpec((tm, D_MODEL), row), hist],
        out_shape=[jax.ShapeDtypeStruct((ntok, POOL_WIDTH), BF16), jax.ShapeDtypeStruct((ntok, D_MODEL), BF16),
                   jax.ShapeDtypeStruct((D_MODEL, ntok), BF16), jax.ShapeDtypeStruct((ntok, D_MODEL), BF16),
                   jax.ShapeDtypeStruct((ntok, GATE_LANES), F32), jax.ShapeDtypeStruct((SUBLANES, ntok), F32),
                   jax.ShapeDtypeStruct((ntok, D_MODEL), BF16),
                   jax.ShapeDtypeStruct((POOL_HIST, nseq, POOL_WIDTH), F32)],
        scratch_shapes=[pltpu.VMEM((len(POOL_WINDOWS), tm, POOL_GROUP_DIM), F32)] * 2,
        compiler_params=_params(("arbitrary",)),
        name="inproj_sample",
    )(x, mod, mod, g_mix, w_head, w_gate, bgate, w_pool, s_pool, hist_tm)


def _mlstm_core(q, kT, v, b_c, b_r, ig_r, mask, m_prev, num_inter, qn):
    logd = jnp.where(mask, (b_c - b_r) + ig_r, -jnp.inf)
    a_c = b_c + m_prev
    m_t = jnp.maximum(a_c, jnp.max(logd, axis=-1, keepdims=True))
    w_intra = jnp.exp(logd - m_t)
    w_inter = jnp.exp(a_c - m_t)
    s = _dot(q, kT) * w_intra
    num = w_inter * num_inter + _dot(s.astype(BF16), v)
    den = w_inter * qn + jnp.sum(s, axis=-1, keepdims=True)
    hh = num / jnp.maximum(jnp.abs(den), jnp.exp(-m_t))
    return hh, m_t, a_c


def _mlstm_chunk(q_ref, kT_ref, v_ref, gr_ref, hh_ref, cn_ref, m_ref, r0, chunk):
    rows = slice(r0, r0 + chunk)
    gr = gr_ref[:, rows]
    ri = lax.broadcasted_iota(jnp.int32, (chunk, chunk), 0)
    ci = lax.broadcasted_iota(jnp.int32, (chunk, chunk), 1)
    causal = ri >= ci
    brow = _cumsum_rows(gr, (ri <= ci).astype(BF16))
    bcol = brow.T
    ones = jnp.ones((chunk, LANES), BF16)
    tile_to = lambda x, width: jnp.concatenate([x] * (width // LANES), axis=-1)
    for h in range(ML_HEADS):
        hs = slice(h * ML_HEAD_DIM, (h + 1) * ML_HEAD_DIM)
        b_r = brow[ML_HEADS + h:ML_HEADS + h + 1, :]
        g_r = gr[h:h + 1, :] - b_r
        m_prev = m_ref[h:h + 1, 0:1]
        cn = cn_ref[h]
        q = q_ref[rows, hs]
        kT = kT_ref[hs, rows]
        vaug = jnp.concatenate([v_ref[rows, hs], ones], axis=-1)
        gm = jnp.where(causal, g_r, -jnp.inf)
        big_m = jnp.maximum(m_prev, jnp.max(gm, axis=-1, keepdims=True))
        m_rep = jnp.broadcast_to(big_m, (chunk, LANES))
        mt_rep = jnp.broadcast_to(bcol[:, ML_HEADS + h:ML_HEADS + h + 1] + big_m, (chunk, LANES))
        qkc = _dot(q, jnp.concatenate([kT, cn.astype(BF16)], axis=1))
        qc = qkc[:, chunk:]
        s = qkc[:, :chunk] * jnp.exp(gm - tile_to(m_rep, chunk))
        b_last = b_r[:, chunk - 1:chunk]
        m_new = b_last + jnp.maximum(m_prev, jnp.max(g_r, axis=-1, keepdims=True))
        decay = jnp.exp((b_last + m_prev) - m_new)
        w_end = jnp.exp((g_r + b_last) - m_new)
        both = _dot(jnp.concatenate([s.astype(BF16), (kT.astype(F32) * w_end).astype(BF16)], axis=0), vaug)
        sva = both[0:chunk]
        w_inter = jnp.exp(m_prev - m_rep)
        den = w_inter * qc[:, ML_HEAD_DIM:] + sva[:, ML_HEAD_DIM:]
        rinv = 1.0 / jnp.maximum(jnp.abs(den), jnp.exp(-mt_rep))
        hh_ref[rows, hs] = ((tile_to(w_inter, ML_HEAD_DIM) * qc[:, :ML_HEAD_DIM] + sva[:, :ML_HEAD_DIM])
                            * tile_to(rinv, ML_HEAD_DIM))
        cn_ref[h] = decay * cn + both[chunk:]
        m_ref[h:h + 1, :] = jnp.broadcast_to(m_new, (1, LANES))


def _mlstm_prompt_kernel(q_ref, kT_ref, v_ref, gr_ref, hh_ref, cout_ref, nout_ref, mout_ref, cn_ref, m_ref, *, chunk):
    t = pl.program_id(1)
    nb = q_ref.shape[0]

    @pl.when(t == 0)
    def _():
        cn_ref[...] = jnp.zeros(cn_ref.shape, F32)
        m_ref[...] = jnp.full(m_ref.shape, M_INIT, F32)

    for r0 in range(0, q_ref.shape[1], chunk):
        for s in range(nb):
            _mlstm_chunk(q_ref.at[s], kT_ref.at[s], v_ref.at[s], gr_ref.at[s], hh_ref.at[s], cn_ref.at[s],
                         m_ref.at[s], r0, chunk)

    @pl.when(t == pl.num_programs(1) - 1)
    def _():
        cout_ref[...] = cn_ref[:, :, :, 0:ML_HEAD_DIM]
        nout_ref[...] = cn_ref[:, :, :, ML_HEAD_DIM:AUG]
        mout_ref[...] = m_ref[...]


def _mlstm_prompt(q, kT, v, gr):
    nbatch, seq, _ = q.shape
    tm, nb = MLSTM_TILE, MLSTM_BATCHES
    row = pl.BlockSpec((nb, tm, D_MODEL), lambda b, t: (b, t, 0))
    state = lambda last: pl.BlockSpec((nb, ML_HEADS, ML_HEAD_DIM, last), lambda b, t: (b, 0, 0, 0))
    return pl.pallas_call(
        functools.partial(_mlstm_prompt_kernel, chunk=PROMPT_CHUNK),
        grid=(nbatch // nb, seq // tm),
        in_specs=[row, pl.BlockSpec((nb, D_MODEL, tm), lambda b, t: (b, 0, t)), row,
                  pl.BlockSpec((nb, SUBLANES, tm), lambda b, t: (b, 0, t))],
        out_specs=[row, state(ML_HEAD_DIM), state(LANES), pl.BlockSpec((nb, SUBLANES, LANES), lambda b, t: (b, 0, 0))],
        out_shape=[jax.ShapeDtypeStruct((nbatch, seq, D_MODEL), F32),
                   jax.ShapeDtypeStruct((nbatch, ML_HEADS, ML_HEAD_DIM, ML_HEAD_DIM), F32),
                   jax.ShapeDtypeStruct((nbatch, ML_HEADS, ML_HEAD_DIM, LANES), F32),
                   jax.ShapeDtypeStruct((nbatch, SUBLANES, LANES), F32)],
        scratch_shapes=[pltpu.VMEM((nb, ML_HEADS, ML_HEAD_DIM, AUG), F32), pltpu.VMEM((nb, SUBLANES, LANES), F32)],
        compiler_params=_params(("arbitrary", "arbitrary")),
        name="mlstm_prompt",
    )(q, kT, v, gr)


def _last_in_group(x, group):
    rows = x.shape[0]
    x3 = jnp.broadcast_to(x, (rows, LANES)).reshape(rows // group, group, LANES)
    last = jnp.broadcast_to(x3[:, group - 1:group, :], x3.shape)
    return last.reshape(rows, LANES)[:, 0:1]


def _mlstm_sample_body(h, q_ref, kT_ref, k_ref, v_ref, gc_ref, gr_ref, m0_ref, c_ref, n_ref,
                       hh_ref, cout_ref, nout_ref, mt_ref, ni_ref, qn_ref, dec_ref, wk_ref, *, tt):
    L = q_ref.shape[0]
    nseq = L // tt
    gc = gc_ref[...]
    gr = gr_ref[...]
    ri = lax.broadcasted_iota(jnp.int32, (L, L), 0)
    ci = lax.broadcasted_iota(jnp.int32, (L, L), 1)
    same = (ri // tt) == (ci // tt)
    mask = same & (ri >= ci)
    bcol = _cumsum_cols(mask.astype(BF16), gc)
    brow = _cumsum_rows(gr, (same & (ri <= ci)).astype(BF16))
    lane = lax.broadcasted_iota(jnp.int32, (L, GATE_LANES), 1)
    sub = lax.broadcasted_iota(jnp.int32, (SUBLANES, L), 0)
    pick_col = lambda arr, idx: jnp.sum(jnp.where(lane == idx, arr, 0.0), axis=-1, keepdims=True)
    pick_row = lambda arr, idx: jnp.sum(jnp.where(sub == idx, arr, 0.0), axis=0, keepdims=True)
    ig_c = pick_col(gc, h)
    b_c = pick_col(bcol, ML_HEADS + h)
    ig_r = pick_row(gr, h)
    b_r = pick_row(brow, ML_HEADS + h)
    m_prev = m0_ref[0]

    for j in range(nseq):
        rows = slice(j * tt, (j + 1) * tt)
        qj = q_ref[rows, :]
        ni_ref[rows, :] = _dot(qj, c_ref[j, 0].astype(BF16))
        nj = n_ref[j, pl.ds(h, 1), :].astype(BF16).astype(F32)
        qn = jnp.sum(qj.astype(F32) * nj, axis=-1, keepdims=True)
        qn_ref[rows, :] = jnp.broadcast_to(qn, (tt, LANES))

    q = q_ref[...]
    kT = kT_ref[...]
    v = v_ref[...]
    hh, m_t, a_c = _mlstm_core(q, kT, v, b_c, b_r, ig_r, mask, m_prev, ni_ref[...], qn_ref[:, 0:1])
    hh_ref[...] = hh
    m_new = _last_in_group(m_t, tt)
    decay = jnp.exp(_last_in_group(a_c, tt) - m_new)
    w_end = jnp.exp((_last_in_group(b_c, tt) - b_c) + ig_c - m_new)
    mt_ref[0] = jnp.broadcast_to(m_t, (L, LANES))
    dec_ref[...] = jnp.broadcast_to(decay, (L, LANES))
    wv = w_end * v.astype(F32)
    wk_ref[...] = w_end.astype(BF16).astype(F32) * k_ref[...].astype(F32)
    rowi = lax.broadcasted_iota(jnp.int32, (L, 1), 0)

    for j in range(nseq):
        rows = slice(j * tt, (j + 1) * tt)
        upd = _dot(kT, jnp.where((rowi // tt) == j, wv, 0.0).astype(BF16))
        dj = dec_ref[j * tt:j * tt + 1, 0:1]
        cout_ref[j, 0] = dj * c_ref[j, 0] + upd
        nout_ref[j, pl.ds(h, 1), :] = (dj * n_ref[j, pl.ds(h, 1), :]
                                       + jnp.sum(wk_ref[rows, :], axis=0, keepdims=True))


def _mlstm_sample_specs(ntok, tt):
    nseq = ntok // tt
    sb = SAMPLE_SEQ_BLOCK
    L = sb * tt
    nh = ML_HEADS
    qspec = pl.BlockSpec((L, ML_HEAD_DIM), lambda i: (i // nh, i % nh))
    cspec = pl.BlockSpec((sb, 1, ML_HEAD_DIM, ML_HEAD_DIM), lambda i: (i // nh, i % nh, 0, 0))
    nspec = pl.BlockSpec((sb, nh, ML_HEAD_DIM), lambda i: (i // nh, 0, 0))
    in_specs = [qspec, pl.BlockSpec((ML_HEAD_DIM, L), lambda i: (i % nh, i // nh)), qspec, qspec,
                pl.BlockSpec((L, GATE_LANES), lambda i: (i // nh, 0)),
                pl.BlockSpec((SUBLANES, L), lambda i: (0, i // nh)),
                pl.BlockSpec((1, L, 1), lambda i: (i % nh, i // nh, 0)), cspec, nspec]
    out_specs = [qspec, cspec, nspec, pl.BlockSpec((1, L, LANES), lambda i: (i % nh, i // nh, 0))]
    out_shapes = [jax.ShapeDtypeStruct((ntok, D_MODEL), F32),
                  jax.ShapeDtypeStruct((nseq, ML_HEADS, ML_HEAD_DIM, ML_HEAD_DIM), F32),
                  jax.ShapeDtypeStruct((nseq, ML_HEADS, ML_HEAD_DIM), F32),
                  jax.ShapeDtypeStruct((ML_HEADS, ntok, LANES), F32)]
    scratch = [pltpu.VMEM((L, ML_HEAD_DIM), F32), pltpu.VMEM((L, LANES), F32),
               pltpu.VMEM((L, LANES), F32), pltpu.VMEM((L, ML_HEAD_DIM), F32)]
    return (nseq // sb) * nh, in_specs, out_specs, out_shapes, scratch


def _post_kernel(x_ref, sc_ref, sh_ref, gt_ref, g_ref, hh_ref, yp_ref, wt_ref, ghead_ref, wbp_ref, wbm_ref, wout_ref,
                 o_ref):
    pool = _dot(yp_ref[...], wbp_ref[...])
    x, hmod = _norm_mod(x_ref, sc_ref, sh_ref, g_ref)
    hb = hmod.astype(BF16)
    nb, tt, d = x_ref.shape
    o = _dot(hb, wt_ref[:, 0:D_MODEL])
    parts = []
    for h in range(ML_HEADS):
        hh = hh_ref[:, h * ML_HEAD_DIM:(h + 1) * ML_HEAD_DIM]
        parts.append(hh * lax.rsqrt(jnp.mean(hh * hh, axis=-1, keepdims=True) + EPS))
    yml = (jnp.concatenate(parts, axis=-1) * ghead_ref[...]) * jax.nn.sigmoid(o)
    gp = _dot(hb, wt_ref[:, D_MODEL:2 * D_MODEL])
    gm = _dot(hb, wt_ref[:, 2 * D_MODEL:3 * D_MODEL])
    merged = jax.nn.sigmoid(gp) * pool + jax.nn.sigmoid(gm) * _dot(yml.astype(BF16), wbm_ref[...])
    y = _dot(merged.astype(BF16), wout_ref[...]).reshape(nb, tt, d)
    o_ref[...] = x_ref[...] + gt_ref[0] * y


def _tile_blocks(x, tile):
    g, t, _ = x.shape
    if t >= tile:
        nb, tt = 1, tile
    else:
        nb, tt = tile // t, t
    return nb, tt, (g // nb) * (t // tt), t // tt


def _post(x, mod, row0, g_mix, hh, yp, w_tail, g_head, w_bp, w_bm, w_out):
    nb, tt, steps, per = _tile_blocks(x, TOKEN_TILE)
    tm = nb * tt
    xspec = pl.BlockSpec((nb, tt, D_MODEL), lambda i: (i // per, i % per, 0))
    sc, sh, gt = (_mod_spec(k, nb, row0, lambda i: i // per) for k in (_SC1, _SH1, _GT1))
    row = lambda i: (i, 0)
    return pl.pallas_call(
        _post_kernel,
        grid=(steps,),
        in_specs=[xspec, sc, sh, gt, _const_spec((1, D_MODEL)),
                  pl.BlockSpec((tm, D_MODEL), row), pl.BlockSpec((tm, POOL_WIDTH), row),
                  _const_spec((D_MODEL, 3 * D_MODEL)), _const_spec((1, D_MODEL)),
                  _const_spec((POOL_WIDTH, D_MODEL)), _const_spec((D_MODEL, D_MODEL)),
                  _const_spec((D_MODEL, D_MODEL))],
        out_specs=xspec,
        out_shape=jax.ShapeDtypeStruct(x.shape, F32),
        compiler_params=_params(("arbitrary",)),
        name="post",
    )(x, mod, mod, mod, g_mix, hh, yp, w_tail, g_head, w_bp, w_bm, w_out)


_FF_SPLITS = ((0, 768), (768, 1536), (1536, 2304), (2304, D_FF))


def _ffn_kernel(x_ref, sc_ref, sh_ref, gt_ref, g_ref, gfin_ref, wgu_ref, wdn_ref, o_ref):
    _, hmod = _norm_mod(x_ref, sc_ref, sh_ref, g_ref)
    hb = hmod.astype(BF16)
    nb, tt, d = x_ref.shape
    dn = None
    for lo, hi in _FF_SPLITS:
        gate = _dot(hb, wgu_ref[:, lo:hi])
        up = _dot(hb, wgu_ref[:, D_FF + lo:D_FF + hi])
        act = (gate * jax.nn.sigmoid(gate) * up).astype(BF16)
        part = _dot(act, wdn_ref[lo:hi, :])
        dn = part if dn is None else dn + part
    x2 = x_ref[...] + gt_ref[0] * dn.reshape(nb, tt, d)
    ms = jnp.mean(x2 * x2, axis=-1, keepdims=True)
    o_ref[...] = x2 * lax.rsqrt(ms + EPS) * gfin_ref[...]


_N_FFN_IN = 8


def _ffn_mlstm_kernel(*refs, n_ml_in, tt):
    ffn_in = refs[:_N_FFN_IN]
    ml_in = refs[_N_FFN_IN:_N_FFN_IN + n_ml_in]
    o_ref = refs[_N_FFN_IN + n_ml_in]
    ml_rest = refs[_N_FFN_IN + n_ml_in + 1:]
    _mlstm_sample_body(pl.program_id(0) % ML_HEADS, *ml_in, *ml_rest, tt=tt)
    _ffn_kernel(*ffn_in, o_ref)


def _ffn(x, mod, row0, g_ffn, g_final, w_gu, w_down, tile, mlstm_sample=None):
    nb, tt, steps, per = _tile_blocks(x, tile)
    xspec = pl.BlockSpec((nb, tt, D_MODEL), lambda i: (i // per, i % per, 0))
    sc, sh, gt = (_mod_spec(k, nb, row0, lambda i: i // per) for k in (_SC2, _SH2, _GT2))
    in_specs = [xspec, sc, sh, gt, _const_spec((1, D_MODEL)), _const_spec((1, D_MODEL)),
                _const_spec((D_MODEL, 2 * D_FF)), _const_spec((D_FF, D_MODEL))]
    operands = (x, mod, mod, mod, g_ffn, g_final, w_gu, w_down)
    if mlstm_sample is None:
        return pl.pallas_call(
            _ffn_kernel,
            grid=(steps,),
            in_specs=in_specs,
            out_specs=xspec,
            out_shape=jax.ShapeDtypeStruct(x.shape, F32),
            compiler_params=_params(("arbitrary",)),
            name="ffn",
        )(*operands)
    ml_operands, ml_tt = mlstm_sample
    ml_steps, ml_in, ml_out, ml_shapes, ml_scratch = _mlstm_sample_specs(ml_operands[0].shape[0], ml_tt)
    assert ml_steps == steps, "one sample mLSTM step per FFN tile"
    return pl.pallas_call(
        functools.partial(_ffn_mlstm_kernel, n_ml_in=len(ml_in), tt=ml_tt),
        grid=(steps,),
        in_specs=in_specs + ml_in,
        out_specs=[xspec] + ml_out,
        out_shape=[jax.ShapeDtypeStruct(x.shape, F32)] + ml_shapes,
        scratch_shapes=ml_scratch,
        compiler_params=_params(("arbitrary",)),
        name="ffn_mlstm",
    )(*operands, *ml_operands)


def kernel(x_prompt, x_sample, c_prompt, c_sample, state_pool, state_mlstm_c, state_mlstm_n, state_mlstm_m, g_mix, g_ffn, g_final, w_ada, b_ada, w_in, b_igate, b_fgate, w_pool, s_pool, g_head, w_branch_pool, w_branch_mlstm, w_out, w_gate_up, w_down):
    depth = w_in.shape[0]
    assert depth == 1, "single-layer trunk"
    nbatch, seq, _ = x_prompt.shape
    nseq, tt, _ = x_sample.shape
    l = 0

    bgate = jnp.pad(jnp.concatenate([b_igate[l], b_fgate[l]])[None, :], ((0, 0), (0, GATE_LANES - 2 * ML_HEADS)))
    g_mix_r, g_ffn_r, g_fin_r = g_mix[l][None, :], g_ffn[l][None, :], g_final[None, :]
    s_pool_r, g_head_r = s_pool[l][None, :], g_head[l][None, :]

    w_in_t = jnp.swapaxes(w_in, 1, 2)
    w_head, w_gate, w_pool_b = _wprep(w_in_t, w_pool)
    mod = _ada(c_sample, c_prompt, w_ada[l], b_ada[l][None, :])
    row_s, row_p = 0, nseq

    (yp, q, kT, v, gc, gr, hist_p, w_bp, w_bm, w_o, w_gu, w_dn, w_tail) = _inproj_prompt(
        x_prompt, mod, row_p, g_mix_r, w_head, w_gate, bgate, w_pool_b, s_pool_r,
        (w_branch_pool, w_branch_mlstm, w_out, w_gate_up, w_down), w_in_t)
    hh, c_p, n_p, m_p = _mlstm_prompt(q.reshape(nbatch, seq, D_MODEL), kT, v.reshape(nbatch, seq, D_MODEL), gr)
    hh = hh.reshape(nbatch * seq, D_MODEL)
    x1 = _post(x_prompt, mod, row_p, g_mix_r, hh, yp, w_tail, g_head_r, w_bp, w_bm, w_o)

    hist_tm = jnp.swapaxes(state_pool[l], 0, 1)
    yp, q, kT, v, gc, gr, k, hist_s = _inproj_sample(x_sample, mod, row_s, g_mix_r, w_head, w_gate, bgate, w_pool_b,
                                                     s_pool_r, hist_tm)
    m0_tok = jnp.repeat(state_mlstm_m[l].astype(F32).T, tt, axis=1)[:, :, None]
    ml_operands = (q, kT, k, v, gc, gr, m0_tok, state_mlstm_c[l].astype(F32), state_mlstm_n[l].astype(F32))
    y_prompt, hh, c_s, n_s, mt = _ffn(x1, mod, row_p, g_ffn_r, g_fin_r, w_gu, w_dn, PROMPT_FFN_TILE,
                                      (ml_operands, tt))

    x1 = _post(x_sample, mod, row_s, g_mix_r, hh, yp, w_tail, g_head_r, w_bp, w_bm, w_o)
    y_sample = _ffn(x1, mod, row_s, g_ffn_r, g_fin_r, w_gu, w_dn, FFN_TILE)

    cd, nd, md = state_mlstm_c.dtype, state_mlstm_n.dtype, state_mlstm_m.dtype
    return (y_prompt, y_sample,
            hist_p[None, :, HIST_ROWS - POOL_HIST:, :],
            c_p.astype(cd)[None], n_p[..., 0].astype(nd)[None], m_p[:, :ML_HEADS, 0].astype(md)[None],
            jnp.swapaxes(hist_s, 0, 1)[None].astype(state_pool.dtype),
            c_s.astype(cd)[None], n_s.astype(nd)[None],
            mt[:, tt - 1::tt, 0].T.astype(md)[None])
```

```python
import functools

import jax
import jax.numpy as jnp
from jax import lax
from jax.experimental import pallas as pl
from jax.experimental.pallas import tpu as pltpu

D_MODEL = 1024
POOL_WINDOWS = (2, 4, 8, 16)
POOL_GROUP_DIM = 128
POOL_WIDTH = 512
POOL_HIST = 15
HIST_ROWS = 16
ML_HEADS = 4
ML_HEAD_DIM = 256
D_FF = 2816
EPS = 1e-6
M_INIT = -1e30
PAST_LEN = 16384
LANES = 128
SUBLANES = 8
BF16_ROWS = 16
GATE_LANES = LANES
AUG = ML_HEAD_DIM + LANES
VMEM_LIMIT = 56 * 1024 * 1024

TOKEN_TILE = 512
FFN_TILE = 1024
PROMPT_FFN_TILE = 512
PROMPT_CHUNK = 256
MLSTM_TILE = 1024
MLSTM_BATCHES = 1
SAMPLE_SEQ_BLOCK = 16

BF16 = jnp.bfloat16
F32 = jnp.float32

_U0, _Q0, _K0, _V0, _G0 = 0, 512, 1536, 2560, 3584


def _dot(a, b):
    return jnp.dot(a, b, preferred_element_type=F32)


def _const_spec(shape):
    zeros = (0,) * len(shape)
    return pl.BlockSpec(shape, lambda *_: zeros, pipeline_mode=pl.Buffered(1))


def _params(sem):
    return pltpu.CompilerParams(dimension_semantics=sem, vmem_limit_bytes=VMEM_LIMIT)


def _norm_mod(x_ref, sc_ref, sh_ref, g_ref):
    x = x_ref[...]
    nb, tt, d = x.shape
    ms = jnp.mean(x * x, axis=-1, keepdims=True)
    y = x * lax.rsqrt(ms + EPS) * g_ref[...]
    hmod = y * (1.0 + sc_ref[0]) + sh_ref[0]
    return x.reshape(nb * tt, d), hmod.reshape(nb * tt, d)


def _split3(x):
    hi = x.astype(BF16)
    r1 = x - hi.astype(F32)
    mid = r1.astype(BF16)
    lo = (r1 - mid.astype(F32)).astype(BF16)
    return hi, mid, lo


def _cumsum_cols(tri, x):
    n = x.shape[1]
    y = _dot(tri, jnp.concatenate(_split3(x), axis=1))
    return (y[:, 0:n] + y[:, n:2 * n]) + y[:, 2 * n:3 * n]


def _cumsum_rows(x, tri):
    n = x.shape[0]
    y = _dot(jnp.concatenate(_split3(x), axis=0), tri)
    return (y[0:n] + y[n:2 * n]) + y[2 * n:3 * n]


def _ada_kernel(cs_ref, cp_ref, w_ref, b_ref, o_ref):
    c = jnp.concatenate([cs_ref[...], cp_ref[...]], axis=0)
    a = (c * jax.nn.sigmoid(c)).astype(BF16)
    res = _dot(a, w_ref[...].astype(BF16)) + b_ref[...]
    o_ref[0] = res.reshape(res.shape[0], 1, res.shape[1])


def _ada(c_sample, c_prompt, w_ada, b_ada):
    rows = c_sample.shape[0] + c_prompt.shape[0]
    nchunk = w_ada.shape[1] // D_MODEL
    return pl.pallas_call(
        _ada_kernel,
        grid=(nchunk,),
        in_specs=[pl.BlockSpec(c_sample.shape, lambda j: (0, 0)),
                  pl.BlockSpec(c_prompt.shape, lambda j: (0, 0)),
                  pl.BlockSpec((D_MODEL, D_MODEL), lambda j: (0, j)),
                  pl.BlockSpec((1, D_MODEL), lambda j: (0, j))],
        out_specs=pl.BlockSpec((1, rows, 1, D_MODEL), lambda j: (j, 0, 0, 0)),
        out_shape=jax.ShapeDtypeStruct((nchunk, rows, 1, D_MODEL), F32),
        compiler_params=_params(("arbitrary",)),
        name="ada",
    )(c_sample, c_prompt, w_ada, b_ada)


_SH1, _SC1, _GT1, _SH2, _SC2, _GT2 = range(6)


def _mod_spec(k, nb, row0, block_index):
    return pl.BlockSpec((1, nb, 1, D_MODEL), lambda *g: (k, row0 // nb + block_index(*g), 0, 0))


_PREP_COLS = 512


def _wprep_kernel(wt_ref, gt_ref, wp_ref, head_ref, gate_ref, pool_ref):
    head_ref[...] = wt_ref[0].T.astype(BF16)
    gates = gt_ref[0].T
    pad = jnp.zeros((gates.shape[0], GATE_LANES - gates.shape[1]), F32)
    gate_ref[...] = jnp.concatenate([gates, pad], axis=1).astype(BF16)
    pool_ref[...] = wp_ref[0].astype(BF16)


def _wprep(w_in_t, w_pool):
    g0 = _G0 + D_MODEL
    return pl.pallas_call(
        _wprep_kernel,
        grid=(_G0 // _PREP_COLS,),
        in_specs=[pl.BlockSpec((1, _PREP_COLS, D_MODEL), lambda j: (0, j, 0)),
                  pl.BlockSpec((1, 2 * ML_HEADS, D_MODEL), lambda j: (0, g0 // (2 * ML_HEADS), 0)),
                  pl.BlockSpec(w_pool.shape, lambda j: (0, 0, 0, 0))],
        out_specs=[pl.BlockSpec((D_MODEL, _PREP_COLS), lambda j: (0, j)),
                   pl.BlockSpec((D_MODEL, GATE_LANES), lambda j: (0, 0)),
                   pl.BlockSpec(w_pool.shape[1:], lambda j: (0, 0, 0))],
        out_shape=[jax.ShapeDtypeStruct((D_MODEL, _G0), BF16),
                   jax.ShapeDtypeStruct((D_MODEL, GATE_LANES), BF16),
                   jax.ShapeDtypeStruct(w_pool.shape[1:], BF16)],
        compiler_params=_params(("arbitrary",)),
        name="wprep",
    )(w_in_t, w_in_t, w_pool)


_TAIL_BLOCKS = 3 * D_MODEL // LANES
_TAIL_SHIFT_BLOCK = D_MODEL // LANES
_TAIL_SHIFT = 2 * ML_HEADS


def _tail_block(step):
    return jnp.minimum(step, _TAIL_BLOCKS - 1)


def _tail_prep(step, wa_ref, wb_ref, tail_ref):
    c = _tail_block(step)
    a = wa_ref[0]
    shifted = jnp.concatenate([a[_TAIL_SHIFT:, :], wb_ref[0]], axis=0)
    tail_ref[...] = jnp.where(c >= _TAIL_SHIFT_BLOCK, shifted, a).T.astype(BF16)


def _project(hb, w_ref, wg_ref, bg_ref, q_ref, kT_ref, v_ref, gc_ref, gr_ref):
    zg = _dot(hb, wg_ref[...]) + bg_ref[...]
    k = _dot(hb, w_ref[:, _K0:_V0]) * (ML_HEAD_DIM ** -0.5)
    log_f = jnp.minimum(zg, 0.0) - jnp.log1p(jnp.exp(-jnp.abs(zg)))
    lane = lax.broadcasted_iota(jnp.int32, zg.shape, 1)
    gc = jnp.where(lane < ML_HEADS, zg, log_f)
    gc_ref[...] = gc
    gr_ref[...] = gc.T[0:SUBLANES, :]
    kT_ref[...] = k.T.astype(BF16)
    q_ref[...] = _dot(hb, w_ref[:, _Q0:_K0]).astype(BF16)
    v_ref[...] = _dot(hb, w_ref[:, _V0:_G0]).astype(BF16)
    return k


def _pool_deltas(acc_fn, u, cnt_fn):
    deltas = []
    for g, w in enumerate(POOL_WINDOWS):
        ug = u[:, g * POOL_GROUP_DIM:(g + 1) * POOL_GROUP_DIM]
        deltas.append((acc_fn(g, w, ug) / cnt_fn(w) - ug).astype(BF16))
    return deltas


def _pool_project(deltas, wpool_ref, spool_ref):
    outs = [_dot(d, wpool_ref[g]) for g, d in enumerate(deltas)]
    return (jnp.concatenate(outs, axis=-1) * spool_ref[...]).astype(BF16)


def _inproj_prompt_kernel(x_ref, sc_ref, sh_ref, g_ref, w_ref, wg_ref, bg_ref, wpool_ref, spool_ref,
                          wbp_ref, wbm_ref, wout_ref, wgu_ref, wdn_ref, wa_ref, wb_ref,
                          yp_ref, q_ref, kT_ref, v_ref, gc_ref, gr_ref, hout_ref,
                          wbp_o, wbm_o, wout_o, wgu_o, wdn_o, tail_o, ext_ref, *, tm):
    t = pl.program_id(1)

    @pl.when(t == 0)
    def _():
        ext_ref[0:HIST_ROWS, :] = jnp.zeros((HIST_ROWS, POOL_WIDTH), F32)

    _, hmod = _norm_mod(x_ref, sc_ref, sh_ref, g_ref)
    hb = hmod.astype(BF16)
    u = _dot(hb, w_ref[:, _U0:_Q0])
    ext_ref[HIST_ROWS:HIST_ROWS + tm, :] = u
    pos = t * tm + lax.broadcasted_iota(jnp.int32, (tm, 1), 0)

    def acc_fn(g, w, ug):
        acc = ug
        for j in range(1, w):
            acc = acc + ext_ref[pl.ds(HIST_ROWS - j, tm), g * POOL_GROUP_DIM:(g + 1) * POOL_GROUP_DIM]
        return acc

    def cnt_fn(w):
        return jnp.minimum(pos + 1, w).astype(F32)

    deltas = _pool_deltas(acc_fn, u, cnt_fn)
    last = ext_ref[tm:tm + HIST_ROWS, :]
    hout_ref[0] = last
    ext_ref[0:HIST_ROWS, :] = last

    _project(hb, w_ref, wg_ref, bg_ref, q_ref, kT_ref.at[0], v_ref, gc_ref, gr_ref.at[0])
    yp_ref[...] = _pool_project(deltas, wpool_ref, spool_ref)

    _tail_prep(pl.program_id(0) * pl.num_programs(1) + t, wa_ref, wb_ref, tail_o)
    for src, dst in ((wbp_ref, wbp_o), (wbm_ref, wbm_o), (wout_ref, wout_o), (wgu_ref, wgu_o), (wdn_ref, wdn_o)):
        dst[...] = src[0].astype(BF16)

def _inproj_sample_kernel(x_ref, sc_ref, sh_ref, g_ref, w_ref, wg_ref, bg_ref, wpool_ref, spool_ref, hist_ref,
                          yp_ref, q_ref, kT_ref, v_ref, gc_ref, gr_ref, k_ref, hout_ref, u_ref, d_ref, *, pos0):
    _, hmod = _norm_mod(x_ref, sc_ref, sh_ref, g_ref)
    hb = hmod.astype(BF16)
    nb, tt = x_ref.shape[0], x_ref.shape[1]
    u = _dot(hb, w_ref[:, _U0:_Q0])
    k_ref[...] = _project(hb, w_ref, wg_ref, bg_ref, q_ref, kT_ref, v_ref, gc_ref, gr_ref).astype(BF16)
    for g, w in enumerate(POOL_WINDOWS):
        cols = slice(g * POOL_GROUP_DIM, (g + 1) * POOL_GROUP_DIM)
        u_ref[g] = u[:, cols]
        new = [u_ref[g, pl.ds(t, nb, stride=tt), :] for t in range(tt)]

        def ext(r):
            return hist_ref[r, :, cols] if r < POOL_HIST else new[r - POOL_HIST]

        for t in range(tt):
            acc = new[t]
            for j in range(1, w):
                acc = acc + ext(POOL_HIST + t - j)
            cnt = float(min(pos0 + t + 1, w))
            d_ref[g, pl.ds(t, nb, stride=tt), :] = acc / cnt - new[t]
        for r in range(POOL_HIST):
            hout_ref[r, :, cols] = ext(r + tt)
    yp_ref[...] = _pool_project([d_ref[g].astype(BF16) for g in range(len(POOL_WINDOWS))], wpool_ref, spool_ref)


def _cast_specs(weights, nt, steps):
    in_specs, out_specs, out_shapes = [], [], []
    for w in weights:
        _, r, c = w.shape
        n = steps
        while r % n or (r // n) % BF16_ROWS:
            n //= 2
        rows = r // n
        idx = lambda b, t, n=n: jnp.minimum(b * nt + t, n - 1)
        in_specs.append(pl.BlockSpec((1, rows, c), lambda b, t, idx=idx: (0, idx(b, t), 0)))
        out_specs.append(pl.BlockSpec((rows, c), lambda b, t, idx=idx: (idx(b, t), 0)))
        out_shapes.append(jax.ShapeDtypeStruct((r, c), BF16))
    return in_specs, out_specs, out_shapes


def _inproj_prompt(x, mod, row0, g_mix, w_head, w_gate, bgate, w_pool, s_pool, later_weights, w_in_t):
    nbatch, seq, _ = x.shape
    tm = TOKEN_TILE
    nt = seq // tm
    ntok = nbatch * seq
    row = lambda b, t: (b * nt + t, 0)
    sc, sh = (_mod_spec(k, 1, row0, lambda b, t: b) for k in (_SC1, _SH1))
    cast_in, cast_out, cast_shapes = _cast_specs(later_weights, nt, nbatch * nt)
    tail_row = lambda b, t: _G0 // LANES + _tail_block(b * nt + t)
    cast_in = cast_in + [
        pl.BlockSpec((1, LANES, D_MODEL), lambda b, t: (0, tail_row(b, t), 0)),
        pl.BlockSpec((1, _TAIL_SHIFT, D_MODEL), lambda b, t: (0, (tail_row(b, t) + 1) * (LANES // _TAIL_SHIFT), 0))]
    cast_out = cast_out + [pl.BlockSpec((D_MODEL, LANES), lambda b, t: (0, _tail_block(b * nt + t)))]
    cast_shapes = cast_shapes + [jax.ShapeDtypeStruct((D_MODEL, _TAIL_BLOCKS * LANES), BF16)]
    return pl.pallas_call(
        functools.partial(_inproj_prompt_kernel, tm=tm),
        grid=(nbatch, nt),
        in_specs=[pl.BlockSpec((1, tm, D_MODEL), lambda b, t: (b, t, 0)), sc, sh,
                  _const_spec((1, D_MODEL)), _const_spec((D_MODEL, _G0)), _const_spec((D_MODEL, GATE_LANES)),
                  _const_spec((1, GATE_LANES)), _const_spec((4, POOL_GROUP_DIM, POOL_GROUP_DIM)),
                  _const_spec((1, POOL_WIDTH))] + cast_in,
        out_specs=[pl.BlockSpec((tm, POOL_WIDTH), row), pl.BlockSpec((tm, D_MODEL), row),
                   pl.BlockSpec((1, D_MODEL, tm), lambda b, t: (b, 0, t)), pl.BlockSpec((tm, D_MODEL), row),
                   pl.BlockSpec((tm, GATE_LANES), row), pl.BlockSpec((1, SUBLANES, tm), lambda b, t: (b, 0, t)),
                   pl.BlockSpec((1, HIST_ROWS, POOL_WIDTH), lambda b, t: (b, 0, 0))] + cast_out,
        out_shape=[jax.ShapeDtypeStruct((ntok, POOL_WIDTH), BF16), jax.ShapeDtypeStruct((ntok, D_MODEL), BF16),
                   jax.ShapeDtypeStruct((nbatch, D_MODEL, seq), BF16), jax.ShapeDtypeStruct((ntok, D_MODEL), BF16),
                   jax.ShapeDtypeStruct((ntok, GATE_LANES), F32), jax.ShapeDtypeStruct((nbatch, SUBLANES, seq), F32),
                   jax.ShapeDtypeStruct((nbatch, HIST_ROWS, POOL_WIDTH), F32)] + cast_shapes,
        scratch_shapes=[pltpu.VMEM((tm + HIST_ROWS, POOL_WIDTH), F32)],
        compiler_params=_params(("arbitrary", "arbitrary")),
        name="inproj_prompt",
    )(x, mod, mod, g_mix, w_head, w_gate, bgate, w_pool, s_pool, *later_weights, w_in_t, w_in_t)


def _inproj_sample(x, mod, row0, g_mix, w_head, w_gate, bgate, w_pool, s_pool, hist_tm):
    nseq, tt, _ = x.shape
    nb = TOKEN_TILE // tt
    tm = nb * tt
    ntok = nseq * tt
    row = lambda i: (i, 0)
    col = lambda i: (0, i)
    sc, sh = (_mod_spec(k, nb, row0, lambda i: i) for k in (_SC1, _SH1))
    hist = pl.BlockSpec((POOL_HIST, nb, POOL_WIDTH), lambda i: (0, i, 0))
    return pl.pallas_call(
        functools.partial(_inproj_sample_kernel, pos0=PAST_LEN),
        grid=(nseq // nb,),
        in_specs=[pl.BlockSpec((nb, tt, D_MODEL), lambda i: (i, 0, 0)), sc, sh,
                  _const_spec((1, D_MODEL)), _const_spec((D_MODEL, _G0)), _const_spec((D_MODEL, GATE_LANES)),
                  _const_spec((1, GATE_LANES)), _const_spec((4, POOL_GROUP_DIM, POOL_GROUP_DIM)),
                  _const_spec((1, POOL_WIDTH)), hist],
        out_specs=[pl.BlockSpec((tm, POOL_WIDTH), row), pl.BlockSpec((tm, D_MODEL), row),
                   pl.BlockSpec((D_MODEL, tm), col), pl.BlockSpec((tm, D_MODEL), row),
                   pl.BlockSpec((tm, GATE_LANES), row), pl.BlockSpec((SUBLANES, tm), col),
                   pl.BlockSpec((tm, D_MODEL), row), hist],
        out_shape=[jax.ShapeDtypeStruct((ntok, POOL_WIDTH), BF16), jax.ShapeDtypeStruct((ntok, D_MODEL), BF16),
                   jax.ShapeDtypeStruct((D_MODEL, ntok), BF16), jax.ShapeDtypeStruct((ntok, D_MODEL), BF16),
                   jax.ShapeDtypeStruct((ntok, GATE_LANES), F32), jax.ShapeDtypeStruct((SUBLANES, ntok), F32),
                   jax.ShapeDtypeStruct((ntok, D_MODEL), BF16),
                   jax.ShapeDtypeStruct((POOL_HIST, nseq, POOL_WIDTH), F32)],
        scratch_shapes=[pltpu.VMEM((len(POOL_WINDOWS), tm, POOL_GROUP_DIM), F32)] * 2,
        compiler_params=_params(("arbitrary",)),
        name="inproj_sample",
    )(x, mod, mod, g_mix, w_head, w_gate, bgate, w_pool, s_pool, hist_tm)


def _mlstm_core(q, kT, v, b_c, b_r, ig_r, mask, m_prev, num_inter, qn):
    logd = jnp.where(mask, (b_c - b_r) + ig_r, -jnp.inf)
    a_c = b_c + m_prev
    m_t = jnp.maximum(a_c, jnp.max(logd, axis=-1, keepdims=True))
    w_intra = jnp.exp(logd - m_t)
    w_inter = jnp.exp(a_c - m_t)
    s = _dot(q, kT) * w_intra
    num = w_inter * num_inter + _dot(s.astype(BF16), v)
    den = w_inter * qn + jnp.sum(s, axis=-1, keepdims=True)
    hh = num / jnp.maximum(jnp.abs(den), jnp.exp(-m_t))
    return hh, m_t, a_c


def _mlstm_chunk(q_ref, kT_ref, v_ref, gr_ref, hh_ref, cn_ref, m_ref, r0, chunk):
    rows = slice(r0, r0 + chunk)
    gr = gr_ref[:, rows]
    ri = lax.broadcasted_iota(jnp.int32, (chunk, chunk), 0)
    ci = lax.broadcasted_iota(jnp.int32, (chunk, chunk), 1)
    causal = ri >= ci
    brow = _cumsum_rows(gr, (ri <= ci).astype(BF16))
    bcol = brow.T
    ones = jnp.ones((chunk, LANES), BF16)
    tile_to = lambda x, width: jnp.concatenate([x] * (width // LANES), axis=-1)
    stage = []
    for h in range(ML_HEADS):
        hs = slice(h * ML_HEAD_DIM, (h + 1) * ML_HEAD_DIM)
        b_r = brow[ML_HEADS + h:ML_HEADS + h + 1, :]
        g_r = gr[h:h + 1, :] - b_r
        m_prev = m_ref[h:h + 1, 0:1]
        cn = cn_ref[h]
        q = q_ref[rows, hs]
        kT = kT_ref[hs, rows]
        gm = jnp.where(causal, g_r, -jnp.inf)
        big_m = jnp.maximum(m_prev, jnp.max(gm, axis=-1, keepdims=True))
        m_rep = jnp.broadcast_to(big_m, (chunk, LANES))
        mt_rep = jnp.broadcast_to(bcol[:, ML_HEADS + h:ML_HEADS + h + 1] + big_m, (chunk, LANES))
        qkc = _dot(q, jnp.concatenate([kT, cn.astype(BF16)], axis=1))
        s = qkc[:, :chunk] * jnp.exp(gm - tile_to(m_rep, chunk))
        b_last = b_r[:, chunk - 1:chunk]
        m_new = b_last + jnp.maximum(m_prev, jnp.max(g_r, axis=-1, keepdims=True))
        decay = jnp.exp((b_last + m_prev) - m_new)
        w_end = jnp.exp((g_r + b_last) - m_new)
        lhs = jnp.concatenate([s.astype(BF16), (kT.astype(F32) * w_end).astype(BF16)], axis=0)
        stage.append((lhs, qkc[:, chunk:], m_prev, m_rep, mt_rep, m_new, decay, cn))

    for h, (lhs, qc, m_prev, m_rep, mt_rep, m_new, decay, cn) in enumerate(stage):
        hs = slice(h * ML_HEAD_DIM, (h + 1) * ML_HEAD_DIM)
        vaug = jnp.concatenate([v_ref[rows, hs], ones], axis=-1)
        both = _dot(lhs, vaug)
        sva = both[0:chunk]
        w_inter = jnp.exp(m_prev - m_rep)
        den = w_inter * qc[:, ML_HEAD_DIM:] + sva[:, ML_HEAD_DIM:]
        rinv = 1.0 / jnp.maximum(jnp.abs(den), jnp.exp(-mt_rep))
        hh_ref[rows, hs] = ((tile_to(w_inter, ML_HEAD_DIM) * qc[:, :ML_HEAD_DIM] + sva[:, :ML_HEAD_DIM])
                            * tile_to(rinv, ML_HEAD_DIM))
        cn_ref[h] = decay * cn + both[chunk:]
        m_ref[h:h + 1, :] = jnp.broadcast_to(m_new, (1, LANES))


def _mlstm_prompt_kernel(q_ref, kT_ref, v_ref, gr_ref, hh_ref, cout_ref, nout_ref, mout_ref, cn_ref, m_ref, *, chunk):
    t = pl.program_id(1)
    nb = q_ref.shape[0]

    @pl.when(t == 0)
    def _():
        cn_ref[...] = jnp.zeros(cn_ref.shape, F32)
        m_ref[...] = jnp.full(m_ref.shape, M_INIT, F32)

    for r0 in range(0, q_ref.shape[1], chunk):
        for s in range(nb):
            _mlstm_chunk(q_ref.at[s], kT_ref.at[s], v_ref.at[s], gr_ref.at[s], hh_ref.at[s], cn_ref.at[s],
                         m_ref.at[s], r0, chunk)

    @pl.when(t == pl.num_programs(1) - 1)
    def _():
        cout_ref[...] = cn_ref[:, :, :, 0:ML_HEAD_DIM]
        nout_ref[...] = cn_ref[:, :, :, ML_HEAD_DIM:AUG]
        mout_ref[...] = m_ref[...]


def _mlstm_prompt(q, kT, v, gr):
    nbatch, seq, _ = q.shape
    tm, nb = MLSTM_TILE, MLSTM_BATCHES
    row = pl.BlockSpec((nb, tm, D_MODEL), lambda b, t: (b, t, 0))
    state = lambda last: pl.BlockSpec((nb, ML_HEADS, ML_HEAD_DIM, last), lambda b, t: (b, 0, 0, 0))
    return pl.pallas_call(
        functools.partial(_mlstm_prompt_kernel, chunk=PROMPT_CHUNK),
        grid=(nbatch // nb, seq // tm),
        in_specs=[row, pl.BlockSpec((nb, D_MODEL, tm), lambda b, t: (b, 0, t)), row,
                  pl.BlockSpec((nb, SUBLANES, tm), lambda b, t: (b, 0, t))],
        out_specs=[row, state(ML_HEAD_DIM), state(LANES), pl.BlockSpec((nb, SUBLANES, LANES), lambda b, t: (b, 0, 0))],
        out_shape=[jax.ShapeDtypeStruct((nbatch, seq, D_MODEL), F32),
                   jax.ShapeDtypeStruct((nbatch, ML_HEADS, ML_HEAD_DIM, ML_HEAD_DIM), F32),
                   jax.ShapeDtypeStruct((nbatch, ML_HEADS, ML_HEAD_DIM, LANES), F32),
                   jax.ShapeDtypeStruct((nbatch, SUBLANES, LANES), F32)],
        scratch_shapes=[pltpu.VMEM((nb, ML_HEADS, ML_HEAD_DIM, AUG), F32), pltpu.VMEM((nb, SUBLANES, LANES), F32)],
        compiler_params=_params(("arbitrary", "arbitrary")),
        name="mlstm_prompt",
    )(q, kT, v, gr)


def _last_in_group(x, group):
    rows = x.shape[0]
    x3 = jnp.broadcast_to(x, (rows, LANES)).reshape(rows // group, group, LANES)
    last = jnp.broadcast_to(x3[:, group - 1:group, :], x3.shape)
    return last.reshape(rows, LANES)[:, 0:1]


def _mlstm_sample_body(h, q_ref, kT_ref, k_ref, v_ref, gc_ref, gr_ref, m0_ref, c_ref, n_ref,
                       hh_ref, cout_ref, nout_ref, mt_ref, ni_ref, qn_ref, dec_ref, wk_ref, *, tt):
    L = q_ref.shape[0]
    nseq = L // tt
    gc = gc_ref[...]
    gr = gr_ref[...]
    ri = lax.broadcasted_iota(jnp.int32, (L, L), 0)
    ci = lax.broadcasted_iota(jnp.int32, (L, L), 1)
    same = (ri // tt) == (ci // tt)
    mask = same & (ri >= ci)
    bcol = _cumsum_cols(mask.astype(BF16), gc)
    brow = _cumsum_rows(gr, (same & (ri <= ci)).astype(BF16))
    lane = lax.broadcasted_iota(jnp.int32, (L, GATE_LANES), 1)
    sub = lax.broadcasted_iota(jnp.int32, (SUBLANES, L), 0)
    pick_col = lambda arr, idx: jnp.sum(jnp.where(lane == idx, arr, 0.0), axis=-1, keepdims=True)
    pick_row = lambda arr, idx: jnp.sum(jnp.where(sub == idx, arr, 0.0), axis=0, keepdims=True)
    ig_c = pick_col(gc, h)
    b_c = pick_col(bcol, ML_HEADS + h)
    ig_r = pick_row(gr, h)
    b_r = pick_row(brow, ML_HEADS + h)
    m_prev = m0_ref[0]

    for j in range(nseq):
        rows = slice(j * tt, (j + 1) * tt)
        qj = q_ref[rows, :]
        ni_ref[rows, :] = _dot(qj, c_ref[j, 0].astype(BF16))
        nj = n_ref[j, pl.ds(h, 1), :].astype(BF16).astype(F32)
        qn = jnp.sum(qj.astype(F32) * nj, axis=-1, keepdims=True)
        qn_ref[rows, :] = jnp.broadcast_to(qn, (tt, LANES))

    q = q_ref[...]
    kT = kT_ref[...]
    v = v_ref[...]
    hh, m_t, a_c = _mlstm_core(q, kT, v, b_c, b_r, ig_r, mask, m_prev, ni_ref[...], qn_ref[:, 0:1])
    hh_ref[...] = hh
    m_new = _last_in_group(m_t, tt)
    decay = jnp.exp(_last_in_group(a_c, tt) - m_new)
    w_end = jnp.exp((_last_in_group(b_c, tt) - b_c) + ig_c - m_new)
    mt_ref[0] = jnp.broadcast_to(m_t, (L, LANES))
    dec_ref[...] = jnp.broadcast_to(decay, (L, LANES))
    wv = w_end * v.astype(F32)
    wk_ref[...] = w_end.astype(BF16).astype(F32) * k_ref[...].astype(F32)
    rowi = lax.broadcasted_iota(jnp.int32, (L, 1), 0)

    for j in range(nseq):
        rows = slice(j * tt, (j + 1) * tt)
        upd = _dot(kT, jnp.where((rowi // tt) == j, wv, 0.0).astype(BF16))
        dj = dec_ref[j * tt:j * tt + 1, 0:1]
        cout_ref[j, 0] = dj * c_ref[j, 0] + upd
        nout_ref[j, pl.ds(h, 1), :] = (dj * n_ref[j, pl.ds(h, 1), :]
                                       + jnp.sum(wk_ref[rows, :], axis=0, keepdims=True))


def _mlstm_sample_specs(ntok, tt):
    nseq = ntok // tt
    sb = SAMPLE_SEQ_BLOCK
    L = sb * tt
    nh = ML_HEADS
    qspec = pl.BlockSpec((L, ML_HEAD_DIM), lambda i: (i // nh, i % nh))
    cspec = pl.BlockSpec((sb, 1, ML_HEAD_DIM, ML_HEAD_DIM), lambda i: (i // nh, i % nh, 0, 0))
    nspec = pl.BlockSpec((sb, nh, ML_HEAD_DIM), lambda i: (i // nh, 0, 0))
    in_specs = [qspec, pl.BlockSpec((ML_HEAD_DIM, L), lambda i: (i % nh, i // nh)), qspec, qspec,
                pl.BlockSpec((L, GATE_LANES), lambda i: (i // nh, 0)),
                pl.BlockSpec((SUBLANES, L), lambda i: (0, i // nh)),
                pl.BlockSpec((1, L, 1), lambda i: (i % nh, i // nh, 0)), cspec, nspec]
    out_specs = [qspec, cspec, nspec, pl.BlockSpec((1, L, LANES), lambda i: (i % nh, i // nh, 0))]
    out_shapes = [jax.ShapeDtypeStruct((ntok, D_MODEL), F32),
                  jax.ShapeDtypeStruct((nseq, ML_HEADS, ML_HEAD_DIM, ML_HEAD_DIM), F32),
                  jax.ShapeDtypeStruct((nseq, ML_HEADS, ML_HEAD_DIM), F32),
                  jax.ShapeDtypeStruct((ML_HEADS, ntok, LANES), F32)]
    scratch = [pltpu.VMEM((L, ML_HEAD_DIM), F32), pltpu.VMEM((L, LANES), F32),
               pltpu.VMEM((L, LANES), F32), pltpu.VMEM((L, ML_HEAD_DIM), F32)]
    return (nseq // sb) * nh, in_specs, out_specs, out_shapes, scratch


def _post_kernel(x_ref, sc_ref, sh_ref, gt_ref, g_ref, hh_ref, yp_ref, wt_ref, ghead_ref, wbp_ref, wbm_ref, wout_ref,
                 o_ref):
    pool = _dot(yp_ref[...], wbp_ref[...])
    x, hmod = _norm_mod(x_ref, sc_ref, sh_ref, g_ref)
    hb = hmod.astype(BF16)
    nb, tt, d = x_ref.shape
    o = _dot(hb, wt_ref[:, 0:D_MODEL])
    parts = []
    for h in range(ML_HEADS):
        hh = hh_ref[:, h * ML_HEAD_DIM:(h + 1) * ML_HEAD_DIM]
        parts.append(hh * lax.rsqrt(jnp.mean(hh * hh, axis=-1, keepdims=True) + EPS))
    yml = (jnp.concatenate(parts, axis=-1) * ghead_ref[...]) * jax.nn.sigmoid(o)
    gp = _dot(hb, wt_ref[:, D_MODEL:2 * D_MODEL])
    gm = _dot(hb, wt_ref[:, 2 * D_MODEL:3 * D_MODEL])
    merged = jax.nn.sigmoid(gp) * pool + jax.nn.sigmoid(gm) * _dot(yml.astype(BF16), wbm_ref[...])
    y = _dot(merged.astype(BF16), wout_ref[...]).reshape(nb, tt, d)
    o_ref[...] = x_ref[...] + gt_ref[0] * y


def _tile_blocks(x, tile):
    g, t, _ = x.shape
    if t >= tile:
        nb, tt = 1, tile
    else:
        nb, tt = tile // t, t
    return nb, tt, (g // nb) * (t // tt), t // tt


def _post(x, mod, row0, g_mix, hh, yp, w_tail, g_head, w_bp, w_bm, w_out):
    nb, tt, steps, per = _tile_blocks(x, TOKEN_TILE)
    tm = nb * tt
    xspec = pl.BlockSpec((nb, tt, D_MODEL), lambda i: (i // per, i % per, 0))
    sc, sh, gt = (_mod_spec(k, nb, row0, lambda i: i // per) for k in (_SC1, _SH1, _GT1))
    row = lambda i: (i, 0)
    return pl.pallas_call(
        _post_kernel,
        grid=(steps,),
        in_specs=[xspec, sc, sh, gt, _const_spec((1, D_MODEL)),
                  pl.BlockSpec((tm, D_MODEL), row), pl.BlockSpec((tm, POOL_WIDTH), row),
                  _const_spec((D_MODEL, 3 * D_MODEL)), _const_spec((1, D_MODEL)),
                  _const_spec((POOL_WIDTH, D_MODEL)), _const_spec((D_MODEL, D_MODEL)),
                  _const_spec((D_MODEL, D_MODEL))],
        out_specs=xspec,
        out_shape=jax.ShapeDtypeStruct(x.shape, F32),
        compiler_params=_params(("arbitrary",)),
        name="post",
    )(x, mod, mod, mod, g_mix, hh, yp, w_tail, g_head, w_bp, w_bm, w_out)


_FF_SPLITS = ((0, 768), (768, 1536), (1536, 2304), (2304, D_FF))


def _ffn_kernel(x_ref, sc_ref, sh_ref, gt_ref, g_ref, gfin_ref, wgu_ref, wdn_ref, o_ref):
    _, hmod = _norm_mod(x_ref, sc_ref, sh_ref, g_ref)
    hb = hmod.astype(BF16)
    nb, tt, d = x_ref.shape
    dn = None
    for lo, hi in _FF_SPLITS:
        gate = _dot(hb, wgu_ref[:, lo:hi])
        up = _dot(hb, wgu_ref[:, D_FF + lo:D_FF + hi])
        act = (gate * jax.nn.sigmoid(gate) * up).astype(BF16)
        part = _dot(act, wdn_ref[lo:hi, :])
        dn = part if dn is None else dn + part
    x2 = x_ref[...] + gt_ref[0] * dn.reshape(nb, tt, d)
    ms = jnp.mean(x2 * x2, axis=-1, keepdims=True)
    o_ref[...] = x2 * lax.rsqrt(ms + EPS) * gfin_ref[...]


_N_FFN_IN = 8


def _ffn_mlstm_kernel(*refs, n_ml_in, tt):
    ffn_in = refs[:_N_FFN_IN]
    ml_in = refs[_N_FFN_IN:_N_FFN_IN + n_ml_in]
    o_ref = refs[_N_FFN_IN + n_ml_in]
    ml_rest = refs[_N_FFN_IN + n_ml_in + 1:]
    _mlstm_sample_body(pl.program_id(0) % ML_HEADS, *ml_in, *ml_rest, tt=tt)
    _ffn_kernel(*ffn_in, o_ref)


def _ffn(x, mod, row0, g_ffn, g_final, w_gu, w_down, tile, mlstm_sample=None):
    nb, tt, steps, per = _tile_blocks(x, tile)
    xspec = pl.BlockSpec((nb, tt, D_MODEL), lambda i: (i // per, i % per, 0))
    sc, sh, gt = (_mod_spec(k, nb, row0, lambda i: i // per) for k in (_SC2, _SH2, _GT2))
    in_specs = [xspec, sc, sh, gt, _const_spec((1, D_MODEL)), _const_spec((1, D_MODEL)),
                _const_spec((D_MODEL, 2 * D_FF)), _const_spec((D_FF, D_MODEL))]
    operands = (x, mod, mod, mod, g_ffn, g_final, w_gu, w_down)
    if mlstm_sample is None:
        return pl.pallas_call(
            _ffn_kernel,
            grid=(steps,),
            in_specs=in_specs,
            out_specs=xspec,
            out_shape=jax.ShapeDtypeStruct(x.shape, F32),
            compiler_params=_params(("arbitrary",)),
            name="ffn",
        )(*operands)
    ml_operands, ml_tt = mlstm_sample
    ml_steps, ml_in, ml_out, ml_shapes, ml_scratch = _mlstm_sample_specs(ml_operands[0].shape[0], ml_tt)
    assert ml_steps == steps, "one sample mLSTM step per FFN tile"
    return pl.pallas_call(
        functools.partial(_ffn_mlstm_kernel, n_ml_in=len(ml_in), tt=ml_tt),
        grid=(steps,),
        in_specs=in_specs + ml_in,
        out_specs=[xspec] + ml_out,
        out_shape=[jax.ShapeDtypeStruct(x.shape, F32)] + ml_shapes,
        scratch_shapes=ml_scratch,
        compiler_params=_params(("arbitrary",)),
        name="ffn_mlstm",
    )(*operands, *ml_operands)


def kernel(x_prompt, x_sample, c_prompt, c_sample, state_pool, state_mlstm_c, state_mlstm_n, state_mlstm_m, g_mix, g_ffn, g_final, w_ada, b_ada, w_in, b_igate, b_fgate, w_pool, s_pool, g_head, w_branch_pool, w_branch_mlstm, w_out, w_gate_up, w_down):
    depth = w_in.shape[0]
    assert depth == 1, "single-layer trunk"
    nbatch, seq, _ = x_prompt.shape
    nseq, tt, _ = x_sample.shape
    l = 0

    bgate = jnp.pad(jnp.concatenate([b_igate[l], b_fgate[l]])[None, :], ((0, 0), (0, GATE_LANES - 2 * ML_HEADS)))
    g_mix_r, g_ffn_r, g_fin_r = g_mix[l][None, :], g_ffn[l][None, :], g_final[None, :]
    s_pool_r, g_head_r = s_pool[l][None, :], g_head[l][None, :]

    w_in_t = jnp.swapaxes(w_in, 1, 2)
    w_head, w_gate, w_pool_b = _wprep(w_in_t, w_pool)
    mod = _ada(c_sample, c_prompt, w_ada[l], b_ada[l][None, :])
    row_s, row_p = 0, nseq

    (yp, q, kT, v, gc, gr, hist_p, w_bp, w_bm, w_o, w_gu, w_dn, w_tail) = _inproj_prompt(
        x_prompt, mod, row_p, g_mix_r, w_head, w_gate, bgate, w_pool_b, s_pool_r,
        (w_branch_pool, w_branch_mlstm, w_out, w_gate_up, w_down), w_in_t)
    hh, c_p, n_p, m_p = _mlstm_prompt(q.reshape(nbatch, seq, D_MODEL), kT, v.reshape(nbatch, seq, D_MODEL), gr)
    hh = hh.reshape(nbatch * seq, D_MODEL)
    x1 = _post(x_prompt, mod, row_p, g_mix_r, hh, yp, w_tail, g_head_r, w_bp, w_bm, w_o)

    hist_tm = jnp.swapaxes(state_pool[l], 0, 1)
    yp, q, kT, v, gc, gr, k, hist_s = _inproj_sample(x_sample, mod, row_s, g_mix_r, w_head, w_gate, bgate, w_pool_b,
                                                     s_pool_r, hist_tm)
    m0_tok = jnp.repeat(state_mlstm_m[l].astype(F32).T, tt, axis=1)[:, :, None]
    ml_operands = (q, kT, k, v, gc, gr, m0_tok, state_mlstm_c[l].astype(F32), state_mlstm_n[l].astype(F32))
    y_prompt, hh, c_s, n_s, mt = _ffn(x1, mod, row_p, g_ffn_r, g_fin_r, w_gu, w_dn, PROMPT_FFN_TILE,
                                      (ml_operands, tt))

    x1 = _post(x_sample, mod, row_s, g_mix_r, hh, yp, w_tail, g_head_r, w_bp, w_bm, w_o)
    y_sample = _ffn(x1, mod, row_s, g_ffn_r, g_fin_r, w_gu, w_dn, FFN_TILE)

    cd, nd, md = state_mlstm_c.dtype, state_mlstm_n.dtype, state_mlstm_m.dtype
    return (y_prompt, y_sample,
            hist_p[None, :, HIST_ROWS - POOL_HIST:, :],
            c_p.astype(cd)[None], n_p[..., 0].astype(nd)[None], m_p[:, :ML_HEADS, 0].astype(md)[None],
            jnp.swapaxes(hist_s, 0, 1)[None].astype(state_pool.dtype),
            c_s.astype(cd)[None], n_s.astype(nd)[None],
            mt[:, tt - 1::tt, 0].T.astype(md)[None])
```

```python
import functools

import jax
import jax.numpy as jnp
from jax import lax
from jax.experimental import pallas as pl
from jax.experimental.pallas import tpu as pltpu

D_MODEL = 1024
POOL_WINDOWS = (2, 4, 8, 16)
POOL_GROUP_DIM = 128
POOL_WIDTH = 512
POOL_HIST = 15
HIST_ROWS = 16
ML_HEADS = 4
ML_HEAD_DIM = 256
D_FF = 2816
EPS = 1e-6
M_INIT = -1e30
PAST_LEN = 16384
LANES = 128
SUBLANES = 8
BF16_ROWS = 16
GATE_LANES = LANES
AUG = ML_HEAD_DIM + LANES
VMEM_LIMIT = 56 * 1024 * 1024

TOKEN_TILE = 512
FFN_TILE = 1024
PROMPT_FFN_TILE = 512
PROMPT_CHUNK = 512
MLSTM_TILE = 1024
MLSTM_BATCHES = 1
SAMPLE_SEQ_BLOCK = 16

BF16 = jnp.bfloat16
F32 = jnp.float32

_U0, _Q0, _K0, _V0, _G0 = 0, 512, 1536, 2560, 3584


def _dot(a, b):
    return jnp.dot(a, b, preferred_element_type=F32)


def _const_spec(shape):
    zeros = (0,) * len(shape)
    return pl.BlockSpec(shape, lambda *_: zeros, pipeline_mode=pl.Buffered(1))


def _params(sem):
    return pltpu.CompilerParams(dimension_semantics=sem, vmem_limit_bytes=VMEM_LIMIT)


def _norm_mod(x_ref, sc_ref, sh_ref, g_ref):
    x = x_ref[...]
    nb, tt, d = x.shape
    ms = jnp.mean(x * x, axis=-1, keepdims=True)
    y = x * lax.rsqrt(ms + EPS) * g_ref[...]
    hmod = y * (1.0 + sc_ref[0]) + sh_ref[0]
    return x.reshape(nb * tt, d), hmod.reshape(nb * tt, d)


def _split3(x):
    hi = x.astype(BF16)
    r1 = x - hi.astype(F32)
    mid = r1.astype(BF16)
    lo = (r1 - mid.astype(F32)).astype(BF16)
    return hi, mid, lo


def _cumsum_cols(tri, x):
    n = x.shape[1]
    y = _dot(tri, jnp.concatenate(_split3(x), axis=1))
    return (y[:, 0:n] + y[:, n:2 * n]) + y[:, 2 * n:3 * n]


def _cumsum_rows(x, tri):
    n = x.shape[0]
    y = _dot(jnp.concatenate(_split3(x), axis=0), tri)
    return (y[0:n] + y[n:2 * n]) + y[2 * n:3 * n]


def _ada_kernel(cs_ref, cp_ref, w_ref, b_ref, o_ref):
    c = jnp.concatenate([cs_ref[...], cp_ref[...]], axis=0)
    a = (c * jax.nn.sigmoid(c)).astype(BF16)
    res = _dot(a, w_ref[...].astype(BF16)) + b_ref[...]
    o_ref[0] = res.reshape(res.shape[0], 1, res.shape[1])


def _ada(c_sample, c_prompt, w_ada, b_ada):
    rows = c_sample.shape[0] + c_prompt.shape[0]
    nchunk = w_ada.shape[1] // D_MODEL
    return pl.pallas_call(
        _ada_kernel,
        grid=(nchunk,),
        in_specs=[pl.BlockSpec(c_sample.shape, lambda j: (0, 0)),
                  pl.BlockSpec(c_prompt.shape, lambda j: (0, 0)),
                  pl.BlockSpec((D_MODEL, D_MODEL), lambda j: (0, j)),
                  pl.BlockSpec((1, D_MODEL), lambda j: (0, j))],
        out_specs=pl.BlockSpec((1, rows, 1, D_MODEL), lambda j: (j, 0, 0, 0)),
        out_shape=jax.ShapeDtypeStruct((nchunk, rows, 1, D_MODEL), F32),
        compiler_params=_params(("arbitrary",)),
        name="ada",
    )(c_sample, c_prompt, w_ada, b_ada)


_SH1, _SC1, _GT1, _SH2, _SC2, _GT2 = range(6)


def _mod_spec(k, nb, row0, block_index):
    return pl.BlockSpec((1, nb, 1, D_MODEL), lambda *g: (k, row0 // nb + block_index(*g), 0, 0))


_PREP_COLS = 512


def _wprep_kernel(wt_ref, gt_ref, wp_ref, head_ref, gate_ref, pool_ref):
    head_ref[...] = wt_ref[0].T.astype(BF16)
    gates = gt_ref[0].T
    pad = jnp.zeros((gates.shape[0], GATE_LANES - gates.shape[1]), F32)
    gate_ref[...] = jnp.concatenate([gates, pad], axis=1).astype(BF16)
    pool_ref[...] = wp_ref[0].astype(BF16)


def _wprep(w_in_t, w_pool):
    g0 = _G0 + D_MODEL
    return pl.pallas_call(
        _wprep_kernel,
        grid=(_G0 // _PREP_COLS,),
        in_specs=[pl.BlockSpec((1, _PREP_COLS, D_MODEL), lambda j: (0, j, 0)),
                  pl.BlockSpec((1, 2 * ML_HEADS, D_MODEL), lambda j: (0, g0 // (2 * ML_HEADS), 0)),
                  pl.BlockSpec(w_pool.shape, lambda j: (0, 0, 0, 0))],
        out_specs=[pl.BlockSpec((D_MODEL, _PREP_COLS), lambda j: (0, j)),
                   pl.BlockSpec((D_MODEL, GATE_LANES), lambda j: (0, 0)),
                   pl.BlockSpec(w_pool.shape[1:], lambda j: (0, 0, 0))],
        out_shape=[jax.ShapeDtypeStruct((D_MODEL, _G0), BF16),
                   jax.ShapeDtypeStruct((D_MODEL, GATE_LANES), BF16),
                   jax.ShapeDtypeStruct(w_pool.shape[1:], BF16)],
        compiler_params=_params(("arbitrary",)),
        name="wprep",
    )(w_in_t, w_in_t, w_pool)


_TAIL_BLOCKS = 3 * D_MODEL // LANES
_TAIL_SHIFT_BLOCK = D_MODEL // LANES
_TAIL_SHIFT = 2 * ML_HEADS


def _tail_block(step):
    return jnp.minimum(step, _TAIL_BLOCKS - 1)


def _tail_prep(step, wa_ref, wb_ref, tail_ref):
    c = _tail_block(step)
    a = wa_ref[0]
    shifted = jnp.concatenate([a[_TAIL_SHIFT:, :], wb_ref[0]], axis=0)
    tail_ref[...] = jnp.where(c >= _TAIL_SHIFT_BLOCK, shifted, a).T.astype(BF16)


def _project(hb, w_ref, wg_ref, bg_ref, q_ref, kT_ref, v_ref, gc_ref, gr_ref):
    zg = _dot(hb, wg_ref[...]) + bg_ref[...]
    k = _dot(hb, w_ref[:, _K0:_V0]) * (ML_HEAD_DIM ** -0.5)
    log_f = jnp.minimum(zg, 0.0) - jnp.log1p(jnp.exp(-jnp.abs(zg)))
    lane = lax.broadcasted_iota(jnp.int32, zg.shape, 1)
    gc = jnp.where(lane < ML_HEADS, zg, log_f)
    gc_ref[...] = gc
    gr_ref[...] = gc.T[0:SUBLANES, :]
    kT_ref[...] = k.T.astype(BF16)
    q_ref[...] = _dot(hb, w_ref[:, _Q0:_K0]).astype(BF16)
    v_ref[...] = _dot(hb, w_ref[:, _V0:_G0]).astype(BF16)
    return k


def _pool_deltas(acc_fn, u, cnt_fn):
    deltas = []
    for g, w in enumerate(POOL_WINDOWS):
        ug = u[:, g * POOL_GROUP_DIM:(g + 1) * POOL_GROUP_DIM]
        deltas.append((acc_fn(g, w, ug) / cnt_fn(w) - ug).astype(BF16))
    return deltas


def _pool_project(deltas, wpool_ref, spool_ref):
    outs = [_dot(d, wpool_ref[g]) for g, d in enumerate(deltas)]
    return (jnp.concatenate(outs, axis=-1) * spool_ref[...]).astype(BF16)


def _inproj_prompt_kernel(x_ref, sc_ref, sh_ref, g_ref, w_ref, wg_ref, bg_ref, wpool_ref, spool_ref,
                          wbp_ref, wbm_ref, wout_ref, wgu_ref, wdn_ref, wa_ref, wb_ref,
                          yp_ref, q_ref, kT_ref, v_ref, gc_ref, gr_ref, hout_ref,
                          wbp_o, wbm_o, wout_o, wgu_o, wdn_o, tail_o, ext_ref, *, tm):
    t = pl.program_id(1)

    @pl.when(t == 0)
    def _():
        ext_ref[0:HIST_ROWS, :] = jnp.zeros((HIST_ROWS, POOL_WIDTH), F32)

    _, hmod = _norm_mod(x_ref, sc_ref, sh_ref, g_ref)
    hb = hmod.astype(BF16)
    u = _dot(hb, w_ref[:, _U0:_Q0])
    ext_ref[HIST_ROWS:HIST_ROWS + tm, :] = u
    pos = t * tm + lax.broadcasted_iota(jnp.int32, (tm, 1), 0)

    def acc_fn(g, w, ug):
        acc = ug
        for j in range(1, w):
            acc = acc + ext_ref[pl.ds(HIST_ROWS - j, tm), g * POOL_GROUP_DIM:(g + 1) * POOL_GROUP_DIM]
        return acc

    def cnt_fn(w):
        return jnp.minimum(pos + 1, w).astype(F32)

    deltas = _pool_deltas(acc_fn, u, cnt_fn)
    last = ext_ref[tm:tm + HIST_ROWS, :]
    hout_ref[0] = last
    ext_ref[0:HIST_ROWS, :] = last

    _project(hb, w_ref, wg_ref, bg_ref, q_ref, kT_ref.at[0], v_ref, gc_ref, gr_ref.at[0])
    yp_ref[...] = _pool_project(deltas, wpool_ref, spool_ref)

    _tail_prep(pl.program_id(0) * pl.num_programs(1) + t, wa_ref, wb_ref, tail_o)
    for src, dst in ((wbp_ref, wbp_o), (wbm_ref, wbm_o), (wout_ref, wout_o), (wgu_ref, wgu_o), (wdn_ref, wdn_o)):
        dst[...] = src[0].astype(BF16)

def _inproj_sample_kernel(x_ref, sc_ref, sh_ref, g_ref, w_ref, wg_ref, bg_ref, wpool_ref, spool_ref, hist_ref,
                          yp_ref, q_ref, kT_ref, v_ref, gc_ref, gr_ref, k_ref, hout_ref, u_ref, d_ref, *, pos0):
    _, hmod = _norm_mod(x_ref, sc_ref, sh_ref, g_ref)
    hb = hmod.astype(BF16)
    nb, tt = x_ref.shape[0], x_ref.shape[1]
    u = _dot(hb, w_ref[:, _U0:_Q0])
    k_ref[...] = _project(hb, w_ref, wg_ref, bg_ref, q_ref, kT_ref, v_ref, gc_ref, gr_ref).astype(BF16)
    for g, w in enumerate(POOL_WINDOWS):
        cols = slice(g * POOL_GROUP_DIM, (g + 1) * POOL_GROUP_DIM)
        u_ref[g] = u[:, cols]
        new = [u_ref[g, pl.ds(t, nb, stride=tt), :] for t in range(tt)]

        def ext(r):
            return hist_ref[r, :, cols] if r < POOL_HIST else new[r - POOL_HIST]

        for t in range(tt):
            acc = new[t]
            for j in range(1, w):
                acc = acc + ext(POOL_HIST + t - j)
            cnt = float(min(pos0 + t + 1, w))
            d_ref[g, pl.ds(t, nb, stride=tt), :] = acc / cnt - new[t]
        for r in range(POOL_HIST):
            hout_ref[r, :, cols] = ext(r + tt)
    yp_ref[...] = _pool_project([d_ref[g].astype(BF16) for g in range(len(POOL_WINDOWS))], wpool_ref, spool_ref)


def _cast_specs(weights, nt, steps):
    in_specs, out_specs, out_shapes = [], [], []
    for w in weights:
        _, r, c = w.shape
        n = steps
        while r % n or (r // n) % BF16_ROWS:
            n //= 2
        rows = r // n
        idx = lambda b, t, n=n: jnp.minimum(b * nt + t, n - 1)
        in_specs.append(pl.BlockSpec((1, rows, c), lambda b, t, idx=idx: (0, idx(b, t), 0)))
        out_specs.append(pl.BlockSpec((rows, c), lambda b, t, idx=idx: (idx(b, t), 0)))
        out_shapes.append(jax.ShapeDtypeStruct((r, c), BF16))
    return in_specs, out_specs, out_shapes


def _inproj_prompt(x, mod, row0, g_mix, w_head, w_gate, bgate, w_pool, s_pool, later_weights, w_in_t):
    nbatch, seq, _ = x.shape
    tm = TOKEN_TILE
    nt = seq // tm
    ntok = nbatch * seq
    row = lambda b, t: (b * nt + t, 0)
    sc, sh = (_mod_spec(k, 1, row0, lambda b, t: b) for k in (_SC1, _SH1))
    cast_in, cast_out, cast_shapes = _cast_specs(later_weights, nt, nbatch * nt)
    tail_row = lambda b, t: _G0 // LANES + _tail_block(b * nt + t)
    cast_in = cast_in + [
        pl.BlockSpec((1, LANES, D_MODEL), lambda b, t: (0, tail_row(b, t), 0)),
        pl.BlockSpec((1, _TAIL_SHIFT, D_MODEL), lambda b, t: (0, (tail_row(b, t) + 1) * (LANES // _TAIL_SHIFT), 0))]
    cast_out = cast_out + [pl.BlockSpec((D_MODEL, LANES), lambda b, t: (0, _tail_block(b * nt + t)))]
    cast_shapes = cast_shapes + [jax.ShapeDtypeStruct((D_MODEL, _TAIL_BLOCKS * LANES), BF16)]
    return pl.pallas_call(
        functools.partial(_inproj_prompt_kernel, tm=tm),
        grid=(nbatch, nt),
        in_specs=[pl.BlockSpec((1, tm, D_MODEL), lambda b, t: (b, t, 0)), sc, sh,
                  _const_spec((1, D_MODEL)), _const_spec((D_MODEL, _G0)), _const_spec((D_MODEL, GATE_LANES)),
                  _const_spec((1, GATE_LANES)), _const_spec((4, POOL_GROUP_DIM, POOL_GROUP_DIM)),
                  _const_spec((1, POOL_WIDTH))] + cast_in,
        out_specs=[pl.BlockSpec((tm, POOL_WIDTH), row), pl.BlockSpec((tm, D_MODEL), row),
                   pl.BlockSpec((1, D_MODEL, tm), lambda b, t: (b, 0, t)), pl.BlockSpec((tm, D_MODEL), row),
                   pl.BlockSpec((tm, GATE_LANES), row), pl.BlockSpec((1, SUBLANES, tm), lambda b, t: (b, 0, t)),
                   pl.BlockSpec((1, HIST_ROWS, POOL_WIDTH), lambda b, t: (b, 0, 0))] + cast_out,
        out_shape=[jax.ShapeDtypeStruct((ntok, POOL_WIDTH), BF16), jax.ShapeDtypeStruct((ntok, D_MODEL), BF16),
                   jax.ShapeDtypeStruct((nbatch, D_MODEL, seq), BF16), jax.ShapeDtypeStruct((ntok, D_MODEL), BF16),
                   jax.ShapeDtypeStruct((ntok, GATE_LANES), F32), jax.ShapeDtypeStruct((nbatch, SUBLANES, seq), F32),
                   jax.ShapeDtypeStruct((nbatch, HIST_ROWS, POOL_WIDTH), F32)] + cast_shapes,
        scratch_shapes=[pltpu.VMEM((tm + HIST_ROWS, POOL_WIDTH), F32)],
        compiler_params=_params(("arbitrary", "arbitrary")),
        name="inproj_prompt",
    )(x, mod, mod, g_mix, w_head, w_gate, bgate, w_pool, s_pool, *later_weights, w_in_t, w_in_t)


def _inproj_sample(x, mod, row0, g_mix, w_head, w_gate, bgate, w_pool, s_pool, hist_tm):
    nseq, tt, _ = x.shape
    nb = TOKEN_TILE // tt
    tm = nb * tt
    ntok = nseq * tt
    row = lambda i: (i, 0)
    col = lambda i: (0, i)
    sc, sh = (_mod_spec(k, nb, row0, lambda i: i) for k in (_SC1, _SH1))
    hist = pl.BlockSpec((POOL_HIST, nb, POOL_WIDTH), lambda i: (0, i, 0))
    return pl.pallas_call(
        functools.partial(_inproj_sample_kernel, pos0=PAST_LEN),
        grid=(nseq // nb,),
        in_specs=[pl.BlockSpec((nb, tt, D_MODEL), lambda i: (i, 0, 0)), sc, sh,
                  _const_spec((1, D_MODEL)), _const_spec((D_MODEL, _G0)), _const_spec((D_MODEL, GATE_LANES)),
                  _const_spec((1, GATE_LANES)), _const_spec((4, POOL_GROUP_DIM, POOL_GROUP_DIM)),
                  _const_spec((1, POOL_WIDTH)), hist],
        out_specs=[pl.BlockSpec((tm, POOL_WIDTH), row), pl.BlockSpec((tm, D_MODEL), row),
                   pl.BlockSpec((D_MODEL, tm), col), pl.BlockSpec((tm, D_MODEL), row),
                   pl.BlockSpec((tm, GATE_LANES), row), pl.BlockSpec((SUBLANES, tm), col),
                   pl.BlockSpec((tm, D_MODEL), row), hist],
        out_shape=[jax.ShapeDtypeStruct((ntok, POOL_WIDTH), BF16), jax.ShapeDtypeStruct((ntok, D_MODEL), BF16),
                   jax.ShapeDtypeStruct((D_MODEL, ntok), BF16), jax.ShapeDtypeStruct((ntok, D_MODEL), BF16),
                   jax.ShapeDtypeStruct((ntok, GATE_LANES), F32), jax.ShapeDtypeStruct((SUBLANES, ntok), F32),
                   jax.ShapeDtypeStruct((ntok, D_MODEL), BF16),
                   jax.ShapeDtypeStruct((POOL_HIST, nseq, POOL_WIDTH), F32)],
        scratch_shapes=[pltpu.VMEM((len(POOL_WINDOWS), tm, POOL_GROUP_DIM), F32)] * 2,
        compiler_params=_params(("arbitrary",)),
        name="inproj_sample",
    )(x, mod, mod, g_mix, w_head, w_gate, bgate, w_pool, s_pool, hist_tm)


def _mlstm_chunk(q_ref, kT_ref, v_ref, gr_ref, hh_ref, cn_ref, m_ref, r0, chunk):
    rows = slice(r0, r0 + chunk)
    gr = gr_ref[:, rows]
    ri = lax.broadcasted_iota(jnp.int32, (chunk, chunk), 0)
    ci = lax.broadcasted_iota(jnp.int32, (chunk, chunk), 1)
    causal = ri >= ci
    brow = _cumsum_rows(gr, (ri <= ci).astype(BF16))
    bcol = brow.T
    ones = jnp.ones((chunk, LANES), BF16)
    tile_to = lambda x, width: jnp.concatenate([x] * (width // LANES), axis=-1)
    stage = []
    for h in range(ML_HEADS):
        hs = slice(h * ML_HEAD_DIM, (h + 1) * ML_HEAD_DIM)
        b_r = brow[ML_HEADS + h:ML_HEADS + h + 1, :]
        g_r = gr[h:h + 1, :] - b_r
        m_prev = m_ref[h:h + 1, 0:1]
        cn = cn_ref[h]
        q = q_ref[rows, hs]
        kT = kT_ref[hs, rows]
        gm = jnp.where(causal, g_r, -jnp.inf)
        big_m = jnp.maximum(m_prev, jnp.max(gm, axis=-1, keepdims=True))
        m_rep = jnp.broadcast_to(big_m, (chunk, LANES))
        mt_rep = jnp.broadcast_to(bcol[:, ML_HEADS + h:ML_HEADS + h + 1] + big_m, (chunk, LANES))
        qkc = _dot(q, jnp.concatenate([kT, cn.astype(BF16)], axis=1))
        s = qkc[:, :chunk] * jnp.exp(gm - tile_to(m_rep, chunk))
        b_last = b_r[:, chunk - 1:chunk]
        m_new = b_last + jnp.maximum(m_prev, jnp.max(g_r, axis=-1, keepdims=True))
        decay = jnp.exp((b_last + m_prev) - m_new)
        w_end = jnp.exp((g_r + b_last) - m_new)
        lhs = jnp.concatenate([s.astype(BF16), (kT.astype(F32) * w_end).astype(BF16)], axis=0)
        stage.append((lhs, qkc[:, chunk:], m_prev, m_rep, mt_rep, m_new, decay, cn))

    for h, (lhs, qc, m_prev, m_rep, mt_rep, m_new, decay, cn) in enumerate(stage):
        hs = slice(h * ML_HEAD_DIM, (h + 1) * ML_HEAD_DIM)
        vaug = jnp.concatenate([v_ref[rows, hs], ones], axis=-1)
        both = _dot(lhs, vaug)
        sva = both[0:chunk]
        w_inter = jnp.exp(m_prev - m_rep)
        den = w_inter * qc[:, ML_HEAD_DIM:] + sva[:, ML_HEAD_DIM:]
        rinv = 1.0 / jnp.maximum(jnp.abs(den), jnp.exp(-mt_rep))
        hh_ref[rows, hs] = ((tile_to(w_inter, ML_HEAD_DIM) * qc[:, :ML_HEAD_DIM] + sva[:, :ML_HEAD_DIM])
                            * tile_to(rinv, ML_HEAD_DIM))
        cn_ref[h] = decay * cn + both[chunk:]
        m_ref[h:h + 1, :] = jnp.broadcast_to(m_new, (1, LANES))


def _mlstm_prompt_kernel(q_ref, kT_ref, v_ref, gr_ref, hh_ref, cout_ref, nout_ref, mout_ref, cn_ref, m_ref, *, chunk):
    t = pl.program_id(1)
    nb = q_ref.shape[0]

    @pl.when(t == 0)
    def _():
        cn_ref[...] = jnp.zeros(cn_ref.shape, F32)
        m_ref[...] = jnp.full(m_ref.shape, M_INIT, F32)

    for r0 in range(0, q_ref.shape[1], chunk):
        for s in range(nb):
            _mlstm_chunk(q_ref.at[s], kT_ref.at[s], v_ref.at[s], gr_ref.at[s], hh_ref.at[s], cn_ref.at[s],
                         m_ref.at[s], r0, chunk)

    @pl.when(t == pl.num_programs(1) - 1)
    def _():
        cout_ref[...] = cn_ref[:, :, :, 0:ML_HEAD_DIM]
        nout_ref[...] = cn_ref[:, :, :, ML_HEAD_DIM:AUG]
        mout_ref[...] = m_ref[...]


def _mlstm_prompt(q, kT, v, gr):
    nbatch, seq, _ = q.shape
    tm, nb = MLSTM_TILE, MLSTM_BATCHES
    row = pl.BlockSpec((nb, tm, D_MODEL), lambda b, t: (b, t, 0))
    state = lambda last: pl.BlockSpec((nb, ML_HEADS, ML_HEAD_DIM, last), lambda b, t: (b, 0, 0, 0))
    return pl.pallas_call(
        functools.partial(_mlstm_prompt_kernel, chunk=PROMPT_CHUNK),
        grid=(nbatch // nb, seq // tm),
        in_specs=[row, pl.BlockSpec((nb, D_MODEL, tm), lambda b, t: (b, 0, t)), row,
                  pl.BlockSpec((nb, SUBLANES, tm), lambda b, t: (b, 0, t))],
        out_specs=[row, state(ML_HEAD_DIM), state(LANES), pl.BlockSpec((nb, SUBLANES, LANES), lambda b, t: (b, 0, 0))],
        out_shape=[jax.ShapeDtypeStruct((nbatch, seq, D_MODEL), F32),
                   jax.ShapeDtypeStruct((nbatch, ML_HEADS, ML_HEAD_DIM, ML_HEAD_DIM), F32),
                   jax.ShapeDtypeStruct((nbatch, ML_HEADS, ML_HEAD_DIM, LANES), F32),
                   jax.ShapeDtypeStruct((nbatch, SUBLANES, LANES), F32)],
        scratch_shapes=[pltpu.VMEM((nb, ML_HEADS, ML_HEAD_DIM, AUG), F32), pltpu.VMEM((nb, SUBLANES, LANES), F32)],
        compiler_params=_params(("arbitrary", "arbitrary")),
        name="mlstm_prompt",
    )(q, kT, v, gr)


def _last_in_group(x, group):
    rows = x.shape[0]
    x3 = jnp.broadcast_to(x, (rows, LANES)).reshape(rows // group, group, LANES)
    last = jnp.broadcast_to(x3[:, group - 1:group, :], x3.shape)
    return last.reshape(rows, LANES)[:, 0:1]


def _mlstm_sample_body(h, q_ref, kT_ref, k_ref, v_ref, gc_ref, gr_ref, m0_ref, c_ref, n_ref,
                       hh_ref, cout_ref, nout_ref, mt_ref, ni_ref, qn_ref, dec_ref, wk_ref, *, tt):
    L = q_ref.shape[0]
    nseq = L // tt
    gc = gc_ref[...]
    gr = gr_ref[...]
    ri = lax.broadcasted_iota(jnp.int32, (L, L), 0)
    ci = lax.broadcasted_iota(jnp.int32, (L, L), 1)
    same = (ri // tt) == (ci // tt)
    mask = same & (ri >= ci)
    bcol = _cumsum_cols(mask.astype(BF16), gc)
    brow = _cumsum_rows(gr, (same & (ri <= ci)).astype(BF16))
    lane = lax.broadcasted_iota(jnp.int32, (L, GATE_LANES), 1)
    sub = lax.broadcasted_iota(jnp.int32, (SUBLANES, L), 0)
    pick_col = lambda arr, idx: jnp.sum(jnp.where(lane == idx, arr, 0.0), axis=-1, keepdims=True)
    pick_row = lambda arr, idx: jnp.sum(jnp.where(sub == idx, arr, 0.0), axis=0, keepdims=True)
    ig_c = pick_col(gc, h)
    b_c = pick_col(bcol, ML_HEADS + h)
    ig_r = pick_row(gr, h)
    b_r = pick_row(brow, ML_HEADS + h)
    m_prev = m0_ref[0]

    for j in range(nseq):
        rows = slice(j * tt, (j + 1) * tt)
        qj = q_ref[rows, :]
        ni_ref[rows, :] = _dot(qj, c_ref[j, 0].astype(BF16))
        nj = n_ref[j, pl.ds(h, 1), :].astype(BF16).astype(F32)
        qn = jnp.sum(qj.astype(F32) * nj, axis=-1, keepdims=True)
        qn_ref[rows, :] = jnp.broadcast_to(qn, (tt, LANES))

    q = q_ref[...]
    kT = kT_ref[...]
    v = v_ref[...]
    qk = _dot(q, kT)
    logd = jnp.where(mask, (b_c - b_r) + ig_r, -jnp.inf)
    a_c = b_c + m_prev
    m_t = jnp.maximum(a_c, jnp.max(logd, axis=-1, keepdims=True))
    w_inter = jnp.exp(a_c - m_t)
    s = qk * jnp.exp(logd - m_t)
    num = w_inter * ni_ref[...] + _dot(s.astype(BF16), v)
    den = w_inter * qn_ref[:, 0:1] + jnp.sum(s, axis=-1, keepdims=True)
    hh_ref[...] = num / jnp.maximum(jnp.abs(den), jnp.exp(-m_t))
    m_new = _last_in_group(m_t, tt)
    decay = jnp.exp(_last_in_group(a_c, tt) - m_new)
    w_end = jnp.exp((_last_in_group(b_c, tt) - b_c) + ig_c - m_new)
    mt_ref[0] = jnp.broadcast_to(m_t, (L, LANES))
    dec_ref[...] = jnp.broadcast_to(decay, (L, LANES))
    wv = w_end * v.astype(F32)
    wk_ref[...] = w_end.astype(BF16).astype(F32) * k_ref[...].astype(F32)
    rowi = lax.broadcasted_iota(jnp.int32, (L, 1), 0)

    for j in range(nseq):
        rows = slice(j * tt, (j + 1) * tt)
        upd = _dot(kT, jnp.where((rowi // tt) == j, wv, 0.0).astype(BF16))
        dj = dec_ref[j * tt:j * tt + 1, 0:1]
        cout_ref[j, 0] = dj * c_ref[j, 0] + upd
        nout_ref[j, pl.ds(h, 1), :] = (dj * n_ref[j, pl.ds(h, 1), :]
                                       + jnp.sum(wk_ref[rows, :], axis=0, keepdims=True))


def _mlstm_sample_specs(ntok, tt):
    nseq = ntok // tt
    sb = SAMPLE_SEQ_BLOCK
    L = sb * tt
    nh = ML_HEADS
    qspec = pl.BlockSpec((L, ML_HEAD_DIM), lambda i: (i // nh, i % nh))
    cspec = pl.BlockSpec((sb, 1, ML_HEAD_DIM, ML_HEAD_DIM), lambda i: (i // nh, i % nh, 0, 0))
    nspec = pl.BlockSpec((sb, nh, ML_HEAD_DIM), lambda i: (i // nh, 0, 0))
    in_specs = [qspec, pl.BlockSpec((ML_HEAD_DIM, L), lambda i: (i % nh, i // nh)), qspec, qspec,
                pl.BlockSpec((L, GATE_LANES), lambda i: (i // nh, 0)),
                pl.BlockSpec((SUBLANES, L), lambda i: (0, i // nh)),
                pl.BlockSpec((1, L, 1), lambda i: (i % nh, i // nh, 0)), cspec, nspec]
    out_specs = [qspec, cspec, nspec, pl.BlockSpec((1, L, LANES), lambda i: (i % nh, i // nh, 0))]
    out_shapes = [jax.ShapeDtypeStruct((ntok, D_MODEL), F32),
                  jax.ShapeDtypeStruct((nseq, ML_HEADS, ML_HEAD_DIM, ML_HEAD_DIM), F32),
                  jax.ShapeDtypeStruct((nseq, ML_HEADS, ML_HEAD_DIM), F32),
                  jax.ShapeDtypeStruct((ML_HEADS, ntok, LANES), F32)]
    scratch = [pltpu.VMEM((L, ML_HEAD_DIM), F32), pltpu.VMEM((L, LANES), F32),
               pltpu.VMEM((L, LANES), F32), pltpu.VMEM((L, ML_HEAD_DIM), F32)]
    return (nseq // sb) * nh, in_specs, out_specs, out_shapes, scratch


def _post_kernel(x_ref, sc_ref, sh_ref, gt_ref, g_ref, hh_ref, yp_ref, wt_ref, ghead_ref, wbp_ref, wbm_ref, wout_ref,
                 o_ref):
    pool = _dot(yp_ref[...], wbp_ref[...])
    x, hmod = _norm_mod(x_ref, sc_ref, sh_ref, g_ref)
    hb = hmod.astype(BF16)
    nb, tt, d = x_ref.shape
    o = _dot(hb, wt_ref[:, 0:D_MODEL])
    parts = []
    for h in range(ML_HEADS):
        hh = hh_ref[:, h * ML_HEAD_DIM:(h + 1) * ML_HEAD_DIM]
        parts.append(hh * lax.rsqrt(jnp.mean(hh * hh, axis=-1, keepdims=True) + EPS))
    yml = (jnp.concatenate(parts, axis=-1) * ghead_ref[...]) * jax.nn.sigmoid(o)
    gp = _dot(hb, wt_ref[:, D_MODEL:2 * D_MODEL])
    gm = _dot(hb, wt_ref[:, 2 * D_MODEL:3 * D_MODEL])
    merged = jax.nn.sigmoid(gp) * pool + jax.nn.sigmoid(gm) * _dot(yml.astype(BF16), wbm_ref[...])
    y = _dot(merged.astype(BF16), wout_ref[...]).reshape(nb, tt, d)
    o_ref[...] = x_ref[...] + gt_ref[0] * y


def _tile_blocks(x, tile):
    g, t, _ = x.shape
    if t >= tile:
        nb, tt = 1, tile
    else:
        nb, tt = tile // t, t
    return nb, tt, (g // nb) * (t // tt), t // tt


def _post(x, mod, row0, g_mix, hh, yp, w_tail, g_head, w_bp, w_bm, w_out):
    nb, tt, steps, per = _tile_blocks(x, TOKEN_TILE)
    tm = nb * tt
    xspec = pl.BlockSpec((nb, tt, D_MODEL), lambda i: (i // per, i % per, 0))
    sc, sh, gt = (_mod_spec(k, nb, row0, lambda i: i // per) for k in (_SC1, _SH1, _GT1))
    row = lambda i: (i, 0)
    return pl.pallas_call(
        _post_kernel,
        grid=(steps,),
        in_specs=[xspec, sc, sh, gt, _const_spec((1, D_MODEL)),
                  pl.BlockSpec((tm, D_MODEL), row), pl.BlockSpec((tm, POOL_WIDTH), row),
                  _const_spec((D_MODEL, 3 * D_MODEL)), _const_spec((1, D_MODEL)),
                  _const_spec((POOL_WIDTH, D_MODEL)), _const_spec((D_MODEL, D_MODEL)),
                  _const_spec((D_MODEL, D_MODEL))],
        out_specs=xspec,
        out_shape=jax.ShapeDtypeStruct(x.shape, F32),
        compiler_params=_params(("arbitrary",)),
        name="post",
    )(x, mod, mod, mod, g_mix, hh, yp, w_tail, g_head, w_bp, w_bm, w_out)


_FF_SPLITS = ((0, 768), (768, 1536), (1536, 2304), (2304, D_FF))


def _ffn_kernel(x_ref, sc_ref, sh_ref, gt_ref, g_ref, gfin_ref, wgu_ref, wdn_ref, o_ref):
    _, hmod = _norm_mod(x_ref, sc_ref, sh_ref, g_ref)
    hb = hmod.astype(BF16)
    nb, tt, d = x_ref.shape
    dn = None
    for lo, hi in _FF_SPLITS:
        gate = _dot(hb, wgu_ref[:, lo:hi])
        up = _dot(hb, wgu_ref[:, D_FF + lo:D_FF + hi])
        act = (gate * jax.nn.sigmoid(gate) * up).astype(BF16)
        part = _dot(act, wdn_ref[lo:hi, :])
        dn = part if dn is None else dn + part
    x2 = x_ref[...] + gt_ref[0] * dn.reshape(nb, tt, d)
    ms = jnp.mean(x2 * x2, axis=-1, keepdims=True)
    o_ref[...] = x2 * lax.rsqrt(ms + EPS) * gfin_ref[...]


_N_FFN_IN = 8


def _ffn_mlstm_kernel(*refs, n_ml_in, tt):
    ffn_in = refs[:_N_FFN_IN]
    ml_in = refs[_N_FFN_IN:_N_FFN_IN + n_ml_in]
    o_ref = refs[_N_FFN_IN + n_ml_in]
    ml_rest = refs[_N_FFN_IN + n_ml_in + 1:]
    _mlstm_sample_body(pl.program_id(0) % ML_HEADS, *ml_in, *ml_rest, tt=tt)
    _ffn_kernel(*ffn_in, o_ref)


def _ffn(x, mod, row0, g_ffn, g_final, w_gu, w_down, tile, mlstm_sample=None):
    nb, tt, steps, per = _tile_blocks(x, tile)
    xspec = pl.BlockSpec((nb, tt, D_MODEL), lambda i: (i // per, i % per, 0))
    sc, sh, gt = (_mod_spec(k, nb, row0, lambda i: i // per) for k in (_SC2, _SH2, _GT2))
    in_specs = [xspec, sc, sh, gt, _const_spec((1, D_MODEL)), _const_spec((1, D_MODEL)),
                _const_spec((D_MODEL, 2 * D_FF)), _const_spec((D_FF, D_MODEL))]
    operands = (x, mod, mod, mod, g_ffn, g_final, w_gu, w_down)
    if mlstm_sample is None:
        return pl.pallas_call(
            _ffn_kernel,
            grid=(steps,),
            in_specs=in_specs,
            out_specs=xspec,
            out_shape=jax.ShapeDtypeStruct(x.shape, F32),
            compiler_params=_params(("arbitrary",)),
            name="ffn",
        )(*operands)
    ml_operands, ml_tt = mlstm_sample
    ml_steps, ml_in, ml_out, ml_shapes, ml_scratch = _mlstm_sample_specs(ml_operands[0].shape[0], ml_tt)
    assert ml_steps == steps, "one sample mLSTM step per FFN tile"
    return pl.pallas_call(
        functools.partial(_ffn_mlstm_kernel, n_ml_in=len(ml_in), tt=ml_tt),
        grid=(steps,),
        in_specs=in_specs + ml_in,
        out_specs=[xspec] + ml_out,
        out_shape=[jax.ShapeDtypeStruct(x.shape, F32)] + ml_shapes,
        scratch_shapes=ml_scratch,
        compiler_params=_params(("arbitrary",)),
        name="ffn_mlstm",
    )(*operands, *ml_operands)


def kernel(x_prompt, x_sample, c_prompt, c_sample, state_pool, state_mlstm_c, state_mlstm_n, state_mlstm_m, g_mix, g_ffn, g_final, w_ada, b_ada, w_in, b_igate, b_fgate, w_pool, s_pool, g_head, w_branch_pool, w_branch_mlstm, w_out, w_gate_up, w_down):
    depth = w_in.shape[0]
    assert depth == 1, "single-layer trunk"
    nbatch, seq, _ = x_prompt.shape
    nseq, tt, _ = x_sample.shape
    l = 0

    bgate = jnp.pad(jnp.concatenate([b_igate[l], b_fgate[l]])[None, :], ((0, 0), (0, GATE_LANES - 2 * ML_HEADS)))
    g_mix_r, g_ffn_r, g_fin_r = g_mix[l][None, :], g_ffn[l][None, :], g_final[None, :]
    s_pool_r, g_head_r = s_pool[l][None, :], g_head[l][None, :]

    w_in_t = jnp.swapaxes(w_in, 1, 2)
    w_head, w_gate, w_pool_b = _wprep(w_in_t, w_pool)
    mod = _ada(c_sample, c_prompt, w_ada[l], b_ada[l][None, :])
    row_s, row_p = 0, nseq

    (yp, q, kT, v, gc, gr, hist_p, w_bp, w_bm, w_o, w_gu, w_dn, w_tail) = _inproj_prompt(
        x_prompt, mod, row_p, g_mix_r, w_head, w_gate, bgate, w_pool_b, s_pool_r,
        (w_branch_pool, w_branch_mlstm, w_out, w_gate_up, w_down), w_in_t)
    hh, c_p, n_p, m_p = _mlstm_prompt(q.reshape(nbatch, seq, D_MODEL), kT, v.reshape(nbatch, seq, D_MODEL), gr)
    hh = hh.reshape(nbatch * seq, D_MODEL)
    x1 = _post(x_prompt, mod, row_p, g_mix_r, hh, yp, w_tail, g_head_r, w_bp, w_bm, w_o)

    hist_tm = jnp.swapaxes(state_pool[l], 0, 1)
    yp, q, kT, v, gc, gr, k, hist_s = _inproj_sample(x_sample, mod, row_s, g_mix_r, w_head, w_gate, bgate, w_pool_b,
                                                     s_pool_r, hist_tm)
    m0_tok = jnp.repeat(state_mlstm_m[l].astype(F32).T, tt, axis=1)[:, :, None]
    ml_operands = (q, kT, k, v, gc, gr, m0_tok, state_mlstm_c[l].astype(F32), state_mlstm_n[l].astype(F32))
    y_prompt, hh, c_s, n_s, mt = _ffn(x1, mod, row_p, g_ffn_r, g_fin_r, w_gu, w_dn, PROMPT_FFN_TILE,
                                      (ml_operands, tt))

    x1 = _post(x_sample, mod, row_s, g_mix_r, hh, yp, w_tail, g_head_r, w_bp, w_bm, w_o)
    y_sample = _ffn(x1, mod, row_s, g_ffn_r, g_fin_r, w_gu, w_dn, FFN_TILE)

    cd, nd, md = state_mlstm_c.dtype, state_mlstm_n.dtype, state_mlstm_m.dtype
    return (y_prompt, y_sample,
            hist_p[None, :, HIST_ROWS - POOL_HIST:, :],
            c_p.astype(cd)[None], n_p[..., 0].astype(nd)[None], m_p[:, :ML_HEADS, 0].astype(md)[None],
            jnp.swapaxes(hist_s, 0, 1)[None].astype(state_pool.dtype),
            c_s.astype(cd)[None], n_s.astype(nd)[None],
            mt[:, tt - 1::tt, 0].T.astype(md)[None])
```

```python
import functools

import jax
import jax.numpy as jnp
from jax import lax
from jax.experimental import pallas as pl
from jax.experimental.pallas import tpu as pltpu

D_MODEL = 1024
POOL_WINDOWS = (2, 4, 8, 16)
POOL_GROUP_DIM = 128
POOL_WIDTH = 512
POOL_HIST = 15
HIST_ROWS = 16
ML_HEADS = 4
ML_HEAD_DIM = 256
D_FF = 2816
EPS = 1e-6
M_INIT = -1e30
PAST_LEN = 16384
LANES = 128
SUBLANES = 8
BF16_ROWS = 16
GATE_LANES = LANES
AUG = ML_HEAD_DIM + LANES
VMEM_LIMIT = 56 * 1024 * 1024

TOKEN_TILE = 512
PROMPT_INPROJ_TILE = 1024
FFN_TILE = 1024
PROMPT_FFN_TILE = 512
PROMPT_CHUNK = 512
MLSTM_TILE = 1024
MLSTM_BATCHES = 1
SAMPLE_SEQ_BLOCK = 16

BF16 = jnp.bfloat16
F32 = jnp.float32

_U0, _Q0, _K0, _V0, _G0 = 0, 512, 1536, 2560, 3584


def _dot(a, b):
    return jnp.dot(a, b, preferred_element_type=F32)


def _const_spec(shape):
    zeros = (0,) * len(shape)
    return pl.BlockSpec(shape, lambda *_: zeros, pipeline_mode=pl.Buffered(1))


def _params(sem):
    return pltpu.CompilerParams(dimension_semantics=sem, vmem_limit_bytes=VMEM_LIMIT)


def _norm_mod(x_ref, sc_ref, sh_ref, g_ref):
    x = x_ref[...]
    nb, tt, d = x.shape
    ms = jnp.mean(x * x, axis=-1, keepdims=True)
    y = x * lax.rsqrt(ms + EPS) * g_ref[...]
    hmod = y * (1.0 + sc_ref[0]) + sh_ref[0]
    return x.reshape(nb * tt, d), hmod.reshape(nb * tt, d)


def _split3(x):
    hi = x.astype(BF16)
    r1 = x - hi.astype(F32)
    mid = r1.astype(BF16)
    lo = (r1 - mid.astype(F32)).astype(BF16)
    return hi, mid, lo


def _cumsum_cols(tri, x):
    n = x.shape[1]
    y = _dot(tri, jnp.concatenate(_split3(x), axis=1))
    return (y[:, 0:n] + y[:, n:2 * n]) + y[:, 2 * n:3 * n]


def _cumsum_rows(x, tri):
    n = x.shape[0]
    y = _dot(jnp.concatenate(_split3(x), axis=0), tri)
    return (y[0:n] + y[n:2 * n]) + y[2 * n:3 * n]


def _ada_kernel(cs_ref, cp_ref, w_ref, b_ref, o_ref):
    c = jnp.concatenate([cs_ref[...], cp_ref[...]], axis=0)
    a = (c * jax.nn.sigmoid(c)).astype(BF16)
    res = _dot(a, w_ref[...].astype(BF16)) + b_ref[...]
    o_ref[0] = res.reshape(res.shape[0], 1, res.shape[1])


def _ada(c_sample, c_prompt, w_ada, b_ada):
    rows = c_sample.shape[0] + c_prompt.shape[0]
    nchunk = w_ada.shape[1] // D_MODEL
    return pl.pallas_call(
        _ada_kernel,
        grid=(nchunk,),
        in_specs=[pl.BlockSpec(c_sample.shape, lambda j: (0, 0)),
                  pl.BlockSpec(c_prompt.shape, lambda j: (0, 0)),
                  pl.BlockSpec((D_MODEL, D_MODEL), lambda j: (0, j)),
                  pl.BlockSpec((1, D_MODEL), lambda j: (0, j))],
        out_specs=pl.BlockSpec((1, rows, 1, D_MODEL), lambda j: (j, 0, 0, 0)),
        out_shape=jax.ShapeDtypeStruct((nchunk, rows, 1, D_MODEL), F32),
        compiler_params=_params(("arbitrary",)),
        name="ada",
    )(c_sample, c_prompt, w_ada, b_ada)


_SH1, _SC1, _GT1, _SH2, _SC2, _GT2 = range(6)


def _mod_spec(k, nb, row0, block_index):
    return pl.BlockSpec((1, nb, 1, D_MODEL), lambda *g: (k, row0 // nb + block_index(*g), 0, 0))


_PREP_COLS = 512


def _wprep_kernel(wt_ref, gt_ref, wp_ref, head_ref, gate_ref, pool_ref):
    head_ref[...] = wt_ref[0].T.astype(BF16)
    gates = gt_ref[0].T
    pad = jnp.zeros((gates.shape[0], GATE_LANES - gates.shape[1]), F32)
    gate_ref[...] = jnp.concatenate([gates, pad], axis=1).astype(BF16)
    pool_ref[...] = wp_ref[0].astype(BF16)


def _wprep(w_in_t, w_pool):
    g0 = _G0 + D_MODEL
    return pl.pallas_call(
        _wprep_kernel,
        grid=(_G0 // _PREP_COLS,),
        in_specs=[pl.BlockSpec((1, _PREP_COLS, D_MODEL), lambda j: (0, j, 0)),
                  pl.BlockSpec((1, 2 * ML_HEADS, D_MODEL), lambda j: (0, g0 // (2 * ML_HEADS), 0)),
                  pl.BlockSpec(w_pool.shape, lambda j: (0, 0, 0, 0))],
        out_specs=[pl.BlockSpec((D_MODEL, _PREP_COLS), lambda j: (0, j)),
                   pl.BlockSpec((D_MODEL, GATE_LANES), lambda j: (0, 0)),
                   pl.BlockSpec(w_pool.shape[1:], lambda j: (0, 0, 0))],
        out_shape=[jax.ShapeDtypeStruct((D_MODEL, _G0), BF16),
                   jax.ShapeDtypeStruct((D_MODEL, GATE_LANES), BF16),
                   jax.ShapeDtypeStruct(w_pool.shape[1:], BF16)],
        compiler_params=_params(("arbitrary",)),
        name="wprep",
    )(w_in_t, w_in_t, w_pool)


_TAIL_COLS = 256
_TAIL_BLOCKS = 3 * D_MODEL // _TAIL_COLS
_TAIL_SHIFT_BLOCK = D_MODEL // _TAIL_COLS
_TAIL_SHIFT = 2 * ML_HEADS


def _tail_block(step):
    return jnp.minimum(step, _TAIL_BLOCKS - 1)


def _tail_prep(step, wa_ref, wb_ref, tail_ref):
    c = _tail_block(step)
    a = wa_ref[0]
    shifted = jnp.concatenate([a[_TAIL_SHIFT:, :], wb_ref[0]], axis=0)
    tail_ref[...] = jnp.where(c >= _TAIL_SHIFT_BLOCK, shifted, a).T.astype(BF16)


def _project(hb, w_ref, wg_ref, bg_ref, q_ref, kT_ref, v_ref, gc_ref, gr_ref):
    zg = _dot(hb, wg_ref[...]) + bg_ref[...]
    k = _dot(hb, w_ref[:, _K0:_V0]) * (ML_HEAD_DIM ** -0.5)
    log_f = jnp.minimum(zg, 0.0) - jnp.log1p(jnp.exp(-jnp.abs(zg)))
    lane = lax.broadcasted_iota(jnp.int32, zg.shape, 1)
    gc = jnp.where(lane < ML_HEADS, zg, log_f)
    gc_ref[...] = gc
    gr_ref[...] = gc.T[0:SUBLANES, :]
    kT_ref[...] = k.T.astype(BF16)
    q_ref[...] = _dot(hb, w_ref[:, _Q0:_K0]).astype(BF16)
    v_ref[...] = _dot(hb, w_ref[:, _V0:_G0]).astype(BF16)
    return k


def _pool_deltas(acc_fn, u, cnt_fn):
    deltas = []
    for g, w in enumerate(POOL_WINDOWS):
        ug = u[:, g * POOL_GROUP_DIM:(g + 1) * POOL_GROUP_DIM]
        deltas.append((acc_fn(g, w, ug) / cnt_fn(w) - ug).astype(BF16))
    return deltas


def _pool_project(deltas, wpool_ref, spool_ref):
    outs = [_dot(d, wpool_ref[g]) for g, d in enumerate(deltas)]
    return (jnp.concatenate(outs, axis=-1) * spool_ref[...]).astype(BF16)


def _inproj_prompt_kernel(x_ref, sc_ref, sh_ref, g_ref, w_ref, wg_ref, bg_ref, wpool_ref, spool_ref,
                          wbp_ref, wbm_ref, wout_ref, wgu_ref, wdn_ref, wa_ref, wb_ref,
                          yp_ref, q_ref, kT_ref, v_ref, gc_ref, gr_ref, hout_ref,
                          wbp_o, wbm_o, wout_o, wgu_o, wdn_o, tail_o, ext_ref, *, tm):
    t = pl.program_id(1)

    @pl.when(t == 0)
    def _():
        ext_ref[0:HIST_ROWS, :] = jnp.zeros((HIST_ROWS, POOL_WIDTH), F32)

    _, hmod = _norm_mod(x_ref, sc_ref, sh_ref, g_ref)
    hb = hmod.astype(BF16)
    u = _dot(hb, w_ref[:, _U0:_Q0])
    ext_ref[HIST_ROWS:HIST_ROWS + tm, :] = u
    pos = t * tm + lax.broadcasted_iota(jnp.int32, (tm, 1), 0)

    def acc_fn(g, w, ug):
        acc = ug
        for j in range(1, w):
            acc = acc + ext_ref[pl.ds(HIST_ROWS - j, tm), g * POOL_GROUP_DIM:(g + 1) * POOL_GROUP_DIM]
        return acc

    def cnt_fn(w):
        return jnp.minimum(pos + 1, w).astype(F32)

    deltas = _pool_deltas(acc_fn, u, cnt_fn)
    last = ext_ref[tm:tm + HIST_ROWS, :]
    hout_ref[0] = last
    ext_ref[0:HIST_ROWS, :] = last

    _project(hb, w_ref, wg_ref, bg_ref, q_ref, kT_ref.at[0], v_ref, gc_ref, gr_ref.at[0])
    yp_ref[...] = _pool_project(deltas, wpool_ref, spool_ref)

    _tail_prep(pl.program_id(0) * pl.num_programs(1) + t, wa_ref, wb_ref, tail_o)
    for src, dst in ((wbp_ref, wbp_o), (wbm_ref, wbm_o), (wout_ref, wout_o), (wgu_ref, wgu_o), (wdn_ref, wdn_o)):
        dst[...] = src[0].astype(BF16)

def _inproj_sample_kernel(x_ref, sc_ref, sh_ref, g_ref, w_ref, wg_ref, bg_ref, wpool_ref, spool_ref, hist_ref,
                          yp_ref, q_ref, kT_ref, v_ref, gc_ref, gr_ref, k_ref, hout_ref, u_ref, d_ref, *, pos0):
    _, hmod = _norm_mod(x_ref, sc_ref, sh_ref, g_ref)
    hb = hmod.astype(BF16)
    nb, tt = x_ref.shape[0], x_ref.shape[1]
    u = _dot(hb, w_ref[:, _U0:_Q0])
    k_ref[...] = _project(hb, w_ref, wg_ref, bg_ref, q_ref, kT_ref, v_ref, gc_ref, gr_ref).astype(BF16)
    for g, w in enumerate(POOL_WINDOWS):
        cols = slice(g * POOL_GROUP_DIM, (g + 1) * POOL_GROUP_DIM)
        u_ref[g] = u[:, cols]
        new = [u_ref[g, pl.ds(t, nb, stride=tt), :] for t in range(tt)]

        def ext(r):
            return hist_ref[r, :, cols] if r < POOL_HIST else new[r - POOL_HIST]

        for t in range(tt):
            acc = new[t]
            for j in range(1, w):
                acc = acc + ext(POOL_HIST + t - j)
            cnt = float(min(pos0 + t + 1, w))
            d_ref[g, pl.ds(t, nb, stride=tt), :] = acc / cnt - new[t]
        for r in range(POOL_HIST):
            hout_ref[r, :, cols] = ext(r + tt)
    yp_ref[...] = _pool_project([d_ref[g].astype(BF16) for g in range(len(POOL_WINDOWS))], wpool_ref, spool_ref)


def _cast_specs(weights, nt, steps):
    in_specs, out_specs, out_shapes = [], [], []
    for w in weights:
        _, r, c = w.shape
        n = steps
        while r % n or (r // n) % BF16_ROWS:
            n //= 2
        rows = r // n
        idx = lambda b, t, n=n: jnp.minimum(b * nt + t, n - 1)
        in_specs.append(pl.BlockSpec((1, rows, c), lambda b, t, idx=idx: (0, idx(b, t), 0)))
        out_specs.append(pl.BlockSpec((rows, c), lambda b, t, idx=idx: (idx(b, t), 0)))
        out_shapes.append(jax.ShapeDtypeStruct((r, c), BF16))
    return in_specs, out_specs, out_shapes


def _inproj_prompt(x, mod, row0, g_mix, w_head, w_gate, bgate, w_pool, s_pool, later_weights, w_in_t):
    nbatch, seq, _ = x.shape
    tm = PROMPT_INPROJ_TILE
    nt = seq // tm
    ntok = nbatch * seq
    row = lambda b, t: (b * nt + t, 0)
    sc, sh = (_mod_spec(k, 1, row0, lambda b, t: b) for k in (_SC1, _SH1))
    cast_in, cast_out, cast_shapes = _cast_specs(later_weights, nt, nbatch * nt)
    tail_row = lambda b, t: _G0 // _TAIL_COLS + _tail_block(b * nt + t)
    cast_in = cast_in + [
        pl.BlockSpec((1, _TAIL_COLS, D_MODEL), lambda b, t: (0, tail_row(b, t), 0)),
        pl.BlockSpec((1, _TAIL_SHIFT, D_MODEL), lambda b, t: (0, (tail_row(b, t) + 1) * (_TAIL_COLS // _TAIL_SHIFT), 0))]
    cast_out = cast_out + [pl.BlockSpec((D_MODEL, _TAIL_COLS), lambda b, t: (0, _tail_block(b * nt + t)))]
    cast_shapes = cast_shapes + [jax.ShapeDtypeStruct((D_MODEL, _TAIL_BLOCKS * _TAIL_COLS), BF16)]
    return pl.pallas_call(
        functools.partial(_inproj_prompt_kernel, tm=tm),
        grid=(nbatch, nt),
        in_specs=[pl.BlockSpec((1, tm, D_MODEL), lambda b, t: (b, t, 0)), sc, sh,
                  _const_spec((1, D_MODEL)), _const_spec((D_MODEL, _G0)), _const_spec((D_MODEL, GATE_LANES)),
                  _const_spec((1, GATE_LANES)), _const_spec((4, POOL_GROUP_DIM, POOL_GROUP_DIM)),
                  _const_spec((1, POOL_WIDTH))] + cast_in,
        out_specs=[pl.BlockSpec((tm, POOL_WIDTH), row), pl.BlockSpec((tm, D_MODEL), row),
                   pl.BlockSpec((1, D_MODEL, tm), lambda b, t: (b, 0, t)), pl.BlockSpec((tm, D_MODEL), row),
                   pl.BlockSpec((tm, GATE_LANES), row), pl.BlockSpec((1, SUBLANES, tm), lambda b, t: (b, 0, t)),
                   pl.BlockSpec((1, HIST_ROWS, POOL_WIDTH), lambda b, t: (b, 0, 0))] + cast_out,
        out_shape=[jax.ShapeDtypeStruct((ntok, POOL_WIDTH), BF16), jax.ShapeDtypeStruct((ntok, D_MODEL), BF16),
                   jax.ShapeDtypeStruct((nbatch, D_MODEL, seq), BF16), jax.ShapeDtypeStruct((ntok, D_MODEL), BF16),
                   jax.ShapeDtypeStruct((ntok, GATE_LANES), F32), jax.ShapeDtypeStruct((nbatch, SUBLANES, seq), F32),
                   jax.ShapeDtypeStruct((nbatch, HIST_ROWS, POOL_WIDTH), F32)] + cast_shapes,
        scratch_shapes=[pltpu.VMEM((tm + HIST_ROWS, POOL_WIDTH), F32)],
        compiler_params=_params(("arbitrary", "arbitrary")),
        name="inproj_prompt",
    )(x, mod, mod, g_mix, w_head, w_gate, bgate, w_pool, s_pool, *later_weights, w_in_t, w_in_t)


def _inproj_sample(x, mod, row0, g_mix, w_head, w_gate, bgate, w_pool, s_pool, hist_tm):
    nseq, tt, _ = x.shape
    nb = TOKEN_TILE // tt
    tm = nb * tt
    ntok = nseq * tt
    row = lambda i: (i, 0)
    col = lambda i: (0, i)
    sc, sh = (_mod_spec(k, nb, row0, lambda i: i) for k in (_SC1, _SH1))
    hist = pl.BlockSpec((POOL_HIST, nb, POOL_WIDTH), lambda i: (0, i, 0))
    return pl.pallas_call(
        functools.partial(_inproj_sample_kernel, pos0=PAST_LEN),
        grid=(nseq // nb,),
        in_specs=[pl.BlockSpec((nb, tt, D_MODEL), lambda i: (i, 0, 0)), sc, sh,
                  _const_spec((1, D_MODEL)), _const_spec((D_MODEL, _G0)), _const_spec((D_MODEL, GATE_LANES)),
                  _const_spec((1, GATE_LANES)), _const_spec((4, POOL_GROUP_DIM, POOL_GROUP_DIM)),
                  _const_spec((1, POOL_WIDTH)), hist],
        out_specs=[pl.BlockSpec((tm, POOL_WIDTH), row), pl.BlockSpec((tm, D_MODEL), row),
                   pl.BlockSpec((D_MODEL, tm), col), pl.BlockSpec((tm, D_MODEL), row),
                   pl.BlockSpec((tm, GATE_LANES), row), pl.BlockSpec((SUBLANES, tm), col),
                   pl.BlockSpec((tm, D_MODEL), row), hist],
        out_shape=[jax.ShapeDtypeStruct((ntok, POOL_WIDTH), BF16), jax.ShapeDtypeStruct((ntok, D_MODEL), BF16),
                   jax.ShapeDtypeStruct((D_MODEL, ntok), BF16), jax.ShapeDtypeStruct((ntok, D_MODEL), BF16),
                   jax.ShapeDtypeStruct((ntok, GATE_LANES), F32), jax.ShapeDtypeStruct((SUBLANES, ntok), F32),
                   jax.ShapeDtypeStruct((ntok, D_MODEL), BF16),
                   jax.ShapeDtypeStruct((POOL_HIST, nseq, POOL_WIDTH), F32)],
        scratch_shapes=[pltpu.VMEM((len(POOL_WINDOWS), tm, POOL_GROUP_DIM), F32)] * 2,
        compiler_params=_params(("arbitrary",)),
        name="inproj_sample",
    )(x, mod, mod, g_mix, w_head, w_gate, bgate, w_pool, s_pool, hist_tm)


def _mlstm_chunk(q_ref, kT_ref, v_ref, gr_ref, hh_ref, cn_ref, m_ref, r0, chunk):
    rows = slice(r0, r0 + chunk)
    gr = gr_ref[:, rows]
    ri = lax.broadcasted_iota(jnp.int32, (chunk, chunk), 0)
    ci = lax.broadcasted_iota(jnp.int32, (chunk, chunk), 1)
    causal = ri >= ci
    brow = _cumsum_rows(gr, (ri <= ci).astype(BF16))
    bcol = brow.T
    ones = jnp.ones((chunk, LANES), BF16)
    tile_to = lambda x, width: jnp.concatenate([x] * (width // LANES), axis=-1)
    stage = []
    for h in range(ML_HEADS):
        hs = slice(h * ML_HEAD_DIM, (h + 1) * ML_HEAD_DIM)
        b_r = brow[ML_HEADS + h:ML_HEADS + h + 1, :]
        g_r = gr[h:h + 1, :] - b_r
        m_prev = m_ref[h:h + 1, 0:1]
        cn = cn_ref[h]
        q = q_ref[rows, hs]
        kT = kT_ref[hs, rows]
        gm = jnp.where(causal, g_r, -jnp.inf)
        big_m = jnp.maximum(m_prev, jnp.max(gm, axis=-1, keepdims=True))
        m_rep = jnp.broadcast_to(big_m, (chunk, LANES))
        mt_rep = jnp.broadcast_to(bcol[:, ML_HEADS + h:ML_HEADS + h + 1] + big_m, (chunk, LANES))
        qkc = _dot(q, jnp.concatenate([kT, cn.astype(BF16)], axis=1))
        s = qkc[:, :chunk] * jnp.exp(gm - tile_to(m_rep, chunk))
        b_last = b_r[:, chunk - 1:chunk]
        m_new = b_last + jnp.maximum(m_prev, jnp.max(g_r, axis=-1, keepdims=True))
        decay = jnp.exp((b_last + m_prev) - m_new)
        w_end = jnp.exp((g_r + b_last) - m_new)
        lhs = jnp.concatenate([s.astype(BF16), (kT.astype(F32) * w_end).astype(BF16)], axis=0)
        stage.append((lhs, qkc[:, chunk:], m_prev, m_rep, mt_rep, m_new, decay, cn))

    for h, (lhs, qc, m_prev, m_rep, mt_rep, m_new, decay, cn) in enumerate(stage):
        hs = slice(h * ML_HEAD_DIM, (h + 1) * ML_HEAD_DIM)
        vaug = jnp.concatenate([v_ref[rows, hs], ones], axis=-1)
        both = _dot(lhs, vaug)
        sva = both[0:chunk]
        w_inter = jnp.exp(m_prev - m_rep)
        den = w_inter * qc[:, ML_HEAD_DIM:] + sva[:, ML_HEAD_DIM:]
        rinv = 1.0 / jnp.maximum(jnp.abs(den), jnp.exp(-mt_rep))
        hh_ref[rows, hs] = ((tile_to(w_inter, ML_HEAD_DIM) * qc[:, :ML_HEAD_DIM] + sva[:, :ML_HEAD_DIM])
                            * tile_to(rinv, ML_HEAD_DIM))
        cn_ref[h] = decay * cn + both[chunk:]
        m_ref[h:h + 1, :] = jnp.broadcast_to(m_new, (1, LANES))


def _mlstm_prompt_kernel(q_ref, kT_ref, v_ref, gr_ref, hh_ref, cout_ref, nout_ref, mout_ref, cn_ref, m_ref, *, chunk):
    t = pl.program_id(1)
    nb = q_ref.shape[0]

    @pl.when(t == 0)
    def _():
        cn_ref[...] = jnp.zeros(cn_ref.shape, F32)
        m_ref[...] = jnp.full(m_ref.shape, M_INIT, F32)

    for r0 in range(0, q_ref.shape[1], chunk):
        for s in range(nb):
            _mlstm_chunk(q_ref.at[s], kT_ref.at[s], v_ref.at[s], gr_ref.at[s], hh_ref.at[s], cn_ref.at[s],
                         m_ref.at[s], r0, chunk)

    @pl.when(t == pl.num_programs(1) - 1)
    def _():
        cout_ref[...] = cn_ref[:, :, :, 0:ML_HEAD_DIM]
        nout_ref[...] = cn_ref[:, :, :, ML_HEAD_DIM:AUG]
        mout_ref[...] = m_ref[...]


def _mlstm_prompt(q, kT, v, gr):
    nbatch, seq, _ = q.shape
    tm, nb = MLSTM_TILE, MLSTM_BATCHES
    row = pl.BlockSpec((nb, tm, D_MODEL), lambda b, t: (b, t, 0))
    state = lambda last: pl.BlockSpec((nb, ML_HEADS, ML_HEAD_DIM, last), lambda b, t: (b, 0, 0, 0))
    return pl.pallas_call(
        functools.partial(_mlstm_prompt_kernel, chunk=PROMPT_CHUNK),
        grid=(nbatch // nb, seq // tm),
        in_specs=[row, pl.BlockSpec((nb, D_MODEL, tm), lambda b, t: (b, 0, t)), row,
                  pl.BlockSpec((nb, SUBLANES, tm), lambda b, t: (b, 0, t))],
        out_specs=[row, state(ML_HEAD_DIM), state(LANES), pl.BlockSpec((nb, SUBLANES, LANES), lambda b, t: (b, 0, 0))],
        out_shape=[jax.ShapeDtypeStruct((nbatch, seq, D_MODEL), F32),
                   jax.ShapeDtypeStruct((nbatch, ML_HEADS, ML_HEAD_DIM, ML_HEAD_DIM), F32),
                   jax.ShapeDtypeStruct((nbatch, ML_HEADS, ML_HEAD_DIM, LANES), F32),
                   jax.ShapeDtypeStruct((nbatch, SUBLANES, LANES), F32)],
        scratch_shapes=[pltpu.VMEM((nb, ML_HEADS, ML_HEAD_DIM, AUG), F32), pltpu.VMEM((nb, SUBLANES, LANES), F32)],
        compiler_params=_params(("arbitrary", "arbitrary")),
        name="mlstm_prompt",
    )(q, kT, v, gr)


def _last_in_group(x, group):
    rows = x.shape[0]
    x3 = jnp.broadcast_to(x, (rows, LANES)).reshape(rows // group, group, LANES)
    last = jnp.broadcast_to(x3[:, group - 1:group, :], x3.shape)
    return last.reshape(rows, LANES)[:, 0:1]


def _mlstm_sample_body(h, q_ref, kT_ref, k_ref, v_ref, gc_ref, gr_ref, m0_ref, c_ref, n_ref,
                       hh_ref, cout_ref, nout_ref, mt_ref, ni_ref, qn_ref, dec_ref, wk_ref, *, tt):
    L = q_ref.shape[0]
    nseq = L // tt
    gc = gc_ref[...]
    gr = gr_ref[...]
    ri = lax.broadcasted_iota(jnp.int32, (L, L), 0)
    ci = lax.broadcasted_iota(jnp.int32, (L, L), 1)
    same = (ri // tt) == (ci // tt)
    mask = same & (ri >= ci)
    bcol = _cumsum_cols(mask.astype(BF16), gc)
    brow = _cumsum_rows(gr, (same & (ri <= ci)).astype(BF16))
    lane = lax.broadcasted_iota(jnp.int32, (L, GATE_LANES), 1)
    sub = lax.broadcasted_iota(jnp.int32, (SUBLANES, L), 0)
    pick_col = lambda arr, idx: jnp.sum(jnp.where(lane == idx, arr, 0.0), axis=-1, keepdims=True)
    pick_row = lambda arr, idx: jnp.sum(jnp.where(sub == idx, arr, 0.0), axis=0, keepdims=True)
    ig_c = pick_col(gc, h)
    b_c = pick_col(bcol, ML_HEADS + h)
    ig_r = pick_row(gr, h)
    b_r = pick_row(brow, ML_HEADS + h)
    m_prev = m0_ref[0]

    for j in range(nseq):
        rows = slice(j * tt, (j + 1) * tt)
        qj = q_ref[rows, :]
        ni_ref[rows, :] = _dot(qj, c_ref[j, 0].astype(BF16))
        nj = n_ref[j, pl.ds(h, 1), :].astype(BF16).astype(F32)
        qn = jnp.sum(qj.astype(F32) * nj, axis=-1, keepdims=True)
        qn_ref[rows, :] = jnp.broadcast_to(qn, (tt, LANES))

    q = q_ref[...]
    kT = kT_ref[...]
    v = v_ref[...]
    qk = _dot(q, kT)
    logd = jnp.where(mask, (b_c - b_r) + ig_r, -jnp.inf)
    a_c = b_c + m_prev
    m_t = jnp.maximum(a_c, jnp.max(logd, axis=-1, keepdims=True))
    w_inter = jnp.exp(a_c - m_t)
    s = qk * jnp.exp(logd - m_t)
    num = w_inter * ni_ref[...] + _dot(s.astype(BF16), v)
    den = w_inter * qn_ref[:, 0:1] + jnp.sum(s, axis=-1, keepdims=True)
    hh_ref[...] = num / jnp.maximum(jnp.abs(den), jnp.exp(-m_t))
    m_new = _last_in_group(m_t, tt)
    decay = jnp.exp(_last_in_group(a_c, tt) - m_new)
    w_end = jnp.exp((_last_in_group(b_c, tt) - b_c) + ig_c - m_new)
    mt_ref[0] = jnp.broadcast_to(m_t, (L, LANES))
    dec_ref[...] = jnp.broadcast_to(decay, (L, LANES))
    wv = w_end * v.astype(F32)
    wk_ref[...] = w_end.astype(BF16).astype(F32) * k_ref[...].astype(F32)
    rowi = lax.broadcasted_iota(jnp.int32, (L, 1), 0)

    for j in range(nseq):
        rows = slice(j * tt, (j + 1) * tt)
        upd = _dot(kT, jnp.where((rowi // tt) == j, wv, 0.0).astype(BF16))
        dj = dec_ref[j * tt:j * tt + 1, 0:1]
        cout_ref[j, 0] = dj * c_ref[j, 0] + upd
        nout_ref[j, pl.ds(h, 1), :] = (dj * n_ref[j, pl.ds(h, 1), :]
                                       + jnp.sum(wk_ref[rows, :], axis=0, keepdims=True))


def _mlstm_sample_specs(ntok, tt):
    nseq = ntok // tt
    sb = SAMPLE_SEQ_BLOCK
    L = sb * tt
    nh = ML_HEADS
    qspec = pl.BlockSpec((L, ML_HEAD_DIM), lambda i: (i // nh, i % nh))
    cspec = pl.BlockSpec((sb, 1, ML_HEAD_DIM, ML_HEAD_DIM), lambda i: (i // nh, i % nh, 0, 0))
    nspec = pl.BlockSpec((sb, nh, ML_HEAD_DIM), lambda i: (i // nh, 0, 0))
    in_specs = [qspec, pl.BlockSpec((ML_HEAD_DIM, L), lambda i: (i % nh, i // nh)), qspec, qspec,
                pl.BlockSpec((L, GATE_LANES), lambda i: (i // nh, 0)),
                pl.BlockSpec((SUBLANES, L), lambda i: (0, i // nh)),
                pl.BlockSpec((1, L, 1), lambda i: (i % nh, i // nh, 0)), cspec, nspec]
    out_specs = [qspec, cspec, nspec, pl.BlockSpec((1, L, LANES), lambda i: (i % nh, i // nh, 0))]
    out_shapes = [jax.ShapeDtypeStruct((ntok, D_MODEL), F32),
                  jax.ShapeDtypeStruct((nseq, ML_HEADS, ML_HEAD_DIM, ML_HEAD_DIM), F32),
                  jax.ShapeDtypeStruct((nseq, ML_HEADS, ML_HEAD_DIM), F32),
                  jax.ShapeDtypeStruct((ML_HEADS, ntok, LANES), F32)]
    scratch = [pltpu.VMEM((L, ML_HEAD_DIM), F32), pltpu.VMEM((L, LANES), F32),
               pltpu.VMEM((L, LANES), F32), pltpu.VMEM((L, ML_HEAD_DIM), F32)]
    return (nseq // sb) * nh, in_specs, out_specs, out_shapes, scratch


def _post_kernel(x_ref, sc_ref, sh_ref, gt_ref, g_ref, hh_ref, yp_ref, wt_ref, ghead_ref, wbp_ref, wbm_ref, wout_ref,
                 o_ref):
    pool = _dot(yp_ref[...], wbp_ref[...])
    x, hmod = _norm_mod(x_ref, sc_ref, sh_ref, g_ref)
    hb = hmod.astype(BF16)
    nb, tt, d = x_ref.shape
    o = _dot(hb, wt_ref[:, 0:D_MODEL])
    parts = []
    for h in range(ML_HEADS):
        hh = hh_ref[:, h * ML_HEAD_DIM:(h + 1) * ML_HEAD_DIM]
        parts.append(hh * lax.rsqrt(jnp.mean(hh * hh, axis=-1, keepdims=True) + EPS))
    yml = (jnp.concatenate(parts, axis=-1) * ghead_ref[...]) * jax.nn.sigmoid(o)
    gp = _dot(hb, wt_ref[:, D_MODEL:2 * D_MODEL])
    gm = _dot(hb, wt_ref[:, 2 * D_MODEL:3 * D_MODEL])
    merged = jax.nn.sigmoid(gp) * pool + jax.nn.sigmoid(gm) * _dot(yml.astype(BF16), wbm_ref[...])
    y = _dot(merged.astype(BF16), wout_ref[...]).reshape(nb, tt, d)
    o_ref[...] = x_ref[...] + gt_ref[0] * y


def _tile_blocks(x, tile):
    g, t, _ = x.shape
    if t >= tile:
        nb, tt = 1, tile
    else:
        nb, tt = tile // t, t
    return nb, tt, (g // nb) * (t // tt), t // tt


def _post(x, mod, row0, g_mix, hh, yp, w_tail, g_head, w_bp, w_bm, w_out):
    nb, tt, steps, per = _tile_blocks(x, TOKEN_TILE)
    tm = nb * tt
    xspec = pl.BlockSpec((nb, tt, D_MODEL), lambda i: (i // per, i % per, 0))
    sc, sh, gt = (_mod_spec(k, nb, row0, lambda i: i // per) for k in (_SC1, _SH1, _GT1))
    row = lambda i: (i, 0)
    return pl.pallas_call(
        _post_kernel,
        grid=(steps,),
        in_specs=[xspec, sc, sh, gt, _const_spec((1, D_MODEL)),
                  pl.BlockSpec((tm, D_MODEL), row), pl.BlockSpec((tm, POOL_WIDTH), row),
                  _const_spec((D_MODEL, 3 * D_MODEL)), _const_spec((1, D_MODEL)),
                  _const_spec((POOL_WIDTH, D_MODEL)), _const_spec((D_MODEL, D_MODEL)),
                  _const_spec((D_MODEL, D_MODEL))],
        out_specs=xspec,
        out_shape=jax.ShapeDtypeStruct(x.shape, F32),
        compiler_params=_params(("arbitrary",)),
        name="post",
    )(x, mod, mod, mod, g_mix, hh, yp, w_tail, g_head, w_bp, w_bm, w_out)


_FF_SPLITS = ((0, 768), (768, 1536), (1536, 2304), (2304, D_FF))


def _ffn_kernel(x_ref, sc_ref, sh_ref, gt_ref, g_ref, gfin_ref, wgu_ref, wdn_ref, o_ref):
    _, hmod = _norm_mod(x_ref, sc_ref, sh_ref, g_ref)
    hb = hmod.astype(BF16)
    nb, tt, d = x_ref.shape
    dn = None
    for lo, hi in _FF_SPLITS:
        gate = _dot(hb, wgu_ref[:, lo:hi])
        up = _dot(hb, wgu_ref[:, D_FF + lo:D_FF + hi])
        act = (gate * jax.nn.sigmoid(gate) * up).astype(BF16)
        part = _dot(act, wdn_ref[lo:hi, :])
        dn = part if dn is None else dn + part
    x2 = x_ref[...] + gt_ref[0] * dn.reshape(nb, tt, d)
    ms = jnp.mean(x2 * x2, axis=-1, keepdims=True)
    o_ref[...] = x2 * lax.rsqrt(ms + EPS) * gfin_ref[...]


_N_FFN_IN = 8


def _ffn_mlstm_kernel(*refs, n_ml_in, tt):
    ffn_in = refs[:_N_FFN_IN]
    ml_in = refs[_N_FFN_IN:_N_FFN_IN + n_ml_in]
    o_ref = refs[_N_FFN_IN + n_ml_in]
    ml_rest = refs[_N_FFN_IN + n_ml_in + 1:]
    _mlstm_sample_body(pl.program_id(0) % ML_HEADS, *ml_in, *ml_rest, tt=tt)
    _ffn_kernel(*ffn_in, o_ref)


def _ffn(x, mod, row0, g_ffn, g_final, w_gu, w_down, tile, mlstm_sample=None):
    nb, tt, steps, per = _tile_blocks(x, tile)
    xspec = pl.BlockSpec((nb, tt, D_MODEL), lambda i: (i // per, i % per, 0))
    sc, sh, gt = (_mod_spec(k, nb, row0, lambda i: i // per) for k in (_SC2, _SH2, _GT2))
    in_specs = [xspec, sc, sh, gt, _const_spec((1, D_MODEL)), _const_spec((1, D_MODEL)),
                _const_spec((D_MODEL, 2 * D_FF)), _const_spec((D_FF, D_MODEL))]
    operands = (x, mod, mod, mod, g_ffn, g_final, w_gu, w_down)
    if mlstm_sample is None:
        return pl.pallas_call(
            _ffn_kernel,
            grid=(steps,),
            in_specs=in_specs,
            out_specs=xspec,
            out_shape=jax.ShapeDtypeStruct(x.shape, F32),
            compiler_params=_params(("arbitrary",)),
            name="ffn",
        )(*operands)
    ml_operands, ml_tt = mlstm_sample
    ml_steps, ml_in, ml_out, ml_shapes, ml_scratch = _mlstm_sample_specs(ml_operands[0].shape[0], ml_tt)
    assert ml_steps == steps, "one sample mLSTM step per FFN tile"
    return pl.pallas_call(
        functools.partial(_ffn_mlstm_kernel, n_ml_in=len(ml_in), tt=ml_tt),
        grid=(steps,),
        in_specs=in_specs + ml_in,
        out_specs=[xspec] + ml_out,
        out_shape=[jax.ShapeDtypeStruct(x.shape, F32)] + ml_shapes,
        scratch_shapes=ml_scratch,
        compiler_params=_params(("arbitrary",)),
        name="ffn_mlstm",
    )(*operands, *ml_operands)


def kernel(x_prompt, x_sample, c_prompt, c_sample, state_pool, state_mlstm_c, state_mlstm_n, state_mlstm_m, g_mix, g_ffn, g_final, w_ada, b_ada, w_in, b_igate, b_fgate, w_pool, s_pool, g_head, w_branch_pool, w_branch_mlstm, w_out, w_gate_up, w_down):
    depth = w_in.shape[0]
    assert depth == 1, "single-layer trunk"
    nbatch, seq, _ = x_prompt.shape
    nseq, tt, _ = x_sample.shape
    l = 0

    bgate = jnp.pad(jnp.concatenate([b_igate[l], b_fgate[l]])[None, :], ((0, 0), (0, GATE_LANES - 2 * ML_HEADS)))
    g_mix_r, g_ffn_r, g_fin_r = g_mix[l][None, :], g_ffn[l][None, :], g_final[None, :]
    s_pool_r, g_head_r = s_pool[l][None, :], g_head[l][None, :]

    w_in_t = jnp.swapaxes(w_in, 1, 2)
    w_head, w_gate, w_pool_b = _wprep(w_in_t, w_pool)
    mod = _ada(c_sample, c_prompt, w_ada[l], b_ada[l][None, :])
    row_s, row_p = 0, nseq

    (yp, q, kT, v, gc, gr, hist_p, w_bp, w_bm, w_o, w_gu, w_dn, w_tail) = _inproj_prompt(
        x_prompt, mod, row_p, g_mix_r, w_head, w_gate, bgate, w_pool_b, s_pool_r,
        (w_branch_pool, w_branch_mlstm, w_out, w_gate_up, w_down), w_in_t)
    hh, c_p, n_p, m_p = _mlstm_prompt(q.reshape(nbatch, seq, D_MODEL), kT, v.reshape(nbatch, seq, D_MODEL), gr)
    hh = hh.reshape(nbatch * seq, D_MODEL)
    x1 = _post(x_prompt, mod, row_p, g_mix_r, hh, yp, w_tail, g_head_r, w_bp, w_bm, w_o)

    hist_tm = jnp.swapaxes(state_pool[l], 0, 1)
    yp, q, kT, v, gc, gr, k, hist_s = _inproj_sample(x_sample, mod, row_s, g_mix_r, w_head, w_gate, bgate, w_pool_b,
                                                     s_pool_r, hist_tm)
    m0_tok = jnp.repeat(state_mlstm_m[l].astype(F32).T, tt, axis=1)[:, :, None]
    ml_operands = (q, kT, k, v, gc, gr, m0_tok, state_mlstm_c[l].astype(F32), state_mlstm_n[l].astype(F32))
    y_prompt, hh, c_s, n_s, mt = _ffn(x1, mod, row_p, g_ffn_r, g_fin_r, w_gu, w_dn, PROMPT_FFN_TILE,
                                      (ml_operands, tt))

    x1 = _post(x_sample, mod, row_s, g_mix_r, hh, yp, w_tail, g_head_r, w_bp, w_bm, w_o)
    y_sample = _ffn(x1, mod, row_s, g_ffn_r, g_fin_r, w_gu, w_dn, FFN_TILE)

    cd, nd, md = state_mlstm_c.dtype, state_mlstm_n.dtype, state_mlstm_m.dtype
    return (y_prompt, y_sample,
            hist_p[None, :, HIST_ROWS - POOL_HIST:, :],
            c_p.astype(cd)[None], n_p[..., 0].astype(nd)[None], m_p[:, :ML_HEADS, 0].astype(md)[None],
            jnp.swapaxes(hist_s, 0, 1)[None].astype(state_pool.dtype),
            c_s.astype(cd)[None], n_s.astype(nd)[None],
            mt[:, tt - 1::tt, 0].T.astype(md)[None])
```

```python
import functools

import jax
import jax.numpy as jnp
from jax import lax
from jax.experimental import pallas as pl
from jax.experimental.pallas import tpu as pltpu

D_MODEL = 1024
POOL_WINDOWS = (2, 4, 8, 16)
POOL_GROUP_DIM = 128
POOL_WIDTH = 512
POOL_HIST = 15
HIST_ROWS = 16
ML_HEADS = 4
ML_HEAD_DIM = 256
D_FF = 2816
EPS = 1e-6
M_INIT = -1e30
PAST_LEN = 16384
LANES = 128
SUBLANES = 8
BF16_ROWS = 16
GATE_LANES = LANES
AUG = ML_HEAD_DIM + LANES
VMEM_LIMIT = 56 * 1024 * 1024

TOKEN_TILE = 512
PROMPT_INPROJ_TILE = 1024
FFN_TILE = 1024
PROMPT_FFN_TILE = 512
PROMPT_CHUNK = 512
MLSTM_TILE = 1024
MLSTM_BATCHES = 1
SAMPLE_SEQ_BLOCK = 16

BF16 = jnp.bfloat16
F32 = jnp.float32

_U0, _Q0, _K0, _V0, _G0 = 0, 512, 1536, 2560, 3584


def _dot(a, b):
    return jnp.dot(a, b, preferred_element_type=F32)


def _const_spec(shape):
    zeros = (0,) * len(shape)
    return pl.BlockSpec(shape, lambda *_: zeros, pipeline_mode=pl.Buffered(1))


def _params(sem):
    return pltpu.CompilerParams(dimension_semantics=sem, vmem_limit_bytes=VMEM_LIMIT)


def _norm_mod(x_ref, sc_ref, sh_ref, g_ref):
    x = x_ref[...]
    nb, tt, d = x.shape
    ms = jnp.mean(x * x, axis=-1, keepdims=True)
    y = x * lax.rsqrt(ms + EPS) * g_ref[...]
    hmod = y * (1.0 + sc_ref[0]) + sh_ref[0]
    return x.reshape(nb * tt, d), hmod.reshape(nb * tt, d)


def _split3(x):
    hi = x.astype(BF16)
    r1 = x - hi.astype(F32)
    mid = r1.astype(BF16)
    lo = (r1 - mid.astype(F32)).astype(BF16)
    return hi, mid, lo


def _cumsum_cols(tri, x):
    n = x.shape[1]
    y = _dot(tri, jnp.concatenate(_split3(x), axis=1))
    return (y[:, 0:n] + y[:, n:2 * n]) + y[:, 2 * n:3 * n]


def _cumsum_rows(x, tri):
    n = x.shape[0]
    y = _dot(jnp.concatenate(_split3(x), axis=0), tri)
    return (y[0:n] + y[n:2 * n]) + y[2 * n:3 * n]


_ADA_COLS = 256
_SH1, _SC1 = 0, 1
_GT1, _SH2, _SC2, _GT2 = 0, 1, 2, 3
_N_MOD_A = 2


def _ada_block(cs_ref, cp_ref, w_ref, b_ref, o_ref):
    c = jnp.concatenate([cs_ref[...], cp_ref[...]], axis=0)
    a = (c * jax.nn.sigmoid(c)).astype(BF16)
    res = _dot(a, w_ref[...].astype(BF16)) + b_ref[...]
    o_ref[0] = res.reshape(res.shape[0], 1, res.shape[1])


def _ada_specs(c_sample, c_prompt, first_block, step_of):
    rows = c_sample.shape[0] + c_prompt.shape[0]
    per_chunk = D_MODEL // _ADA_COLS
    const2 = lambda *g: (0, 0)
    in_specs = [pl.BlockSpec(c_sample.shape, const2), pl.BlockSpec(c_prompt.shape, const2),
                pl.BlockSpec((D_MODEL, _ADA_COLS), lambda *g: (0, first_block + step_of(*g))),
                pl.BlockSpec((1, _ADA_COLS), lambda *g: (0, first_block + step_of(*g)))]
    out_spec = pl.BlockSpec((1, rows, 1, _ADA_COLS), lambda *g: (step_of(*g) // per_chunk, 0, 0, step_of(*g) % per_chunk))
    return in_specs, out_spec, rows


def _mod_spec(k, nb, row0, block_index):
    return pl.BlockSpec((1, nb, 1, D_MODEL), lambda *g: (k, row0 // nb + block_index(*g), 0, 0))


_PREP_COLS = 512


def _prep_kernel(wt_ref, gt_ref, wp_ref, cs_ref, cp_ref, wa_ref, ba_ref, head_ref, gate_ref, pool_ref, mod_ref, *,
                 head_steps):
    @pl.when(pl.program_id(0) < head_steps)
    def _():
        head_ref[...] = wt_ref[0].T.astype(BF16)

    gates = gt_ref[0].T
    pad = jnp.zeros((gates.shape[0], GATE_LANES - gates.shape[1]), F32)
    gate_ref[...] = jnp.concatenate([gates, pad], axis=1).astype(BF16)
    pool_ref[...] = wp_ref[0].astype(BF16)
    _ada_block(cs_ref, cp_ref, wa_ref, ba_ref, mod_ref)


def _prep(w_in_t, w_pool, c_sample, c_prompt, w_ada, b_ada):
    g0 = _G0 + D_MODEL
    head_steps = _G0 // _PREP_COLS
    steps = _N_MOD_A * D_MODEL // _ADA_COLS
    assert steps >= head_steps
    head_blk = lambda j: jnp.minimum(j, head_steps - 1)
    ada_in, ada_out, rows = _ada_specs(c_sample, c_prompt, 0, lambda j: j)
    return pl.pallas_call(
        functools.partial(_prep_kernel, head_steps=head_steps),
        grid=(steps,),
        in_specs=[pl.BlockSpec((1, _PREP_COLS, D_MODEL), lambda j: (0, head_blk(j), 0)),
                  pl.BlockSpec((1, 2 * ML_HEADS, D_MODEL), lambda j: (0, g0 // (2 * ML_HEADS), 0)),
                  pl.BlockSpec(w_pool.shape, lambda j: (0, 0, 0, 0))] + ada_in,
        out_specs=[pl.BlockSpec((D_MODEL, _PREP_COLS), lambda j: (0, head_blk(j))),
                   pl.BlockSpec((D_MODEL, GATE_LANES), lambda j: (0, 0)),
                   pl.BlockSpec(w_pool.shape[1:], lambda j: (0, 0, 0)), ada_out],
        out_shape=[jax.ShapeDtypeStruct((D_MODEL, _G0), BF16),
                   jax.ShapeDtypeStruct((D_MODEL, GATE_LANES), BF16),
                   jax.ShapeDtypeStruct(w_pool.shape[1:], BF16),
                   jax.ShapeDtypeStruct((_N_MOD_A, rows, 1, D_MODEL), F32)],
        compiler_params=_params(("arbitrary",)),
        name="prep",
    )(w_in_t, w_in_t, w_pool, c_sample, c_prompt, w_ada, b_ada)


_TAIL_COLS = 256
_TAIL_BLOCKS = 3 * D_MODEL // _TAIL_COLS
_TAIL_SHIFT_BLOCK = D_MODEL // _TAIL_COLS
_TAIL_SHIFT = 2 * ML_HEADS


def _tail_block(step):
    return jnp.minimum(step, _TAIL_BLOCKS - 1)


def _tail_prep(step, wa_ref, wb_ref, tail_ref):
    c = _tail_block(step)
    a = wa_ref[0]
    shifted = jnp.concatenate([a[_TAIL_SHIFT:, :], wb_ref[0]], axis=0)
    tail_ref[...] = jnp.where(c >= _TAIL_SHIFT_BLOCK, shifted, a).T.astype(BF16)


def _project(hb, w_ref, wg_ref, bg_ref, q_ref, kT_ref, v_ref, gc_ref, gr_ref):
    zg = _dot(hb, wg_ref[...]) + bg_ref[...]
    k = _dot(hb, w_ref[:, _K0:_V0]) * (ML_HEAD_DIM ** -0.5)
    log_f = jnp.minimum(zg, 0.0) - jnp.log1p(jnp.exp(-jnp.abs(zg)))
    lane = lax.broadcasted_iota(jnp.int32, zg.shape, 1)
    gc = jnp.where(lane < ML_HEADS, zg, log_f)
    gc_ref[...] = gc
    gr_ref[...] = gc.T[0:SUBLANES, :]
    kT_ref[...] = k.T.astype(BF16)
    q_ref[...] = _dot(hb, w_ref[:, _Q0:_K0]).astype(BF16)
    v_ref[...] = _dot(hb, w_ref[:, _V0:_G0]).astype(BF16)
    return k


def _pool_deltas(acc_fn, u, cnt_fn):
    deltas = []
    for g, w in enumerate(POOL_WINDOWS):
        ug = u[:, g * POOL_GROUP_DIM:(g + 1) * POOL_GROUP_DIM]
        deltas.append((acc_fn(g, w, ug) / cnt_fn(w) - ug).astype(BF16))
    return deltas


def _pool_project(deltas, wpool_ref, spool_ref):
    outs = [_dot(d, wpool_ref[g]) for g, d in enumerate(deltas)]
    return (jnp.concatenate(outs, axis=-1) * spool_ref[...]).astype(BF16)


def _inproj_prompt_kernel(x_ref, sc_ref, sh_ref, g_ref, w_ref, wg_ref, bg_ref, wpool_ref, spool_ref,
                          wbp_ref, wbm_ref, wout_ref, wgu_ref, wdn_ref, wa_ref, wb_ref, cs_ref, cp_ref, wada_ref, bada_ref,
                          yp_ref, q_ref, kT_ref, v_ref, gc_ref, gr_ref, hout_ref,
                          wbp_o, wbm_o, wout_o, wgu_o, wdn_o, tail_o, mod_o, ext_ref, *, tm):
    t = pl.program_id(1)

    @pl.when(t == 0)
    def _():
        ext_ref[0:HIST_ROWS, :] = jnp.zeros((HIST_ROWS, POOL_WIDTH), F32)

    _, hmod = _norm_mod(x_ref, sc_ref, sh_ref, g_ref)
    hb = hmod.astype(BF16)
    u = _dot(hb, w_ref[:, _U0:_Q0])
    ext_ref[HIST_ROWS:HIST_ROWS + tm, :] = u
    pos = t * tm + lax.broadcasted_iota(jnp.int32, (tm, 1), 0)

    def acc_fn(g, w, ug):
        acc = ug
        for j in range(1, w):
            acc = acc + ext_ref[pl.ds(HIST_ROWS - j, tm), g * POOL_GROUP_DIM:(g + 1) * POOL_GROUP_DIM]
        return acc

    def cnt_fn(w):
        return jnp.minimum(pos + 1, w).astype(F32)

    deltas = _pool_deltas(acc_fn, u, cnt_fn)
    last = ext_ref[tm:tm + HIST_ROWS, :]
    hout_ref[0] = last
    ext_ref[0:HIST_ROWS, :] = last

    _project(hb, w_ref, wg_ref, bg_ref, q_ref, kT_ref.at[0], v_ref, gc_ref, gr_ref.at[0])
    yp_ref[...] = _pool_project(deltas, wpool_ref, spool_ref)

    _tail_prep(pl.program_id(0) * pl.num_programs(1) + t, wa_ref, wb_ref, tail_o)
    for src, dst in ((wbp_ref, wbp_o), (wbm_ref, wbm_o), (wout_ref, wout_o), (wgu_ref, wgu_o), (wdn_ref, wdn_o)):
        dst[...] = src[0].astype(BF16)
    _ada_block(cs_ref, cp_ref, wada_ref, bada_ref, mod_o)

def _inproj_sample_kernel(x_ref, sc_ref, sh_ref, g_ref, w_ref, wg_ref, bg_ref, wpool_ref, spool_ref, hist_ref,
                          yp_ref, q_ref, kT_ref, v_ref, gc_ref, gr_ref, k_ref, hout_ref, u_ref, d_ref, *, pos0):
    _, hmod = _norm_mod(x_ref, sc_ref, sh_ref, g_ref)
    hb = hmod.astype(BF16)
    nb, tt = x_ref.shape[0], x_ref.shape[1]
    u = _dot(hb, w_ref[:, _U0:_Q0])
    k_ref[...] = _project(hb, w_ref, wg_ref, bg_ref, q_ref, kT_ref, v_ref, gc_ref, gr_ref).astype(BF16)
    for g, w in enumerate(POOL_WINDOWS):
        cols = slice(g * POOL_GROUP_DIM, (g + 1) * POOL_GROUP_DIM)
        u_ref[g] = u[:, cols]
        new = [u_ref[g, pl.ds(t, nb, stride=tt), :] for t in range(tt)]

        def ext(r):
            return hist_ref[r, :, cols] if r < POOL_HIST else new[r - POOL_HIST]

        for t in range(tt):
            acc = new[t]
            for j in range(1, w):
                acc = acc + ext(POOL_HIST + t - j)
            cnt = float(min(pos0 + t + 1, w))
            d_ref[g, pl.ds(t, nb, stride=tt), :] = acc / cnt - new[t]
        for r in range(POOL_HIST):
            hout_ref[r, :, cols] = ext(r + tt)
    yp_ref[...] = _pool_project([d_ref[g].astype(BF16) for g in range(len(POOL_WINDOWS))], wpool_ref, spool_ref)


def _cast_specs(weights, nt, steps):
    in_specs, out_specs, out_shapes = [], [], []
    for w in weights:
        _, r, c = w.shape
        n = steps
        while r % n or (r // n) % BF16_ROWS:
            n //= 2
        rows = r // n
        idx = lambda b, t, n=n: jnp.minimum(b * nt + t, n - 1)
        in_specs.append(pl.BlockSpec((1, rows, c), lambda b, t, idx=idx: (0, idx(b, t), 0)))
        out_specs.append(pl.BlockSpec((rows, c), lambda b, t, idx=idx: (idx(b, t), 0)))
        out_shapes.append(jax.ShapeDtypeStruct((r, c), BF16))
    return in_specs, out_specs, out_shapes


def _inproj_prompt(x, mod, row0, g_mix, w_head, w_gate, bgate, w_pool, s_pool, later_weights, w_in_t, ada_operands):
    nbatch, seq, _ = x.shape
    tm = PROMPT_INPROJ_TILE
    nt = seq // tm
    ntok = nbatch * seq
    row = lambda b, t: (b * nt + t, 0)
    sc, sh = (_mod_spec(k, 1, row0, lambda b, t: b) for k in (_SC1, _SH1))
    cast_in, cast_out, cast_shapes = _cast_specs(later_weights, nt, nbatch * nt)
    tail_row = lambda b, t: _G0 // _TAIL_COLS + _tail_block(b * nt + t)
    cast_in = cast_in + [
        pl.BlockSpec((1, _TAIL_COLS, D_MODEL), lambda b, t: (0, tail_row(b, t), 0)),
        pl.BlockSpec((1, _TAIL_SHIFT, D_MODEL), lambda b, t: (0, (tail_row(b, t) + 1) * (_TAIL_COLS // _TAIL_SHIFT), 0))]
    cast_out = cast_out + [pl.BlockSpec((D_MODEL, _TAIL_COLS), lambda b, t: (0, _tail_block(b * nt + t)))]
    cast_shapes = cast_shapes + [jax.ShapeDtypeStruct((D_MODEL, _TAIL_BLOCKS * _TAIL_COLS), BF16)]
    c_sample, c_prompt, w_ada, _ = ada_operands
    n_mod_b = w_ada.shape[1] // D_MODEL - _N_MOD_A
    assert nbatch * nt == n_mod_b * D_MODEL // _ADA_COLS, "one modulation block per grid step"
    ada_in, ada_out, mod_rows = _ada_specs(c_sample, c_prompt, _N_MOD_A * D_MODEL // _ADA_COLS, lambda b, t: b * nt + t)
    cast_in, cast_out = cast_in + ada_in, cast_out + [ada_out]
    cast_shapes = cast_shapes + [jax.ShapeDtypeStruct((n_mod_b, mod_rows, 1, D_MODEL), F32)]
    return pl.pallas_call(
        functools.partial(_inproj_prompt_kernel, tm=tm),
        grid=(nbatch, nt),
        in_specs=[pl.BlockSpec((1, tm, D_MODEL), lambda b, t: (b, t, 0)), sc, sh,
                  _const_spec((1, D_MODEL)), _const_spec((D_MODEL, _G0)), _const_spec((D_MODEL, GATE_LANES)),
                  _const_spec((1, GATE_LANES)), _const_spec((4, POOL_GROUP_DIM, POOL_GROUP_DIM)),
                  _const_spec((1, POOL_WIDTH))] + cast_in,
        out_specs=[pl.BlockSpec((tm, POOL_WIDTH), row), pl.BlockSpec((tm, D_MODEL), row),
                   pl.BlockSpec((1, D_MODEL, tm), lambda b, t: (b, 0, t)), pl.BlockSpec((tm, D_MODEL), row),
                   pl.BlockSpec((tm, GATE_LANES), row), pl.BlockSpec((1, SUBLANES, tm), lambda b, t: (b, 0, t)),
                   pl.BlockSpec((1, HIST_ROWS, POOL_WIDTH), lambda b, t: (b, 0, 0))] + cast_out,
        out_shape=[jax.ShapeDtypeStruct((ntok, POOL_WIDTH), BF16), jax.ShapeDtypeStruct((ntok, D_MODEL), BF16),
                   jax.ShapeDtypeStruct((nbatch, D_MODEL, seq), BF16), jax.ShapeDtypeStruct((ntok, D_MODEL), BF16),
                   jax.ShapeDtypeStruct((ntok, GATE_LANES), F32), jax.ShapeDtypeStruct((nbatch, SUBLANES, seq), F32),
                   jax.ShapeDtypeStruct((nbatch, HIST_ROWS, POOL_WIDTH), F32)] + cast_shapes,
        scratch_shapes=[pltpu.VMEM((tm + HIST_ROWS, POOL_WIDTH), F32)],
        compiler_params=_params(("arbitrary", "arbitrary")),
        name="inproj_prompt",
    )(x, mod, mod, g_mix, w_head, w_gate, bgate, w_pool, s_pool, *later_weights, w_in_t, w_in_t, *ada_operands)


def _inproj_sample(x, mod, row0, g_mix, w_head, w_gate, bgate, w_pool, s_pool, hist_tm):
    nseq, tt, _ = x.shape
    nb = PROMPT_INPROJ_TILE // tt
    tm = nb * tt
    ntok = nseq * tt
    row = lambda i: (i, 0)
    col = lambda i: (0, i)
    sc, sh = (_mod_spec(k, nb, row0, lambda i: i) for k in (_SC1, _SH1))
    hist = pl.BlockSpec((POOL_HIST, nb, POOL_WIDTH), lambda i: (0, i, 0))
    return pl.pallas_call(
        functools.partial(_inproj_sample_kernel, pos0=PAST_LEN),
        grid=(nseq // nb,),
        in_specs=[pl.BlockSpec((nb, tt, D_MODEL), lambda i: (i, 0, 0)), sc, sh,
                  _const_spec((1, D_MODEL)), _const_spec((D_MODEL, _G0)), _const_spec((D_MODEL, GATE_LANES)),
                  _const_spec((1, GATE_LANES)), _const_spec((4, POOL_GROUP_DIM, POOL_GROUP_DIM)),
                  _const_spec((1, POOL_WIDTH)), hist],
        out_specs=[pl.BlockSpec((tm, POOL_WIDTH), row), pl.BlockSpec((tm, D_MODEL), row),
                   pl.BlockSpec((D_MODEL, tm), col), pl.BlockSpec((tm, D_MODEL), row),
                   pl.BlockSpec((tm, GATE_LANES), row), pl.BlockSpec((SUBLANES, tm), col),
                   pl.BlockSpec((tm, D_MODEL), row), hist],
        out_shape=[jax.ShapeDtypeStruct((ntok, POOL_WIDTH), BF16), jax.ShapeDtypeStruct((ntok, D_MODEL), BF16),
                   jax.ShapeDtypeStruct((D_MODEL, ntok), BF16), jax.ShapeDtypeStruct((ntok, D_MODEL), BF16),
                   jax.ShapeDtypeStruct((ntok, GATE_LANES), F32), jax.ShapeDtypeStruct((SUBLANES, ntok), F32),
                   jax.ShapeDtypeStruct((ntok, D_MODEL), BF16),
                   jax.ShapeDtypeStruct((POOL_HIST, nseq, POOL_WIDTH), F32)],
        scratch_shapes=[pltpu.VMEM((len(POOL_WINDOWS), tm, POOL_GROUP_DIM), F32)] * 2,
        compiler_params=_params(("arbitrary",)),
        name="inproj_sample",
    )(x, mod, mod, g_mix, w_head, w_gate, bgate, w_pool, s_pool, hist_tm)


def _mlstm_chunk(q_ref, kT_ref, v_ref, gr_ref, hh_ref, cn_ref, m_ref, r0, chunk):
    rows = slice(r0, r0 + chunk)
    gr = gr_ref[:, rows]
    ri = lax.broadcasted_iota(jnp.int32, (chunk, chunk), 0)
    ci = lax.broadcasted_iota(jnp.int32, (chunk, chunk), 1)
    causal = ri >= ci
    brow = _cumsum_rows(gr, (ri <= ci).astype(BF16))
    bcol = brow.T
    ones = jnp.ones((chunk, LANES), BF16)
    tile_to = lambda x, width: jnp.concatenate([x] * (width // LANES), axis=-1)
    stage = []
    for h in range(ML_HEADS):
        hs = slice(h * ML_HEAD_DIM, (h + 1) * ML_HEAD_DIM)
        b_r = brow[ML_HEADS + h:ML_HEADS + h + 1, :]
        g_r = gr[h:h + 1, :] - b_r
        m_prev = m_ref[h:h + 1, 0:1]
        cn = cn_ref[h]
        q = q_ref[rows, hs]
        kT = kT_ref[hs, rows]
        gm = jnp.where(causal, g_r, -jnp.inf)
        big_m = jnp.maximum(m_prev, jnp.max(gm, axis=-1, keepdims=True))
        m_rep = jnp.broadcast_to(big_m, (chunk, LANES))
        mt_rep = jnp.broadcast_to(bcol[:, ML_HEADS + h:ML_HEADS + h + 1] + big_m, (chunk, LANES))
        qkc = _dot(q, jnp.concatenate([kT, cn.astype(BF16)], axis=1))
        s = qkc[:, :chunk] * jnp.exp(gm - tile_to(m_rep, chunk))
        b_last = b_r[:, chunk - 1:chunk]
        m_new = b_last + jnp.maximum(m_prev, jnp.max(g_r, axis=-1, keepdims=True))
        decay = jnp.exp((b_last + m_prev) - m_new)
        w_end = jnp.exp((g_r + b_last) - m_new)
        lhs = jnp.concatenate([s.astype(BF16), (kT.astype(F32) * w_end).astype(BF16)], axis=0)
        stage.append((lhs, qkc[:, chunk:], m_prev, m_rep, mt_rep, m_new, decay, cn))

    for h, (lhs, qc, m_prev, m_rep, mt_rep, m_new, decay, cn) in enumerate(stage):
        hs = slice(h * ML_HEAD_DIM, (h + 1) * ML_HEAD_DIM)
        vaug = jnp.concatenate([v_ref[rows, hs], ones], axis=-1)
        both = _dot(lhs, vaug)
        sva = both[0:chunk]
        w_inter = jnp.exp(m_prev - m_rep)
        den = w_inter * qc[:, ML_HEAD_DIM:] + sva[:, ML_HEAD_DIM:]
        rinv = 1.0 / jnp.maximum(jnp.abs(den), jnp.exp(-mt_rep))
        hh_ref[rows, hs] = ((tile_to(w_inter, ML_HEAD_DIM) * qc[:, :ML_HEAD_DIM] + sva[:, :ML_HEAD_DIM])
                            * tile_to(rinv, ML_HEAD_DIM))
        cn_ref[h] = decay * cn + both[chunk:]
        m_ref[h:h + 1, :] = jnp.broadcast_to(m_new, (1, LANES))


def _mlstm_prompt_kernel(q_ref, kT_ref, v_ref, gr_ref, hh_ref, cout_ref, nout_ref, mout_ref, cn_ref, m_ref, *, chunk):
    t = pl.program_id(1)
    nb = q_ref.shape[0]

    @pl.when(t == 0)
    def _():
        cn_ref[...] = jnp.zeros(cn_ref.shape, F32)
        m_ref[...] = jnp.full(m_ref.shape, M_INIT, F32)

    for r0 in range(0, q_ref.shape[1], chunk):
        for s in range(nb):
            _mlstm_chunk(q_ref.at[s], kT_ref.at[s], v_ref.at[s], gr_ref.at[s], hh_ref.at[s], cn_ref.at[s],
                         m_ref.at[s], r0, chunk)

    @pl.when(t == pl.num_programs(1) - 1)
    def _():
        cout_ref[...] = cn_ref[:, :, :, 0:ML_HEAD_DIM]
        nout_ref[...] = cn_ref[:, :, :, ML_HEAD_DIM:AUG]
        mout_ref[...] = m_ref[...]


def _mlstm_prompt(q, kT, v, gr):
    nbatch, seq, _ = q.shape
    tm, nb = MLSTM_TILE, MLSTM_BATCHES
    row = pl.BlockSpec((nb, tm, D_MODEL), lambda b, t: (b, t, 0))
    state = lambda last: pl.BlockSpec((nb, ML_HEADS, ML_HEAD_DIM, last), lambda b, t: (b, 0, 0, 0))
    return pl.pallas_call(
        functools.partial(_mlstm_prompt_kernel, chunk=PROMPT_CHUNK),
        grid=(nbatch // nb, seq // tm),
        in_specs=[row, pl.BlockSpec((nb, D_MODEL, tm), lambda b, t: (b, 0, t)), row,
                  pl.BlockSpec((nb, SUBLANES, tm), lambda b, t: (b, 0, t))],
        out_specs=[row, state(ML_HEAD_DIM), state(LANES), pl.BlockSpec((nb, SUBLANES, LANES), lambda b, t: (b, 0, 0))],
        out_shape=[jax.ShapeDtypeStruct((nbatch, seq, D_MODEL), F32),
                   jax.ShapeDtypeStruct((nbatch, ML_HEADS, ML_HEAD_DIM, ML_HEAD_DIM), F32),
                   jax.ShapeDtypeStruct((nbatch, ML_HEADS, ML_HEAD_DIM, LANES), F32),
                   jax.ShapeDtypeStruct((nbatch, SUBLANES, LANES), F32)],
        scratch_shapes=[pltpu.VMEM((nb, ML_HEADS, ML_HEAD_DIM, AUG), F32), pltpu.VMEM((nb, SUBLANES, LANES), F32)],
        compiler_params=_params(("arbitrary", "arbitrary")),
        name="mlstm_prompt",
    )(q, kT, v, gr)


def _last_in_group(x, group):
    rows = x.shape[0]
    x3 = jnp.broadcast_to(x, (rows, LANES)).reshape(rows // group, group, LANES)
    last = jnp.broadcast_to(x3[:, group - 1:group, :], x3.shape)
    return last.reshape(rows, LANES)[:, 0:1]


def _mlstm_sample_body(h, q_ref, kT_ref, k_ref, v_ref, gc_ref, gr_ref, m0_ref, c_ref, n_ref,
                       hh_ref, cout_ref, nout_ref, mt_ref, ni_ref, qn_ref, dec_ref, wk_ref, *, tt):
    L = q_ref.shape[0]
    nseq = L // tt
    gc = gc_ref[...]
    gr = gr_ref[...]
    ri = lax.broadcasted_iota(jnp.int32, (L, L), 0)
    ci = lax.broadcasted_iota(jnp.int32, (L, L), 1)
    same = (ri // tt) == (ci // tt)
    mask = same & (ri >= ci)
    bcol = _cumsum_cols(mask.astype(BF16), gc)
    brow = _cumsum_rows(gr, (same & (ri <= ci)).astype(BF16))
    lane = lax.broadcasted_iota(jnp.int32, (L, GATE_LANES), 1)
    sub = lax.broadcasted_iota(jnp.int32, (SUBLANES, L), 0)
    pick_col = lambda arr, idx: jnp.sum(jnp.where(lane == idx, arr, 0.0), axis=-1, keepdims=True)
    pick_row = lambda arr, idx: jnp.sum(jnp.where(sub == idx, arr, 0.0), axis=0, keepdims=True)
    ig_c = pick_col(gc, h)
    b_c = pick_col(bcol, ML_HEADS + h)
    ig_r = pick_row(gr, h)
    b_r = pick_row(brow, ML_HEADS + h)
    m_prev = m0_ref[0]

    for j in range(nseq):
        rows = slice(j * tt, (j + 1) * tt)
        qj = q_ref[rows, :]
        ni_ref[rows, :] = _dot(qj, c_ref[j, 0].astype(BF16))
        nj = n_ref[j, pl.ds(h, 1), :].astype(BF16).astype(F32)
        qn = jnp.sum(qj.astype(F32) * nj, axis=-1, keepdims=True)
        qn_ref[rows, :] = jnp.broadcast_to(qn, (tt, LANES))

    q = q_ref[...]
    kT = kT_ref[...]
    v = v_ref[...]
    qk = _dot(q, kT)
    logd = jnp.where(mask, (b_c - b_r) + ig_r, -jnp.inf)
    a_c = b_c + m_prev
    m_t = jnp.maximum(a_c, jnp.max(logd, axis=-1, keepdims=True))
    w_inter = jnp.exp(a_c - m_t)
    s = qk * jnp.exp(logd - m_t)
    num = w_inter * ni_ref[...] + _dot(s.astype(BF16), v)
    den = w_inter * qn_ref[:, 0:1] + jnp.sum(s, axis=-1, keepdims=True)
    hh_ref[...] = num / jnp.maximum(jnp.abs(den), jnp.exp(-m_t))
    m_new = _last_in_group(m_t, tt)
    decay = jnp.exp(_last_in_group(a_c, tt) - m_new)
    w_end = jnp.exp((_last_in_group(b_c, tt) - b_c) + ig_c - m_new)
    mt_ref[0] = jnp.broadcast_to(m_t, (L, LANES))
    dec_ref[...] = jnp.broadcast_to(decay, (L, LANES))
    wv = w_end * v.astype(F32)
    wk_ref[...] = w_end.astype(BF16).astype(F32) * k_ref[...].astype(F32)
    rowi = lax.broadcasted_iota(jnp.int32, (L, 1), 0)

    for j in range(nseq):
        rows = slice(j * tt, (j + 1) * tt)
        upd = _dot(kT, jnp.where((rowi // tt) == j, wv, 0.0).astype(BF16))
        dj = dec_ref[j * tt:j * tt + 1, 0:1]
        cout_ref[j, 0] = dj * c_ref[j, 0] + upd
        nout_ref[j, pl.ds(h, 1), :] = (dj * n_ref[j, pl.ds(h, 1), :]
                                       + jnp.sum(wk_ref[rows, :], axis=0, keepdims=True))


def _mlstm_sample_specs(ntok, tt):
    nseq = ntok // tt
    sb = SAMPLE_SEQ_BLOCK
    L = sb * tt
    nh = ML_HEADS
    qspec = pl.BlockSpec((L, ML_HEAD_DIM), lambda i: (i // nh, i % nh))
    cspec = pl.BlockSpec((sb, 1, ML_HEAD_DIM, ML_HEAD_DIM), lambda i: (i // nh, i % nh, 0, 0))
    nspec = pl.BlockSpec((sb, nh, ML_HEAD_DIM), lambda i: (i // nh, 0, 0))
    in_specs = [qspec, pl.BlockSpec((ML_HEAD_DIM, L), lambda i: (i % nh, i // nh)), qspec, qspec,
                pl.BlockSpec((L, GATE_LANES), lambda i: (i // nh, 0)),
                pl.BlockSpec((SUBLANES, L), lambda i: (0, i // nh)),
                pl.BlockSpec((1, L, 1), lambda i: (i % nh, i // nh, 0)), cspec, nspec]
    out_specs = [qspec, cspec, nspec, pl.BlockSpec((1, L, LANES), lambda i: (i % nh, i // nh, 0))]
    out_shapes = [jax.ShapeDtypeStruct((ntok, D_MODEL), F32),
                  jax.ShapeDtypeStruct((nseq, ML_HEADS, ML_HEAD_DIM, ML_HEAD_DIM), F32),
                  jax.ShapeDtypeStruct((nseq, ML_HEADS, ML_HEAD_DIM), F32),
                  jax.ShapeDtypeStruct((ML_HEADS, ntok, LANES), F32)]
    scratch = [pltpu.VMEM((L, ML_HEAD_DIM), F32), pltpu.VMEM((L, LANES), F32),
               pltpu.VMEM((L, LANES), F32), pltpu.VMEM((L, ML_HEAD_DIM), F32)]
    return (nseq // sb) * nh, in_specs, out_specs, out_shapes, scratch


def _post_kernel(x_ref, sc_ref, sh_ref, gt_ref, g_ref, hh_ref, yp_ref, wt_ref, ghead_ref, wbp_ref, wbm_ref, wout_ref,
                 o_ref):
    pool = _dot(yp_ref[...], wbp_ref[...])
    x, hmod = _norm_mod(x_ref, sc_ref, sh_ref, g_ref)
    hb = hmod.astype(BF16)
    nb, tt, d = x_ref.shape
    o = _dot(hb, wt_ref[:, 0:D_MODEL])
    parts = []
    for h in range(ML_HEADS):
        hh = hh_ref[:, h * ML_HEAD_DIM:(h + 1) * ML_HEAD_DIM]
        parts.append(hh * lax.rsqrt(jnp.mean(hh * hh, axis=-1, keepdims=True) + EPS))
    yml = (jnp.concatenate(parts, axis=-1) * ghead_ref[...]) * jax.nn.sigmoid(o)
    gp = _dot(hb, wt_ref[:, D_MODEL:2 * D_MODEL])
    gm = _dot(hb, wt_ref[:, 2 * D_MODEL:3 * D_MODEL])
    merged = jax.nn.sigmoid(gp) * pool + jax.nn.sigmoid(gm) * _dot(yml.astype(BF16), wbm_ref[...])
    y = _dot(merged.astype(BF16), wout_ref[...]).reshape(nb, tt, d)
    o_ref[...] = x_ref[...] + gt_ref[0] * y


def _tile_blocks(x, tile):
    g, t, _ = x.shape
    if t >= tile:
        nb, tt = 1, tile
    else:
        nb, tt = tile // t, t
    return nb, tt, (g // nb) * (t // tt), t // tt


def _post(x, mod_a, mod_b, row0, g_mix, hh, yp, w_tail, g_head, w_bp, w_bm, w_out):
    nb, tt, steps, per = _tile_blocks(x, TOKEN_TILE)
    tm = nb * tt
    xspec = pl.BlockSpec((nb, tt, D_MODEL), lambda i: (i // per, i % per, 0))
    sc, sh, gt = (_mod_spec(k, nb, row0, lambda i: i // per) for k in (_SC1, _SH1, _GT1))
    row = lambda i: (i, 0)
    return pl.pallas_call(
        _post_kernel,
        grid=(steps,),
        in_specs=[xspec, sc, sh, gt, _const_spec((1, D_MODEL)),
                  pl.BlockSpec((tm, D_MODEL), row), pl.BlockSpec((tm, POOL_WIDTH), row),
                  _const_spec((D_MODEL, 3 * D_MODEL)), _const_spec((1, D_MODEL)),
                  _const_spec((POOL_WIDTH, D_MODEL)), _const_spec((D_MODEL, D_MODEL)),
                  _const_spec((D_MODEL, D_MODEL))],
        out_specs=xspec,
        out_shape=jax.ShapeDtypeStruct(x.shape, F32),
        compiler_params=_params(("arbitrary",)),
        name="post",
    )(x, mod_a, mod_a, mod_b, g_mix, hh, yp, w_tail, g_head, w_bp, w_bm, w_out)


_FF_SPLITS = ((0, 768), (768, 1536), (1536, 2304), (2304, D_FF))


def _ffn_kernel(x_ref, sc_ref, sh_ref, gt_ref, g_ref, gfin_ref, wgu_ref, wdn_ref, o_ref):
    _, hmod = _norm_mod(x_ref, sc_ref, sh_ref, g_ref)
    hb = hmod.astype(BF16)
    nb, tt, d = x_ref.shape
    dn = None
    for lo, hi in _FF_SPLITS:
        gate = _dot(hb, wgu_ref[:, lo:hi])
        up = _dot(hb, wgu_ref[:, D_FF + lo:D_FF + hi])
        act = (gate * jax.nn.sigmoid(gate) * up).astype(BF16)
        part = _dot(act, wdn_ref[lo:hi, :])
        dn = part if dn is None else dn + part
    x2 = x_ref[...] + gt_ref[0] * dn.reshape(nb, tt, d)
    ms = jnp.mean(x2 * x2, axis=-1, keepdims=True)
    o_ref[...] = x2 * lax.rsqrt(ms + EPS) * gfin_ref[...]


_N_FFN_IN = 8


def _ffn_mlstm_kernel(*refs, n_ml_in, tt):
    ffn_in = refs[:_N_FFN_IN]
    ml_in = refs[_N_FFN_IN:_N_FFN_IN + n_ml_in]
    o_ref = refs[_N_FFN_IN + n_ml_in]
    ml_rest = refs[_N_FFN_IN + n_ml_in + 1:]
    _mlstm_sample_body(pl.program_id(0) % ML_HEADS, *ml_in, *ml_rest, tt=tt)
    _ffn_kernel(*ffn_in, o_ref)


def _ffn(x, mod, row0, g_ffn, g_final, w_gu, w_down, tile, mlstm_sample=None):
    nb, tt, steps, per = _tile_blocks(x, tile)
    xspec = pl.BlockSpec((nb, tt, D_MODEL), lambda i: (i // per, i % per, 0))
    sc, sh, gt = (_mod_spec(k, nb, row0, lambda i: i // per) for k in (_SC2, _SH2, _GT2))
    in_specs = [xspec, sc, sh, gt, _const_spec((1, D_MODEL)), _const_spec((1, D_MODEL)),
                _const_spec((D_MODEL, 2 * D_FF)), _const_spec((D_FF, D_MODEL))]
    operands = (x, mod, mod, mod, g_ffn, g_final, w_gu, w_down)
    if mlstm_sample is None:
        return pl.pallas_call(
            _ffn_kernel,
            grid=(steps,),
            in_specs=in_specs,
            out_specs=xspec,
            out_shape=jax.ShapeDtypeStruct(x.shape, F32),
            compiler_params=_params(("arbitrary",)),
            name="ffn",
        )(*operands)
    ml_operands, ml_tt = mlstm_sample
    ml_steps, ml_in, ml_out, ml_shapes, ml_scratch = _mlstm_sample_specs(ml_operands[0].shape[0], ml_tt)
    assert ml_steps == steps, "one sample mLSTM step per FFN tile"
    return pl.pallas_call(
        functools.partial(_ffn_mlstm_kernel, n_ml_in=len(ml_in), tt=ml_tt),
        grid=(steps,),
        in_specs=in_specs + ml_in,
        out_specs=[xspec] + ml_out,
        out_shape=[jax.ShapeDtypeStruct(x.shape, F32)] + ml_shapes,
        scratch_shapes=ml_scratch,
        compiler_params=_params(("arbitrary",)),
        name="ffn_mlstm",
    )(*operands, *ml_operands)


def kernel(x_prompt, x_sample, c_prompt, c_sample, state_pool, state_mlstm_c, state_mlstm_n, state_mlstm_m, g_mix, g_ffn, g_final, w_ada, b_ada, w_in, b_igate, b_fgate, w_pool, s_pool, g_head, w_branch_pool, w_branch_mlstm, w_out, w_gate_up, w_down):
    depth = w_in.shape[0]
    assert depth == 1, "single-layer trunk"
    nbatch, seq, _ = x_prompt.shape
    nseq, tt, _ = x_sample.shape
    l = 0

    bgate = jnp.pad(jnp.concatenate([b_igate[l], b_fgate[l]])[None, :], ((0, 0), (0, GATE_LANES - 2 * ML_HEADS)))
    g_mix_r, g_ffn_r, g_fin_r = g_mix[l][None, :], g_ffn[l][None, :], g_final[None, :]
    s_pool_r, g_head_r = s_pool[l][None, :], g_head[l][None, :]

    w_in_t = jnp.swapaxes(w_in, 1, 2)
    ada_operands = (c_sample, c_prompt, w_ada[l], b_ada[l][None, :])
    w_head, w_gate, w_pool_b, mod = _prep(w_in_t, w_pool, *ada_operands)
    row_s, row_p = 0, nseq

    (yp, q, kT, v, gc, gr, hist_p, w_bp, w_bm, w_o, w_gu, w_dn, w_tail, mod2) = _inproj_prompt(
        x_prompt, mod, row_p, g_mix_r, w_head, w_gate, bgate, w_pool_b, s_pool_r,
        (w_branch_pool, w_branch_mlstm, w_out, w_gate_up, w_down), w_in_t, ada_operands)
    hh, c_p, n_p, m_p = _mlstm_prompt(q.reshape(nbatch, seq, D_MODEL), kT, v.reshape(nbatch, seq, D_MODEL), gr)
    hh = hh.reshape(nbatch * seq, D_MODEL)
    x1 = _post(x_prompt, mod, mod2, row_p, g_mix_r, hh, yp, w_tail, g_head_r, w_bp, w_bm, w_o)

    hist_tm = jnp.swapaxes(state_pool[l], 0, 1)
    yp, q, kT, v, gc, gr, k, hist_s = _inproj_sample(x_sample, mod, row_s, g_mix_r, w_head, w_gate, bgate, w_pool_b,
                                                     s_pool_r, hist_tm)
    m0_tok = jnp.repeat(state_mlstm_m[l].astype(F32).T, tt, axis=1)[:, :, None]
    ml_operands = (q, kT, k, v, gc, gr, m0_tok, state_mlstm_c[l].astype(F32), state_mlstm_n[l].astype(F32))
    y_prompt, hh, c_s, n_s, mt = _ffn(x1, mod2, row_p, g_ffn_r, g_fin_r, w_gu, w_dn, PROMPT_FFN_TILE,
                                      (ml_operands, tt))

    x1 = _post(x_sample, mod, mod2, row_s, g_mix_r, hh, yp, w_tail, g_head_r, w_bp, w_bm, w_o)
    y_sample = _ffn(x1, mod2, row_s, g_ffn_r, g_fin_r, w_gu, w_dn, FFN_TILE)

    cd, nd, md = state_mlstm_c.dtype, state_mlstm_n.dtype, state_mlstm_m.dtype
    return (y_prompt, y_sample,
            hist_p[None, :, HIST_ROWS - POOL_HIST:, :],
            c_p.astype(cd)[None], n_p[..., 0].astype(nd)[None], m_p[:, :ML_HEADS, 0].astype(md)[None],
            jnp.swapaxes(hist_s, 0, 1)[None].astype(state_pool.dtype),
            c_s.astype(cd)[None], n_s.astype(nd)[None],
            mt[:, tt - 1::tt, 0].T.astype(md)[None])
```

```python
import functools

import jax
import jax.numpy as jnp
from jax import lax
from jax.experimental import pallas as pl
from jax.experimental.pallas import tpu as pltpu

D_MODEL = 1024
POOL_WINDOWS = (2, 4, 8, 16)
POOL_GROUP_DIM = 128
POOL_WIDTH = 512
POOL_HIST = 15
HIST_ROWS = 16
ML_HEADS = 4
ML_HEAD_DIM = 256
D_FF = 2816
EPS = 1e-6
M_INIT = -1e30
PAST_LEN = 16384
LANES = 128
SUBLANES = 8
BF16_ROWS = 16
GATE_LANES = LANES
AUG = ML_HEAD_DIM + LANES
VMEM_LIMIT = 56 * 1024 * 1024

TOKEN_TILE = 512
PROMPT_INPROJ_TILE = 1024
FFN_TILE = 512
PROMPT_FFN_TILE = 512
PROMPT_CHUNK = 512
MLSTM_TILE = 1024
MLSTM_BATCHES = 1
SAMPLE_SEQ_BLOCK = 16

BF16 = jnp.bfloat16
F32 = jnp.float32

_U0, _Q0, _K0, _V0, _G0 = 0, 512, 1536, 2560, 3584


def _dot(a, b):
    return jnp.dot(a, b, preferred_element_type=F32)


def _const_spec(shape):
    zeros = (0,) * len(shape)
    return pl.BlockSpec(shape, lambda *_: zeros, pipeline_mode=pl.Buffered(1))


def _params(sem):
    return pltpu.CompilerParams(dimension_semantics=sem, vmem_limit_bytes=VMEM_LIMIT)


def _norm_mod(x_ref, sc_ref, sh_ref, g_ref):
    x = x_ref[...]
    nb, tt, d = x.shape
    ms = jnp.mean(x * x, axis=-1, keepdims=True)
    y = x * lax.rsqrt(ms + EPS) * g_ref[...]
    hmod = y * (1.0 + sc_ref[0]) + sh_ref[0]
    return x.reshape(nb * tt, d), hmod.reshape(nb * tt, d)


def _split3(x):
    hi = x.astype(BF16)
    r1 = x - hi.astype(F32)
    mid = r1.astype(BF16)
    lo = (r1 - mid.astype(F32)).astype(BF16)
    return hi, mid, lo


def _cumsum_cols(tri, x):
    n = x.shape[1]
    y = _dot(tri, jnp.concatenate(_split3(x), axis=1))
    return (y[:, 0:n] + y[:, n:2 * n]) + y[:, 2 * n:3 * n]


def _cumsum_rows(x, tri):
    n = x.shape[0]
    y = _dot(jnp.concatenate(_split3(x), axis=0), tri)
    return (y[0:n] + y[n:2 * n]) + y[2 * n:3 * n]


_ADA_COLS = 256
_SH1, _SC1 = 0, 1
_GT1, _SH2, _SC2, _GT2 = 0, 1, 2, 3
_N_MOD_A = 2


def _ada_block(cs_ref, cp_ref, w_ref, b_ref, o_ref):
    c = jnp.concatenate([cs_ref[...], cp_ref[...]], axis=0)
    a = (c * jax.nn.sigmoid(c)).astype(BF16)
    res = _dot(a, w_ref[...].astype(BF16)) + b_ref[...]
    o_ref[0] = res.reshape(res.shape[0], 1, res.shape[1])


def _ada_specs(c_sample, c_prompt, first_block, step_of):
    rows = c_sample.shape[0] + c_prompt.shape[0]
    per_chunk = D_MODEL // _ADA_COLS
    const2 = lambda *g: (0, 0)
    in_specs = [pl.BlockSpec(c_sample.shape, const2), pl.BlockSpec(c_prompt.shape, const2),
                pl.BlockSpec((D_MODEL, _ADA_COLS), lambda *g: (0, first_block + step_of(*g))),
                pl.BlockSpec((1, _ADA_COLS), lambda *g: (0, first_block + step_of(*g)))]
    out_spec = pl.BlockSpec((1, rows, 1, _ADA_COLS), lambda *g: (step_of(*g) // per_chunk, 0, 0, step_of(*g) % per_chunk))
    return in_specs, out_spec, rows


def _mod_spec(k, nb, row0, block_index):
    return pl.BlockSpec((1, nb, 1, D_MODEL), lambda *g: (k, row0 // nb + block_index(*g), 0, 0))


_PREP_COLS = 512


def _prep_kernel(wt_ref, gt_ref, wp_ref, cs_ref, cp_ref, wa_ref, ba_ref, head_ref, gate_ref, pool_ref, mod_ref, *,
                 head_steps):
    @pl.when(pl.program_id(0) < head_steps)
    def _():
        head_ref[...] = wt_ref[0].T.astype(BF16)

    gates = gt_ref[0].T
    pad = jnp.zeros((gates.shape[0], GATE_LANES - gates.shape[1]), F32)
    gate_ref[...] = jnp.concatenate([gates, pad], axis=1).astype(BF16)
    pool_ref[...] = wp_ref[0].astype(BF16)
    _ada_block(cs_ref, cp_ref, wa_ref, ba_ref, mod_ref)


def _prep(w_in_t, w_pool, c_sample, c_prompt, w_ada, b_ada):
    g0 = _G0 + D_MODEL
    head_steps = _G0 // _PREP_COLS
    steps = _N_MOD_A * D_MODEL // _ADA_COLS
    assert steps >= head_steps
    head_blk = lambda j: jnp.minimum(j, head_steps - 1)
    ada_in, ada_out, rows = _ada_specs(c_sample, c_prompt, 0, lambda j: j)
    return pl.pallas_call(
        functools.partial(_prep_kernel, head_steps=head_steps),
        grid=(steps,),
        in_specs=[pl.BlockSpec((1, _PREP_COLS, D_MODEL), lambda j: (0, head_blk(j), 0)),
                  pl.BlockSpec((1, 2 * ML_HEADS, D_MODEL), lambda j: (0, g0 // (2 * ML_HEADS), 0)),
                  pl.BlockSpec(w_pool.shape, lambda j: (0, 0, 0, 0))] + ada_in,
        out_specs=[pl.BlockSpec((D_MODEL, _PREP_COLS), lambda j: (0, head_blk(j))),
                   pl.BlockSpec((D_MODEL, GATE_LANES), lambda j: (0, 0)),
                   pl.BlockSpec(w_pool.shape[1:], lambda j: (0, 0, 0)), ada_out],
        out_shape=[jax.ShapeDtypeStruct((D_MODEL, _G0), BF16),
                   jax.ShapeDtypeStruct((D_MODEL, GATE_LANES), BF16),
                   jax.ShapeDtypeStruct(w_pool.shape[1:], BF16),
                   jax.ShapeDtypeStruct((_N_MOD_A, rows, 1, D_MODEL), F32)],
        compiler_params=_params(("arbitrary",)),
        name="prep",
    )(w_in_t, w_in_t, w_pool, c_sample, c_prompt, w_ada, b_ada)


_TAIL_COLS = 256
_TAIL_BLOCKS = 3 * D_MODEL // _TAIL_COLS
_TAIL_SHIFT_BLOCK = D_MODEL // _TAIL_COLS
_TAIL_SHIFT = 2 * ML_HEADS


def _tail_block(step):
    return jnp.minimum(step, _TAIL_BLOCKS - 1)


def _tail_prep(step, wa_ref, wb_ref, tail_ref):
    c = _tail_block(step)
    a = wa_ref[0]
    shifted = jnp.concatenate([a[_TAIL_SHIFT:, :], wb_ref[0]], axis=0)
    tail_ref[...] = jnp.where(c >= _TAIL_SHIFT_BLOCK, shifted, a).T.astype(BF16)


def _project(hb, w_ref, wg_ref, bg_ref, q_ref, kT_ref, v_ref, gc_ref, gr_ref):
    zg = _dot(hb, wg_ref[...]) + bg_ref[...]
    k = _dot(hb, w_ref[:, _K0:_V0]) * (ML_HEAD_DIM ** -0.5)
    log_f = jnp.minimum(zg, 0.0) - jnp.log1p(jnp.exp(-jnp.abs(zg)))
    lane = lax.broadcasted_iota(jnp.int32, zg.shape, 1)
    gc = jnp.where(lane < ML_HEADS, zg, log_f)
    if gc_ref is not None:
        gc_ref[...] = gc
    gr_ref[...] = gc.T[0:SUBLANES, :]
    kT_ref[...] = k.T.astype(BF16)
    q_ref[...] = _dot(hb, w_ref[:, _Q0:_K0]).astype(BF16)
    v_ref[...] = _dot(hb, w_ref[:, _V0:_G0]).astype(BF16)
    return k


def _pool_deltas(acc_fn, u, cnt_fn):
    deltas = []
    for g, w in enumerate(POOL_WINDOWS):
        ug = u[:, g * POOL_GROUP_DIM:(g + 1) * POOL_GROUP_DIM]
        deltas.append((acc_fn(g, w, ug) / cnt_fn(w) - ug).astype(BF16))
    return deltas


def _pool_project(deltas, wpool_ref, spool_ref):
    outs = [_dot(d, wpool_ref[g]) for g, d in enumerate(deltas)]
    return (jnp.concatenate(outs, axis=-1) * spool_ref[...]).astype(BF16)


def _inproj_prompt_kernel(x_ref, sc_ref, sh_ref, g_ref, w_ref, wg_ref, bg_ref, wpool_ref, spool_ref,
                          wbp_ref, wbm_ref, wout_ref, wgu_ref, wdn_ref, wa_ref, wb_ref, cs_ref, cp_ref, wada_ref, bada_ref,
                          yp_ref, q_ref, kT_ref, v_ref, gr_ref, hout_ref,
                          wbp_o, wbm_o, wout_o, wgu_o, wdn_o, tail_o, mod_o, ext_ref, *, tm):
    t = pl.program_id(1)

    @pl.when(t == 0)
    def _():
        ext_ref[0:HIST_ROWS, :] = jnp.zeros((HIST_ROWS, POOL_WIDTH), F32)

    _, hmod = _norm_mod(x_ref, sc_ref, sh_ref, g_ref)
    hb = hmod.astype(BF16)
    u = _dot(hb, w_ref[:, _U0:_Q0])
    ext_ref[HIST_ROWS:HIST_ROWS + tm, :] = u
    pos = t * tm + lax.broadcasted_iota(jnp.int32, (tm, 1), 0)

    def acc_fn(g, w, ug):
        acc = ug
        for j in range(1, w):
            acc = acc + ext_ref[pl.ds(HIST_ROWS - j, tm), g * POOL_GROUP_DIM:(g + 1) * POOL_GROUP_DIM]
        return acc

    def cnt_fn(w):
        return jnp.minimum(pos + 1, w).astype(F32)

    deltas = _pool_deltas(acc_fn, u, cnt_fn)
    last = ext_ref[tm:tm + HIST_ROWS, :]
    hout_ref[0] = last
    ext_ref[0:HIST_ROWS, :] = last

    _project(hb, w_ref, wg_ref, bg_ref, q_ref, kT_ref.at[0], v_ref, None, gr_ref.at[0])
    yp_ref[...] = _pool_project(deltas, wpool_ref, spool_ref)

    _tail_prep(pl.program_id(0) * pl.num_programs(1) + t, wa_ref, wb_ref, tail_o)
    for src, dst in ((wbp_ref, wbp_o), (wbm_ref, wbm_o), (wout_ref, wout_o), (wgu_ref, wgu_o), (wdn_ref, wdn_o)):
        dst[...] = src[0].astype(BF16)
    _ada_block(cs_ref, cp_ref, wada_ref, bada_ref, mod_o)

def _inproj_sample_kernel(x_ref, sc_ref, sh_ref, g_ref, w_ref, wg_ref, bg_ref, wpool_ref, spool_ref, hist_ref,
                          yp_ref, q_ref, kT_ref, v_ref, gc_ref, gr_ref, k_ref, hout_ref, u_ref, d_ref, *, pos0):
    _, hmod = _norm_mod(x_ref, sc_ref, sh_ref, g_ref)
    hb = hmod.astype(BF16)
    nb, tt = x_ref.shape[0], x_ref.shape[1]
    u = _dot(hb, w_ref[:, _U0:_Q0])
    k_ref[...] = _project(hb, w_ref, wg_ref, bg_ref, q_ref, kT_ref, v_ref, gc_ref, gr_ref).astype(BF16)
    for g, w in enumerate(POOL_WINDOWS):
        cols = slice(g * POOL_GROUP_DIM, (g + 1) * POOL_GROUP_DIM)
        u_ref[g] = u[:, cols]
        new = [u_ref[g, pl.ds(t, nb, stride=tt), :] for t in range(tt)]

        def ext(r):
            return hist_ref[r, :, cols] if r < POOL_HIST else new[r - POOL_HIST]

        for t in range(tt):
            acc = new[t]
            for j in range(1, w):
                acc = acc + ext(POOL_HIST + t - j)
            cnt = float(min(pos0 + t + 1, w))
            d_ref[g, pl.ds(t, nb, stride=tt), :] = acc / cnt - new[t]
        for r in range(POOL_HIST):
            hout_ref[r, :, cols] = ext(r + tt)
    yp_ref[...] = _pool_project([d_ref[g].astype(BF16) for g in range(len(POOL_WINDOWS))], wpool_ref, spool_ref)


def _cast_specs(weights, nt, steps):
    in_specs, out_specs, out_shapes = [], [], []
    for w in weights:
        _, r, c = w.shape
        n = steps
        while r % n or (r // n) % BF16_ROWS:
            n //= 2
        rows = r // n
        idx = lambda b, t, n=n: jnp.minimum(b * nt + t, n - 1)
        in_specs.append(pl.BlockSpec((1, rows, c), lambda b, t, idx=idx: (0, idx(b, t), 0)))
        out_specs.append(pl.BlockSpec((rows, c), lambda b, t, idx=idx: (idx(b, t), 0)))
        out_shapes.append(jax.ShapeDtypeStruct((r, c), BF16))
    return in_specs, out_specs, out_shapes


def _inproj_prompt(x, mod, row0, g_mix, w_head, w_gate, bgate, w_pool, s_pool, later_weights, w_in_t, ada_operands):
    nbatch, seq, _ = x.shape
    tm = PROMPT_INPROJ_TILE
    nt = seq // tm
    ntok = nbatch * seq
    row = lambda b, t: (b * nt + t, 0)
    sc, sh = (_mod_spec(k, 1, row0, lambda b, t: b) for k in (_SC1, _SH1))
    cast_in, cast_out, cast_shapes = _cast_specs(later_weights, nt, nbatch * nt)
    tail_row = lambda b, t: _G0 // _TAIL_COLS + _tail_block(b * nt + t)
    cast_in = cast_in + [
        pl.BlockSpec((1, _TAIL_COLS, D_MODEL), lambda b, t: (0, tail_row(b, t), 0)),
        pl.BlockSpec((1, _TAIL_SHIFT, D_MODEL), lambda b, t: (0, (tail_row(b, t) + 1) * (_TAIL_COLS // _TAIL_SHIFT), 0))]
    cast_out = cast_out + [pl.BlockSpec((D_MODEL, _TAIL_COLS), lambda b, t: (0, _tail_block(b * nt + t)))]
    cast_shapes = cast_shapes + [jax.ShapeDtypeStruct((D_MODEL, _TAIL_BLOCKS * _TAIL_COLS), BF16)]
    c_sample, c_prompt, w_ada, _ = ada_operands
    n_mod_b = w_ada.shape[1] // D_MODEL - _N_MOD_A
    assert nbatch * nt == n_mod_b * D_MODEL // _ADA_COLS, "one modulation block per grid step"
    ada_in, ada_out, mod_rows = _ada_specs(c_sample, c_prompt, _N_MOD_A * D_MODEL // _ADA_COLS, lambda b, t: b * nt + t)
    cast_in, cast_out = cast_in + ada_in, cast_out + [ada_out]
    cast_shapes = cast_shapes + [jax.ShapeDtypeStruct((n_mod_b, mod_rows, 1, D_MODEL), F32)]
    return pl.pallas_call(
        functools.partial(_inproj_prompt_kernel, tm=tm),
        grid=(nbatch, nt),
        in_specs=[pl.BlockSpec((1, tm, D_MODEL), lambda b, t: (b, t, 0)), sc, sh,
                  _const_spec((1, D_MODEL)), _const_spec((D_MODEL, _G0)), _const_spec((D_MODEL, GATE_LANES)),
                  _const_spec((1, GATE_LANES)), _const_spec((4, POOL_GROUP_DIM, POOL_GROUP_DIM)),
                  _const_spec((1, POOL_WIDTH))] + cast_in,
        out_specs=[pl.BlockSpec((tm, POOL_WIDTH), row), pl.BlockSpec((tm, D_MODEL), row),
                   pl.BlockSpec((1, D_MODEL, tm), lambda b, t: (b, 0, t)), pl.BlockSpec((tm, D_MODEL), row),
                   pl.BlockSpec((1, SUBLANES, tm), lambda b, t: (b, 0, t)),
                   pl.BlockSpec((1, HIST_ROWS, POOL_WIDTH), lambda b, t: (b, 0, 0))] + cast_out,
        out_shape=[jax.ShapeDtypeStruct((ntok, POOL_WIDTH), BF16), jax.ShapeDtypeStruct((ntok, D_MODEL), BF16),
                   jax.ShapeDtypeStruct((nbatch, D_MODEL, seq), BF16), jax.ShapeDtypeStruct((ntok, D_MODEL), BF16),
                   jax.ShapeDtypeStruct((nbatch, SUBLANES, seq), F32),
                   jax.ShapeDtypeStruct((nbatch, HIST_ROWS, POOL_WIDTH), F32)] + cast_shapes,
        scratch_shapes=[pltpu.VMEM((tm + HIST_ROWS, POOL_WIDTH), F32)],
        compiler_params=_params(("arbitrary", "arbitrary")),
        name="inproj_prompt",
    )(x, mod, mod, g_mix, w_head, w_gate, bgate, w_pool, s_pool, *later_weights, w_in_t, w_in_t, *ada_operands)


def _inproj_sample(x, mod, row0, g_mix, w_head, w_gate, bgate, w_pool, s_pool, hist_tm):
    nseq, tt, _ = x.shape
    nb = TOKEN_TILE // tt
    tm = nb * tt
    ntok = nseq * tt
    row = lambda i: (i, 0)
    col = lambda i: (0, i)
    sc, sh = (_mod_spec(k, nb, row0, lambda i: i) for k in (_SC1, _SH1))
    hist = pl.BlockSpec((POOL_HIST, nb, POOL_WIDTH), lambda i: (0, i, 0))
    return pl.pallas_call(
        functools.partial(_inproj_sample_kernel, pos0=PAST_LEN),
        grid=(nseq // nb,),
        in_specs=[pl.BlockSpec((nb, tt, D_MODEL), lambda i: (i, 0, 0)), sc, sh,
                  _const_spec((1, D_MODEL)), _const_spec((D_MODEL, _G0)), _const_spec((D_MODEL, GATE_LANES)),
                  _const_spec((1, GATE_LANES)), _const_spec((4, POOL_GROUP_DIM, POOL_GROUP_DIM)),
                  _const_spec((1, POOL_WIDTH)), hist],
        out_specs=[pl.BlockSpec((tm, POOL_WIDTH), row), pl.BlockSpec((tm, D_MODEL), row),
                   pl.BlockSpec((D_MODEL, tm), col), pl.BlockSpec((tm, D_MODEL), row),
                   pl.BlockSpec((tm, GATE_LANES), row), pl.BlockSpec((SUBLANES, tm), col),
                   pl.BlockSpec((tm, D_MODEL), row), hist],
        out_shape=[jax.ShapeDtypeStruct((ntok, POOL_WIDTH), BF16), jax.ShapeDtypeStruct((ntok, D_MODEL), BF16),
                   jax.ShapeDtypeStruct((D_MODEL, ntok), BF16), jax.ShapeDtypeStruct((ntok, D_MODEL), BF16),
                   jax.ShapeDtypeStruct((ntok, GATE_LANES), F32), jax.ShapeDtypeStruct((SUBLANES, ntok), F32),
                   jax.ShapeDtypeStruct((ntok, D_MODEL), BF16),
                   jax.ShapeDtypeStruct((POOL_HIST, nseq, POOL_WIDTH), F32)],
        scratch_shapes=[pltpu.VMEM((len(POOL_WINDOWS), tm, POOL_GROUP_DIM), F32)] * 2,
        compiler_params=_params(("arbitrary",)),
        name="inproj_sample",
    )(x, mod, mod, g_mix, w_head, w_gate, bgate, w_pool, s_pool, hist_tm)


def _mlstm_chunk(q_ref, kT_ref, v_ref, gr_ref, hh_ref, cn_ref, m_ref, r0, chunk):
    rows = slice(r0, r0 + chunk)
    gr = gr_ref[:, rows]
    ri = lax.broadcasted_iota(jnp.int32, (chunk, chunk), 0)
    ci = lax.broadcasted_iota(jnp.int32, (chunk, chunk), 1)
    causal = ri >= ci
    brow = _cumsum_rows(gr, (ri <= ci).astype(BF16))
    bcol = brow.T
    ones = jnp.ones((chunk, LANES), BF16)
    tile_to = lambda x, width: jnp.concatenate([x] * (width // LANES), axis=-1)
    stage = []
    for h in range(ML_HEADS):
        hs = slice(h * ML_HEAD_DIM, (h + 1) * ML_HEAD_DIM)
        b_r = brow[ML_HEADS + h:ML_HEADS + h + 1, :]
        g_r = gr[h:h + 1, :] - b_r
        m_prev = m_ref[h:h + 1, 0:1]
        cn = cn_ref[h]
        q = q_ref[rows, hs]
        kT = kT_ref[hs, rows]
        gm = jnp.where(causal, g_r, -jnp.inf)
        big_m = jnp.maximum(m_prev, jnp.max(gm, axis=-1, keepdims=True))
        m_rep = jnp.broadcast_to(big_m, (chunk, LANES))
        mt_rep = jnp.broadcast_to(bcol[:, ML_HEADS + h:ML_HEADS + h + 1] + big_m, (chunk, LANES))
        qkc = _dot(q, jnp.concatenate([kT, cn.astype(BF16)], axis=1))
        s = qkc[:, :chunk] * jnp.exp(gm - tile_to(m_rep, chunk))
        b_last = b_r[:, chunk - 1:chunk]
        m_new = b_last + jnp.maximum(m_prev, jnp.max(g_r, axis=-1, keepdims=True))
        decay = jnp.exp((b_last + m_prev) - m_new)
        w_end = jnp.exp((g_r + b_last) - m_new)
        lhs = jnp.concatenate([s.astype(BF16), (kT.astype(F32) * w_end).astype(BF16)], axis=0)
        stage.append((lhs, qkc[:, chunk:], m_prev, m_rep, mt_rep, m_new, decay, cn))

    for h, (lhs, qc, m_prev, m_rep, mt_rep, m_new, decay, cn) in enumerate(stage):
        hs = slice(h * ML_HEAD_DIM, (h + 1) * ML_HEAD_DIM)
        vaug = jnp.concatenate([v_ref[rows, hs], ones], axis=-1)
        both = _dot(lhs, vaug)
        sva = both[0:chunk]
        w_inter = jnp.exp(m_prev - m_rep)
        den = w_inter * qc[:, ML_HEAD_DIM:] + sva[:, ML_HEAD_DIM:]
        rinv = 1.0 / jnp.maximum(jnp.abs(den), jnp.exp(-mt_rep))
        hh_ref[rows, hs] = ((tile_to(w_inter, ML_HEAD_DIM) * qc[:, :ML_HEAD_DIM] + sva[:, :ML_HEAD_DIM])
                            * tile_to(rinv, ML_HEAD_DIM))
        cn_ref[h] = decay * cn + both[chunk:]
        m_ref[h:h + 1, :] = jnp.broadcast_to(m_new, (1, LANES))


def _mlstm_prompt_kernel(q_ref, kT_ref, v_ref, gr_ref, hh_ref, cout_ref, nout_ref, mout_ref, cn_ref, m_ref, *, chunk):
    t = pl.program_id(1)
    nb = q_ref.shape[0]

    @pl.when(t == 0)
    def _():
        cn_ref[...] = jnp.zeros(cn_ref.shape, F32)
        m_ref[...] = jnp.full(m_ref.shape, M_INIT, F32)

    for r0 in range(0, q_ref.shape[1], chunk):
        for s in range(nb):
            _mlstm_chunk(q_ref.at[s], kT_ref.at[s], v_ref.at[s], gr_ref.at[s], hh_ref.at[s], cn_ref.at[s],
                         m_ref.at[s], r0, chunk)

    @pl.when(t == pl.num_programs(1) - 1)
    def _():
        cout_ref[...] = cn_ref[:, :, :, 0:ML_HEAD_DIM]
        nout_ref[...] = cn_ref[:, :, :, ML_HEAD_DIM:AUG]
        mout_ref[...] = m_ref[...]


def _mlstm_prompt(q, kT, v, gr):
    nbatch, seq, _ = q.shape
    tm, nb = MLSTM_TILE, MLSTM_BATCHES
    row = pl.BlockSpec((nb, tm, D_MODEL), lambda b, t: (b, t, 0))
    state = lambda last: pl.BlockSpec((nb, ML_HEADS, ML_HEAD_DIM, last), lambda b, t: (b, 0, 0, 0))
    return pl.pallas_call(
        functools.partial(_mlstm_prompt_kernel, chunk=PROMPT_CHUNK),
        grid=(nbatch // nb, seq // tm),
        in_specs=[row, pl.BlockSpec((nb, D_MODEL, tm), lambda b, t: (b, 0, t)), row,
                  pl.BlockSpec((nb, SUBLANES, tm), lambda b, t: (b, 0, t))],
        out_specs=[row, state(ML_HEAD_DIM), state(LANES), pl.BlockSpec((nb, SUBLANES, LANES), lambda b, t: (b, 0, 0))],
        out_shape=[jax.ShapeDtypeStruct((nbatch, seq, D_MODEL), F32),
                   jax.ShapeDtypeStruct((nbatch, ML_HEADS, ML_HEAD_DIM, ML_HEAD_DIM), F32),
                   jax.ShapeDtypeStruct((nbatch, ML_HEADS, ML_HEAD_DIM, LANES), F32),
                   jax.ShapeDtypeStruct((nbatch, SUBLANES, LANES), F32)],
        scratch_shapes=[pltpu.VMEM((nb, ML_HEADS, ML_HEAD_DIM, AUG), F32), pltpu.VMEM((nb, SUBLANES, LANES), F32)],
        compiler_params=_params(("arbitrary", "arbitrary")),
        name="mlstm_prompt",
    )(q, kT, v, gr)


def _last_in_group(x, group):
    rows = x.shape[0]
    x3 = jnp.broadcast_to(x, (rows, LANES)).reshape(rows // group, group, LANES)
    last = jnp.broadcast_to(x3[:, group - 1:group, :], x3.shape)
    return last.reshape(rows, LANES)[:, 0:1]


def _mlstm_sample_body(h, q_ref, kT_ref, k_ref, v_ref, gc_ref, gr_ref, m0_ref, c_ref, n_ref,
                       hh_ref, cout_ref, nout_ref, mt_ref, ni_ref, qn_ref, dec_ref, wk_ref, *, tt):
    L = q_ref.shape[0]
    nseq = L // tt
    gc = gc_ref[...]
    gr = gr_ref[...]
    ri = lax.broadcasted_iota(jnp.int32, (L, L), 0)
    ci = lax.broadcasted_iota(jnp.int32, (L, L), 1)
    same = (ri // tt) == (ci // tt)
    mask = same & (ri >= ci)
    bcol = _cumsum_cols(mask.astype(BF16), gc)
    brow = _cumsum_rows(gr, (same & (ri <= ci)).astype(BF16))
    lane = lax.broadcasted_iota(jnp.int32, (L, GATE_LANES), 1)
    sub = lax.broadcasted_iota(jnp.int32, (SUBLANES, L), 0)
    pick_col = lambda arr, idx: jnp.sum(jnp.where(lane == idx, arr, 0.0), axis=-1, keepdims=True)
    pick_row = lambda arr, idx: jnp.sum(jnp.where(sub == idx, arr, 0.0), axis=0, keepdims=True)
    ig_c = pick_col(gc, h)
    b_c = pick_col(bcol, ML_HEADS + h)
    ig_r = pick_row(gr, h)
    b_r = pick_row(brow, ML_HEADS + h)
    m_prev = m0_ref[0]

    for j in range(nseq):
        rows = slice(j * tt, (j + 1) * tt)
        qj = q_ref[rows, :]
        ni_ref[rows, :] = _dot(qj, c_ref[j, 0].astype(BF16))
        nj = n_ref[j, pl.ds(h, 1), :].astype(BF16).astype(F32)
        qn = jnp.sum(qj.astype(F32) * nj, axis=-1, keepdims=True)
        qn_ref[rows, :] = jnp.broadcast_to(qn, (tt, LANES))

    q = q_ref[...]
    kT = kT_ref[...]
    v = v_ref[...]
    qk = _dot(q, kT)
    logd = jnp.where(mask, (b_c - b_r) + ig_r, -jnp.inf)
    a_c = b_c + m_prev
    m_t = jnp.maximum(a_c, jnp.max(logd, axis=-1, keepdims=True))
    w_inter = jnp.exp(a_c - m_t)
    s = qk * jnp.exp(logd - m_t)
    num = w_inter * ni_ref[...] + _dot(s.astype(BF16), v)
    den = w_inter * qn_ref[:, 0:1] + jnp.sum(s, axis=-1, keepdims=True)
    hh_ref[...] = num / jnp.maximum(jnp.abs(den), jnp.exp(-m_t))
    m_new = _last_in_group(m_t, tt)
    decay = jnp.exp(_last_in_group(a_c, tt) - m_new)
    w_end = jnp.exp((_last_in_group(b_c, tt) - b_c) + ig_c - m_new)
    mt_ref[0] = jnp.broadcast_to(m_t, (L, LANES))
    dec_ref[...] = jnp.broadcast_to(decay, (L, LANES))
    wv = w_end * v.astype(F32)
    wk_ref[...] = w_end.astype(BF16).astype(F32) * k_ref[...].astype(F32)
    rowi = lax.broadcasted_iota(jnp.int32, (L, 1), 0)

    for j in range(nseq):
        rows = slice(j * tt, (j + 1) * tt)
        upd = _dot(kT, jnp.where((rowi // tt) == j, wv, 0.0).astype(BF16))
        dj = dec_ref[j * tt:j * tt + 1, 0:1]
        cout_ref[j, 0] = dj * c_ref[j, 0] + upd
        nout_ref[j, pl.ds(h, 1), :] = (dj * n_ref[j, pl.ds(h, 1), :]
                                       + jnp.sum(wk_ref[rows, :], axis=0, keepdims=True))


def _mlstm_sample_specs(ntok, tt):
    nseq = ntok // tt
    sb = SAMPLE_SEQ_BLOCK
    L = sb * tt
    nh = ML_HEADS
    qspec = pl.BlockSpec((L, ML_HEAD_DIM), lambda i: (i // nh, i % nh))
    cspec = pl.BlockSpec((sb, 1, ML_HEAD_DIM, ML_HEAD_DIM), lambda i: (i // nh, i % nh, 0, 0))
    nspec = pl.BlockSpec((sb, nh, ML_HEAD_DIM), lambda i: (i // nh, 0, 0))
    in_specs = [qspec, pl.BlockSpec((ML_HEAD_DIM, L), lambda i: (i % nh, i // nh)), qspec, qspec,
                pl.BlockSpec((L, GATE_LANES), lambda i: (i // nh, 0)),
                pl.BlockSpec((SUBLANES, L), lambda i: (0, i // nh)),
                pl.BlockSpec((1, L, 1), lambda i: (i % nh, i // nh, 0)), cspec, nspec]
    out_specs = [qspec, cspec, nspec, pl.BlockSpec((1, L, LANES), lambda i: (i % nh, i // nh, 0))]
    out_shapes = [jax.ShapeDtypeStruct((ntok, D_MODEL), F32),
                  jax.ShapeDtypeStruct((nseq, ML_HEADS, ML_HEAD_DIM, ML_HEAD_DIM), F32),
                  jax.ShapeDtypeStruct((nseq, ML_HEADS, ML_HEAD_DIM), F32),
                  jax.ShapeDtypeStruct((ML_HEADS, ntok, LANES), F32)]
    scratch = [pltpu.VMEM((L, ML_HEAD_DIM), F32), pltpu.VMEM((L, LANES), F32),
               pltpu.VMEM((L, LANES), F32), pltpu.VMEM((L, ML_HEAD_DIM), F32)]
    return (nseq // sb) * nh, in_specs, out_specs, out_shapes, scratch


def _post_kernel(x_ref, sc_ref, sh_ref, gt_ref, g_ref, hh_ref, yp_ref, wt_ref, ghead_ref, wbp_ref, wbm_ref, wout_ref,
                 o_ref):
    pool = _dot(yp_ref[...], wbp_ref[...])
    x, hmod = _norm_mod(x_ref, sc_ref, sh_ref, g_ref)
    hb = hmod.astype(BF16)
    nb, tt, d = x_ref.shape
    o = _dot(hb, wt_ref[:, 0:D_MODEL])
    parts = []
    for h in range(ML_HEADS):
        hh = hh_ref[:, h * ML_HEAD_DIM:(h + 1) * ML_HEAD_DIM]
        parts.append(hh * lax.rsqrt(jnp.mean(hh * hh, axis=-1, keepdims=True) + EPS))
    yml = (jnp.concatenate(parts, axis=-1) * ghead_ref[...]) * jax.nn.sigmoid(o)
    gp = _dot(hb, wt_ref[:, D_MODEL:2 * D_MODEL])
    gm = _dot(hb, wt_ref[:, 2 * D_MODEL:3 * D_MODEL])
    merged = jax.nn.sigmoid(gp) * pool + jax.nn.sigmoid(gm) * _dot(yml.astype(BF16), wbm_ref[...])
    y = _dot(merged.astype(BF16), wout_ref[...]).reshape(nb, tt, d)
    o_ref[...] = x_ref[...] + gt_ref[0] * y


def _tile_blocks(x, tile):
    g, t, _ = x.shape
    if t >= tile:
        nb, tt = 1, tile
    else:
        nb, tt = tile // t, t
    return nb, tt, (g // nb) * (t // tt), t // tt


def _post(x, mod_a, mod_b, row0, g_mix, hh, yp, w_tail, g_head, w_bp, w_bm, w_out):
    nb, tt, steps, per = _tile_blocks(x, TOKEN_TILE)
    tm = nb * tt
    xspec = pl.BlockSpec((nb, tt, D_MODEL), lambda i: (i // per, i % per, 0))
    sc, sh, gt = (_mod_spec(k, nb, row0, lambda i: i // per) for k in (_SC1, _SH1, _GT1))
    row = lambda i: (i, 0)
    return pl.pallas_call(
        _post_kernel,
        grid=(steps,),
        in_specs=[xspec, sc, sh, gt, _const_spec((1, D_MODEL)),
                  pl.BlockSpec((tm, D_MODEL), row), pl.BlockSpec((tm, POOL_WIDTH), row),
                  _const_spec((D_MODEL, 3 * D_MODEL)), _const_spec((1, D_MODEL)),
                  _const_spec((POOL_WIDTH, D_MODEL)), _const_spec((D_MODEL, D_MODEL)),
                  _const_spec((D_MODEL, D_MODEL))],
        out_specs=xspec,
        out_shape=jax.ShapeDtypeStruct(x.shape, F32),
        compiler_params=_params(("arbitrary",)),
        name="post",
    )(x, mod_a, mod_a, mod_b, g_mix, hh, yp, w_tail, g_head, w_bp, w_bm, w_out)


_FF_SPLITS = ((0, 768), (768, 1536), (1536, 2304), (2304, D_FF))


def _ffn_kernel(x_ref, sc_ref, sh_ref, gt_ref, g_ref, gfin_ref, wgu_ref, wdn_ref, o_ref):
    _, hmod = _norm_mod(x_ref, sc_ref, sh_ref, g_ref)
    hb = hmod.astype(BF16)
    nb, tt, d = x_ref.shape
    dn = None
    for lo, hi in _FF_SPLITS:
        gate = _dot(hb, wgu_ref[:, lo:hi])
        up = _dot(hb, wgu_ref[:, D_FF + lo:D_FF + hi])
        act = (gate * jax.nn.sigmoid(gate) * up).astype(BF16)
        part = _dot(act, wdn_ref[lo:hi, :])
        dn = part if dn is None else dn + part
    x2 = x_ref[...] + gt_ref[0] * dn.reshape(nb, tt, d)
    ms = jnp.mean(x2 * x2, axis=-1, keepdims=True)
    o_ref[...] = x2 * lax.rsqrt(ms + EPS) * gfin_ref[...]


_N_FFN_IN = 8


def _ffn_mlstm_kernel(*refs, n_ml_in, tt):
    ffn_in = refs[:_N_FFN_IN]
    ml_in = refs[_N_FFN_IN:_N_FFN_IN + n_ml_in]
    o_ref = refs[_N_FFN_IN + n_ml_in]
    ml_rest = refs[_N_FFN_IN + n_ml_in + 1:]
    _mlstm_sample_body(pl.program_id(0) % ML_HEADS, *ml_in, *ml_rest, tt=tt)
    _ffn_kernel(*ffn_in, o_ref)


def _ffn(x, mod, row0, g_ffn, g_final, w_gu, w_down, tile, mlstm_sample=None):
    nb, tt, steps, per = _tile_blocks(x, tile)
    xspec = pl.BlockSpec((nb, tt, D_MODEL), lambda i: (i // per, i % per, 0))
    sc, sh, gt = (_mod_spec(k, nb, row0, lambda i: i // per) for k in (_SC2, _SH2, _GT2))
    in_specs = [xspec, sc, sh, gt, _const_spec((1, D_MODEL)), _const_spec((1, D_MODEL)),
                _const_spec((D_MODEL, 2 * D_FF)), _const_spec((D_FF, D_MODEL))]
    operands = (x, mod, mod, mod, g_ffn, g_final, w_gu, w_down)
    if mlstm_sample is None:
        return pl.pallas_call(
            _ffn_kernel,
            grid=(steps,),
            in_specs=in_specs,
            out_specs=xspec,
            out_shape=jax.ShapeDtypeStruct(x.shape, F32),
            compiler_params=_params(("arbitrary",)),
            name="ffn",
        )(*operands)
    ml_operands, ml_tt = mlstm_sample
    ml_steps, ml_in, ml_out, ml_shapes, ml_scratch = _mlstm_sample_specs(ml_operands[0].shape[0], ml_tt)
    assert ml_steps == steps, "one sample mLSTM step per FFN tile"
    return pl.pallas_call(
        functools.partial(_ffn_mlstm_kernel, n_ml_in=len(ml_in), tt=ml_tt),
        grid=(steps,),
        in_specs=in_specs + ml_in,
        out_specs=[xspec] + ml_out,
        out_shape=[jax.ShapeDtypeStruct(x.shape, F32)] + ml_shapes,
        scratch_shapes=ml_scratch,
        compiler_params=_params(("arbitrary",)),
        name="ffn_mlstm",
    )(*operands, *ml_operands)


def kernel(x_prompt, x_sample, c_prompt, c_sample, state_pool, state_mlstm_c, state_mlstm_n, state_mlstm_m, g_mix, g_ffn, g_final, w_ada, b_ada, w_in, b_igate, b_fgate, w_pool, s_pool, g_head, w_branch_pool, w_branch_mlstm, w_out, w_gate_up, w_down):
    depth = w_in.shape[0]
    assert depth == 1, "single-layer trunk"
    nbatch, seq, _ = x_prompt.shape
    nseq, tt, _ = x_sample.shape
    l = 0

    bgate = jnp.pad(jnp.concatenate([b_igate[l], b_fgate[l]])[None, :], ((0, 0), (0, GATE_LANES - 2 * ML_HEADS)))
    g_mix_r, g_ffn_r, g_fin_r = g_mix[l][None, :], g_ffn[l][None, :], g_final[None, :]
    s_pool_r, g_head_r = s_pool[l][None, :], g_head[l][None, :]

    w_in_t = jnp.swapaxes(w_in, 1, 2)
    ada_operands = (c_sample, c_prompt, w_ada[l], b_ada[l][None, :])
    w_head, w_gate, w_pool_b, mod = _prep(w_in_t, w_pool, *ada_operands)
    row_s, row_p = 0, nseq

    (yp, q, kT, v, gr, hist_p, w_bp, w_bm, w_o, w_gu, w_dn, w_tail, mod2) = _inproj_prompt(
        x_prompt, mod, row_p, g_mix_r, w_head, w_gate, bgate, w_pool_b, s_pool_r,
        (w_branch_pool, w_branch_mlstm, w_out, w_gate_up, w_down), w_in_t, ada_operands)
    hh, c_p, n_p, m_p = _mlstm_prompt(q.reshape(nbatch, seq, D_MODEL), kT, v.reshape(nbatch, seq, D_MODEL), gr)
    hh = hh.reshape(nbatch * seq, D_MODEL)
    x1 = _post(x_prompt, mod, mod2, row_p, g_mix_r, hh, yp, w_tail, g_head_r, w_bp, w_bm, w_o)

    hist_tm = jnp.swapaxes(state_pool[l], 0, 1)
    yp, q, kT, v, gc, gr, k, hist_s = _inproj_sample(x_sample, mod, row_s, g_mix_r, w_head, w_gate, bgate, w_pool_b,
                                                     s_pool_r, hist_tm)
    m0_tok = jnp.repeat(state_mlstm_m[l].astype(F32).T, tt, axis=1)[:, :, None]
    ml_operands = (q, kT, k, v, gc, gr, m0_tok, state_mlstm_c[l].astype(F32), state_mlstm_n[l].astype(F32))
    y_prompt, hh, c_s, n_s, mt = _ffn(x1, mod2, row_p, g_ffn_r, g_fin_r, w_gu, w_dn, PROMPT_FFN_TILE,
                                      (ml_operands, tt))

    x1 = _post(x_sample, mod, mod2, row_s, g_mix_r, hh, yp, w_tail, g_head_r, w_bp, w_bm, w_o)
    y_sample = _ffn(x1, mod2, row_s, g_ffn_r, g_fin_r, w_gu, w_dn, FFN_TILE)

    cd, nd, md = state_mlstm_c.dtype, state_mlstm_n.dtype, state_mlstm_m.dtype
    return (y_prompt, y_sample,
            hist_p[None, :, HIST_ROWS - POOL_HIST:, :],
            c_p.astype(cd)[None], n_p[..., 0].astype(nd)[None], m_p[:, :ML_HEADS, 0].astype(md)[None],
            jnp.swapaxes(hist_s, 0, 1)[None].astype(state_pool.dtype),
            c_s.astype(cd)[None], n_s.astype(nd)[None],
            mt[:, tt - 1::tt, 0].T.astype(md)[None])
```

```python
import functools

import jax
import jax.numpy as jnp
from jax import lax
from jax.experimental import pallas as pl
from jax.experimental.pallas import tpu as pltpu

D_MODEL = 1024
POOL_WINDOWS = (2, 4, 8, 16)
POOL_GROUP_DIM = 128
POOL_WIDTH = 512
POOL_HIST = 15
HIST_ROWS = 16
ML_HEADS = 4
ML_HEAD_DIM = 256
D_FF = 2816
EPS = 1e-6
M_INIT = -1e30
PAST_LEN = 16384
LANES = 128
SUBLANES = 8
BF16_ROWS = 16
GATE_LANES = LANES
AUG = ML_HEAD_DIM + LANES
VMEM_LIMIT = 56 * 1024 * 1024

TOKEN_TILE = 512
PROMPT_INPROJ_TILE = 1024
FFN_TILE = 512
PROMPT_FFN_TILE = 512
PROMPT_CHUNK = 512
MLSTM_TILE = 1024
MLSTM_BATCHES = 1
SAMPLE_SEQ_BLOCK = 16

BF16 = jnp.bfloat16
F32 = jnp.float32

_U0, _Q0, _K0, _V0, _G0 = 0, 512, 1536, 2560, 3584


def _dot(a, b):
    return jnp.dot(a, b, preferred_element_type=F32)


def _const_spec(shape):
    zeros = (0,) * len(shape)
    return pl.BlockSpec(shape, lambda *_: zeros, pipeline_mode=pl.Buffered(1))


def _params(sem):
    return pltpu.CompilerParams(dimension_semantics=sem, vmem_limit_bytes=VMEM_LIMIT)


def _norm_mod(x_ref, sc_ref, sh_ref, g_ref):
    x = x_ref[...]
    nb, tt, d = x.shape
    ms = jnp.mean(x * x, axis=-1, keepdims=True)
    y = x * lax.rsqrt(ms + EPS) * g_ref[...]
    hmod = y * (1.0 + sc_ref[0]) + sh_ref[0]
    return x.reshape(nb * tt, d), hmod.reshape(nb * tt, d)


def _split3(x):
    hi = x.astype(BF16)
    r1 = x - hi.astype(F32)
    mid = r1.astype(BF16)
    lo = (r1 - mid.astype(F32)).astype(BF16)
    return hi, mid, lo


def _cumsum_cols(tri, x):
    n = x.shape[1]
    y = _dot(tri, jnp.concatenate(_split3(x), axis=1))
    return (y[:, 0:n] + y[:, n:2 * n]) + y[:, 2 * n:3 * n]


def _cumsum_rows(x, tri):
    n = x.shape[0]
    y = _dot(jnp.concatenate(_split3(x), axis=0), tri)
    return (y[0:n] + y[n:2 * n]) + y[2 * n:3 * n]


_ADA_COLS = 256
_SH1, _SC1 = 0, 1
_GT1, _SH2, _SC2, _GT2 = 0, 1, 2, 3
_N_MOD_A = 2


def _ada_block(cs_ref, cp_ref, w_ref, b_ref, o_ref):
    c = jnp.concatenate([cs_ref[...], cp_ref[...]], axis=0)
    a = (c * jax.nn.sigmoid(c)).astype(BF16)
    res = _dot(a, w_ref[...].astype(BF16)) + b_ref[...]
    o_ref[0] = res.reshape(res.shape[0], 1, res.shape[1])


def _ada_specs(c_sample, c_prompt, first_block, step_of):
    rows = c_sample.shape[0] + c_prompt.shape[0]
    per_chunk = D_MODEL // _ADA_COLS
    const2 = lambda *g: (0, 0)
    in_specs = [pl.BlockSpec(c_sample.shape, const2), pl.BlockSpec(c_prompt.shape, const2),
                pl.BlockSpec((D_MODEL, _ADA_COLS), lambda *g: (0, first_block + step_of(*g))),
                pl.BlockSpec((1, _ADA_COLS), lambda *g: (0, first_block + step_of(*g)))]
    out_spec = pl.BlockSpec((1, rows, 1, _ADA_COLS), lambda *g: (step_of(*g) // per_chunk, 0, 0, step_of(*g) % per_chunk))
    return in_specs, out_spec, rows


def _mod_spec(k, nb, row0, block_index):
    return pl.BlockSpec((1, nb, 1, D_MODEL), lambda *g: (k, row0 // nb + block_index(*g), 0, 0))


_PREP_COLS = 512


def _prep_kernel(wt_ref, gt_ref, wp_ref, cs_ref, cp_ref, wa_ref, ba_ref, head_ref, gate_ref, pool_ref, mod_ref, *,
                 head_steps):
    @pl.when(pl.program_id(0) < head_steps)
    def _():
        head_ref[...] = wt_ref[0].T.astype(BF16)

    gates = gt_ref[0].T
    pad = jnp.zeros((gates.shape[0], GATE_LANES - gates.shape[1]), F32)
    gate_ref[...] = jnp.concatenate([gates, pad], axis=1).astype(BF16)
    pool_ref[...] = wp_ref[0].astype(BF16)
    _ada_block(cs_ref, cp_ref, wa_ref, ba_ref, mod_ref)


def _prep(w_in_t, w_pool, c_sample, c_prompt, w_ada, b_ada):
    g0 = _G0 + D_MODEL
    head_steps = _G0 // _PREP_COLS
    steps = _N_MOD_A * D_MODEL // _ADA_COLS
    assert steps >= head_steps
    head_blk = lambda j: jnp.minimum(j, head_steps - 1)
    ada_in, ada_out, rows = _ada_specs(c_sample, c_prompt, 0, lambda j: j)
    return pl.pallas_call(
        functools.partial(_prep_kernel, head_steps=head_steps),
        grid=(steps,),
        in_specs=[pl.BlockSpec((1, _PREP_COLS, D_MODEL), lambda j: (0, head_blk(j), 0)),
                  pl.BlockSpec((1, 2 * ML_HEADS, D_MODEL), lambda j: (0, g0 // (2 * ML_HEADS), 0)),
                  pl.BlockSpec(w_pool.shape, lambda j: (0, 0, 0, 0))] + ada_in,
        out_specs=[pl.BlockSpec((D_MODEL, _PREP_COLS), lambda j: (0, head_blk(j))),
                   pl.BlockSpec((D_MODEL, GATE_LANES), lambda j: (0, 0)),
                   pl.BlockSpec(w_pool.shape[1:], lambda j: (0, 0, 0)), ada_out],
        out_shape=[jax.ShapeDtypeStruct((D_MODEL, _G0), BF16),
                   jax.ShapeDtypeStruct((D_MODEL, GATE_LANES), BF16),
                   jax.ShapeDtypeStruct(w_pool.shape[1:], BF16),
                   jax.ShapeDtypeStruct((_N_MOD_A, rows, 1, D_MODEL), F32)],
        compiler_params=_params(("arbitrary",)),
        name="prep",
    )(w_in_t, w_in_t, w_pool, c_sample, c_prompt, w_ada, b_ada)


_TAIL_COLS = 256
_TAIL_BLOCKS = 3 * D_MODEL // _TAIL_COLS
_TAIL_SHIFT_BLOCK = D_MODEL // _TAIL_COLS
_TAIL_SHIFT = 2 * ML_HEADS


def _tail_block(step):
    return jnp.minimum(step, _TAIL_BLOCKS - 1)


def _tail_prep(step, wa_ref, wb_ref, tail_ref):
    c = _tail_block(step)
    a = wa_ref[0]
    shifted = jnp.concatenate([a[_TAIL_SHIFT:, :], wb_ref[0]], axis=0)
    tail_ref[...] = jnp.where(c >= _TAIL_SHIFT_BLOCK, shifted, a).T.astype(BF16)


def _project(hb, w_ref, wg_ref, bg_ref, q_ref, kT_ref, v_ref, gc_ref, gr_ref):
    zg = _dot(hb, wg_ref[...]) + bg_ref[...]
    k = _dot(hb, w_ref[:, _K0:_V0]) * (ML_HEAD_DIM ** -0.5)
    log_f = jnp.minimum(zg, 0.0) - jnp.log1p(jnp.exp(-jnp.abs(zg)))
    lane = lax.broadcasted_iota(jnp.int32, zg.shape, 1)
    gc = jnp.where(lane < ML_HEADS, zg, log_f)
    if gc_ref is not None:
        gc_ref[...] = gc
    gr_ref[...] = gc.T[0:SUBLANES, :]
    kT_ref[...] = k.T.astype(BF16)
    q_ref[...] = _dot(hb, w_ref[:, _Q0:_K0]).astype(BF16)
    v_ref[...] = _dot(hb, w_ref[:, _V0:_G0]).astype(BF16)
    return k


def _pool_deltas(acc_fn, u, cnt_fn):
    deltas = []
    for g, w in enumerate(POOL_WINDOWS):
        ug = u[:, g * POOL_GROUP_DIM:(g + 1) * POOL_GROUP_DIM]
        deltas.append((acc_fn(g, w, ug) / cnt_fn(w) - ug).astype(BF16))
    return deltas


def _pool_project(deltas, wpool_ref, spool_ref):
    outs = [_dot(d, wpool_ref[g]) for g, d in enumerate(deltas)]
    return (jnp.concatenate(outs, axis=-1) * spool_ref[...]).astype(BF16)


def _inproj_prompt_kernel(x_ref, sc_ref, sh_ref, g_ref, w_ref, wg_ref, bg_ref, wpool_ref, spool_ref,
                          wbp_ref, wbm_ref, wout_ref, wgu_ref, wdn_ref, wa_ref, wb_ref, cs_ref, cp_ref, wada_ref, bada_ref,
                          yp_ref, q_ref, kT_ref, v_ref, gr_ref, hout_ref,
                          wbp_o, wbm_o, wout_o, wgu_o, wdn_o, tail_o, mod_o, ext_ref, *, tm):
    t = pl.program_id(1)

    @pl.when(t == 0)
    def _():
        ext_ref[0:HIST_ROWS, :] = jnp.zeros((HIST_ROWS, POOL_WIDTH), F32)

    _, hmod = _norm_mod(x_ref, sc_ref, sh_ref, g_ref)
    hb = hmod.astype(BF16)
    u = _dot(hb, w_ref[:, _U0:_Q0])
    ext_ref[HIST_ROWS:HIST_ROWS + tm, :] = u
    pos = t * tm + lax.broadcasted_iota(jnp.int32, (tm, 1), 0)

    def acc_fn(g, w, ug):
        acc = ug
        for j in range(1, w):
            acc = acc + ext_ref[pl.ds(HIST_ROWS - j, tm), g * POOL_GROUP_DIM:(g + 1) * POOL_GROUP_DIM]
        return acc

    def cnt_fn(w):
        return jnp.minimum(pos + 1, w).astype(F32)

    deltas = _pool_deltas(acc_fn, u, cnt_fn)
    last = ext_ref[tm:tm + HIST_ROWS, :]
    hout_ref[0] = last
    ext_ref[0:HIST_ROWS, :] = last

    _project(hb, w_ref, wg_ref, bg_ref, q_ref, kT_ref.at[0], v_ref, None, gr_ref.at[0])
    yp_ref[...] = _pool_project(deltas, wpool_ref, spool_ref)

    _tail_prep(pl.program_id(0) * pl.num_programs(1) + t, wa_ref, wb_ref, tail_o)
    for src, dst in ((wbp_ref, wbp_o), (wbm_ref, wbm_o), (wout_ref, wout_o), (wgu_ref, wgu_o), (wdn_ref, wdn_o)):
        dst[...] = src[0].astype(BF16)
    _ada_block(cs_ref, cp_ref, wada_ref, bada_ref, mod_o)

def _inproj_sample_kernel(x_ref, sc_ref, sh_ref, g_ref, w_ref, wg_ref, bg_ref, wpool_ref, spool_ref, hist_ref,
                          yp_ref, q_ref, kT_ref, v_ref, gc_ref, gr_ref, k_ref, hout_ref, u_ref, d_ref, *, pos0):
    _, hmod = _norm_mod(x_ref, sc_ref, sh_ref, g_ref)
    hb = hmod.astype(BF16)
    nb, tt = x_ref.shape[0], x_ref.shape[1]
    u = _dot(hb, w_ref[:, _U0:_Q0])
    k_ref[...] = _project(hb, w_ref, wg_ref, bg_ref, q_ref, kT_ref, v_ref, gc_ref, gr_ref).astype(BF16)
    for g, w in enumerate(POOL_WINDOWS):
        cols = slice(g * POOL_GROUP_DIM, (g + 1) * POOL_GROUP_DIM)
        u_ref[g] = u[:, cols]
        new = [u_ref[g, pl.ds(t, nb, stride=tt), :] for t in range(tt)]

        def ext(r):
            return hist_ref[r, :, cols] if r < POOL_HIST else new[r - POOL_HIST]

        for t in range(tt):
            acc = new[t]
            for j in range(1, w):
                acc = acc + ext(POOL_HIST + t - j)
            cnt = float(min(pos0 + t + 1, w))
            d_ref[g, pl.ds(t, nb, stride=tt), :] = acc / cnt - new[t]
        for r in range(POOL_HIST):
            hout_ref[r, :, cols] = ext(r + tt)
    yp_ref[...] = _pool_project([d_ref[g].astype(BF16) for g in range(len(POOL_WINDOWS))], wpool_ref, spool_ref)


def _cast_specs(weights, nt, steps):
    in_specs, out_specs, out_shapes = [], [], []
    for w in weights:
        _, r, c = w.shape
        n = steps
        while r % n or (r // n) % BF16_ROWS:
            n //= 2
        rows = r // n
        idx = lambda b, t, n=n: jnp.minimum(b * nt + t, n - 1)
        in_specs.append(pl.BlockSpec((1, rows, c), lambda b, t, idx=idx: (0, idx(b, t), 0)))
        out_specs.append(pl.BlockSpec((rows, c), lambda b, t, idx=idx: (idx(b, t), 0)))
        out_shapes.append(jax.ShapeDtypeStruct((r, c), BF16))
    return in_specs, out_specs, out_shapes


def _inproj_prompt(x, mod, row0, g_mix, w_head, w_gate, bgate, w_pool, s_pool, later_weights, w_in_t, ada_operands):
    nbatch, seq, _ = x.shape
    tm = PROMPT_INPROJ_TILE
    nt = seq // tm
    ntok = nbatch * seq
    row = lambda b, t: (b * nt + t, 0)
    sc, sh = (_mod_spec(k, 1, row0, lambda b, t: b) for k in (_SC1, _SH1))
    cast_in, cast_out, cast_shapes = _cast_specs(later_weights, nt, nbatch * nt)
    tail_row = lambda b, t: _G0 // _TAIL_COLS + _tail_block(b * nt + t)
    cast_in = cast_in + [
        pl.BlockSpec((1, _TAIL_COLS, D_MODEL), lambda b, t: (0, tail_row(b, t), 0)),
        pl.BlockSpec((1, _TAIL_SHIFT, D_MODEL), lambda b, t: (0, (tail_row(b, t) + 1) * (_TAIL_COLS // _TAIL_SHIFT), 0))]
    cast_out = cast_out + [pl.BlockSpec((D_MODEL, _TAIL_COLS), lambda b, t: (0, _tail_block(b * nt + t)))]
    cast_shapes = cast_shapes + [jax.ShapeDtypeStruct((D_MODEL, _TAIL_BLOCKS * _TAIL_COLS), BF16)]
    c_sample, c_prompt, w_ada, _ = ada_operands
    n_mod_b = w_ada.shape[1] // D_MODEL - _N_MOD_A
    assert nbatch * nt == n_mod_b * D_MODEL // _ADA_COLS, "one modulation block per grid step"
    ada_in, ada_out, mod_rows = _ada_specs(c_sample, c_prompt, _N_MOD_A * D_MODEL // _ADA_COLS, lambda b, t: b * nt + t)
    cast_in, cast_out = cast_in + ada_in, cast_out + [ada_out]
    cast_shapes = cast_shapes + [jax.ShapeDtypeStruct((n_mod_b, mod_rows, 1, D_MODEL), F32)]
    return pl.pallas_call(
        functools.partial(_inproj_prompt_kernel, tm=tm),
        grid=(nbatch, nt),
        in_specs=[pl.BlockSpec((1, tm, D_MODEL), lambda b, t: (b, t, 0)), sc, sh,
                  _const_spec((1, D_MODEL)), _const_spec((D_MODEL, _G0)), _const_spec((D_MODEL, GATE_LANES)),
                  _const_spec((1, GATE_LANES)), _const_spec((4, POOL_GROUP_DIM, POOL_GROUP_DIM)),
                  _const_spec((1, POOL_WIDTH))] + cast_in,
        out_specs=[pl.BlockSpec((tm, POOL_WIDTH), row), pl.BlockSpec((tm, D_MODEL), row),
                   pl.BlockSpec((1, D_MODEL, tm), lambda b, t: (b, 0, t)), pl.BlockSpec((tm, D_MODEL), row),
                   pl.BlockSpec((1, SUBLANES, tm), lambda b, t: (b, 0, t)),
                   pl.BlockSpec((1, HIST_ROWS, POOL_WIDTH), lambda b, t: (b, 0, 0))] + cast_out,
        out_shape=[jax.ShapeDtypeStruct((ntok, POOL_WIDTH), BF16), jax.ShapeDtypeStruct((ntok, D_MODEL), BF16),
                   jax.ShapeDtypeStruct((nbatch, D_MODEL, seq), BF16), jax.ShapeDtypeStruct((ntok, D_MODEL), BF16),
                   jax.ShapeDtypeStruct((nbatch, SUBLANES, seq), F32),
                   jax.ShapeDtypeStruct((nbatch, HIST_ROWS, POOL_WIDTH), F32)] + cast_shapes,
        scratch_shapes=[pltpu.VMEM((tm + HIST_ROWS, POOL_WIDTH), F32)],
        compiler_params=_params(("arbitrary", "arbitrary")),
        name="inproj_prompt",
    )(x, mod, mod, g_mix, w_head, w_gate, bgate, w_pool, s_pool, *later_weights, w_in_t, w_in_t, *ada_operands)


def _inproj_sample(x, mod, row0, g_mix, w_head, w_gate, bgate, w_pool, s_pool, hist_tm):
    nseq, tt, _ = x.shape
    nb = TOKEN_TILE // tt
    tm = nb * tt
    ntok = nseq * tt
    row = lambda i: (i, 0)
    col = lambda i: (0, i)
    sc, sh = (_mod_spec(k, nb, row0, lambda i: i) for k in (_SC1, _SH1))
    hist = pl.BlockSpec((POOL_HIST, nb, POOL_WIDTH), lambda i: (0, i, 0))
    return pl.pallas_call(
        functools.partial(_inproj_sample_kernel, pos0=PAST_LEN),
        grid=(nseq // nb,),
        in_specs=[pl.BlockSpec((nb, tt, D_MODEL), lambda i: (i, 0, 0)), sc, sh,
                  _const_spec((1, D_MODEL)), _const_spec((D_MODEL, _G0)), _const_spec((D_MODEL, GATE_LANES)),
                  _const_spec((1, GATE_LANES)), _const_spec((4, POOL_GROUP_DIM, POOL_GROUP_DIM)),
                  _const_spec((1, POOL_WIDTH)), hist],
        out_specs=[pl.BlockSpec((tm, POOL_WIDTH), row), pl.BlockSpec((tm, D_MODEL), row),
                   pl.BlockSpec((D_MODEL, tm), col), pl.BlockSpec((tm, D_MODEL), row),
                   pl.BlockSpec((tm, GATE_LANES), row), pl.BlockSpec((SUBLANES, tm), col),
                   pl.BlockSpec((tm, D_MODEL), row), hist],
        out_shape=[jax.ShapeDtypeStruct((ntok, POOL_WIDTH), BF16), jax.ShapeDtypeStruct((ntok, D_MODEL), BF16),
                   jax.ShapeDtypeStruct((D_MODEL, ntok), BF16), jax.ShapeDtypeStruct((ntok, D_MODEL), BF16),
                   jax.ShapeDtypeStruct((ntok, GATE_LANES), F32), jax.ShapeDtypeStruct((SUBLANES, ntok), F32),
                   jax.ShapeDtypeStruct((ntok, D_MODEL), BF16),
                   jax.ShapeDtypeStruct((POOL_HIST, nseq, POOL_WIDTH), F32)],
        scratch_shapes=[pltpu.VMEM((len(POOL_WINDOWS), tm, POOL_GROUP_DIM), F32)] * 2,
        compiler_params=_params(("arbitrary",)),
        name="inproj_sample",
    )(x, mod, mod, g_mix, w_head, w_gate, bgate, w_pool, s_pool, hist_tm)


def _mlstm_chunk(q_ref, kT_ref, v_ref, gr_ref, hh_ref, cn_ref, m_ref, r0, chunk):
    rows = slice(r0, r0 + chunk)
    gr = gr_ref[:, rows]
    ri = lax.broadcasted_iota(jnp.int32, (chunk, chunk), 0)
    ci = lax.broadcasted_iota(jnp.int32, (chunk, chunk), 1)
    causal = ri >= ci
    brow = _cumsum_rows(gr, (ri <= ci).astype(BF16))
    bcol = brow.T
    ones = jnp.ones((chunk, LANES), BF16)
    tile_to = lambda x, width: jnp.concatenate([x] * (width // LANES), axis=-1)
    stage = []
    for h in range(ML_HEADS):
        hs = slice(h * ML_HEAD_DIM, (h + 1) * ML_HEAD_DIM)
        b_r = brow[ML_HEADS + h:ML_HEADS + h + 1, :]
        g_r = gr[h:h + 1, :] - b_r
        m_prev = m_ref[h:h + 1, 0:1]
        cn = cn_ref[h]
        q = q_ref[rows, hs]
        kT = kT_ref[hs, rows]
        gm = jnp.where(causal, g_r, -jnp.inf)
        big_m = jnp.maximum(m_prev, jnp.max(gm, axis=-1, keepdims=True))
        m_rep = jnp.broadcast_to(big_m, (chunk, LANES))
        mt_rep = jnp.broadcast_to(bcol[:, ML_HEADS + h:ML_HEADS + h + 1] + big_m, (chunk, LANES))
        qkc = _dot(q, jnp.concatenate([kT, cn.astype(BF16)], axis=1))
        s = qkc[:, :chunk] * jnp.exp(gm - tile_to(m_rep, chunk))
        b_last = b_r[:, chunk - 1:chunk]
        m_new = b_last + jnp.maximum(m_prev, jnp.max(g_r, axis=-1, keepdims=True))
        decay = jnp.exp((b_last + m_prev) - m_new)
        w_end = jnp.exp((g_r + b_last) - m_new)
        lhs = jnp.concatenate([s.astype(BF16), (kT.astype(F32) * w_end).astype(BF16)], axis=0)
        stage.append((lhs, qkc[:, chunk:], m_prev, m_rep, mt_rep, m_new, decay, cn))

    for h, (lhs, qc, m_prev, m_rep, mt_rep, m_new, decay, cn) in enumerate(stage):
        hs = slice(h * ML_HEAD_DIM, (h + 1) * ML_HEAD_DIM)
        vaug = jnp.concatenate([v_ref[rows, hs], ones], axis=-1)
        both = _dot(lhs, vaug)
        sva = both[0:chunk]
        w_inter = jnp.exp(m_prev - m_rep)
        den = w_inter * qc[:, ML_HEAD_DIM:] + sva[:, ML_HEAD_DIM:]
        rinv = 1.0 / jnp.maximum(jnp.abs(den), jnp.exp(-mt_rep))
        hh_ref[rows, hs] = ((tile_to(w_inter, ML_HEAD_DIM) * qc[:, :ML_HEAD_DIM] + sva[:, :ML_HEAD_DIM])
                            * tile_to(rinv, ML_HEAD_DIM))
        cn_ref[h] = decay * cn + both[chunk:]
        m_ref[h:h + 1, :] = jnp.broadcast_to(m_new, (1, LANES))


def _mlstm_prompt_kernel(q_ref, kT_ref, v_ref, gr_ref, hh_ref, cout_ref, nout_ref, mout_ref, cn_ref, m_ref, *, chunk):
    t = pl.program_id(1)
    nb = q_ref.shape[0]

    @pl.when(t == 0)
    def _():
        cn_ref[...] = jnp.zeros(cn_ref.shape, F32)
        m_ref[...] = jnp.full(m_ref.shape, M_INIT, F32)

    for r0 in range(0, q_ref.shape[1], chunk):
        for s in range(nb):
            _mlstm_chunk(q_ref.at[s], kT_ref.at[s], v_ref.at[s], gr_ref.at[s], hh_ref.at[s], cn_ref.at[s],
                         m_ref.at[s], r0, chunk)

    @pl.when(t == pl.num_programs(1) - 1)
    def _():
        cout_ref[...] = cn_ref[:, :, :, 0:ML_HEAD_DIM]
        nout_ref[...] = cn_ref[:, :, :, ML_HEAD_DIM:AUG]
        mout_ref[...] = m_ref[...]


def _mlstm_prompt(q, kT, v, gr):
    nbatch, seq, _ = q.shape
    tm, nb = MLSTM_TILE, MLSTM_BATCHES
    row = pl.BlockSpec((nb, tm, D_MODEL), lambda b, t: (b, t, 0))
    state = lambda last: pl.BlockSpec((nb, ML_HEADS, ML_HEAD_DIM, last), lambda b, t: (b, 0, 0, 0))
    return pl.pallas_call(
        functools.partial(_mlstm_prompt_kernel, chunk=PROMPT_CHUNK),
        grid=(nbatch // nb, seq // tm),
        in_specs=[row, pl.BlockSpec((nb, D_MODEL, tm), lambda b, t: (b, 0, t)), row,
                  pl.BlockSpec((nb, SUBLANES, tm), lambda b, t: (b, 0, t))],
        out_specs=[row, state(ML_HEAD_DIM), state(LANES), pl.BlockSpec((nb, SUBLANES, LANES), lambda b, t: (b, 0, 0))],
        out_shape=[jax.ShapeDtypeStruct((nbatch, seq, D_MODEL), F32),
                   jax.ShapeDtypeStruct((nbatch, ML_HEADS, ML_HEAD_DIM, ML_HEAD_DIM), F32),
                   jax.ShapeDtypeStruct((nbatch, ML_HEADS, ML_HEAD_DIM, LANES), F32),
                   jax.ShapeDtypeStruct((nbatch, SUBLANES, LANES), F32)],
        scratch_shapes=[pltpu.VMEM((nb, ML_HEADS, ML_HEAD_DIM, AUG), F32), pltpu.VMEM((nb, SUBLANES, LANES), F32)],
        compiler_params=_params(("arbitrary", "arbitrary")),
        name="mlstm_prompt",
    )(q, kT, v, gr)


def _last_in_group(x, group):
    rows = x.shape[0]
    x3 = jnp.broadcast_to(x, (rows, LANES)).reshape(rows // group, group, LANES)
    last = jnp.broadcast_to(x3[:, group - 1:group, :], x3.shape)
    return last.reshape(rows, LANES)[:, 0:1]


def _mlstm_sample_body(h, q_ref, kT_ref, k_ref, v_ref, gc_ref, gr_ref, m0_ref, c_ref, n_ref,
                       hh_ref, cout_ref, nout_ref, mt_ref, ni_ref, qn_ref, dec_ref, wk_ref, *, tt):
    L = q_ref.shape[0]
    nseq = L // tt
    gc = gc_ref[...]
    gr = gr_ref[...]
    ri = lax.broadcasted_iota(jnp.int32, (L, L), 0)
    ci = lax.broadcasted_iota(jnp.int32, (L, L), 1)
    same = (ri // tt) == (ci // tt)
    mask = same & (ri >= ci)
    bcol = _cumsum_cols(mask.astype(BF16), gc)
    brow = _cumsum_rows(gr, (same & (ri <= ci)).astype(BF16))
    lane = lax.broadcasted_iota(jnp.int32, (L, GATE_LANES), 1)
    sub = lax.broadcasted_iota(jnp.int32, (SUBLANES, L), 0)
    pick_col = lambda arr, idx: jnp.sum(jnp.where(lane == idx, arr, 0.0), axis=-1, keepdims=True)
    pick_row = lambda arr, idx: jnp.sum(jnp.where(sub == idx, arr, 0.0), axis=0, keepdims=True)
    ig_c = pick_col(gc, h)
    b_c = pick_col(bcol, ML_HEADS + h)
    ig_r = pick_row(gr, h)
    b_r = pick_row(brow, ML_HEADS + h)
    m_prev = m0_ref[0]

    for j in range(nseq):
        rows = slice(j * tt, (j + 1) * tt)
        qj = q_ref[rows, :]
        ni_ref[rows, :] = _dot(qj, c_ref[j, 0].astype(BF16))
        nj = n_ref[j, pl.ds(h, 1), :].astype(BF16).astype(F32)
        qn = jnp.sum(qj.astype(F32) * nj, axis=-1, keepdims=True)
        qn_ref[rows, :] = jnp.broadcast_to(qn, (tt, LANES))

    q = q_ref[...]
    kT = kT_ref[...]
    v = v_ref[...]
    qk = _dot(q, kT)
    logd = jnp.where(mask, (b_c - b_r) + ig_r, -jnp.inf)
    a_c = b_c + m_prev
    m_t = jnp.maximum(a_c, jnp.max(logd, axis=-1, keepdims=True))
    w_inter = jnp.exp(a_c - m_t)
    s = qk * jnp.exp(logd - m_t)
    num = w_inter * ni_ref[...] + _dot(s.astype(BF16), v)
    den = w_inter * qn_ref[:, 0:1] + jnp.sum(s, axis=-1, keepdims=True)
    hh_ref[...] = num / jnp.maximum(jnp.abs(den), jnp.exp(-m_t))
    m_new = _last_in_group(m_t, tt)
    decay = jnp.exp(_last_in_group(a_c, tt) - m_new)
    w_end = jnp.exp((_last_in_group(b_c, tt) - b_c) + ig_c - m_new)
    mt_ref[0] = jnp.broadcast_to(m_t, (L, LANES))
    dec_ref[...] = jnp.broadcast_to(decay, (L, LANES))
    wv = w_end * v.astype(F32)
    wk_ref[...] = w_end.astype(BF16).astype(F32) * k_ref[...].astype(F32)
    rowi = lax.broadcasted_iota(jnp.int32, (L, 1), 0)

    for j in range(nseq):
        rows = slice(j * tt, (j + 1) * tt)
        upd = _dot(kT, jnp.where((rowi // tt) == j, wv, 0.0).astype(BF16))
        dj = dec_ref[j * tt:j * tt + 1, 0:1]
        cout_ref[j, 0] = dj * c_ref[j, 0] + upd
        nout_ref[j, pl.ds(h, 1), :] = (dj * n_ref[j, pl.ds(h, 1), :]
                                       + jnp.sum(wk_ref[rows, :], axis=0, keepdims=True))


def _mlstm_sample_specs(ntok, tt):
    nseq = ntok // tt
    sb = SAMPLE_SEQ_BLOCK
    L = sb * tt
    nh = ML_HEADS
    qspec = pl.BlockSpec((L, ML_HEAD_DIM), lambda i: (i // nh, i % nh))
    cspec = pl.BlockSpec((sb, 1, ML_HEAD_DIM, ML_HEAD_DIM), lambda i: (i // nh, i % nh, 0, 0))
    nspec = pl.BlockSpec((sb, nh, ML_HEAD_DIM), lambda i: (i // nh, 0, 0))
    in_specs = [qspec, pl.BlockSpec((ML_HEAD_DIM, L), lambda i: (i % nh, i // nh)), qspec, qspec,
                pl.BlockSpec((L, GATE_LANES), lambda i: (i // nh, 0)),
                pl.BlockSpec((SUBLANES, L), lambda i: (0, i // nh)),
                pl.BlockSpec((1, L, 1), lambda i: (i % nh, i // nh, 0)), cspec, nspec]
    out_specs = [qspec, cspec, nspec, pl.BlockSpec((1, L, LANES), lambda i: (i % nh, i // nh, 0))]
    out_shapes = [jax.ShapeDtypeStruct((ntok, D_MODEL), F32),
                  jax.ShapeDtypeStruct((nseq, ML_HEADS, ML_HEAD_DIM, ML_HEAD_DIM), F32),
                  jax.ShapeDtypeStruct((nseq, ML_HEADS, ML_HEAD_DIM), F32),
                  jax.ShapeDtypeStruct((ML_HEADS, ntok, LANES), F32)]
    scratch = [pltpu.VMEM((L, ML_HEAD_DIM), F32), pltpu.VMEM((L, LANES), F32),
               pltpu.VMEM((L, LANES), F32), pltpu.VMEM((L, ML_HEAD_DIM), F32)]
    return (nseq // sb) * nh, in_specs, out_specs, out_shapes, scratch


def _post_kernel(x_ref, sc_ref, sh_ref, gt_ref, g_ref, hh_ref, yp_ref, wt_ref, ghead_ref, wbp_ref, wbm_ref, wout_ref,
                 o_ref):
    pool = _dot(yp_ref[...], wbp_ref[...])
    x, hmod = _norm_mod(x_ref, sc_ref, sh_ref, g_ref)
    hb = hmod.astype(BF16)
    nb, tt, d = x_ref.shape
    o = _dot(hb, wt_ref[:, 0:D_MODEL])
    parts = []
    for h in range(ML_HEADS):
        hh = hh_ref[:, h * ML_HEAD_DIM:(h + 1) * ML_HEAD_DIM]
        parts.append(hh * lax.rsqrt(jnp.mean(hh * hh, axis=-1, keepdims=True) + EPS))
    yml = (jnp.concatenate(parts, axis=-1) * ghead_ref[...]) * jax.nn.sigmoid(o)
    gp = _dot(hb, wt_ref[:, D_MODEL:2 * D_MODEL])
    gm = _dot(hb, wt_ref[:, 2 * D_MODEL:3 * D_MODEL])
    merged = jax.nn.sigmoid(gp) * pool + jax.nn.sigmoid(gm) * _dot(yml.astype(BF16), wbm_ref[...])
    y = _dot(merged.astype(BF16), wout_ref[...]).reshape(nb, tt, d)
    o_ref[...] = x_ref[...] + gt_ref[0] * y


def _tile_blocks(x, tile):
    g, t, _ = x.shape
    if t >= tile:
        nb, tt = 1, tile
    else:
        nb, tt = tile // t, t
    return nb, tt, (g // nb) * (t // tt), t // tt


def _post(x, mod_a, mod_b, row0, g_mix, hh, yp, w_tail, g_head, w_bp, w_bm, w_out):
    nb, tt, steps, per = _tile_blocks(x, 256)
    tm = nb * tt
    xspec = pl.BlockSpec((nb, tt, D_MODEL), lambda i: (i // per, i % per, 0))
    sc, sh, gt = (_mod_spec(k, nb, row0, lambda i: i // per) for k in (_SC1, _SH1, _GT1))
    row = lambda i: (i, 0)
    return pl.pallas_call(
        _post_kernel,
        grid=(steps,),
        in_specs=[xspec, sc, sh, gt, _const_spec((1, D_MODEL)),
                  pl.BlockSpec((tm, D_MODEL), row), pl.BlockSpec((tm, POOL_WIDTH), row),
                  _const_spec((D_MODEL, 3 * D_MODEL)), _const_spec((1, D_MODEL)),
                  _const_spec((POOL_WIDTH, D_MODEL)), _const_spec((D_MODEL, D_MODEL)),
                  _const_spec((D_MODEL, D_MODEL))],
        out_specs=xspec,
        out_shape=jax.ShapeDtypeStruct(x.shape, F32),
        compiler_params=_params(("arbitrary",)),
        name="post",
    )(x, mod_a, mod_a, mod_b, g_mix, hh, yp, w_tail, g_head, w_bp, w_bm, w_out)


_FF_SPLITS = ((0, 768), (768, 1536), (1536, 2304), (2304, D_FF))


def _ffn_kernel(x_ref, sc_ref, sh_ref, gt_ref, g_ref, gfin_ref, wgu_ref, wdn_ref, o_ref):
    _, hmod = _norm_mod(x_ref, sc_ref, sh_ref, g_ref)
    hb = hmod.astype(BF16)
    nb, tt, d = x_ref.shape
    dn = None
    for lo, hi in _FF_SPLITS:
        gate = _dot(hb, wgu_ref[:, lo:hi])
        up = _dot(hb, wgu_ref[:, D_FF + lo:D_FF + hi])
        act = (gate * jax.nn.sigmoid(gate) * up).astype(BF16)
        part = _dot(act, wdn_ref[lo:hi, :])
        dn = part if dn is None else dn + part
    x2 = x_ref[...] + gt_ref[0] * dn.reshape(nb, tt, d)
    ms = jnp.mean(x2 * x2, axis=-1, keepdims=True)
    o_ref[...] = x2 * lax.rsqrt(ms + EPS) * gfin_ref[...]


_N_FFN_IN = 8


def _ffn_mlstm_kernel(*refs, n_ml_in, tt):
    ffn_in = refs[:_N_FFN_IN]
    ml_in = refs[_N_FFN_IN:_N_FFN_IN + n_ml_in]
    o_ref = refs[_N_FFN_IN + n_ml_in]
    ml_rest = refs[_N_FFN_IN + n_ml_in + 1:]
    _mlstm_sample_body(pl.program_id(0) % ML_HEADS, *ml_in, *ml_rest, tt=tt)
    _ffn_kernel(*ffn_in, o_ref)


def _ffn(x, mod, row0, g_ffn, g_final, w_gu, w_down, tile, mlstm_sample=None):
    nb, tt, steps, per = _tile_blocks(x, tile)
    xspec = pl.BlockSpec((nb, tt, D_MODEL), lambda i: (i // per, i % per, 0))
    sc, sh, gt = (_mod_spec(k, nb, row0, lambda i: i // per) for k in (_SC2, _SH2, _GT2))
    in_specs = [xspec, sc, sh, gt, _const_spec((1, D_MODEL)), _const_spec((1, D_MODEL)),
                _const_spec((D_MODEL, 2 * D_FF)), _const_spec((D_FF, D_MODEL))]
    operands = (x, mod, mod, mod, g_ffn, g_final, w_gu, w_down)
    if mlstm_sample is None:
        return pl.pallas_call(
            _ffn_kernel,
            grid=(steps,),
            in_specs=in_specs,
            out_specs=xspec,
            out_shape=jax.ShapeDtypeStruct(x.shape, F32),
            compiler_params=_params(("arbitrary",)),
            name="ffn",
        )(*operands)
    ml_operands, ml_tt = mlstm_sample
    ml_steps, ml_in, ml_out, ml_shapes, ml_scratch = _mlstm_sample_specs(ml_operands[0].shape[0], ml_tt)
    assert ml_steps == steps, "one sample mLSTM step per FFN tile"
    return pl.pallas_call(
        functools.partial(_ffn_mlstm_kernel, n_ml_in=len(ml_in), tt=ml_tt),
        grid=(steps,),
        in_specs=in_specs + ml_in,
        out_specs=[xspec] + ml_out,
        out_shape=[jax.ShapeDtypeStruct(x.shape, F32)] + ml_shapes,
        scratch_shapes=ml_scratch,
        compiler_params=_params(("arbitrary",)),
        name="ffn_mlstm",
    )(*operands, *ml_operands)


def kernel(x_prompt, x_sample, c_prompt, c_sample, state_pool, state_mlstm_c, state_mlstm_n, state_mlstm_m, g_mix, g_ffn, g_final, w_ada, b_ada, w_in, b_igate, b_fgate, w_pool, s_pool, g_head, w_branch_pool, w_branch_mlstm, w_out, w_gate_up, w_down):
    depth = w_in.shape[0]
    assert depth == 1, "single-layer trunk"
    nbatch, seq, _ = x_prompt.shape
    nseq, tt, _ = x_sample.shape
    l = 0

    bgate = jnp.pad(jnp.concatenate([b_igate[l], b_fgate[l]])[None, :], ((0, 0), (0, GATE_LANES - 2 * ML_HEADS)))
    g_mix_r, g_ffn_r, g_fin_r = g_mix[l][None, :], g_ffn[l][None, :], g_final[None, :]
    s_pool_r, g_head_r = s_pool[l][None, :], g_head[l][None, :]

    w_in_t = jnp.swapaxes(w_in, 1, 2)
    ada_operands = (c_sample, c_prompt, w_ada[l], b_ada[l][None, :])
    w_head, w_gate, w_pool_b, mod = _prep(w_in_t, w_pool, *ada_operands)
    row_s, row_p = 0, nseq

    (yp, q, kT, v, gr, hist_p, w_bp, w_bm, w_o, w_gu, w_dn, w_tail, mod2) = _inproj_prompt(
        x_prompt, mod, row_p, g_mix_r, w_head, w_gate, bgate, w_pool_b, s_pool_r,
        (w_branch_pool, w_branch_mlstm, w_out, w_gate_up, w_down), w_in_t, ada_operands)
    hh, c_p, n_p, m_p = _mlstm_prompt(q.reshape(nbatch, seq, D_MODEL), kT, v.reshape(nbatch, seq, D_MODEL), gr)
    hh = hh.reshape(nbatch * seq, D_MODEL)
    x1 = _post(x_prompt, mod, mod2, row_p, g_mix_r, hh, yp, w_tail, g_head_r, w_bp, w_bm, w_o)

    hist_tm = jnp.swapaxes(state_pool[l], 0, 1)
    yp, q, kT, v, gc, gr, k, hist_s = _inproj_sample(x_sample, mod, row_s, g_mix_r, w_head, w_gate, bgate, w_pool_b,
                                                     s_pool_r, hist_tm)
    m0_tok = jnp.repeat(state_mlstm_m[l].astype(F32).T, tt, axis=1)[:, :, None]
    ml_operands = (q, kT, k, v, gc, gr, m0_tok, state_mlstm_c[l].astype(F32), state_mlstm_n[l].astype(F32))
    y_prompt, hh, c_s, n_s, mt = _ffn(x1, mod2, row_p, g_ffn_r, g_fin_r, w_gu, w_dn, PROMPT_FFN_TILE,
                                      (ml_operands, tt))

    x1 = _post(x_sample, mod, mod2, row_s, g_mix_r, hh, yp, w_tail, g_head_r, w_bp, w_bm, w_o)
    y_sample = _ffn(x1, mod2, row_s, g_ffn_r, g_fin_r, w_gu, w_dn, FFN_TILE)

    cd, nd, md = state_mlstm_c.dtype, state_mlstm_n.dtype, state_mlstm_m.dtype
    return (y_prompt, y_sample,
            hist_p[None, :, HIST_ROWS - POOL_HIST:, :],
            c_p.astype(cd)[None], n_p[..., 0].astype(nd)[None], m_p[:, :ML_HEADS, 0].astype(md)[None],
            jnp.swapaxes(hist_s, 0, 1)[None].astype(state_pool.dtype),
            c_s.astype(cd)[None], n_s.astype(nd)[None],
            mt[:, tt - 1::tt, 0].T.astype(md)[None])
```

```python
import functools

import jax
import jax.numpy as jnp
from jax import lax
from jax.experimental import pallas as pl
from jax.experimental.pallas import tpu as pltpu

D_MODEL = 1024
POOL_WINDOWS = (2, 4, 8, 16)
POOL_GROUP_DIM = 128
POOL_WIDTH = 512
POOL_HIST = 15
HIST_ROWS = 16
ML_HEADS = 4
ML_HEAD_DIM = 256
D_FF = 2816
EPS = 1e-6
M_INIT = -1e30
PAST_LEN = 16384
LANES = 128
SUBLANES = 8
BF16_ROWS = 16
GATE_LANES = LANES
AUG = ML_HEAD_DIM + LANES
VMEM_LIMIT = 56 * 1024 * 1024

TOKEN_TILE = 512
POST_TILE = 256
PROMPT_INPROJ_TILE = 1024
FFN_TILE = 512
PROMPT_FFN_TILE = 512
PROMPT_CHUNK = 512
MLSTM_TILE = 512
MLSTM_BATCHES = 1
SAMPLE_SEQ_BLOCK = 16

BF16 = jnp.bfloat16
F32 = jnp.float32

_U0, _Q0, _K0, _V0, _G0 = 0, 512, 1536, 2560, 3584


def _dot(a, b):
    return jnp.dot(a, b, preferred_element_type=F32)


def _const_spec(shape):
    zeros = (0,) * len(shape)
    return pl.BlockSpec(shape, lambda *_: zeros, pipeline_mode=pl.Buffered(1))


def _params(sem):
    return pltpu.CompilerParams(dimension_semantics=sem, vmem_limit_bytes=VMEM_LIMIT)


def _norm_mod(x_ref, sc_ref, sh_ref, g_ref):
    x = x_ref[...]
    nb, tt, d = x.shape
    ms = jnp.mean(x * x, axis=-1, keepdims=True)
    y = x * lax.rsqrt(ms + EPS) * g_ref[...]
    hmod = y * (1.0 + sc_ref[0]) + sh_ref[0]
    return x.reshape(nb * tt, d), hmod.reshape(nb * tt, d)


def _split3(x):
    hi = x.astype(BF16)
    r1 = x - hi.astype(F32)
    mid = r1.astype(BF16)
    lo = (r1 - mid.astype(F32)).astype(BF16)
    return hi, mid, lo


def _cumsum_cols(tri, x):
    n = x.shape[1]
    y = _dot(tri, jnp.concatenate(_split3(x), axis=1))
    return (y[:, 0:n] + y[:, n:2 * n]) + y[:, 2 * n:3 * n]


def _cumsum_rows(x, tri):
    n = x.shape[0]
    y = _dot(jnp.concatenate(_split3(x), axis=0), tri)
    return (y[0:n] + y[n:2 * n]) + y[2 * n:3 * n]


_ADA_COLS = 256
_SH1, _SC1 = 0, 1
_GT1, _SH2, _SC2, _GT2 = 0, 1, 2, 3
_N_MOD_A = 2


def _ada_block(cs_ref, cp_ref, w_ref, b_ref, o_ref):
    c = jnp.concatenate([cs_ref[...], cp_ref[...]], axis=0)
    a = (c * jax.nn.sigmoid(c)).astype(BF16)
    res = _dot(a, w_ref[...].astype(BF16)) + b_ref[...]
    o_ref[0] = res.reshape(res.shape[0], 1, res.shape[1])


def _ada_specs(c_sample, c_prompt, first_block, step_of):
    rows = c_sample.shape[0] + c_prompt.shape[0]
    per_chunk = D_MODEL // _ADA_COLS
    const2 = lambda *g: (0, 0)
    in_specs = [pl.BlockSpec(c_sample.shape, const2), pl.BlockSpec(c_prompt.shape, const2),
                pl.BlockSpec((D_MODEL, _ADA_COLS), lambda *g: (0, first_block + step_of(*g))),
                pl.BlockSpec((1, _ADA_COLS), lambda *g: (0, first_block + step_of(*g)))]
    out_spec = pl.BlockSpec((1, rows, 1, _ADA_COLS), lambda *g: (step_of(*g) // per_chunk, 0, 0, step_of(*g) % per_chunk))
    return in_specs, out_spec, rows


def _mod_spec(k, nb, row0, block_index):
    return pl.BlockSpec((1, nb, 1, D_MODEL), lambda *g: (k, row0 // nb + block_index(*g), 0, 0))


_PREP_COLS = 512


def _prep_kernel(wt_ref, gt_ref, wp_ref, cs_ref, cp_ref, wa_ref, ba_ref, head_ref, gate_ref, pool_ref, mod_ref, *,
                 head_steps):
    @pl.when(pl.program_id(0) < head_steps)
    def _():
        head_ref[...] = wt_ref[0].T.astype(BF16)

    gates = gt_ref[0].T
    pad = jnp.zeros((gates.shape[0], GATE_LANES - gates.shape[1]), F32)
    gate_ref[...] = jnp.concatenate([gates, pad], axis=1).astype(BF16)
    pool_ref[...] = wp_ref[0].astype(BF16)
    _ada_block(cs_ref, cp_ref, wa_ref, ba_ref, mod_ref)


def _prep(w_in_t, w_pool, c_sample, c_prompt, w_ada, b_ada):
    g0 = _G0 + D_MODEL
    head_steps = _G0 // _PREP_COLS
    steps = _N_MOD_A * D_MODEL // _ADA_COLS
    assert steps >= head_steps
    head_blk = lambda j: jnp.minimum(j, head_steps - 1)
    ada_in, ada_out, rows = _ada_specs(c_sample, c_prompt, 0, lambda j: j)
    return pl.pallas_call(
        functools.partial(_prep_kernel, head_steps=head_steps),
        grid=(steps,),
        in_specs=[pl.BlockSpec((1, _PREP_COLS, D_MODEL), lambda j: (0, head_blk(j), 0)),
                  pl.BlockSpec((1, 2 * ML_HEADS, D_MODEL), lambda j: (0, g0 // (2 * ML_HEADS), 0)),
                  pl.BlockSpec(w_pool.shape, lambda j: (0, 0, 0, 0))] + ada_in,
        out_specs=[pl.BlockSpec((D_MODEL, _PREP_COLS), lambda j: (0, head_blk(j))),
                   pl.BlockSpec((D_MODEL, GATE_LANES), lambda j: (0, 0)),
                   pl.BlockSpec(w_pool.shape[1:], lambda j: (0, 0, 0)), ada_out],
        out_shape=[jax.ShapeDtypeStruct((D_MODEL, _G0), BF16),
                   jax.ShapeDtypeStruct((D_MODEL, GATE_LANES), BF16),
                   jax.ShapeDtypeStruct(w_pool.shape[1:], BF16),
                   jax.ShapeDtypeStruct((_N_MOD_A, rows, 1, D_MODEL), F32)],
        compiler_params=_params(("arbitrary",)),
        name="prep",
    )(w_in_t, w_in_t, w_pool, c_sample, c_prompt, w_ada, b_ada)


_TAIL_COLS = 256
_TAIL_BLOCKS = 3 * D_MODEL // _TAIL_COLS
_TAIL_SHIFT_BLOCK = D_MODEL // _TAIL_COLS
_TAIL_SHIFT = 2 * ML_HEADS


def _tail_block(step):
    return jnp.minimum(step, _TAIL_BLOCKS - 1)


def _tail_prep(step, wa_ref, wb_ref, tail_ref):
    c = _tail_block(step)
    a = wa_ref[0]
    shifted = jnp.concatenate([a[_TAIL_SHIFT:, :], wb_ref[0]], axis=0)
    tail_ref[...] = jnp.where(c >= _TAIL_SHIFT_BLOCK, shifted, a).T.astype(BF16)


def _project(hb, w_ref, wg_ref, bg_ref, q_ref, kT_ref, v_ref, gc_ref, gr_ref):
    zg = _dot(hb, wg_ref[...]) + bg_ref[...]
    k = _dot(hb, w_ref[:, _K0:_V0]) * (ML_HEAD_DIM ** -0.5)
    log_f = jnp.minimum(zg, 0.0) - jnp.log1p(jnp.exp(-jnp.abs(zg)))
    lane = lax.broadcasted_iota(jnp.int32, zg.shape, 1)
    gc = jnp.where(lane < ML_HEADS, zg, log_f)
    if gc_ref is not None:
        gc_ref[...] = gc
    gr_ref[...] = gc.T[0:SUBLANES, :]
    kT_ref[...] = k.T.astype(BF16)
    q_ref[...] = _dot(hb, w_ref[:, _Q0:_K0]).astype(BF16)
    v_ref[...] = _dot(hb, w_ref[:, _V0:_G0]).astype(BF16)
    return k


def _pool_deltas(acc_fn, u, cnt_fn):
    deltas = []
    for g, w in enumerate(POOL_WINDOWS):
        ug = u[:, g * POOL_GROUP_DIM:(g + 1) * POOL_GROUP_DIM]
        deltas.append((acc_fn(g, w, ug) / cnt_fn(w) - ug).astype(BF16))
    return deltas


def _pool_project(deltas, wpool_ref, spool_ref):
    outs = [_dot(d, wpool_ref[g]) for g, d in enumerate(deltas)]
    return (jnp.concatenate(outs, axis=-1) * spool_ref[...]).astype(BF16)


def _inproj_prompt_kernel(x_ref, sc_ref, sh_ref, g_ref, w_ref, wg_ref, bg_ref, wpool_ref, spool_ref,
                          wbp_ref, wbm_ref, wout_ref, wgu_ref, wdn_ref, wa_ref, wb_ref, cs_ref, cp_ref, wada_ref, bada_ref,
                          yp_ref, q_ref, kT_ref, v_ref, gr_ref, hout_ref,
                          wbp_o, wbm_o, wout_o, wgu_o, wdn_o, tail_o, mod_o, ext_ref, *, tm):
    t = pl.program_id(1)

    @pl.when(t == 0)
    def _():
        ext_ref[0:HIST_ROWS, :] = jnp.zeros((HIST_ROWS, POOL_WIDTH), F32)

    _, hmod = _norm_mod(x_ref, sc_ref, sh_ref, g_ref)
    hb = hmod.astype(BF16)
    u = _dot(hb, w_ref[:, _U0:_Q0])
    ext_ref[HIST_ROWS:HIST_ROWS + tm, :] = u
    pos = t * tm + lax.broadcasted_iota(jnp.int32, (tm, 1), 0)

    def acc_fn(g, w, ug):
        acc = ug
        for j in range(1, w):
            acc = acc + ext_ref[pl.ds(HIST_ROWS - j, tm), g * POOL_GROUP_DIM:(g + 1) * POOL_GROUP_DIM]
        return acc

    def cnt_fn(w):
        return jnp.minimum(pos + 1, w).astype(F32)

    deltas = _pool_deltas(acc_fn, u, cnt_fn)
    last = ext_ref[tm:tm + HIST_ROWS, :]
    hout_ref[0] = last
    ext_ref[0:HIST_ROWS, :] = last

    _project(hb, w_ref, wg_ref, bg_ref, q_ref, kT_ref.at[0], v_ref, None, gr_ref.at[0])
    yp_ref[...] = _pool_project(deltas, wpool_ref, spool_ref)

    _tail_prep(pl.program_id(0) * pl.num_programs(1) + t, wa_ref, wb_ref, tail_o)
    for src, dst in ((wbp_ref, wbp_o), (wbm_ref, wbm_o), (wout_ref, wout_o), (wgu_ref, wgu_o), (wdn_ref, wdn_o)):
        dst[...] = src[0].astype(BF16)
    _ada_block(cs_ref, cp_ref, wada_ref, bada_ref, mod_o)

def _inproj_sample_kernel(x_ref, sc_ref, sh_ref, g_ref, w_ref, wg_ref, bg_ref, wpool_ref, spool_ref, hist_ref,
                          yp_ref, q_ref, kT_ref, v_ref, gc_ref, gr_ref, k_ref, hout_ref, u_ref, d_ref, *, pos0):
    _, hmod = _norm_mod(x_ref, sc_ref, sh_ref, g_ref)
    hb = hmod.astype(BF16)
    nb, tt = x_ref.shape[0], x_ref.shape[1]
    u = _dot(hb, w_ref[:, _U0:_Q0])
    k_ref[...] = _project(hb, w_ref, wg_ref, bg_ref, q_ref, kT_ref, v_ref, gc_ref, gr_ref).astype(BF16)
    for g, w in enumerate(POOL_WINDOWS):
        cols = slice(g * POOL_GROUP_DIM, (g + 1) * POOL_GROUP_DIM)
        u_ref[g] = u[:, cols]
        new = [u_ref[g, pl.ds(t, nb, stride=tt), :] for t in range(tt)]

        def ext(r):
            return hist_ref[r, :, cols] if r < POOL_HIST else new[r - POOL_HIST]

        for t in range(tt):
            acc = new[t]
            for j in range(1, w):
                acc = acc + ext(POOL_HIST + t - j)
            cnt = float(min(pos0 + t + 1, w))
            d_ref[g, pl.ds(t, nb, stride=tt), :] = acc / cnt - new[t]
        for r in range(POOL_HIST):
            hout_ref[r, :, cols] = ext(r + tt)
    yp_ref[...] = _pool_project([d_ref[g].astype(BF16) for g in range(len(POOL_WINDOWS))], wpool_ref, spool_ref)


def _cast_specs(weights, nt, steps):
    in_specs, out_specs, out_shapes = [], [], []
    for w in weights:
        _, r, c = w.shape
        n = steps
        while r % n or (r // n) % BF16_ROWS:
            n //= 2
        rows = r // n
        idx = lambda b, t, n=n: jnp.minimum(b * nt + t, n - 1)
        in_specs.append(pl.BlockSpec((1, rows, c), lambda b, t, idx=idx: (0, idx(b, t), 0)))
        out_specs.append(pl.BlockSpec((rows, c), lambda b, t, idx=idx: (idx(b, t), 0)))
        out_shapes.append(jax.ShapeDtypeStruct((r, c), BF16))
    return in_specs, out_specs, out_shapes


def _inproj_prompt(x, mod, row0, g_mix, w_head, w_gate, bgate, w_pool, s_pool, later_weights, w_in_t, ada_operands):
    nbatch, seq, _ = x.shape
    tm = PROMPT_INPROJ_TILE
    nt = seq // tm
    ntok = nbatch * seq
    row = lambda b, t: (b * nt + t, 0)
    sc, sh = (_mod_spec(k, 1, row0, lambda b, t: b) for k in (_SC1, _SH1))
    cast_in, cast_out, cast_shapes = _cast_specs(later_weights, nt, nbatch * nt)
    tail_row = lambda b, t: _G0 // _TAIL_COLS + _tail_block(b * nt + t)
    cast_in = cast_in + [
        pl.BlockSpec((1, _TAIL_COLS, D_MODEL), lambda b, t: (0, tail_row(b, t), 0)),
        pl.BlockSpec((1, _TAIL_SHIFT, D_MODEL), lambda b, t: (0, (tail_row(b, t) + 1) * (_TAIL_COLS // _TAIL_SHIFT), 0))]
    cast_out = cast_out + [pl.BlockSpec((D_MODEL, _TAIL_COLS), lambda b, t: (0, _tail_block(b * nt + t)))]
    cast_shapes = cast_shapes + [jax.ShapeDtypeStruct((D_MODEL, _TAIL_BLOCKS * _TAIL_COLS), BF16)]
    c_sample, c_prompt, w_ada, _ = ada_operands
    n_mod_b = w_ada.shape[1] // D_MODEL - _N_MOD_A
    assert nbatch * nt == n_mod_b * D_MODEL // _ADA_COLS, "one modulation block per grid step"
    ada_in, ada_out, mod_rows = _ada_specs(c_sample, c_prompt, _N_MOD_A * D_MODEL // _ADA_COLS, lambda b, t: b * nt + t)
    cast_in, cast_out = cast_in + ada_in, cast_out + [ada_out]
    cast_shapes = cast_shapes + [jax.ShapeDtypeStruct((n_mod_b, mod_rows, 1, D_MODEL), F32)]
    return pl.pallas_call(
        functools.partial(_inproj_prompt_kernel, tm=tm),
        grid=(nbatch, nt),
        in_specs=[pl.BlockSpec((1, tm, D_MODEL), lambda b, t: (b, t, 0)), sc, sh,
                  _const_spec((1, D_MODEL)), _const_spec((D_MODEL, _G0)), _const_spec((D_MODEL, GATE_LANES)),
                  _const_spec((1, GATE_LANES)), _const_spec((4, POOL_GROUP_DIM, POOL_GROUP_DIM)),
                  _const_spec((1, POOL_WIDTH))] + cast_in,
        out_specs=[pl.BlockSpec((tm, POOL_WIDTH), row), pl.BlockSpec((tm, D_MODEL), row),
                   pl.BlockSpec((1, D_MODEL, tm), lambda b, t: (b, 0, t)), pl.BlockSpec((tm, D_MODEL), row),
                   pl.BlockSpec((1, SUBLANES, tm), lambda b, t: (b, 0, t)),
                   pl.BlockSpec((1, HIST_ROWS, POOL_WIDTH), lambda b, t: (b, 0, 0))] + cast_out,
        out_shape=[jax.ShapeDtypeStruct((ntok, POOL_WIDTH), BF16), jax.ShapeDtypeStruct((ntok, D_MODEL), BF16),
                   jax.ShapeDtypeStruct((nbatch, D_MODEL, seq), BF16), jax.ShapeDtypeStruct((ntok, D_MODEL), BF16),
                   jax.ShapeDtypeStruct((nbatch, SUBLANES, seq), F32),
                   jax.ShapeDtypeStruct((nbatch, HIST_ROWS, POOL_WIDTH), F32)] + cast_shapes,
        scratch_shapes=[pltpu.VMEM((tm + HIST_ROWS, POOL_WIDTH), F32)],
        compiler_params=_params(("arbitrary", "arbitrary")),
        name="inproj_prompt",
    )(x, mod, mod, g_mix, w_head, w_gate, bgate, w_pool, s_pool, *later_weights, w_in_t, w_in_t, *ada_operands)


def _inproj_sample(x, mod, row0, g_mix, w_head, w_gate, bgate, w_pool, s_pool, hist_tm):
    nseq, tt, _ = x.shape
    nb = TOKEN_TILE // tt
    tm = nb * tt
    ntok = nseq * tt
    row = lambda i: (i, 0)
    col = lambda i: (0, i)
    sc, sh = (_mod_spec(k, nb, row0, lambda i: i) for k in (_SC1, _SH1))
    hist = pl.BlockSpec((POOL_HIST, nb, POOL_WIDTH), lambda i: (0, i, 0))
    return pl.pallas_call(
        functools.partial(_inproj_sample_kernel, pos0=PAST_LEN),
        grid=(nseq // nb,),
        in_specs=[pl.BlockSpec((nb, tt, D_MODEL), lambda i: (i, 0, 0)), sc, sh,
                  _const_spec((1, D_MODEL)), _const_spec((D_MODEL, _G0)), _const_spec((D_MODEL, GATE_LANES)),
                  _const_spec((1, GATE_LANES)), _const_spec((4, POOL_GROUP_DIM, POOL_GROUP_DIM)),
                  _const_spec((1, POOL_WIDTH)), hist],
        out_specs=[pl.BlockSpec((tm, POOL_WIDTH), row), pl.BlockSpec((tm, D_MODEL), row),
                   pl.BlockSpec((D_MODEL, tm), col), pl.BlockSpec((tm, D_MODEL), row),
                   pl.BlockSpec((tm, GATE_LANES), row), pl.BlockSpec((SUBLANES, tm), col),
                   pl.BlockSpec((tm, D_MODEL), row), hist],
        out_shape=[jax.ShapeDtypeStruct((ntok, POOL_WIDTH), BF16), jax.ShapeDtypeStruct((ntok, D_MODEL), BF16),
                   jax.ShapeDtypeStruct((D_MODEL, ntok), BF16), jax.ShapeDtypeStruct((ntok, D_MODEL), BF16),
                   jax.ShapeDtypeStruct((ntok, GATE_LANES), F32), jax.ShapeDtypeStruct((SUBLANES, ntok), F32),
                   jax.ShapeDtypeStruct((ntok, D_MODEL), BF16),
                   jax.ShapeDtypeStruct((POOL_HIST, nseq, POOL_WIDTH), F32)],
        scratch_shapes=[pltpu.VMEM((len(POOL_WINDOWS), tm, POOL_GROUP_DIM), F32)] * 2,
        compiler_params=_params(("arbitrary",)),
        name="inproj_sample",
    )(x, mod, mod, g_mix, w_head, w_gate, bgate, w_pool, s_pool, hist_tm)


def _mlstm_chunk(q_ref, kT_ref, v_ref, gr_ref, hh_ref, cn_ref, m_ref, r0, chunk):
    rows = slice(r0, r0 + chunk)
    gr = gr_ref[:, rows]
    ri = lax.broadcasted_iota(jnp.int32, (chunk, chunk), 0)
    ci = lax.broadcasted_iota(jnp.int32, (chunk, chunk), 1)
    causal = ri >= ci
    brow = _cumsum_rows(gr, (ri <= ci).astype(BF16))
    bcol = brow.T
    ones = jnp.ones((chunk, LANES), BF16)
    tile_to = lambda x, width: jnp.concatenate([x] * (width // LANES), axis=-1)
    stage = []
    for h in range(ML_HEADS):
        hs = slice(h * ML_HEAD_DIM, (h + 1) * ML_HEAD_DIM)
        b_r = brow[ML_HEADS + h:ML_HEADS + h + 1, :]
        g_r = gr[h:h + 1, :] - b_r
        m_prev = m_ref[h:h + 1, 0:1]
        cn = cn_ref[h]
        q = q_ref[rows, hs]
        kT = kT_ref[hs, rows]
        gm = jnp.where(causal, g_r, -jnp.inf)
        big_m = jnp.maximum(m_prev, jnp.max(gm, axis=-1, keepdims=True))
        m_rep = jnp.broadcast_to(big_m, (chunk, LANES))
        mt_rep = jnp.broadcast_to(bcol[:, ML_HEADS + h:ML_HEADS + h + 1] + big_m, (chunk, LANES))
        qkc = _dot(q, jnp.concatenate([kT, cn.astype(BF16)], axis=1))
        s = qkc[:, :chunk] * jnp.exp(gm - tile_to(m_rep, chunk))
        b_last = b_r[:, chunk - 1:chunk]
        m_new = b_last + jnp.maximum(m_prev, jnp.max(g_r, axis=-1, keepdims=True))
        decay = jnp.exp((b_last + m_prev) - m_new)
        w_end = jnp.exp((g_r + b_last) - m_new)
        lhs = jnp.concatenate([s.astype(BF16), (kT.astype(F32) * w_end).astype(BF16)], axis=0)
        stage.append((lhs, qkc[:, chunk:], m_prev, m_rep, mt_rep, m_new, decay, cn))

    for h, (lhs, qc, m_prev, m_rep, mt_rep, m_new, decay, cn) in enumerate(stage):
        hs = slice(h * ML_HEAD_DIM, (h + 1) * ML_HEAD_DIM)
        vaug = jnp.concatenate([v_ref[rows, hs], ones], axis=-1)
        both = _dot(lhs, vaug)
        sva = both[0:chunk]
        w_inter = jnp.exp(m_prev - m_rep)
        den = w_inter * qc[:, ML_HEAD_DIM:] + sva[:, ML_HEAD_DIM:]
        rinv = 1.0 / jnp.maximum(jnp.abs(den), jnp.exp(-mt_rep))
        hh_ref[rows, hs] = ((tile_to(w_inter, ML_HEAD_DIM) * qc[:, :ML_HEAD_DIM] + sva[:, :ML_HEAD_DIM])
                            * tile_to(rinv, ML_HEAD_DIM))
        cn_ref[h] = decay * cn + both[chunk:]
        m_ref[h:h + 1, :] = jnp.broadcast_to(m_new, (1, LANES))


def _mlstm_prompt_kernel(q_ref, kT_ref, v_ref, gr_ref, hh_ref, cout_ref, nout_ref, mout_ref, cn_ref, m_ref, *, chunk):
    t = pl.program_id(1)
    nb = q_ref.shape[0]

    @pl.when(t == 0)
    def _():
        cn_ref[...] = jnp.zeros(cn_ref.shape, F32)
        m_ref[...] = jnp.full(m_ref.shape, M_INIT, F32)

    for r0 in range(0, q_ref.shape[1], chunk):
        for s in range(nb):
            _mlstm_chunk(q_ref.at[s], kT_ref.at[s], v_ref.at[s], gr_ref.at[s], hh_ref.at[s], cn_ref.at[s],
                         m_ref.at[s], r0, chunk)

    @pl.when(t == pl.num_programs(1) - 1)
    def _():
        cout_ref[...] = cn_ref[:, :, :, 0:ML_HEAD_DIM]
        nout_ref[...] = cn_ref[:, :, :, ML_HEAD_DIM:AUG]
        mout_ref[...] = m_ref[...]


def _mlstm_prompt(q, kT, v, gr):
    nbatch, seq, _ = q.shape
    tm, nb = MLSTM_TILE, MLSTM_BATCHES
    row = pl.BlockSpec((nb, tm, D_MODEL), lambda b, t: (b, t, 0))
    state = lambda last: pl.BlockSpec((nb, ML_HEADS, ML_HEAD_DIM, last), lambda b, t: (b, 0, 0, 0))
    return pl.pallas_call(
        functools.partial(_mlstm_prompt_kernel, chunk=PROMPT_CHUNK),
        grid=(nbatch // nb, seq // tm),
        in_specs=[row, pl.BlockSpec((nb, D_MODEL, tm), lambda b, t: (b, 0, t)), row,
                  pl.BlockSpec((nb, SUBLANES, tm), lambda b, t: (b, 0, t))],
        out_specs=[row, state(ML_HEAD_DIM), state(LANES), pl.BlockSpec((nb, SUBLANES, LANES), lambda b, t: (b, 0, 0))],
        out_shape=[jax.ShapeDtypeStruct((nbatch, seq, D_MODEL), F32),
                   jax.ShapeDtypeStruct((nbatch, ML_HEADS, ML_HEAD_DIM, ML_HEAD_DIM), F32),
                   jax.ShapeDtypeStruct((nbatch, ML_HEADS, ML_HEAD_DIM, LANES), F32),
                   jax.ShapeDtypeStruct((nbatch, SUBLANES, LANES), F32)],
        scratch_shapes=[pltpu.VMEM((nb, ML_HEADS, ML_HEAD_DIM, AUG), F32), pltpu.VMEM((nb, SUBLANES, LANES), F32)],
        compiler_params=_params(("arbitrary", "arbitrary")),
        name="mlstm_prompt",
    )(q, kT, v, gr)


def _last_in_group(x, group):
    rows = x.shape[0]
    x3 = jnp.broadcast_to(x, (rows, LANES)).reshape(rows // group, group, LANES)
    last = jnp.broadcast_to(x3[:, group - 1:group, :], x3.shape)
    return last.reshape(rows, LANES)[:, 0:1]


def _mlstm_sample_body(h, q_ref, kT_ref, k_ref, v_ref, gc_ref, gr_ref, m0_ref, c_ref, n_ref,
                       hh_ref, cout_ref, nout_ref, mt_ref, ni_ref, qn_ref, dec_ref, wk_ref, *, tt):
    L = q_ref.shape[0]
    nseq = L // tt
    gc = gc_ref[...]
    gr = gr_ref[...]
    ri = lax.broadcasted_iota(jnp.int32, (L, L), 0)
    ci = lax.broadcasted_iota(jnp.int32, (L, L), 1)
    same = (ri // tt) == (ci // tt)
    mask = same & (ri >= ci)
    bcol = _cumsum_cols(mask.astype(BF16), gc)
    brow = _cumsum_rows(gr, (same & (ri <= ci)).astype(BF16))
    lane = lax.broadcasted_iota(jnp.int32, (L, GATE_LANES), 1)
    sub = lax.broadcasted_iota(jnp.int32, (SUBLANES, L), 0)
    pick_col = lambda arr, idx: jnp.sum(jnp.where(lane == idx, arr, 0.0), axis=-1, keepdims=True)
    pick_row = lambda arr, idx: jnp.sum(jnp.where(sub == idx, arr, 0.0), axis=0, keepdims=True)
    ig_c = pick_col(gc, h)
    b_c = pick_col(bcol, ML_HEADS + h)
    ig_r = pick_row(gr, h)
    b_r = pick_row(brow, ML_HEADS + h)
    m_prev = m0_ref[0]

    for j in range(nseq):
        rows = slice(j * tt, (j + 1) * tt)
        qj = q_ref[rows, :]
        ni_ref[rows, :] = _dot(qj, c_ref[j, 0].astype(BF16))
        nj = n_ref[j, pl.ds(h, 1), :].astype(BF16).astype(F32)
        qn = jnp.sum(qj.astype(F32) * nj, axis=-1, keepdims=True)
        qn_ref[rows, :] = jnp.broadcast_to(qn, (tt, LANES))

    q = q_ref[...]
    kT = kT_ref[...]
    v = v_ref[...]
    qk = _dot(q, kT)
    logd = jnp.where(mask, (b_c - b_r) + ig_r, -jnp.inf)
    a_c = b_c + m_prev
    m_t = jnp.maximum(a_c, jnp.max(logd, axis=-1, keepdims=True))
    w_inter = jnp.exp(a_c - m_t)
    s = qk * jnp.exp(logd - m_t)
    num = w_inter * ni_ref[...] + _dot(s.astype(BF16), v)
    den = w_inter * qn_ref[:, 0:1] + jnp.sum(s, axis=-1, keepdims=True)
    hh_ref[...] = num / jnp.maximum(jnp.abs(den), jnp.exp(-m_t))
    m_new = _last_in_group(m_t, tt)
    decay = jnp.exp(_last_in_group(a_c, tt) - m_new)
    w_end = jnp.exp((_last_in_group(b_c, tt) - b_c) + ig_c - m_new)
    mt_ref[0] = jnp.broadcast_to(m_t, (L, LANES))
    dec_ref[...] = jnp.broadcast_to(decay, (L, LANES))
    wv = w_end * v.astype(F32)
    wk_ref[...] = w_end.astype(BF16).astype(F32) * k_ref[...].astype(F32)
    rowi = lax.broadcasted_iota(jnp.int32, (L, 1), 0)

    for j in range(nseq):
        rows = slice(j * tt, (j + 1) * tt)
        upd = _dot(kT, jnp.where((rowi // tt) == j, wv, 0.0).astype(BF16))
        dj = dec_ref[j * tt:j * tt + 1, 0:1]
        cout_ref[j, 0] = dj * c_ref[j, 0] + upd
        nout_ref[j, pl.ds(h, 1), :] = (dj * n_ref[j, pl.ds(h, 1), :]
                                       + jnp.sum(wk_ref[rows, :], axis=0, keepdims=True))


def _mlstm_sample_specs(ntok, tt):
    nseq = ntok // tt
    sb = SAMPLE_SEQ_BLOCK
    L = sb * tt
    nh = ML_HEADS
    qspec = pl.BlockSpec((L, ML_HEAD_DIM), lambda i: (i // nh, i % nh))
    cspec = pl.BlockSpec((sb, 1, ML_HEAD_DIM, ML_HEAD_DIM), lambda i: (i // nh, i % nh, 0, 0))
    nspec = pl.BlockSpec((sb, nh, ML_HEAD_DIM), lambda i: (i // nh, 0, 0))
    in_specs = [qspec, pl.BlockSpec((ML_HEAD_DIM, L), lambda i: (i % nh, i // nh)), qspec, qspec,
                pl.BlockSpec((L, GATE_LANES), lambda i: (i // nh, 0)),
                pl.BlockSpec((SUBLANES, L), lambda i: (0, i // nh)),
                pl.BlockSpec((1, L, 1), lambda i: (i % nh, i // nh, 0)), cspec, nspec]
    out_specs = [qspec, cspec, nspec, pl.BlockSpec((1, L, LANES), lambda i: (i % nh, i // nh, 0))]
    out_shapes = [jax.ShapeDtypeStruct((ntok, D_MODEL), F32),
                  jax.ShapeDtypeStruct((nseq, ML_HEADS, ML_HEAD_DIM, ML_HEAD_DIM), F32),
                  jax.ShapeDtypeStruct((nseq, ML_HEADS, ML_HEAD_DIM), F32),
                  jax.ShapeDtypeStruct((ML_HEADS, ntok, LANES), F32)]
    scratch = [pltpu.VMEM((L, ML_HEAD_DIM), F32), pltpu.VMEM((L, LANES), F32),
               pltpu.VMEM((L, LANES), F32), pltpu.VMEM((L, ML_HEAD_DIM), F32)]
    return (nseq // sb) * nh, in_specs, out_specs, out_shapes, scratch


def _post_kernel(x_ref, sc_ref, sh_ref, gt_ref, g_ref, hh_ref, yp_ref, wt_ref, ghead_ref, wbp_ref, wbm_ref, wout_ref,
                 o_ref):
    pool = _dot(yp_ref[...], wbp_ref[...])
    x, hmod = _norm_mod(x_ref, sc_ref, sh_ref, g_ref)
    hb = hmod.astype(BF16)
    nb, tt, d = x_ref.shape
    o = _dot(hb, wt_ref[:, 0:D_MODEL])
    parts = []
    for h in range(ML_HEADS):
        hh = hh_ref[:, h * ML_HEAD_DIM:(h + 1) * ML_HEAD_DIM]
        parts.append(hh * lax.rsqrt(jnp.mean(hh * hh, axis=-1, keepdims=True) + EPS))
    yml = (jnp.concatenate(parts, axis=-1) * ghead_ref[...]) * jax.nn.sigmoid(o)
    gp = _dot(hb, wt_ref[:, D_MODEL:2 * D_MODEL])
    gm = _dot(hb, wt_ref[:, 2 * D_MODEL:3 * D_MODEL])
    merged = jax.nn.sigmoid(gp) * pool + jax.nn.sigmoid(gm) * _dot(yml.astype(BF16), wbm_ref[...])
    y = _dot(merged.astype(BF16), wout_ref[...]).reshape(nb, tt, d)
    o_ref[...] = x_ref[...] + gt_ref[0] * y


def _tile_blocks(x, tile):
    g, t, _ = x.shape
    if t >= tile:
        nb, tt = 1, tile
    else:
        nb, tt = tile // t, t
    return nb, tt, (g // nb) * (t // tt), t // tt


def _post(x, mod_a, mod_b, row0, g_mix, hh, yp, w_tail, g_head, w_bp, w_bm, w_out):
    nb, tt, steps, per = _tile_blocks(x, POST_TILE)
    tm = nb * tt
    xspec = pl.BlockSpec((nb, tt, D_MODEL), lambda i: (i // per, i % per, 0))
    sc, sh, gt = (_mod_spec(k, nb, row0, lambda i: i // per) for k in (_SC1, _SH1, _GT1))
    row = lambda i: (i, 0)
    return pl.pallas_call(
        _post_kernel,
        grid=(steps,),
        in_specs=[xspec, sc, sh, gt, _const_spec((1, D_MODEL)),
                  pl.BlockSpec((tm, D_MODEL), row), pl.BlockSpec((tm, POOL_WIDTH), row),
                  _const_spec((D_MODEL, 3 * D_MODEL)), _const_spec((1, D_MODEL)),
                  _const_spec((POOL_WIDTH, D_MODEL)), _const_spec((D_MODEL, D_MODEL)),
                  _const_spec((D_MODEL, D_MODEL))],
        out_specs=xspec,
        out_shape=jax.ShapeDtypeStruct(x.shape, F32),
        compiler_params=_params(("arbitrary",)),
        name="post",
    )(x, mod_a, mod_a, mod_b, g_mix, hh, yp, w_tail, g_head, w_bp, w_bm, w_out)


_FF_SPLITS = ((0, 768), (768, 1536), (1536, 2304), (2304, D_FF))


def _ffn_kernel(x_ref, sc_ref, sh_ref, gt_ref, g_ref, gfin_ref, wgu_ref, wdn_ref, o_ref):
    _, hmod = _norm_mod(x_ref, sc_ref, sh_ref, g_ref)
    hb = hmod.astype(BF16)
    nb, tt, d = x_ref.shape
    dn = None
    for lo, hi in _FF_SPLITS:
        gate = _dot(hb, wgu_ref[:, lo:hi])
        up = _dot(hb, wgu_ref[:, D_FF + lo:D_FF + hi])
        act = (gate * jax.nn.sigmoid(gate) * up).astype(BF16)
        part = _dot(act, wdn_ref[lo:hi, :])
        dn = part if dn is None else dn + part
    x2 = x_ref[...] + gt_ref[0] * dn.reshape(nb, tt, d)
    ms = jnp.mean(x2 * x2, axis=-1, keepdims=True)
    o_ref[...] = x2 * lax.rsqrt(ms + EPS) * gfin_ref[...]


_N_FFN_IN = 8


def _ffn_mlstm_kernel(*refs, n_ml_in, tt):
    ffn_in = refs[:_N_FFN_IN]
    ml_in = refs[_N_FFN_IN:_N_FFN_IN + n_ml_in]
    o_ref = refs[_N_FFN_IN + n_ml_in]
    ml_rest = refs[_N_FFN_IN + n_ml_in + 1:]
    _mlstm_sample_body(pl.program_id(0) % ML_HEADS, *ml_in, *ml_rest, tt=tt)
    _ffn_kernel(*ffn_in, o_ref)


def _ffn(x, mod, row0, g_ffn, g_final, w_gu, w_down, tile, mlstm_sample=None):
    nb, tt, steps, per = _tile_blocks(x, tile)
    xspec = pl.BlockSpec((nb, tt, D_MODEL), lambda i: (i // per, i % per, 0))
    sc, sh, gt = (_mod_spec(k, nb, row0, lambda i: i // per) for k in (_SC2, _SH2, _GT2))
    in_specs = [xspec, sc, sh, gt, _const_spec((1, D_MODEL)), _const_spec((1, D_MODEL)),
                _const_spec((D_MODEL, 2 * D_FF)), _const_spec((D_FF, D_MODEL))]
    operands = (x, mod, mod, mod, g_ffn, g_final, w_gu, w_down)
    if mlstm_sample is None:
        return pl.pallas_call(
            _ffn_kernel,
            grid=(steps,),
            in_specs=in_specs,
            out_specs=xspec,
            out_shape=jax.ShapeDtypeStruct(x.shape, F32),
            compiler_params=_params(("arbitrary",)),
            name="ffn",
        )(*operands)
    ml_operands, ml_tt = mlstm_sample
    ml_steps, ml_in, ml_out, ml_shapes, ml_scratch = _mlstm_sample_specs(ml_operands[0].shape[0], ml_tt)
    assert ml_steps == steps, "one sample mLSTM step per FFN tile"
    return pl.pallas_call(
        functools.partial(_ffn_mlstm_kernel, n_ml_in=len(ml_in), tt=ml_tt),
        grid=(steps,),
        in_specs=in_specs + ml_in,
        out_specs=[xspec] + ml_out,
        out_shape=[jax.ShapeDtypeStruct(x.shape, F32)] + ml_shapes,
        scratch_shapes=ml_scratch,
        compiler_params=_params(("arbitrary",)),
        name="ffn_mlstm",
    )(*operands, *ml_operands)


def kernel(x_prompt, x_sample, c_prompt, c_sample, state_pool, state_mlstm_c, state_mlstm_n, state_mlstm_m, g_mix, g_ffn, g_final, w_ada, b_ada, w_in, b_igate, b_fgate, w_pool, s_pool, g_head, w_branch_pool, w_branch_mlstm, w_out, w_gate_up, w_down):
    depth = w_in.shape[0]
    assert depth == 1, "single-layer trunk"
    nbatch, seq, _ = x_prompt.shape
    nseq, tt, _ = x_sample.shape
    l = 0

    bgate = jnp.pad(jnp.concatenate([b_igate[l], b_fgate[l]])[None, :], ((0, 0), (0, GATE_LANES - 2 * ML_HEADS)))
    g_mix_r, g_ffn_r, g_fin_r = g_mix[l][None, :], g_ffn[l][None, :], g_final[None, :]
    s_pool_r, g_head_r = s_pool[l][None, :], g_head[l][None, :]

    w_in_t = jnp.swapaxes(w_in, 1, 2)
    ada_operands = (c_sample, c_prompt, w_ada[l], b_ada[l][None, :])
    w_head, w_gate, w_pool_b, mod = _prep(w_in_t, w_pool, *ada_operands)
    row_s, row_p = 0, nseq

    (yp, q, kT, v, gr, hist_p, w_bp, w_bm, w_o, w_gu, w_dn, w_tail, mod2) = _inproj_prompt(
        x_prompt, mod, row_p, g_mix_r, w_head, w_gate, bgate, w_pool_b, s_pool_r,
        (w_branch_pool, w_branch_mlstm, w_out, w_gate_up, w_down), w_in_t, ada_operands)
    hh, c_p, n_p, m_p = _mlstm_prompt(q.reshape(nbatch, seq, D_MODEL), kT, v.reshape(nbatch, seq, D_MODEL), gr)
    hh = hh.reshape(nbatch * seq, D_MODEL)
    x1 = _post(x_prompt, mod, mod2, row_p, g_mix_r, hh, yp, w_tail, g_head_r, w_bp, w_bm, w_o)

    hist_tm = jnp.swapaxes(state_pool[l], 0, 1)
    yp, q, kT, v, gc, gr, k, hist_s = _inproj_sample(x_sample, mod, row_s, g_mix_r, w_head, w_gate, bgate, w_pool_b,
                                                     s_pool_r, hist_tm)
    m0_tok = jnp.repeat(state_mlstm_m[l].astype(F32).T, tt, axis=1)[:, :, None]
    ml_operands = (q, kT, k, v, gc, gr, m0_tok, state_mlstm_c[l].astype(F32), state_mlstm_n[l].astype(F32))
    y_prompt, hh, c_s, n_s, mt = _ffn(x1, mod2, row_p, g_ffn_r, g_fin_r, w_gu, w_dn, PROMPT_FFN_TILE,
                                      (ml_operands, tt))

    x1 = _post(x_sample, mod, mod2, row_s, g_mix_r, hh, yp, w_tail, g_head_r, w_bp, w_bm, w_o)
    y_sample = _ffn(x1, mod2, row_s, g_ffn_r, g_fin_r, w_gu, w_dn, FFN_TILE)

    cd, nd, md = state_mlstm_c.dtype, state_mlstm_n.dtype, state_mlstm_m.dtype
    return (y_prompt, y_sample,
            hist_p[None, :, HIST_ROWS - POOL_HIST:, :],
            c_p.astype(cd)[None], n_p[..., 0].astype(nd)[None], m_p[:, :ML_HEADS, 0].astype(md)[None],
            jnp.swapaxes(hist_s, 0, 1)[None].astype(state_pool.dtype),
            c_s.astype(cd)[None], n_s.astype(nd)[None],
            mt[:, tt - 1::tt, 0].T.astype(md)[None])
```

```python
import functools

import jax
import jax.numpy as jnp
from jax import lax
from jax.experimental import pallas as pl
from jax.experimental.pallas import tpu as pltpu

D_MODEL = 1024
POOL_WINDOWS = (2, 4, 8, 16)
POOL_GROUP_DIM = 128
POOL_WIDTH = 512
POOL_HIST = 15
HIST_ROWS = 16
ML_HEADS = 4
ML_HEAD_DIM = 256
D_FF = 2816
EPS = 1e-6
M_INIT = -1e30
PAST_LEN = 16384
LANES = 128
SUBLANES = 8
BF16_ROWS = 16
GATE_LANES = LANES
AUG = ML_HEAD_DIM + LANES
VMEM_LIMIT = 56 * 1024 * 1024

TOKEN_TILE = 512
POST_TILE = 256
PROMPT_INPROJ_TILE = 1024
FFN_TILE = 512
PROMPT_FFN_TILE = 512
PROMPT_CHUNK = 512
MLSTM_TILE = 512
MLSTM_BATCHES = 1
SAMPLE_SEQ_BLOCK = 16

BF16 = jnp.bfloat16
F32 = jnp.float32

_U0, _Q0, _K0, _V0, _G0 = 0, 512, 1536, 2560, 3584


def _dot(a, b):
    return jnp.dot(a, b, preferred_element_type=F32)


def _const_spec(shape):
    zeros = (0,) * len(shape)
    return pl.BlockSpec(shape, lambda *_: zeros, pipeline_mode=pl.Buffered(1))


def _params(sem):
    return pltpu.CompilerParams(dimension_semantics=sem, vmem_limit_bytes=VMEM_LIMIT)


def _norm_mod(x_ref, sc_ref, sh_ref, g_ref):
    x = x_ref[...]
    nb, tt, d = x.shape
    ms = jnp.mean(x * x, axis=-1, keepdims=True)
    y = x * lax.rsqrt(ms + EPS) * g_ref[...]
    hmod = y * (1.0 + sc_ref[0]) + sh_ref[0]
    return x.reshape(nb * tt, d), hmod.reshape(nb * tt, d)


def _split3(x):
    hi = x.astype(BF16)
    r1 = x - hi.astype(F32)
    mid = r1.astype(BF16)
    lo = (r1 - mid.astype(F32)).astype(BF16)
    return hi, mid, lo


def _cumsum_cols(tri, x):
    n = x.shape[1]
    y = _dot(tri, jnp.concatenate(_split3(x), axis=1))
    return (y[:, 0:n] + y[:, n:2 * n]) + y[:, 2 * n:3 * n]


def _cumsum_rows(x, tri):
    n = x.shape[0]
    y = _dot(jnp.concatenate(_split3(x), axis=0), tri)
    return (y[0:n] + y[n:2 * n]) + y[2 * n:3 * n]


_ADA_COLS = 256
_SH1, _SC1 = 0, 1
_GT1, _SH2, _SC2, _GT2 = 0, 1, 2, 3
_N_MOD_A = 2


def _ada_block(cs_ref, cp_ref, w_ref, b_ref, o_ref):
    c = jnp.concatenate([cs_ref[...], cp_ref[...]], axis=0)
    a = (c * jax.nn.sigmoid(c)).astype(BF16)
    res = _dot(a, w_ref[...].astype(BF16)) + b_ref[...]
    o_ref[0] = res.reshape(res.shape[0], 1, res.shape[1])


def _ada_specs(c_sample, c_prompt, first_block, step_of):
    rows = c_sample.shape[0] + c_prompt.shape[0]
    per_chunk = D_MODEL // _ADA_COLS
    const2 = lambda *g: (0, 0)
    in_specs = [pl.BlockSpec(c_sample.shape, const2), pl.BlockSpec(c_prompt.shape, const2),
                pl.BlockSpec((D_MODEL, _ADA_COLS), lambda *g: (0, first_block + step_of(*g))),
                pl.BlockSpec((1, _ADA_COLS), lambda *g: (0, first_block + step_of(*g)))]
    out_spec = pl.BlockSpec((1, rows, 1, _ADA_COLS), lambda *g: (step_of(*g) // per_chunk, 0, 0, step_of(*g) % per_chunk))
    return in_specs, out_spec, rows


def _mod_spec(k, nb, row0, block_index):
    return pl.BlockSpec((1, nb, 1, D_MODEL), lambda *g: (k, row0 // nb + block_index(*g), 0, 0))


_PREP_COLS = 512


def _prep_kernel(wt_ref, gt_ref, wp_ref, cs_ref, cp_ref, wa_ref, ba_ref, head_ref, gate_ref, pool_ref, mod_ref, *,
                 head_steps):
    @pl.when(pl.program_id(0) < head_steps)
    def _():
        head_ref[...] = wt_ref[0].T.astype(BF16)

    gates = gt_ref[0].T
    pad = jnp.zeros((gates.shape[0], GATE_LANES - gates.shape[1]), F32)
    gate_ref[...] = jnp.concatenate([gates, pad], axis=1).astype(BF16)
    pool_ref[...] = wp_ref[0].astype(BF16)
    _ada_block(cs_ref, cp_ref, wa_ref, ba_ref, mod_ref)


def _prep(w_in_t, w_pool, c_sample, c_prompt, w_ada, b_ada):
    g0 = _G0 + D_MODEL
    head_steps = _G0 // _PREP_COLS
    steps = _N_MOD_A * D_MODEL // _ADA_COLS
    assert steps >= head_steps
    head_blk = lambda j: jnp.minimum(j, head_steps - 1)
    ada_in, ada_out, rows = _ada_specs(c_sample, c_prompt, 0, lambda j: j)
    return pl.pallas_call(
        functools.partial(_prep_kernel, head_steps=head_steps),
        grid=(steps,),
        in_specs=[pl.BlockSpec((1, _PREP_COLS, D_MODEL), lambda j: (0, head_blk(j), 0)),
                  pl.BlockSpec((1, 2 * ML_HEADS, D_MODEL), lambda j: (0, g0 // (2 * ML_HEADS), 0)),
                  pl.BlockSpec(w_pool.shape, lambda j: (0, 0, 0, 0))] + ada_in,
        out_specs=[pl.BlockSpec((D_MODEL, _PREP_COLS), lambda j: (0, head_blk(j))),
                   pl.BlockSpec((D_MODEL, GATE_LANES), lambda j: (0, 0)),
                   pl.BlockSpec(w_pool.shape[1:], lambda j: (0, 0, 0)), ada_out],
        out_shape=[jax.ShapeDtypeStruct((D_MODEL, _G0), BF16),
                   jax.ShapeDtypeStruct((D_MODEL, GATE_LANES), BF16),
                   jax.ShapeDtypeStruct(w_pool.shape[1:], BF16),
                   jax.ShapeDtypeStruct((_N_MOD_A, rows, 1, D_MODEL), F32)],
        compiler_params=_params(("arbitrary",)),
        name="prep",
    )(w_in_t, w_in_t, w_pool, c_sample, c_prompt, w_ada, b_ada)


_TAIL_COLS = 256
_TAIL_BLOCKS = 3 * D_MODEL // _TAIL_COLS
_TAIL_SHIFT_BLOCK = D_MODEL // _TAIL_COLS
_TAIL_SHIFT = 2 * ML_HEADS


def _tail_block(step):
    return jnp.minimum(step, _TAIL_BLOCKS - 1)


def _tail_prep(step, wa_ref, wb_ref, tail_ref):
    c = _tail_block(step)
    a = wa_ref[0]
    shifted = jnp.concatenate([a[_TAIL_SHIFT:, :], wb_ref[0]], axis=0)
    tail_ref[...] = jnp.where(c >= _TAIL_SHIFT_BLOCK, shifted, a).T.astype(BF16)


def _project(hb, w_ref, wg_ref, bg_ref, q_ref, kT_ref, v_ref, gc_ref, gr_ref):
    zg = _dot(hb, wg_ref[...]) + bg_ref[...]
    k = _dot(hb, w_ref[:, _K0:_V0]) * (ML_HEAD_DIM ** -0.5)
    log_f = jnp.minimum(zg, 0.0) - jnp.log1p(jnp.exp(-jnp.abs(zg)))
    lane = lax.broadcasted_iota(jnp.int32, zg.shape, 1)
    gc = jnp.where(lane < ML_HEADS, zg, log_f)
    if gc_ref is not None:
        gc_ref[...] = gc
    gr_ref[...] = gc.T[0:SUBLANES, :]
    kT_ref[...] = k.T.astype(BF16)
    q_ref[...] = _dot(hb, w_ref[:, _Q0:_K0]).astype(BF16)
    v_ref[...] = _dot(hb, w_ref[:, _V0:_G0]).astype(BF16)
    return k


def _pool_deltas(acc_fn, u, cnt_fn):
    deltas = []
    for g, w in enumerate(POOL_WINDOWS):
        ug = u[:, g * POOL_GROUP_DIM:(g + 1) * POOL_GROUP_DIM]
        deltas.append((acc_fn(g, w, ug) / cnt_fn(w) - ug).astype(BF16))
    return deltas


def _pool_project(deltas, wpool_ref, spool_ref):
    outs = [_dot(d, wpool_ref[g]) for g, d in enumerate(deltas)]
    return (jnp.concatenate(outs, axis=-1) * spool_ref[...]).astype(BF16)


def _inproj_prompt_kernel(x_ref, sc_ref, sh_ref, g_ref, w_ref, wg_ref, bg_ref, wpool_ref, spool_ref,
                          wbp_ref, wbm_ref, wout_ref, wgu_ref, wdn_ref, wa_ref, wb_ref, cs_ref, cp_ref, wada_ref, bada_ref,
                          yp_ref, q_ref, kT_ref, v_ref, gr_ref, hout_ref,
                          wbp_o, wbm_o, wout_o, wgu_o, wdn_o, tail_o, mod_o, ext_ref, *, tm):
    t = pl.program_id(1)

    @pl.when(t == 0)
    def _():
        ext_ref[0:HIST_ROWS, :] = jnp.zeros((HIST_ROWS, POOL_WIDTH), F32)

    _, hmod = _norm_mod(x_ref, sc_ref, sh_ref, g_ref)
    hb = hmod.astype(BF16)
    u = _dot(hb, w_ref[:, _U0:_Q0])
    ext_ref[HIST_ROWS:HIST_ROWS + tm, :] = u
    pos = t * tm + lax.broadcasted_iota(jnp.int32, (tm, 1), 0)

    def acc_fn(g, w, ug):
        acc = ug
        for j in range(1, w):
            acc = acc + ext_ref[pl.ds(HIST_ROWS - j, tm), g * POOL_GROUP_DIM:(g + 1) * POOL_GROUP_DIM]
        return acc

    def cnt_fn(w):
        return jnp.minimum(pos + 1, w).astype(F32)

    deltas = _pool_deltas(acc_fn, u, cnt_fn)
    last = ext_ref[tm:tm + HIST_ROWS, :]
    hout_ref[0] = last
    ext_ref[0:HIST_ROWS, :] = last

    _project(hb, w_ref, wg_ref, bg_ref, q_ref, kT_ref.at[0], v_ref, None, gr_ref.at[0])
    yp_ref[...] = _pool_project(deltas, wpool_ref, spool_ref)

    _tail_prep(pl.program_id(0) * pl.num_programs(1) + t, wa_ref, wb_ref, tail_o)
    for src, dst in ((wbp_ref, wbp_o), (wbm_ref, wbm_o), (wout_ref, wout_o), (wgu_ref, wgu_o), (wdn_ref, wdn_o)):
        dst[...] = src[0].astype(BF16)
    _ada_block(cs_ref, cp_ref, wada_ref, bada_ref, mod_o)

def _inproj_sample_kernel(x_ref, sc_ref, sh_ref, g_ref, w_ref, wg_ref, bg_ref, wpool_ref, spool_ref, hist_ref,
                          yp_ref, q_ref, kT_ref, v_ref, gc_ref, gr_ref, k_ref, hout_ref, u_ref, d_ref, *, pos0):
    _, hmod = _norm_mod(x_ref, sc_ref, sh_ref, g_ref)
    hb = hmod.astype(BF16)
    nb, tt = x_ref.shape[0], x_ref.shape[1]
    u = _dot(hb, w_ref[:, _U0:_Q0])
    k_ref[...] = _project(hb, w_ref, wg_ref, bg_ref, q_ref, kT_ref, v_ref, gc_ref, gr_ref).astype(BF16)
    for g, w in enumerate(POOL_WINDOWS):
        cols = slice(g * POOL_GROUP_DIM, (g + 1) * POOL_GROUP_DIM)
        u_ref[g] = u[:, cols]
        new = [u_ref[g, pl.ds(t, nb, stride=tt), :] for t in range(tt)]

        def ext(r):
            return hist_ref[r, :, cols] if r < POOL_HIST else new[r - POOL_HIST]

        for t in range(tt):
            acc = new[t]
            for j in range(1, w):
                acc = acc + ext(POOL_HIST + t - j)
            cnt = float(min(pos0 + t + 1, w))
            d_ref[g, pl.ds(t, nb, stride=tt), :] = acc / cnt - new[t]
        for r in range(POOL_HIST):
            hout_ref[r, :, cols] = ext(r + tt)
    yp_ref[...] = _pool_project([d_ref[g].astype(BF16) for g in range(len(POOL_WINDOWS))], wpool_ref, spool_ref)


def _cast_specs(weights, nt, steps):
    in_specs, out_specs, out_shapes = [], [], []
    for w in weights:
        _, r, c = w.shape
        n = steps
        while r % n or (r // n) % BF16_ROWS:
            n //= 2
        rows = r // n
        idx = lambda b, t, n=n: jnp.minimum(b * nt + t, n - 1)
        in_specs.append(pl.BlockSpec((1, rows, c), lambda b, t, idx=idx: (0, idx(b, t), 0)))
        out_specs.append(pl.BlockSpec((rows, c), lambda b, t, idx=idx: (idx(b, t), 0)))
        out_shapes.append(jax.ShapeDtypeStruct((r, c), BF16))
    return in_specs, out_specs, out_shapes


def _inproj_prompt(x, mod, row0, g_mix, w_head, w_gate, bgate, w_pool, s_pool, later_weights, w_in_t, ada_operands):
    nbatch, seq, _ = x.shape
    tm = PROMPT_INPROJ_TILE
    nt = seq // tm
    ntok = nbatch * seq
    row = lambda b, t: (b * nt + t, 0)
    sc, sh = (_mod_spec(k, 1, row0, lambda b, t: b) for k in (_SC1, _SH1))
    cast_in, cast_out, cast_shapes = _cast_specs(later_weights, nt, nbatch * nt)
    tail_row = lambda b, t: _G0 // _TAIL_COLS + _tail_block(b * nt + t)
    cast_in = cast_in + [
        pl.BlockSpec((1, _TAIL_COLS, D_MODEL), lambda b, t: (0, tail_row(b, t), 0)),
        pl.BlockSpec((1, _TAIL_SHIFT, D_MODEL), lambda b, t: (0, (tail_row(b, t) + 1) * (_TAIL_COLS // _TAIL_SHIFT), 0))]
    cast_out = cast_out + [pl.BlockSpec((D_MODEL, _TAIL_COLS), lambda b, t: (0, _tail_block(b * nt + t)))]
    cast_shapes = cast_shapes + [jax.ShapeDtypeStruct((D_MODEL, _TAIL_BLOCKS * _TAIL_COLS), BF16)]
    c_sample, c_prompt, w_ada, _ = ada_operands
    n_mod_b = w_ada.shape[1] // D_MODEL - _N_MOD_A
    assert nbatch * nt == n_mod_b * D_MODEL // _ADA_COLS, "one modulation block per grid step"
    ada_in, ada_out, mod_rows = _ada_specs(c_sample, c_prompt, _N_MOD_A * D_MODEL // _ADA_COLS, lambda b, t: b * nt + t)
    cast_in, cast_out = cast_in + ada_in, cast_out + [ada_out]
    cast_shapes = cast_shapes + [jax.ShapeDtypeStruct((n_mod_b, mod_rows, 1, D_MODEL), F32)]
    return pl.pallas_call(
        functools.partial(_inproj_prompt_kernel, tm=tm),
        grid=(nbatch, nt),
        in_specs=[pl.BlockSpec((1, tm, D_MODEL), lambda b, t: (b, t, 0)), sc, sh,
                  _const_spec((1, D_MODEL)), _const_spec((D_MODEL, _G0)), _const_spec((D_MODEL, GATE_LANES)),
                  _const_spec((1, GATE_LANES)), _const_spec((len(POOL_WINDOWS), POOL_GROUP_DIM, POOL_GROUP_DIM)),
                  _const_spec((1, POOL_WIDTH))] + cast_in,
        out_specs=[pl.BlockSpec((tm, POOL_WIDTH), row), pl.BlockSpec((tm, D_MODEL), row),
                   pl.BlockSpec((1, D_MODEL, tm), lambda b, t: (b, 0, t)), pl.BlockSpec((tm, D_MODEL), row),
                   pl.BlockSpec((1, SUBLANES, tm), lambda b, t: (b, 0, t)),
                   pl.BlockSpec((1, HIST_ROWS, POOL_WIDTH), lambda b, t: (b, 0, 0))] + cast_out,
        out_shape=[jax.ShapeDtypeStruct((ntok, POOL_WIDTH), BF16), jax.ShapeDtypeStruct((ntok, D_MODEL), BF16),
                   jax.ShapeDtypeStruct((nbatch, D_MODEL, seq), BF16), jax.ShapeDtypeStruct((ntok, D_MODEL), BF16),
                   jax.ShapeDtypeStruct((nbatch, SUBLANES, seq), F32),
                   jax.ShapeDtypeStruct((nbatch, HIST_ROWS, POOL_WIDTH), F32)] + cast_shapes,
        scratch_shapes=[pltpu.VMEM((tm + HIST_ROWS, POOL_WIDTH), F32)],
        compiler_params=_params(("arbitrary", "arbitrary")),
        name="inproj_prompt",
    )(x, mod, mod, g_mix, w_head, w_gate, bgate, w_pool, s_pool, *later_weights, w_in_t, w_in_t, *ada_operands)


def _inproj_sample(x, mod, row0, g_mix, w_head, w_gate, bgate, w_pool, s_pool, hist_tm):
    nseq, tt, _ = x.shape
    nb = TOKEN_TILE // tt
    tm = nb * tt
    ntok = nseq * tt
    row = lambda i: (i, 0)
    col = lambda i: (0, i)
    sc, sh = (_mod_spec(k, nb, row0, lambda i: i) for k in (_SC1, _SH1))
    hist = pl.BlockSpec((POOL_HIST, nb, POOL_WIDTH), lambda i: (0, i, 0))
    return pl.pallas_call(
        functools.partial(_inproj_sample_kernel, pos0=PAST_LEN),
        grid=(nseq // nb,),
        in_specs=[pl.BlockSpec((nb, tt, D_MODEL), lambda i: (i, 0, 0)), sc, sh,
                  _const_spec((1, D_MODEL)), _const_spec((D_MODEL, _G0)), _const_spec((D_MODEL, GATE_LANES)),
                  _const_spec((1, GATE_LANES)), _const_spec((len(POOL_WINDOWS), POOL_GROUP_DIM, POOL_GROUP_DIM)),
                  _const_spec((1, POOL_WIDTH)), hist],
        out_specs=[pl.BlockSpec((tm, POOL_WIDTH), row), pl.BlockSpec((tm, D_MODEL), row),
                   pl.BlockSpec((D_MODEL, tm), col), pl.BlockSpec((tm, D_MODEL), row),
                   pl.BlockSpec((tm, GATE_LANES), row), pl.BlockSpec((SUBLANES, tm), col),
                   pl.BlockSpec((tm, D_MODEL), row), hist],
        out_shape=[jax.ShapeDtypeStruct((ntok, POOL_WIDTH), BF16), jax.ShapeDtypeStruct((ntok, D_MODEL), BF16),
                   jax.ShapeDtypeStruct((D_MODEL, ntok), BF16), jax.ShapeDtypeStruct((ntok, D_MODEL), BF16),
                   jax.ShapeDtypeStruct((ntok, GATE_LANES), F32), jax.ShapeDtypeStruct((SUBLANES, ntok), F32),
                   jax.ShapeDtypeStruct((ntok, D_MODEL), BF16),
                   jax.ShapeDtypeStruct((POOL_HIST, nseq, POOL_WIDTH), F32)],
        scratch_shapes=[pltpu.VMEM((len(POOL_WINDOWS), tm, POOL_GROUP_DIM), F32)] * 2,
        compiler_params=_params(("arbitrary",)),
        name="inproj_sample",
    )(x, mod, mod, g_mix, w_head, w_gate, bgate, w_pool, s_pool, hist_tm)


def _mlstm_chunk(q_ref, kT_ref, v_ref, gr_ref, hh_ref, cn_ref, m_ref, r0, chunk):
    rows = slice(r0, r0 + chunk)
    gr = gr_ref[:, rows]
    ri = lax.broadcasted_iota(jnp.int32, (chunk, chunk), 0)
    ci = lax.broadcasted_iota(jnp.int32, (chunk, chunk), 1)
    causal = ri >= ci
    brow = _cumsum_rows(gr, (ri <= ci).astype(BF16))
    bcol = brow.T
    ones = jnp.ones((chunk, LANES), BF16)
    tile_to = lambda x, width: jnp.concatenate([x] * (width // LANES), axis=-1)
    stage = []
    for h in range(ML_HEADS):
        hs = slice(h * ML_HEAD_DIM, (h + 1) * ML_HEAD_DIM)
        b_r = brow[ML_HEADS + h:ML_HEADS + h + 1, :]
        g_r = gr[h:h + 1, :] - b_r
        m_prev = m_ref[h:h + 1, 0:1]
        cn = cn_ref[h]
        q = q_ref[rows, hs]
        kT = kT_ref[hs, rows]
        gm = jnp.where(causal, g_r, -jnp.inf)
        big_m = jnp.maximum(m_prev, jnp.max(gm, axis=-1, keepdims=True))
        m_rep = jnp.broadcast_to(big_m, (chunk, LANES))
        mt_rep = jnp.broadcast_to(bcol[:, ML_HEADS + h:ML_HEADS + h + 1] + big_m, (chunk, LANES))
        qkc = _dot(q, jnp.concatenate([kT, cn.astype(BF16)], axis=1))
        s = qkc[:, :chunk] * jnp.exp(gm - tile_to(m_rep, chunk))
        b_last = b_r[:, chunk - 1:chunk]
        m_new = b_last + jnp.maximum(m_prev, jnp.max(g_r, axis=-1, keepdims=True))
        decay = jnp.exp((b_last + m_prev) - m_new)
        w_end = jnp.exp((g_r + b_last) - m_new)
        lhs = jnp.concatenate([s.astype(BF16), (kT.astype(F32) * w_end).astype(BF16)], axis=0)
        stage.append((lhs, qkc[:, chunk:], m_prev, m_rep, mt_rep, m_new, decay, cn))

    for h, (lhs, qc, m_prev, m_rep, mt_rep, m_new, decay, cn) in enumerate(stage):
        hs = slice(h * ML_HEAD_DIM, (h + 1) * ML_HEAD_DIM)
        vaug = jnp.concatenate([v_ref[rows, hs], ones], axis=-1)
        both = _dot(lhs, vaug)
        sva = both[0:chunk]
        w_inter = jnp.exp(m_prev - m_rep)
        den = w_inter * qc[:, ML_HEAD_DIM:] + sva[:, ML_HEAD_DIM:]
        rinv = 1.0 / jnp.maximum(jnp.abs(den), jnp.exp(-mt_rep))
        hh_ref[rows, hs] = ((tile_to(w_inter, ML_HEAD_DIM) * qc[:, :ML_HEAD_DIM] + sva[:, :ML_HEAD_DIM])
                            * tile_to(rinv, ML_HEAD_DIM))
        cn_ref[h] = decay * cn + both[chunk:]
        m_ref[h:h + 1, :] = jnp.broadcast_to(m_new, (1, LANES))


def _mlstm_prompt_kernel(q_ref, kT_ref, v_ref, gr_ref, hh_ref, cout_ref, nout_ref, mout_ref, cn_ref, m_ref, *, chunk):
    t = pl.program_id(1)
    nb = q_ref.shape[0]

    @pl.when(t == 0)
    def _():
        cn_ref[...] = jnp.zeros(cn_ref.shape, F32)
        m_ref[...] = jnp.full(m_ref.shape, M_INIT, F32)

    for r0 in range(0, q_ref.shape[1], chunk):
        for s in range(nb):
            _mlstm_chunk(q_ref.at[s], kT_ref.at[s], v_ref.at[s], gr_ref.at[s], hh_ref.at[s], cn_ref.at[s],
                         m_ref.at[s], r0, chunk)

    @pl.when(t == pl.num_programs(1) - 1)
    def _():
        cout_ref[...] = cn_ref[:, :, :, 0:ML_HEAD_DIM]
        nout_ref[...] = cn_ref[:, :, :, ML_HEAD_DIM:AUG]
        mout_ref[...] = m_ref[...]


def _mlstm_prompt(q, kT, v, gr):
    nbatch, seq, _ = q.shape
    tm, nb = MLSTM_TILE, MLSTM_BATCHES
    row = pl.BlockSpec((nb, tm, D_MODEL), lambda b, t: (b, t, 0))
    state = lambda last: pl.BlockSpec((nb, ML_HEADS, ML_HEAD_DIM, last), lambda b, t: (b, 0, 0, 0))
    return pl.pallas_call(
        functools.partial(_mlstm_prompt_kernel, chunk=PROMPT_CHUNK),
        grid=(nbatch // nb, seq // tm),
        in_specs=[row, pl.BlockSpec((nb, D_MODEL, tm), lambda b, t: (b, 0, t)), row,
                  pl.BlockSpec((nb, SUBLANES, tm), lambda b, t: (b, 0, t))],
        out_specs=[row, state(ML_HEAD_DIM), state(LANES), pl.BlockSpec((nb, SUBLANES, LANES), lambda b, t: (b, 0, 0))],
        out_shape=[jax.ShapeDtypeStruct((nbatch, seq, D_MODEL), F32),
                   jax.ShapeDtypeStruct((nbatch, ML_HEADS, ML_HEAD_DIM, ML_HEAD_DIM), F32),
                   jax.ShapeDtypeStruct((nbatch, ML_HEADS, ML_HEAD_DIM, LANES), F32),
                   jax.ShapeDtypeStruct((nbatch, SUBLANES, LANES), F32)],
        scratch_shapes=[pltpu.VMEM((nb, ML_HEADS, ML_HEAD_DIM, AUG), F32), pltpu.VMEM((nb, SUBLANES, LANES), F32)],
        compiler_params=_params(("arbitrary", "arbitrary")),
        name="mlstm_prompt",
    )(q, kT, v, gr)


def _last_in_group(x, group):
    rows = x.shape[0]
    x3 = jnp.broadcast_to(x, (rows, LANES)).reshape(rows // group, group, LANES)
    last = jnp.broadcast_to(x3[:, group - 1:group, :], x3.shape)
    return last.reshape(rows, LANES)[:, 0:1]


def _mlstm_sample_body(h, q_ref, kT_ref, k_ref, v_ref, gc_ref, gr_ref, m0_ref, c_ref, n_ref,
                       hh_ref, cout_ref, nout_ref, mt_ref, ni_ref, qn_ref, dec_ref, wk_ref, *, tt):
    L = q_ref.shape[0]
    nseq = L // tt
    gc = gc_ref[...]
    gr = gr_ref[...]
    ri = lax.broadcasted_iota(jnp.int32, (L, L), 0)
    ci = lax.broadcasted_iota(jnp.int32, (L, L), 1)
    same = (ri // tt) == (ci // tt)
    mask = same & (ri >= ci)
    bcol = _cumsum_cols(mask.astype(BF16), gc)
    brow = _cumsum_rows(gr, (same & (ri <= ci)).astype(BF16))
    lane = lax.broadcasted_iota(jnp.int32, (L, GATE_LANES), 1)
    sub = lax.broadcasted_iota(jnp.int32, (SUBLANES, L), 0)
    pick_col = lambda arr, idx: jnp.sum(jnp.where(lane == idx, arr, 0.0), axis=-1, keepdims=True)
    pick_row = lambda arr, idx: jnp.sum(jnp.where(sub == idx, arr, 0.0), axis=0, keepdims=True)
    ig_c = pick_col(gc, h)
    b_c = pick_col(bcol, ML_HEADS + h)
    ig_r = pick_row(gr, h)
    b_r = pick_row(brow, ML_HEADS + h)
    m_prev = m0_ref[0]

    for j in range(nseq):
        rows = slice(j * tt, (j + 1) * tt)
        qj = q_ref[rows, :]
        ni_ref[rows, :] = _dot(qj, c_ref[j, 0].astype(BF16))
        nj = n_ref[j, pl.ds(h, 1), :].astype(BF16).astype(F32)
        qn = jnp.sum(qj.astype(F32) * nj, axis=-1, keepdims=True)
        qn_ref[rows, :] = jnp.broadcast_to(qn, (tt, LANES))

    q = q_ref[...]
    kT = kT_ref[...]
    v = v_ref[...]
    qk = _dot(q, kT)
    logd = jnp.where(mask, (b_c - b_r) + ig_r, -jnp.inf)
    a_c = b_c + m_prev
    m_t = jnp.maximum(a_c, jnp.max(logd, axis=-1, keepdims=True))
    w_inter = jnp.exp(a_c - m_t)
    s = qk * jnp.exp(logd - m_t)
    num = w_inter * ni_ref[...] + _dot(s.astype(BF16), v)
    den = w_inter * qn_ref[:, 0:1] + jnp.sum(s, axis=-1, keepdims=True)
    hh_ref[...] = num / jnp.maximum(jnp.abs(den), jnp.exp(-m_t))
    m_new = _last_in_group(m_t, tt)
    decay = jnp.exp(_last_in_group(a_c, tt) - m_new)
    w_end = jnp.exp((_last_in_group(b_c, tt) - b_c) + ig_c - m_new)
    mt_ref[0] = jnp.broadcast_to(m_t, (L, LANES))
    dec_ref[...] = jnp.broadcast_to(decay, (L, LANES))
    wv = w_end * v.astype(F32)
    wk_ref[...] = w_end.astype(BF16).astype(F32) * k_ref[...].astype(F32)
    rowi = lax.broadcasted_iota(jnp.int32, (L, 1), 0)

    for j in range(nseq):
        rows = slice(j * tt, (j + 1) * tt)
        upd = _dot(kT, jnp.where((rowi // tt) == j, wv, 0.0).astype(BF16))
        dj = dec_ref[j * tt:j * tt + 1, 0:1]
        cout_ref[j, 0] = dj * c_ref[j, 0] + upd
        nout_ref[j, pl.ds(h, 1), :] = (dj * n_ref[j, pl.ds(h, 1), :]
                                       + jnp.sum(wk_ref[rows, :], axis=0, keepdims=True))


def _mlstm_sample_specs(ntok, tt):
    nseq = ntok // tt
    sb = SAMPLE_SEQ_BLOCK
    L = sb * tt
    nh = ML_HEADS
    qspec = pl.BlockSpec((L, ML_HEAD_DIM), lambda i: (i // nh, i % nh))
    cspec = pl.BlockSpec((sb, 1, ML_HEAD_DIM, ML_HEAD_DIM), lambda i: (i // nh, i % nh, 0, 0))
    nspec = pl.BlockSpec((sb, nh, ML_HEAD_DIM), lambda i: (i // nh, 0, 0))
    in_specs = [qspec, pl.BlockSpec((ML_HEAD_DIM, L), lambda i: (i % nh, i // nh)), qspec, qspec,
                pl.BlockSpec((L, GATE_LANES), lambda i: (i // nh, 0)),
                pl.BlockSpec((SUBLANES, L), lambda i: (0, i // nh)),
                pl.BlockSpec((1, L, 1), lambda i: (i % nh, i // nh, 0)), cspec, nspec]
    out_specs = [qspec, cspec, nspec, pl.BlockSpec((1, L, LANES), lambda i: (i % nh, i // nh, 0))]
    out_shapes = [jax.ShapeDtypeStruct((ntok, D_MODEL), F32),
                  jax.ShapeDtypeStruct((nseq, ML_HEADS, ML_HEAD_DIM, ML_HEAD_DIM), F32),
                  jax.ShapeDtypeStruct((nseq, ML_HEADS, ML_HEAD_DIM), F32),
                  jax.ShapeDtypeStruct((ML_HEADS, ntok, LANES), F32)]
    scratch = [pltpu.VMEM((L, ML_HEAD_DIM), F32), pltpu.VMEM((L, LANES), F32),
               pltpu.VMEM((L, LANES), F32), pltpu.VMEM((L, ML_HEAD_DIM), F32)]
    return (nseq // sb) * nh, in_specs, out_specs, out_shapes, scratch


def _post_kernel(x_ref, sc_ref, sh_ref, gt_ref, g_ref, hh_ref, yp_ref, wt_ref, ghead_ref, wbp_ref, wbm_ref, wout_ref,
                 o_ref):
    pool = _dot(yp_ref[...], wbp_ref[...])
    x, hmod = _norm_mod(x_ref, sc_ref, sh_ref, g_ref)
    hb = hmod.astype(BF16)
    nb, tt, d = x_ref.shape
    o = _dot(hb, wt_ref[:, 0:D_MODEL])
    parts = []
    for h in range(ML_HEADS):
        hh = hh_ref[:, h * ML_HEAD_DIM:(h + 1) * ML_HEAD_DIM]
        parts.append(hh * lax.rsqrt(jnp.mean(hh * hh, axis=-1, keepdims=True) + EPS))
    yml = (jnp.concatenate(parts, axis=-1) * ghead_ref[...]) * jax.nn.sigmoid(o)
    gp = _dot(hb, wt_ref[:, D_MODEL:2 * D_MODEL])
    gm = _dot(hb, wt_ref[:, 2 * D_MODEL:3 * D_MODEL])
    merged = jax.nn.sigmoid(gp) * pool + jax.nn.sigmoid(gm) * _dot(yml.astype(BF16), wbm_ref[...])
    y = _dot(merged.astype(BF16), wout_ref[...]).reshape(nb, tt, d)
    o_ref[...] = x_ref[...] + gt_ref[0] * y


def _tile_blocks(x, tile):
    g, t, _ = x.shape
    if t >= tile:
        nb, tt = 1, tile
    else:
        nb, tt = tile // t, t
    return nb, tt, (g // nb) * (t // tt), t // tt


def _post(x, mod_a, mod_b, row0, g_mix, hh, yp, w_tail, g_head, w_bp, w_bm, w_out):
    nb, tt, steps, per = _tile_blocks(x, POST_TILE)
    tm = nb * tt
    xspec = pl.BlockSpec((nb, tt, D_MODEL), lambda i: (i // per, i % per, 0))
    sc, sh, gt = (_mod_spec(k, nb, row0, lambda i: i // per) for k in (_SC1, _SH1, _GT1))
    row = lambda i: (i, 0)
    return pl.pallas_call(
        _post_kernel,
        grid=(steps,),
        in_specs=[xspec, sc, sh, gt, _const_spec((1, D_MODEL)),
                  pl.BlockSpec((tm, D_MODEL), row), pl.BlockSpec((tm, POOL_WIDTH), row),
                  _const_spec((D_MODEL, 3 * D_MODEL)), _const_spec((1, D_MODEL)),
                  _const_spec((POOL_WIDTH, D_MODEL)), _const_spec((D_MODEL, D_MODEL)),
                  _const_spec((D_MODEL, D_MODEL))],
        out_specs=xspec,
        out_shape=jax.ShapeDtypeStruct(x.shape, F32),
        compiler_params=_params(("arbitrary",)),
        name="post",
    )(x, mod_a, mod_a, mod_b, g_mix, hh, yp, w_tail, g_head, w_bp, w_bm, w_out)


_FF_SPLITS = ((0, 768), (768, 1536), (1536, 2304), (2304, D_FF))


def _ffn_kernel(x_ref, sc_ref, sh_ref, gt_ref, g_ref, gfin_ref, wgu_ref, wdn_ref, o_ref):
    _, hmod = _norm_mod(x_ref, sc_ref, sh_ref, g_ref)
    hb = hmod.astype(BF16)
    nb, tt, d = x_ref.shape
    dn = None
    for lo, hi in _FF_SPLITS:
        gate = _dot(hb, wgu_ref[:, lo:hi])
        up = _dot(hb, wgu_ref[:, D_FF + lo:D_FF + hi])
        act = (gate * jax.nn.sigmoid(gate) * up).astype(BF16)
        part = _dot(act, wdn_ref[lo:hi, :])
        dn = part if dn is None else dn + part
    x2 = x_ref[...] + gt_ref[0] * dn.reshape(nb, tt, d)
    ms = jnp.mean(x2 * x2, axis=-1, keepdims=True)
    o_ref[...] = x2 * lax.rsqrt(ms + EPS) * gfin_ref[...]


_N_FFN_IN = 8


def _ffn_mlstm_kernel(*refs, n_ml_in, tt):
    ffn_in = refs[:_N_FFN_IN]
    ml_in = refs[_N_FFN_IN:_N_FFN_IN + n_ml_in]
    o_ref = refs[_N_FFN_IN + n_ml_in]
    ml_rest = refs[_N_FFN_IN + n_ml_in + 1:]
    _mlstm_sample_body(pl.program_id(0) % ML_HEADS, *ml_in, *ml_rest, tt=tt)
    _ffn_kernel(*ffn_in, o_ref)


def _ffn(x, mod, row0, g_ffn, g_final, w_gu, w_down, tile, mlstm_sample=None):
    nb, tt, steps, per = _tile_blocks(x, tile)
    xspec = pl.BlockSpec((nb, tt, D_MODEL), lambda i: (i // per, i % per, 0))
    sc, sh, gt = (_mod_spec(k, nb, row0, lambda i: i // per) for k in (_SC2, _SH2, _GT2))
    in_specs = [xspec, sc, sh, gt, _const_spec((1, D_MODEL)), _const_spec((1, D_MODEL)),
                _const_spec((D_MODEL, 2 * D_FF)), _const_spec((D_FF, D_MODEL))]
    operands = (x, mod, mod, mod, g_ffn, g_final, w_gu, w_down)
    if mlstm_sample is None:
        return pl.pallas_call(
            _ffn_kernel,
            grid=(steps,),
            in_specs=in_specs,
            out_specs=xspec,
            out_shape=jax.ShapeDtypeStruct(x.shape, F32),
            compiler_params=_params(("arbitrary",)),
            name="ffn",
        )(*operands)
    ml_operands, ml_tt = mlstm_sample
    ml_steps, ml_in, ml_out, ml_shapes, ml_scratch = _mlstm_sample_specs(ml_operands[0].shape[0], ml_tt)
    assert ml_steps == steps, "one sample mLSTM step per FFN tile"
    return pl.pallas_call(
        functools.partial(_ffn_mlstm_kernel, n_ml_in=len(ml_in), tt=ml_tt),
        grid=(steps,),
        in_specs=in_specs + ml_in,
        out_specs=[xspec] + ml_out,
        out_shape=[jax.ShapeDtypeStruct(x.shape, F32)] + ml_shapes,
        scratch_shapes=ml_scratch,
        compiler_params=_params(("arbitrary",)),
        name="ffn_mlstm",
    )(*operands, *ml_operands)


def kernel(x_prompt, x_sample, c_prompt, c_sample, state_pool, state_mlstm_c, state_mlstm_n, state_mlstm_m, g_mix, g_ffn, g_final, w_ada, b_ada, w_in, b_igate, b_fgate, w_pool, s_pool, g_head, w_branch_pool, w_branch_mlstm, w_out, w_gate_up, w_down):
    depth = w_in.shape[0]
    assert depth == 1, "single-layer trunk"
    nbatch, seq, _ = x_prompt.shape
    nseq, tt, _ = x_sample.shape
    l = 0

    bgate = jnp.pad(jnp.concatenate([b_igate[l], b_fgate[l]])[None, :], ((0, 0), (0, GATE_LANES - 2 * ML_HEADS)))
    g_mix_r, g_ffn_r, g_fin_r = g_mix[l][None, :], g_ffn[l][None, :], g_final[None, :]
    s_pool_r, g_head_r = s_pool[l][None, :], g_head[l][None, :]

    w_in_t = jnp.swapaxes(w_in, 1, 2)
    ada_operands = (c_sample, c_prompt, w_ada[l], b_ada[l][None, :])
    w_head, w_gate, w_pool_b, mod = _prep(w_in_t, w_pool, *ada_operands)
    row_s, row_p = 0, nseq

    (yp, q, kT, v, gr, hist_p, w_bp, w_bm, w_o, w_gu, w_dn, w_tail, mod2) = _inproj_prompt(
        x_prompt, mod, row_p, g_mix_r, w_head, w_gate, bgate, w_pool_b, s_pool_r,
        (w_branch_pool, w_branch_mlstm, w_out, w_gate_up, w_down), w_in_t, ada_operands)
    hh, c_p, n_p, m_p = _mlstm_prompt(q.reshape(nbatch, seq, D_MODEL), kT, v.reshape(nbatch, seq, D_MODEL), gr)
    hh = hh.reshape(nbatch * seq, D_MODEL)
    x1 = _post(x_prompt, mod, mod2, row_p, g_mix_r, hh, yp, w_tail, g_head_r, w_bp, w_bm, w_o)

    hist_tm = jnp.swapaxes(state_pool[l], 0, 1)
    yp, q, kT, v, gc, gr, k, hist_s = _inproj_sample(x_sample, mod, row_s, g_mix_r, w_head, w_gate, bgate, w_pool_b,
                                                     s_pool_r, hist_tm)
    m0_tok = jnp.repeat(state_mlstm_m[l].astype(F32).T, tt, axis=1)[:, :, None]
    ml_operands = (q, kT, k, v, gc, gr, m0_tok, state_mlstm_c[l].astype(F32), state_mlstm_n[l].astype(F32))
    y_prompt, hh, c_s, n_s, mt = _ffn(x1, mod2, row_p, g_ffn_r, g_fin_r, w_gu, w_dn, PROMPT_FFN_TILE,
                                      (ml_operands, tt))

    x1 = _post(x_sample, mod, mod2, row_s, g_mix_r, hh, yp, w_tail, g_head_r, w_bp, w_bm, w_o)
    y_sample = _ffn(x1, mod2, row_s, g_ffn_r, g_fin_r, w_gu, w_dn, FFN_TILE)

    cd, nd, md = state_mlstm_c.dtype, state_mlstm_n.dtype, state_mlstm_m.dtype
    return (y_prompt, y_sample,
            hist_p[None, :, HIST_ROWS - POOL_HIST:, :],
            c_p.astype(cd)[None], n_p[..., 0].astype(nd)[None], m_p[:, :ML_HEADS, 0].astype(md)[None],
            jnp.swapaxes(hist_s, 0, 1)[None].astype(state_pool.dtype),
            c_s.astype(cd)[None], n_s.astype(nd)[None],
            mt[:, tt - 1::tt, 0].T.astype(md)[None])
```
